```python
import math
import jax
import jax.numpy as jnp
from jax import lax
import numpy as np

D_MODEL = 1024
BATCH = 16
SEQ = 256
DEPTH = 2
DEC_BATCH = 2
DEC_SEQ = 4096
PAST_LEN = 512

GRID_W = 64
N_BRANCH = 4
MIX_W = D_MODEL // 4
RET_HEADS = 4
RET_DH = MIX_W // RET_HEADS
S5_CH = 16
S5_GROUPS = MIX_W // S5_CH
S5_P = 64
HG_HEADS = 4
HG_DH = MIX_W // HG_HEADS
RW_HEADS = 4
RW_DH = MIX_W // RW_HEADS
RW_DECAY_LORA = 32
RW_A_LORA = 32
RW_GATE_LORA = 64
D_FF = 4 * D_MODEL
CHUNK = 64
EPS = 1e-6
RW_LN_EPS = 64e-5
RET_COLS = 4 * MIX_W
S5_COLS = MIX_W
HG_COLS = 5 * MIX_W
RW_COLS = 3 * MIX_W + 2 * RW_DECAY_LORA + RW_A_LORA + RW_GATE_LORA
GATE_COLS = N_BRANCH * D_MODEL
IN_COLS = RET_COLS + S5_COLS + HG_COLS + RW_COLS + GATE_COLS

kernel_name = 'bidir_hybrid_ret_s5_hgrn2_rwkv7_denoise_step'


def _split(x, sizes):
    idx, acc = [], 0
    for s in sizes[:-1]:
        acc += s
        idx.append(acc)
    return jnp.split(x, idx, axis=-1)


def _heads(t, n_heads):
    B, T, C = t.shape
    return t.reshape(B, T, n_heads, C // n_heads)


def rms_norm(x, w):
    xf = x.astype(jnp.float32)
    return xf * lax.rsqrt(jnp.mean(xf * xf, axis=-1, keepdims=True) + EPS) * w


def head_rms_norm(o, n_heads):
    B, T, C = o.shape
    oh = o.astype(jnp.float32).reshape(B, T, n_heads, C // n_heads)
    oh = oh * lax.rsqrt(jnp.mean(oh * oh, axis=-1, keepdims=True) + EPS)
    return oh.reshape(B, T, C)


def group_norm(o, n_heads, eps):
    B, T, C = o.shape
    oh = o.astype(jnp.float32).reshape(B, T, n_heads, C // n_heads)
    mu = jnp.mean(oh, axis=-1, keepdims=True)
    var = jnp.mean(jnp.square(oh - mu), axis=-1, keepdims=True)
    return ((oh - mu) * lax.rsqrt(var + eps)).reshape(B, T, C)


def shift_seq(z):
    B, T, C = z.shape
    g = z.reshape(B, T, C // 2, 2)
    prev = jnp.pad(g[:, :-1, :, 0], ((0, 0), (1, 0), (0, 0)))
    nxt = jnp.pad(g[:, 1:, :, 1], ((0, 0), (0, 1), (0, 0)))
    return jnp.stack([prev, nxt], axis=-1).reshape(B, T, C)


def shift_grid(z):
    B, T, C = z.shape
    rows = T // GRID_W
    g = z.reshape(B, rows, GRID_W, C // 4, 4)
    left = jnp.pad(g[:, :, :-1, :, 0], ((0, 0), (0, 0), (1, 0), (0, 0)))
    right = jnp.pad(g[:, :, 1:, :, 1], ((0, 0), (0, 0), (0, 1), (0, 0)))
    up = jnp.pad(g[:, :-1, :, :, 2], ((0, 0), (1, 0), (0, 0), (0, 0)))
    down = jnp.pad(g[:, 1:, :, :, 3], ((0, 0), (0, 1), (0, 0), (0, 0)))
    return jnp.stack([left, right, up, down], axis=-1).reshape(B, T, C)


def _to_chunks(t):
    B, T, H, d = t.shape
    return t.reshape(B, T // CHUNK, CHUNK, H, d).transpose(1, 0, 3, 2, 4)


def _from_chunks(t):
    nc, B, H, C, d = t.shape
    return t.transpose(1, 0, 3, 2, 4).reshape(B, nc * C, H, d)


def retention_chunkwise(q, k, v, log_gamma, s0, reverse):
    q, k, v = [t.astype(jnp.float32) for t in (q, k, v)]
    if reverse:
        q, k, v = [jnp.flip(t, 1) for t in (q, k, v)]
    idx = jnp.arange(CHUNK, dtype=jnp.float32)
    rel = idx[:, None] - idx[None, :]
    intra = jnp.where(rel >= 0, jnp.exp(jnp.maximum(rel, 0.0) * log_gamma[:, None, None]), 0.0)
    q_dec = jnp.exp((idx + 1.0) * log_gamma[:, None])[..., None]
    k_dec = jnp.exp((CHUNK - 1.0 - idx) * log_gamma[:, None])[..., None]
    c_dec = jnp.exp(CHUNK * log_gamma)[:, None, None]

    def step(S, blk):
        qc, kc, vc = blk
        att = jnp.einsum('bhid,bhjd->bhij', qc, kc) * intra
        o = jnp.einsum('bhij,bhje->bhie', att, vc) + jnp.einsum('bhid,bhde->bhie', qc * q_dec, S)
        S = S * c_dec + jnp.einsum('bhjd,bhje->bhde', kc * k_dec, vc)
        return S, o

    S, o = lax.scan(step, s0.astype(jnp.float32), (_to_chunks(q), _to_chunks(k), _to_chunks(v)))
    o = _from_chunks(o)
    if reverse:
        o = jnp.flip(o, 1)
    return o, S


def gla_chunkwise(q, k, v, log_f, s0, reverse):
    q, k, v, log_f = [t.astype(jnp.float32) for t in (q, k, v, log_f)]
    if reverse:
        q, k, v, log_f = [jnp.flip(t, 1) for t in (q, k, v, log_f)]
    idx = jnp.arange(CHUNK)
    lower = (idx[:, None] >= idx[None, :])[:, :, None]

    def step(S, blk):
        qc, kc, vc, gc = blk
        A = jnp.cumsum(gc, axis=2)
        dec = jnp.exp(jnp.where(lower, A[:, :, :, None, :] - A[:, :, None, :, :], -jnp.inf))
        att = jnp.einsum('bhid,bhijd,bhjd->bhij', qc, dec, kc)
        o = jnp.einsum('bhij,bhje->bhie', att, vc) + jnp.einsum('bhid,bhde->bhie', qc * jnp.exp(A), S)
        a_last = A[:, :, -1:, :]
        S = S * jnp.exp(a_last)[:, :, 0, :, None] + jnp.einsum('bhjd,bhje->bhde', kc * jnp.exp(a_last - A), vc)
        return S, o

    S, o = lax.scan(step, s0.astype(jnp.float32),
                    (_to_chunks(q), _to_chunks(k), _to_chunks(v), _to_chunks(log_f)))
    o = _from_chunks(o)
    if reverse:
        o = jnp.flip(o, 1)
    return o, S


def rwkv7_scan(r, log_w, k, v, kk, a, s0, reverse):
    xs = tuple(jnp.moveaxis(t.astype(jnp.float32), 1, 0) for t in (r, log_w, k, v, kk, a))

    def step(S, inp):
        rt, lwt, kt, vt, kkt, at = inp
        sa = jnp.einsum('bhvk,bhk->bhv', S, -kkt)
        S = (S * jnp.exp(lwt)[:, :, None, :] + sa[..., None] * (kkt * at)[:, :, None, :]
             + vt[..., None] * kt[:, :, None, :])
        return S, jnp.einsum('bhvk,bhk->bhv', S, rt)

    S, y = lax.scan(step, s0.astype(jnp.float32), xs, reverse=reverse)
    return jnp.moveaxis(y, 0, 1), S


def s5_zoh(lam_re, lam_im, log_dt, b_re, b_im):
    lam_re = jnp.minimum(lam_re.astype(jnp.float32), -1e-4)
    lam_im = lam_im.astype(jnp.float32)
    dt = jnp.exp(log_dt.astype(jnp.float32))[:, None]
    mag = jnp.exp(dt * lam_re)
    ang = dt * lam_im
    a_re, a_im = mag * jnp.cos(ang), mag * jnp.sin(ang)
    den = lam_re * lam_re + lam_im * lam_im
    f_re = ((a_re - 1.0) * lam_re + a_im * lam_im) / den
    f_im = (a_im * lam_re - (a_re - 1.0) * lam_im) / den
    bb_re = f_re[..., None] * b_re - f_im[..., None] * b_im
    bb_im = f_re[..., None] * b_im + f_im[..., None] * b_re
    return a_re, a_im, bb_re, bb_im


def _affine_combine(e1, e2):
    a1r, a1i, b1r, b1i = e1
    a2r, a2i, b2r, b2i = e2
    return (a2r * a1r - a2i * a1i, a2r * a1i + a2i * a1r,
            a2r * b1r - a2i * b1i + b2r, a2r * b1i + a2i * b1r + b2i)


def s5_scan(u, lam_re, lam_im, log_dt, b_re, b_im, x0_re, x0_im, reverse):
    a_re, a_im, bb_re, bb_im = s5_zoh(lam_re, lam_im, log_dt, b_re, b_im)
    bu_re = jnp.einsum('btgh,gph->btgp', u, bb_re)
    bu_im = jnp.einsum('btgh,gph->btgp', u, bb_im)
    ar = jnp.broadcast_to(a_re, bu_re.shape)
    ai = jnp.broadcast_to(a_im, bu_re.shape)
    ar, ai, xr, xi = lax.associative_scan(_affine_combine, (ar, ai, bu_re, bu_im), reverse=reverse, axis=1)
    x0r = x0_re.astype(jnp.float32)[:, None]
    x0i = x0_im.astype(jnp.float32)[:, None]
    return xr + ar * x0r - ai * x0i, xi + ar * x0i + ai * x0r


def retention_branch(z, s0, gn_w):
    B, T, _ = z.shape
    q, k, v, g = _split(z, (MIX_W, MIX_W, MIX_W, MIX_W))
    q, k, v = _heads(q, RET_HEADS), _heads(k, RET_HEADS) * RET_DH ** -0.5, _heads(v, RET_HEADS)
    h = jnp.arange(RET_HEADS, dtype=jnp.float32)
    log_gamma_fwd = jnp.log1p(-jnp.exp2(-5.0 - h))
    log_gamma_bwd = jnp.log1p(-jnp.exp2(-5.5 - h))
    o_f, s_f = retention_chunkwise(q, k, v, log_gamma_fwd, s0[:, 0], False)
    o_b, s_b = retention_chunkwise(q, k, v, log_gamma_bwd, s0[:, 1], True)
    o = (o_f + o_b).reshape(B, T, MIX_W)
    y = group_norm(o, RET_HEADS, EPS) * gn_w * jax.nn.silu(g)
    return y, jnp.stack([s_f, s_b], axis=1)


def s5_branch(z, s0_re, s0_im, lam_re, lam_im, log_dt, b_re, b_im, c_re, c_im, d, glu_w, glu_b):
    B, T, _ = z.shape
    zf = z.astype(jnp.float32)
    u = zf.reshape(B, T, S5_GROUPS, S5_CH)
    y = d * zf
    fin_re, fin_im = [], []
    for di, rev in ((0, False), (1, True)):
        xr, xi = s5_scan(u, lam_re[di], lam_im[di], log_dt[di], b_re[di], b_im[di],
                         s0_re[:, di], s0_im[:, di], rev)
        y = y + (jnp.einsum('btgp,ghp->btgh', xr, c_re)
                 - jnp.einsum('btgp,ghp->btgh', xi, c_im)).reshape(B, T, MIX_W)
        last = 0 if rev else T - 1
        fin_re.append(xr[:, last])
        fin_im.append(xi[:, last])
    yg = jax.nn.gelu(y)
    out = yg * jax.nn.sigmoid(yg @ glu_w + glu_b)
    return out, jnp.stack(fin_re, axis=1), jnp.stack(fin_im, axis=1)


def hgrn2_branch(z, s0, lower_bound, norm_w):
    B, T, _ = z.shape
    q, zf_fwd, zf_bwd, i, g = _split(z, (MIX_W, MIX_W, MIX_W, MIX_W, MIX_W))
    q = _heads(jax.nn.silu(q), HG_HEADS)
    i = _heads(i, HG_HEADS)
    o, finals = 0.0, []
    for di, (zf, rev) in enumerate(((zf_fwd, False), (zf_bwd, True))):
        zf = zf.astype(jnp.float32)
        lb = lower_bound[di]
        log_f = jnp.logaddexp(jnp.log(lb), jnp.log1p(-lb) + jax.nn.log_sigmoid(zf))
        key = (1.0 - lb) * jax.nn.sigmoid(-zf)
        o_d, s_d = gla_chunkwise(q, _heads(key, HG_HEADS), i, _heads(log_f, HG_HEADS), s0[:, di], rev)
        o = o + o_d
        finals.append(s_d)
    y = head_rms_norm(o.reshape(B, T, MIX_W), HG_HEADS) * norm_w * jax.nn.silu(g)
    return y, jnp.stack(finals, axis=1)


def rwkv7_branch(z, s0, shift_fn, mu, w0, w2, a0, a2, g2, k_k, k_a, r_k, ln_w, ln_b):
    B, T, _ = z.shape
    zs = z + mu * (shift_fn(z) - z)
    r, k, v, wd_f, wd_b, ad, gd = _split(
        zs, (MIX_W, MIX_W, MIX_W, RW_DECAY_LORA, RW_DECAY_LORA, RW_A_LORA, RW_GATE_LORA))
    a = jax.nn.sigmoid(a0 + ad @ a2)
    g = jax.nn.sigmoid(gd) @ g2
    kk = _heads(k * k_k, RW_HEADS).astype(jnp.float32)
    kk = kk * lax.rsqrt(jnp.sum(kk * kk, axis=-1, keepdims=True) + 1e-12)
    k2 = k * (1.0 + (a - 1.0) * k_a)
    y, finals = 0.0, []
    for di, (wd, rev) in enumerate(((wd_f, False), (wd_b, True))):
        log_w = -math.exp(-0.5) * jax.nn.sigmoid(w0[di] + jnp.tanh(wd) @ w2[di])
        y_d, s_d = rwkv7_scan(_heads(r, RW_HEADS), _heads(log_w, RW_HEADS), _heads(k2, RW_HEADS),
                              _heads(v, RW_HEADS), kk, _heads(a, RW_HEADS), s0[:, di], rev)
        y = y + y_d
        finals.append(s_d)
    y = group_norm(y.reshape(B, T, MIX_W), RW_HEADS, RW_LN_EPS) * ln_w + ln_b
    bonus = jnp.sum(_heads(r * k2 * r_k, RW_HEADS), axis=-1, keepdims=True) * _heads(v, RW_HEADS)
    y = (y + bonus.reshape(B, T, MIX_W)) * g
    return y, jnp.stack(finals, axis=1)


def trunk_layer(x, cond, init, shift_fn, p, lower_bound, l):
    B, T, _ = x.shape
    mod = jax.nn.silu(cond) @ p['ada_w'][l] + p['ada_b'][l]
    sh1, sc1, g1, sh2, sc2, g2 = jnp.split(mod[:, None, :], 6, axis=-1)
    h = rms_norm(x, p['norm_w'][l, 0]) * (1.0 + sc1) + sh1
    z = h @ p['w_in'][l]
    z_ret, z_s5, z_hg, z_rw, z_gate = _split(z, (RET_COLS, S5_COLS, HG_COLS, RW_COLS, GATE_COLS))
    s_ret, s_s5r, s_s5i, s_hg, s_rw = init
    y_ret, f_ret = retention_branch(z_ret, s_ret, p['ret_gn_w'][l])
    y_s5, f_s5r, f_s5i = s5_branch(z_s5, s_s5r, s_s5i, p['s5_lam_re'][l], p['s5_lam_im'][l],
                                   p['s5_log_dt'][l], p['s5_b_re'][l], p['s5_b_im'][l],
                                   p['s5_c_re'][l], p['s5_c_im'][l], p['s5_d'][l],
                                   p['s5_glu_w'][l], p['s5_glu_b'][l])
    y_hg, f_hg = hgrn2_branch(z_hg, s_hg, lower_bound, p['hg_norm_w'][l])
    y_rw, f_rw = rwkv7_branch(z_rw, s_rw, shift_fn, p['rw_mu'][l], p['rw_w0'][l], p['rw_w2'][l],
                              p['rw_a0'][l], p['rw_a2'][l], p['rw_g2'][l], p['rw_k_k'][l],
                              p['rw_k_a'][l], p['rw_r_k'][l], p['rw_ln_w'][l], p['rw_ln_b'][l])
    ys = jnp.stack([y_ret, y_s5, y_hg, y_rw], axis=2)
    branches = jnp.einsum('btmc,mcd->btmd', ys, p['w_branch'][l])
    gates = jax.nn.sigmoid(z_gate.reshape(B, T, N_BRANCH, D_MODEL))
    mixed = jnp.sum(gates * branches, axis=2) @ p['w_out'][l]
    x = x + g1 * rms_norm(mixed, p['norm_w'][l, 1])
    h2 = rms_norm(x, p['norm_w'][l, 2]) * (1.0 + sc2) + sh2
    f = jnp.square(jax.nn.relu(h2 @ p['ff_w1'][l])) @ p['ff_w2'][l]
    x = x + g2 * rms_norm(f, p['norm_w'][l, 3])
    return x, (f_ret, f_s5r, f_s5i, f_hg, f_rw)


def setup_inputs(seed: int = 0) -> dict:
    key = jax.random.key(seed)
    ks = iter(jax.random.split(key, 64))

    def nrm(shape, scale):
        return scale * jax.random.normal(next(ks), shape, jnp.float32)

    L, G, P = DEPTH, S5_GROUPS, S5_P
    return {
        'x_prompt': nrm((BATCH, SEQ, D_MODEL), 1.0),
        'x_sample': nrm((DEC_BATCH, DEC_SEQ, D_MODEL), 1.0),
        'state_ret': nrm((DEC_BATCH, L, 2, RET_HEADS, RET_DH, RET_DH), 0.5),
        'state_s5_re': nrm((DEC_BATCH, L, 2, G, P), 0.3),
        'state_s5_im': nrm((DEC_BATCH, L, 2, G, P), 0.3),
        'state_hgrn': nrm((DEC_BATCH, L, 2, HG_HEADS, HG_DH, HG_DH), 0.5),
        'state_rwkv': nrm((DEC_BATCH, L, 2, RW_HEADS, RW_DH, RW_DH), 0.5),
        'c': nrm((DEC_BATCH, D_MODEL), 1.0),
        'c_ctx': nrm((D_MODEL,), 1.0),
        'ada_w': nrm((L, D_MODEL, 6 * D_MODEL), 0.5 * D_MODEL ** -0.5),
        'ada_b': nrm((L, 6 * D_MODEL), 0.01),
        'norm_w': 1.0 + nrm((L, 4, D_MODEL), 0.05),
        'w_in': nrm((L, D_MODEL, IN_COLS), D_MODEL ** -0.5),
        'ret_gn_w': 1.0 + nrm((L, MIX_W), 0.05),
        's5_lam_re': -0.5 + nrm((L, 2, G, P), 0.01),
        's5_lam_im': math.pi * jnp.arange(P, dtype=jnp.float32) + nrm((L, 2, G, P), 0.01),
        's5_log_dt': jax.random.uniform(next(ks), (L, 2, G), jnp.float32, math.log(1e-3), math.log(1e-1)),
        's5_b_re': nrm((L, 2, G, P, S5_CH), (2 * S5_CH) ** -0.5),
        's5_b_im': nrm((L, 2, G, P, S5_CH), (2 * S5_CH) ** -0.5),
        's5_c_re': nrm((L, G, S5_CH, P), (2 * P) ** -0.5),
        's5_c_im': nrm((L, G, S5_CH, P), (2 * P) ** -0.5),
        's5_d': nrm((L, MIX_W), 1.0),
        's5_glu_w': nrm((L, MIX_W, MIX_W), MIX_W ** -0.5),
        's5_glu_b': nrm((L, MIX_W), 0.01),
        'hg_lb': nrm((L, 2, MIX_W), 1.0),
        'hg_norm_w': 1.0 + nrm((L, MIX_W), 0.05),
        'rw_mu': jax.random.uniform(next(ks), (L, RW_COLS), jnp.float32),
        'rw_w0': -2.0 + nrm((L, 2, MIX_W), 1.0),
        'rw_w2': nrm((L, 2, RW_DECAY_LORA, MIX_W), 0.1),
        'rw_a0': nrm((L, MIX_W), 0.1),
        'rw_a2': nrm((L, RW_A_LORA, MIX_W), 0.1),
        'rw_g2': nrm((L, RW_GATE_LORA, MIX_W), RW_GATE_LORA ** -0.5),
        'rw_k_k': 1.0 + nrm((L, MIX_W), 0.1),
        'rw_k_a': 1.0 + nrm((L, MIX_W), 0.1),
        'rw_r_k': nrm((L, MIX_W), 0.1),
        'rw_ln_w': 1.0 + nrm((L, MIX_W), 0.05),
        'rw_ln_b': nrm((L, MIX_W), 0.01),
        'w_branch': nrm((L, N_BRANCH, MIX_W, D_MODEL), MIX_W ** -0.5),
        'w_out': nrm((L, D_MODEL, D_MODEL), D_MODEL ** -0.5),
        'ff_w1': nrm((L, D_MODEL, D_FF), D_MODEL ** -0.5),
        'ff_w2': nrm((L, D_FF, D_MODEL), D_FF ** -0.5),
    }


def reference(x_prompt, x_sample, state_ret, state_s5_re, state_s5_im, state_hgrn, state_rwkv,
              c, c_ctx, ada_w, ada_b, norm_w, w_in, ret_gn_w, s5_lam_re, s5_lam_im, s5_log_dt,
              s5_b_re, s5_b_im, s5_c_re, s5_c_im, s5_d, s5_glu_w, s5_glu_b, hg_lb, hg_norm_w,
              rw_mu, rw_w0, rw_w2, rw_a0, rw_a2, rw_g2, rw_k_k, rw_k_a, rw_r_k, rw_ln_w, rw_ln_b,
              w_branch, w_out, ff_w1, ff_w2):
    p = dict(ada_w=ada_w, ada_b=ada_b, norm_w=norm_w, w_in=w_in, ret_gn_w=ret_gn_w,
             s5_lam_re=s5_lam_re, s5_lam_im=s5_lam_im, s5_log_dt=s5_log_dt, s5_b_re=s5_b_re,
             s5_b_im=s5_b_im, s5_c_re=s5_c_re, s5_c_im=s5_c_im, s5_d=s5_d, s5_glu_w=s5_glu_w,
             s5_glu_b=s5_glu_b, hg_norm_w=hg_norm_w, rw_mu=rw_mu, rw_w0=rw_w0, rw_w2=rw_w2,
             rw_a0=rw_a0, rw_a2=rw_a2, rw_g2=rw_g2, rw_k_k=rw_k_k, rw_k_a=rw_k_a, rw_r_k=rw_r_k,
             rw_ln_w=rw_ln_w, rw_ln_b=rw_ln_b, w_branch=w_branch, w_out=w_out,
             ff_w1=ff_w1, ff_w2=ff_w2)
    lb_cum = jnp.cumsum(jax.nn.softmax(hg_lb.astype(jnp.float32), axis=0), axis=0)
    hg_lower = lb_cum - lb_cum[0]
    n_ctx = x_prompt.shape[0]
    zero_state = (jnp.zeros((n_ctx, 2, RET_HEADS, RET_DH, RET_DH), jnp.float32),
                  jnp.zeros((n_ctx, 2, S5_GROUPS, S5_P), jnp.float32),
                  jnp.zeros((n_ctx, 2, S5_GROUPS, S5_P), jnp.float32),
                  jnp.zeros((n_ctx, 2, HG_HEADS, HG_DH, HG_DH), jnp.float32),
                  jnp.zeros((n_ctx, 2, RW_HEADS, RW_DH, RW_DH), jnp.float32))
    xp, xs = x_prompt, x_sample
    ctx_finals = []
    for l in range(DEPTH):
        xp, fin = trunk_layer(xp, c_ctx[None, :], zero_state, shift_seq, p, hg_lower[l], l)
        ctx_finals.append(fin)
        lat_init = (state_ret[:, l], state_s5_re[:, l], state_s5_im[:, l], state_hgrn[:, l], state_rwkv[:, l])
        xs, _ = trunk_layer(xs, c, lat_init, shift_grid, p, hg_lower[l], l)
    y_prompt, y_sample = xp, xs
    new_ret = jnp.stack([f[0] for f in ctx_finals], axis=1)
    new_s5_re = jnp.stack([f[1] for f in ctx_finals], axis=1)
    new_s5_im = jnp.stack([f[2] for f in ctx_finals], axis=1)
    new_hgrn = jnp.stack([f[3] for f in ctx_finals], axis=1)
    new_rwkv = jnp.stack([f[4] for f in ctx_finals], axis=1)
    return (y_prompt, y_sample, new_ret, new_s5_re, new_s5_im, new_hgrn, new_rwkv)
```

```python
import functools
import math

import jax
import jax.numpy as jnp
from jax import lax
from jax.experimental import pallas as pl
from jax.experimental.pallas import tpu as pltpu

F32 = jnp.float32
BF16 = jnp.bfloat16
HIGHEST = lax.Precision.HIGHEST

D_MODEL = 1024
GRID_W = 64
MIX_W = 256
N_HEADS = 4
HEAD_D = 64
HEAD_SHIFT = 6
S5_GROUPS = 16
S5_CH = 16
S5_P = 64
S5_STATE = S5_GROUPS * S5_P
D_FF = 4096
EPS = 1e-6
RW_LN_EPS = 64e-5
RW_COLS = 928

Z_COLS = 7680
ZB_GATE = 0
ZB_RW = 16
ZB_RET = 20
ZB_S5 = 24
ZB_HG = 25

VMEM_LIMIT = 56 * 1024 * 1024

LOG_GAMMA = tuple(
    tuple(math.log1p(-2.0 ** (-(5.0 + 0.5 * di) - h)) for h in range(N_HEADS)) for di in range(2))


def _cparams(*sem):
    return pltpu.CompilerParams(dimension_semantics=sem, vmem_limit_bytes=VMEM_LIMIT)


def _dot(a, b):
    return jnp.dot(a, b, preferred_element_type=F32)


def _dot_nt(a, b):
    return lax.dot_general(a, b, (((1,), (1,)), ((), ())), preferred_element_type=F32)


def _dot_tn(a, b):
    return lax.dot_general(a, b, (((0,), (0,)), ((), ())), preferred_element_type=F32)


def _head_ones():
    r = lax.broadcasted_iota(jnp.int32, (MIX_W, MIX_W), 0) >> HEAD_SHIFT
    c = lax.broadcasted_iota(jnp.int32, (MIX_W, MIX_W), 1) >> HEAD_SHIFT
    return jnp.where(r == c, 1.0, 0.0).astype(F32)


def _head_sum(x, ones_bd):
    return jnp.dot(x, ones_bd, precision=HIGHEST, preferred_element_type=F32)


def _rms(x, w):
    return x * lax.rsqrt(jnp.mean(x * x, axis=-1, keepdims=True) + EPS) * w


def _ada_kernel(c_ref, w_ref, b_ref, o_ref):
    c = c_ref[...]
    s = c * jax.nn.sigmoid(c)
    o_ref[...] = _dot(s.astype(BF16), w_ref[...].astype(BF16)) + b_ref[...]


def _ada_mod(cond, ada_w, ada_b):
    L = ada_w.shape[0]
    n = ada_w.shape[2]
    tn = 1536
    return pl.pallas_call(
        _ada_kernel,
        grid=(L, n // tn),
        in_specs=[
            pl.BlockSpec((8, D_MODEL), lambda l, j: (0, 0)),
            pl.BlockSpec((None, D_MODEL, tn), lambda l, j: (l, 0, j)),
            pl.BlockSpec((None, 1, tn), lambda l, j: (l, 0, j)),
        ],
        out_specs=pl.BlockSpec((None, 8, tn), lambda l, j: (l, 0, j)),
        out_shape=jax.ShapeDtypeStruct((L, 8, n), F32),
        compiler_params=_cparams("parallel", "parallel"),
        name="ada_mod",
    )(cond, ada_w, ada_b.reshape(L, 1, n))


def _proj_in_kernel(x_ref, sc_ref, sh_ref, nw_ref, w_ref, z_ref, h_ref):
    @pl.when(pl.program_id(2) == 0)
    def _():
        h = _rms(x_ref[...], nw_ref[...]) * (1.0 + sc_ref[...]) + sh_ref[...]
        h_ref[...] = h.astype(BF16)

    z_ref[...] = _dot(h_ref[...], w_ref[...])


def _mod_map(bm):
    if bm == 1:
        return lambda b, *_: (0, 0, 0)
    return lambda b, *_: (b, 0, 0)


def _proj_in(x, sc, sh, nw, w_p):
    B, T, _ = x.shape
    tm = min(T, 512)
    tn = 1920
    return pl.pallas_call(
        _proj_in_kernel,
        grid=(B, T // tm, Z_COLS // tn),
        in_specs=[
            pl.BlockSpec((None, tm, D_MODEL), lambda b, i, j: (b, i, 0)),
            pl.BlockSpec((None, 1, D_MODEL), _mod_map(sc.shape[0])),
            pl.BlockSpec((None, 1, D_MODEL), _mod_map(sh.shape[0])),
            pl.BlockSpec((1, D_MODEL), lambda b, i, j: (0, 0)),
            pl.BlockSpec((D_MODEL, tn), lambda b, i, j: (0, j)),
        ],
        out_specs=pl.BlockSpec((None, tm, tn), lambda b, i, j: (b, i, j)),
        out_shape=jax.ShapeDtypeStruct((B, T, Z_COLS), F32),
        scratch_shapes=[pltpu.VMEM((tm, D_MODEL), BF16)],
        compiler_params=_cparams("parallel", "parallel", "arbitrary"),
        name="proj_in",
    )(x, sc, sh, nw, w_p)


def _ret_kernel(q_ref, k_ref, v_ref, s0_ref, o_ref, fin_ref, s_ref, *, lc, nc):
    d = pl.program_id(0)
    c = pl.program_id(2)

    @pl.when(c == 0)
    def _():
        s_ref[...] = s0_ref[...]

    rows = lax.broadcasted_iota(jnp.int32, (lc, lc), 0)
    cols = lax.broadcasted_iota(jnp.int32, (lc, lc), 1)
    rel = jnp.where(d == 0, rows - cols, cols - rows).astype(F32)
    idx = lax.broadcasted_iota(jnp.int32, (lc, 1), 0)
    pos = jnp.where(d == 0, idx, lc - 1 - idx).astype(F32)
    for h in range(N_HEADS):
        lg = jnp.where(d == 0, LOG_GAMMA[0][h], LOG_GAMMA[1][h]).astype(F32)
        sl = slice(h * HEAD_D, (h + 1) * HEAD_D)
        q = q_ref[:, sl]
        k = k_ref[:, sl] * (HEAD_D ** -0.5)
        v = v_ref[:, sl].astype(BF16)
        att = _dot_nt(q.astype(BF16), k.astype(BF16))
        att = att * jnp.where(rel >= 0.0, jnp.exp(jnp.maximum(rel, 0.0) * lg), 0.0)
        qd = q * jnp.exp((pos + 1.0) * lg)
        kd = k * jnp.exp((lc - 1.0 - pos) * lg)
        s = s_ref[h]
        o_ref[:, sl] = _dot(att.astype(BF16), v) + _dot(qd.astype(BF16), s.astype(BF16))
        s_ref[h] = s * jnp.exp(lc * lg) + _dot_tn(kd.astype(BF16), v)

    @pl.when(c == nc - 1)
    def _():
        fin_ref[...] = s_ref[...]


def _retention(z, s0):
    B, T, _ = z.shape
    lc = min(T, 256)
    nc = T // lc

    def tmap(d, c):
        return jnp.where(d == 0, c, nc - 1 - c)

    def zspec(blk):
        return pl.BlockSpec((None, lc, MIX_W), lambda d, b, c: (b, tmap(d, c), blk))

    return pl.pallas_call(
        functools.partial(_ret_kernel, lc=lc, nc=nc),
        grid=(2, B, nc),
        in_specs=[
            zspec(ZB_RET), zspec(ZB_RET + 1), zspec(ZB_RET + 2),
            pl.BlockSpec((None, None, N_HEADS, HEAD_D, HEAD_D), lambda d, b, c: (b, d, 0, 0, 0)),
        ],
        out_specs=[
            pl.BlockSpec((None, None, lc, MIX_W), lambda d, b, c: (d, b, tmap(d, c), 0)),
            pl.BlockSpec((None, None, N_HEADS, HEAD_D, HEAD_D), lambda d, b, c: (b, d, 0, 0, 0)),
        ],
        out_shape=[
            jax.ShapeDtypeStruct((2, B, T, MIX_W), F32),
            jax.ShapeDtypeStruct((B, 2, N_HEADS, HEAD_D, HEAD_D), F32),
        ],
        scratch_shapes=[pltpu.VMEM((N_HEADS, HEAD_D, HEAD_D), F32)],
        compiler_params=_cparams("arbitrary", "arbitrary", "arbitrary"),
        name="retention",
    )(z, z, z, s0)


def _s5_zoh_kernel(lre_ref, lim_ref, ldt_ref, bre_ref, bim_ref, are_ref, aim_ref, bbre_ref, bbim_ref):
    lam_re = jnp.minimum(lre_ref[...], -1e-4)
    lam_im = lim_ref[...]
    dt = jnp.exp(ldt_ref[...])
    mag = jnp.exp(dt * lam_re)
    ang = dt * lam_im
    a_re = mag * jnp.cos(ang)
    a_im = mag * jnp.sin(ang)
    den = lam_re * lam_re + lam_im * lam_im
    f_re = ((a_re - 1.0) * lam_re + a_im * lam_im) / den
    f_im = (a_im * lam_re - (a_re - 1.0) * lam_im) / den
    b_re = bre_ref[...]
    b_im = bim_ref[...]
    are_ref[...] = a_re
    aim_ref[...] = a_im
    bbre_ref[...] = f_re * b_re - f_im * b_im
    bbim_ref[...] = f_re * b_im + f_im * b_re


def _s5_zoh(lam_re, lam_im, log_dt, b_re, b_im):
    n = 2 * S5_STATE
    col = lambda t: t.reshape(n, 1)
    ldt = jnp.broadcast_to(log_dt[:, :, None], (2, S5_GROUPS, S5_P))
    outs = pl.pallas_call(
        _s5_zoh_kernel,
        out_shape=[jax.ShapeDtypeStruct((n, 1), F32), jax.ShapeDtypeStruct((n, 1), F32),
                   jax.ShapeDtypeStruct((n, S5_CH), F32), jax.ShapeDtypeStruct((n, S5_CH), F32)],
        name="s5_zoh",
    )(col(lam_re), col(lam_im), col(ldt), b_re.reshape(n, S5_CH), b_im.reshape(n, S5_CH))
    a_re, a_im, bb_re, bb_im = outs
    shp = (2, S5_GROUPS, S5_P, S5_CH)
    return a_re.reshape(2, 1, S5_STATE), a_im.reshape(2, 1, S5_STATE), bb_re.reshape(shp), bb_im.reshape(shp)


def _s5_kernel(u_ref, are_ref, aim_ref, bre_ref, bim_ref, cre_ref, cim_ref, x0r_ref, x0i_ref,
               y_ref, fr_ref, fi_ref,
               xa_re, xa_im, xb_re, xb_im, p_re, p_im, car_re, car_im, *, L, nc):
    d = pl.program_id(0)
    c = pl.program_id(2)
    pad = L
    nsteps = L.bit_length() - 1
    rows = lax.broadcasted_iota(jnp.int32, (L, 1), 0)

    def scan(bu_re, bu_im, rev):
        bufs = ((xa_re, xa_im), (xb_re, xb_im))
        bufs[0][0][pad:pad + L, :] = bu_re
        bufs[0][1][pad:pad + L, :] = bu_im
        ar = are_ref[...]
        ai = aim_ref[...]
        for ks in range(nsteps):
            src, dst = bufs[ks % 2], bufs[(ks + 1) % 2]
            off = pad + (1 << ks) if rev else pad - (1 << ks)
            sre = src[0][off:off + L, :]
            sim = src[1][off:off + L, :]
            re = src[0][pad:pad + L, :]
            im = src[1][pad:pad + L, :]
            dst[0][pad:pad + L, :] = re + ar * sre - ai * sim
            dst[1][pad:pad + L, :] = im + ar * sim + ai * sre
            ar, ai = ar * ar - ai * ai, 2.0 * ar * ai
        fin = bufs[nsteps % 2]
        return fin[0][pad:pad + L, :], fin[1][pad:pad + L, :]

    def body(rev):
        @pl.when(c == 0)
        def _():
            zeros = jnp.zeros((pad, S5_STATE), F32)
            for buf in (xa_re, xa_im, xb_re, xb_im):
                buf[0:pad, :] = zeros
                buf[pad + L:2 * pad + L, :] = zeros
            first = rows == (L - 1 if rev else 0)
            pr, pi = scan(jnp.where(first, are_ref[...], 0.0), jnp.where(first, aim_ref[...], 0.0), rev)
            p_re[...] = pr
            p_im[...] = pi
            car_re[...] = x0r_ref[...]
            car_im[...] = x0i_ref[...]

        u = u_ref[...].astype(BF16)
        xr, xi = scan(_dot(u, bre_ref[...]), _dot(u, bim_ref[...]), rev)
        cr = car_re[...]
        ci = car_im[...]
        pr = p_re[...]
        pi = p_im[...]
        xr, xi = xr + pr * cr - pi * ci, xi + pr * ci + pi * cr
        y_ref[...] = _dot(xr.astype(BF16), cre_ref[...]) - _dot(xi.astype(BF16), cim_ref[...])
        last = 0 if rev else L - 1
        car_re[...] = xr[last:last + 1, :]
        car_im[...] = xi[last:last + 1, :]

    @pl.when(d == 0)
    def _():
        body(False)

    @pl.when(d == 1)
    def _():
        body(True)

    @pl.when(c == nc - 1)
    def _():
        fr_ref[...] = car_re[...]
        fi_ref[...] = car_im[...]


def _s5(z, a_re, a_im, bblk_re, bblk_im, cblk_re, cblk_im, x0_re, x0_im):
    B, T, _ = z.shape
    L = min(T, 128)
    nc = T // L

    def tmap(d, c):
        return jnp.where(d == 0, c, nc - 1 - c)

    dspec = lambda shape: pl.BlockSpec((None,) + shape, lambda d, b, c: (d, 0, 0))
    sspec = pl.BlockSpec((None, None, 1, S5_STATE), lambda d, b, c: (b, d, 0, 0))
    buf = pltpu.VMEM((3 * L, S5_STATE), F32)
    return pl.pallas_call(
        functools.partial(_s5_kernel, L=L, nc=nc),
        grid=(2, B, nc),
        in_specs=[
            pl.BlockSpec((None, L, MIX_W), lambda d, b, c: (b, tmap(d, c), ZB_S5)),
            dspec((1, S5_STATE)), dspec((1, S5_STATE)),
            dspec((MIX_W, S5_STATE)), dspec((MIX_W, S5_STATE)),
            pl.BlockSpec((S5_STATE, MIX_W), lambda d, b, c: (0, 0)),
            pl.BlockSpec((S5_STATE, MIX_W), lambda d, b, c: (0, 0)),
            sspec, sspec,
        ],
        out_specs=[
            pl.BlockSpec((None, None, L, MIX_W), lambda d, b, c: (d, b, tmap(d, c), 0)),
            sspec, sspec,
        ],
        out_shape=[
            jax.ShapeDtypeStruct((2, B, T, MIX_W), F32),
            jax.ShapeDtypeStruct((B, 2, 1, S5_STATE), F32),
            jax.ShapeDtypeStruct((B, 2, 1, S5_STATE), F32),
        ],
        scratch_shapes=[buf, buf, buf, buf,
                        pltpu.VMEM((L, S5_STATE), F32), pltpu.VMEM((L, S5_STATE), F32),
                        pltpu.VMEM((1, S5_STATE), F32), pltpu.VMEM((1, S5_STATE), F32)],
        compiler_params=_cparams("arbitrary", "arbitrary", "arbitrary"),
        name="s5_scan",
    )(z, a_re, a_im, bblk_re, bblk_im, cblk_re, cblk_im, x0_re, x0_im)


def _hg_prep_kernel(q_ref, zf_ref, zb_ref, lb_ref, qo_ref, f_ref, key_ref):
    q = q_ref[...]
    qo_ref[...] = q * jax.nn.sigmoid(q)
    for di, z_ref in enumerate((zf_ref, zb_ref)):
        lb = lb_ref[di:di + 1, :]
        zf = z_ref[...]
        f_ref[di] = lb + (1.0 - lb) * jax.nn.sigmoid(zf)
        key_ref[di] = (1.0 - lb) * jax.nn.sigmoid(-zf)


def _hg_prep(z, lb):
    B, T, _ = z.shape
    tm = min(T, 512)
    zspec = lambda blk: pl.BlockSpec((None, tm, MIX_W), lambda b, i: (b, i, blk))
    dspec = pl.BlockSpec((2, None, tm, MIX_W), lambda b, i: (0, b, i, 0))
    return pl.pallas_call(
        _hg_prep_kernel,
        grid=(B, T // tm),
        in_specs=[zspec(ZB_HG), zspec(ZB_HG + 1), zspec(ZB_HG + 2),
                  pl.BlockSpec((2, MIX_W), lambda b, i: (0, 0))],
        out_specs=[pl.BlockSpec((None, tm, MIX_W), lambda b, i: (b, i, 0)), dspec, dspec],
        out_shape=[jax.ShapeDtypeStruct((B, T, MIX_W), F32),
                   jax.ShapeDtypeStruct((2, B, T, MIX_W), F32),
                   jax.ShapeDtypeStruct((2, B, T, MIX_W), F32)],
        compiler_params=_cparams("parallel", "parallel"),
        name="hgrn_prep",
    )(z, z, z, lb)


def _rw_prep_kernel(*refs, grid_shift, tm):
    if grid_shift:
        (zc_ref, zu_ref, zd_ref, mu_ref, vec_ref, w0_ref, lora_ref,
         r_ref, k2_ref, v_ref, w_ref, nkk_ref, kka_ref, g_ref, bonus_ref, buf_ref) = refs
    else:
        (zc_ref, mu_ref, vec_ref, w0_ref, lora_ref,
         r_ref, k2_ref, v_ref, w_ref, nkk_ref, kka_ref, g_ref, bonus_ref, buf_ref) = refs
    i = pl.program_id(1)
    nt = pl.num_programs(1)
    halo = GRID_W
    width = 4 * MIX_W
    z = zc_ref[...]
    buf_ref[halo:halo + tm, :] = z
    lane = lax.broadcasted_iota(jnp.int32, (tm, width), 1)
    row = lax.broadcasted_iota(jnp.int32, (tm, width), 0)
    if grid_shift:
        buf_ref[0:halo, :] = jnp.where(i > 0, zu_ref[...], 0.0)
        buf_ref[halo + tm:2 * halo + tm, :] = jnp.where(i < nt - 1, zd_ref[...], 0.0)
        col = row & (GRID_W - 1)
        left = jnp.where(col > 0, buf_ref[halo - 1:halo - 1 + tm, :], 0.0)
        right = jnp.where(col < GRID_W - 1, buf_ref[halo + 1:halo + 1 + tm, :], 0.0)
        up = buf_ref[0:tm, :]
        down = buf_ref[2 * halo:2 * halo + tm, :]
        sel = lane & 3
        shifted = jnp.where(sel == 0, left, jnp.where(sel == 1, right, jnp.where(sel == 2, up, down)))
    else:
        zrow = jnp.zeros((1, width), F32)
        buf_ref[halo - 1:halo, :] = zrow
        buf_ref[halo + tm:halo + tm + 1, :] = zrow
        prev = buf_ref[halo - 1:halo - 1 + tm, :]
        nxt = buf_ref[halo + 1:halo + 1 + tm, :]
        shifted = jnp.where((lane & 1) == 0, prev, nxt)
    zs = z + mu_ref[...] * (shifted - z)
    r = zs[:, 0:MIX_W]
    k = zs[:, MIX_W:2 * MIX_W]
    v = zs[:, 2 * MIX_W:3 * MIX_W]
    sm = zs[:, 3 * MIX_W:4 * MIX_W]
    ones_bd = _head_ones()
    a0, k_k, k_a, r_k = (vec_ref[j:j + 1, :] for j in range(4))
    a = jax.nn.sigmoid(a0 + _dot(sm.astype(BF16), lora_ref[2]))
    g_ref[...] = _dot(jax.nn.sigmoid(sm).astype(BF16), lora_ref[3])
    kk = k * k_k
    kk = kk * lax.rsqrt(_head_sum(kk * kk, ones_bd) + 1e-12)
    k2 = k * (1.0 + (a - 1.0) * k_a)
    th = jnp.tanh(sm).astype(BF16)
    for di in range(2):
        log_w = -math.exp(-0.5) * jax.nn.sigmoid(w0_ref[di:di + 1, :] + _dot(th, lora_ref[di]))
        w_ref[di] = jnp.exp(log_w)
    r_ref[...] = r
    k2_ref[...] = k2
    v_ref[...] = v
    nkk_ref[...] = -kk
    kka_ref[...] = kk * a
    bonus_ref[...] = _head_sum(r * k2 * r_k, ones_bd) * v


def _rw_prep(z, mu_p, vecs, w0, lora, grid_shift):
    B, T, _ = z.shape
    width = 4 * MIX_W
    wblk = ZB_RW // 4
    if grid_shift:
        tm = min(T, 512)
        hb = tm // GRID_W
        nh = T // GRID_W
        z_specs = [
            pl.BlockSpec((None, tm, width), lambda b, i: (b, i, wblk)),
            pl.BlockSpec((None, GRID_W, width), lambda b, i: (b, jnp.maximum(i * hb - 1, 0), wblk)),
            pl.BlockSpec((None, GRID_W, width), lambda b, i: (b, jnp.minimum((i + 1) * hb, nh - 1), wblk)),
        ]
        z_args = (z, z, z)
    else:
        tm = T
        z_specs = [pl.BlockSpec((None, tm, width), lambda b, i: (b, i, wblk))]
        z_args = (z,)
    const = lambda shape: pl.BlockSpec(shape, lambda b, i: (0,) * len(shape))
    ospec = pl.BlockSpec((None, tm, MIX_W), lambda b, i: (b, i, 0))
    oshape = jax.ShapeDtypeStruct((B, T, MIX_W), F32)
    dspec = pl.BlockSpec((2, None, tm, MIX_W), lambda b, i: (0, b, i, 0))
    dshape = jax.ShapeDtypeStruct((2, B, T, MIX_W), F32)
    return pl.pallas_call(
        functools.partial(_rw_prep_kernel, grid_shift=grid_shift, tm=tm),
        grid=(B, T // tm),
        in_specs=z_specs + [const((1, width)), const((4, MIX_W)), const((2, MIX_W)),
                            const((4, MIX_W, MIX_W))],
        out_specs=[ospec, ospec, ospec, dspec, ospec, ospec, ospec, ospec],
        out_shape=[oshape, oshape, oshape, dshape, oshape, oshape, oshape, oshape],
        scratch_shapes=[pltpu.VMEM((tm + 2 * GRID_W, width), F32)],
        compiler_params=_cparams("parallel", "parallel"),
        name="rwkv_prep",
    )(*z_args, mu_p, vecs, w0, lora)


def _seq_kernel(*refs, use_sa, rev, nb, ub, tb, nt):
    if use_sa:
        w_ref, k_ref, r_ref, v_ref, a_ref, b_ref, s0_ref, y_ref, fin_ref, s_ref = refs
    else:
        w_ref, k_ref, r_ref, v_ref, s0_ref, y_ref, fin_ref, s_ref = refs
    j = pl.program_id(0)
    pair_w = 2 * HEAD_D
    group = 8

    @pl.when(j == 0)
    def _():
        s_ref[...] = s0_ref[...]

    lane = lax.broadcasted_iota(jnp.int32, (HEAD_D, pair_w), 1)
    sub = lax.broadcasted_iota(jnp.int32, (HEAD_D, pair_w), 0)
    lo = lane < HEAD_D
    eye2 = jnp.where((lane & (HEAD_D - 1)) == sub, 1.0, 0.0).astype(F32)

    def seg_sum(p):
        s_lo = jnp.sum(jnp.where(lo, p, 0.0), axis=1, keepdims=True)
        s_hi = jnp.sum(jnp.where(lo, 0.0, p), axis=1, keepdims=True)
        return jnp.where(lo, s_lo, s_hi)

    def chain(b, p, t0):
        sl = pl.ds(p * pair_w, pair_w)
        blk = lambda ref: ref[b, pl.ds(t0, group), sl]
        w, k, r, v = blk(w_ref), blk(k_ref), blk(r_ref), blk(v_ref)
        if use_sa:
            a, bb = blk(a_ref), blk(b_ref)
        s = s_ref[b, p]
        ys = [None] * group
        for i in range(group):
            t = group - 1 - i if rev else i
            row = lambda x: x[t:t + 1, :]
            if use_sa:
                s = s * row(w) + seg_sum(s * row(a)) * row(bb)
            else:
                s = s * row(w)
            s = s + seg_sum(eye2 * row(v)) * row(k)
            ys[t] = jnp.sum(eye2 * seg_sum(s * row(r)), axis=0, keepdims=True)
        s_ref[b, p] = s
        y_ref[b, pl.ds(t0, group), sl] = jnp.concatenate(ys, axis=0)

    def group_step(g, carry):
        gg = tb // group - 1 - g if rev else g
        t0 = pl.multiple_of(gg * group, group)

        def batch_step(bg, c):
            for u in range(ub):
                for p in range(2):
                    chain(bg * ub + u, p, t0)
            return c

        return lax.fori_loop(0, nb // ub, batch_step, carry)

    lax.fori_loop(0, tb // group, group_step, 0)

    @pl.when(j == nt - 1)
    def _():
        fin_ref[...] = s_ref[...]


def _seq_scan(w, k, r, v, a, b, s0, rev):
    use_sa = a is not None
    B, T = v[0].shape[-3], v[0].shape[-2]
    tb = min(T, 128)
    nt = T // tb
    ub = 2 if B % 2 == 0 else 1
    tmap = (lambda j: nt - 1 - j) if rev else (lambda j: j)

    def spec(arr, lead, blk):
        if lead is not None:
            return pl.BlockSpec((None, B, tb, MIX_W), lambda j: (lead, 0, tmap(j), blk))
        return pl.BlockSpec((B, tb, MIX_W), lambda j: (0, tmap(j), blk))

    ins = [w, k, r, v] + ([a, b] if use_sa else [])
    sspec = pl.BlockSpec((B, 2, HEAD_D, 2 * HEAD_D), lambda j: (0, 0, 0, 0))
    return pl.pallas_call(
        functools.partial(_seq_kernel, use_sa=use_sa, rev=rev, nb=B, ub=ub, tb=tb, nt=nt),
        grid=(nt,),
        in_specs=[spec(*x) for x in ins] + [sspec],
        out_specs=[pl.BlockSpec((B, tb, MIX_W), lambda j: (0, tmap(j), 0)), sspec],
        out_shape=[jax.ShapeDtypeStruct((B, T, MIX_W), F32),
                   jax.ShapeDtypeStruct((B, 2, HEAD_D, 2 * HEAD_D), F32)],
        scratch_shapes=[pltpu.VMEM((B, 2, HEAD_D, 2 * HEAD_D), F32)],
        compiler_params=_cparams("arbitrary"),
        name=("rwkv_scan" if use_sa else "hgrn_scan") + ("_bwd" if rev else "_fwd"),
    )(*[x[0] for x in ins], s0)


def _seq_scan_bidir(w, k, r, v, a, b, s0):
    ys, fins = [], []
    for di in range(2):
        pick = lambda x: None if x is None else (x[0], di if x[0].ndim == 4 else None, x[1])
        y, fin = _seq_scan(pick(w), pick(k), pick(r), pick(v), pick(a), pick(b), s0[:, di], di == 1)
        ys.append(y)
        fins.append(fin)
    return jnp.stack(ys), jnp.stack(fins, axis=1)


def _pair_tiles(s):
    B = s.shape[0]
    s = s.reshape(B, 2, 2, 2, HEAD_D, HEAD_D).transpose(0, 1, 2, 4, 3, 5)
    return s.reshape(B, 2, 2, HEAD_D, 2 * HEAD_D)


def _unpair_tiles(s):
    B = s.shape[0]
    s = s.reshape(B, 2, 2, HEAD_D, 2, HEAD_D).transpose(0, 1, 2, 4, 3, 5)
    return s.reshape(B, 2, N_HEADS, HEAD_D, HEAD_D)


def _mix_kernel(x_ref, g1_ref, nw_ref, zg_ref,
                ret_ref, retg_ref, s5_ref, s5u_ref, hg_ref, hgg_ref, rw_ref, rwb_ref, rwg_ref,
                vec_ref, glu_w_ref, wbr_ref, wout_ref, o_ref):
    ones_bd = _head_ones()
    gn_w, s5_d, glu_b, hg_w, ln_w, ln_b = (vec_ref[j:j + 1, :] for j in range(6))

    def group_norm(o, eps):
        mu = _head_sum(o, ones_bd) * (1.0 / HEAD_D)
        oc = o - mu
        var = _head_sum(oc * oc, ones_bd) * (1.0 / HEAD_D)
        return oc * lax.rsqrt(var + eps)

    g = retg_ref[...]
    y_ret = group_norm(ret_ref[0] + ret_ref[1], EPS) * gn_w * (g * jax.nn.sigmoid(g))

    y = s5_d * s5u_ref[...] + s5_ref[0] + s5_ref[1]
    yg = jax.nn.gelu(y)
    y_s5 = yg * jax.nn.sigmoid(_dot(yg.astype(BF16), glu_w_ref[...]) + glu_b)

    o = hg_ref[0] + hg_ref[1]
    g = hgg_ref[...]
    ms = _head_sum(o * o, ones_bd) * (1.0 / HEAD_D)
    y_hg = o * lax.rsqrt(ms + EPS) * hg_w * (g * jax.nn.sigmoid(g))

    y = group_norm(rw_ref[0] + rw_ref[1], RW_LN_EPS) * ln_w + ln_b
    y_rw = (y + rwb_ref[...]) * rwg_ref[...]

    mixed = None
    for m, ym in enumerate((y_ret, y_s5, y_hg, y_rw)):
        br = _dot(ym.astype(BF16), wbr_ref[m])
        term = jax.nn.sigmoid(zg_ref[:, m * D_MODEL:(m + 1) * D_MODEL]) * br
        mixed = term if mixed is None else mixed + term
    mixed = _dot(mixed.astype(BF16), wout_ref[...])
    o_ref[...] = x_ref[...] + g1_ref[...] * _rms(mixed, nw_ref[...])


def _mix(x, g1, nw, z, ret_o, s5_y, hg_o, rw_y, rw_bonus, rw_g, vecs, glu_w, w_branch, w_out):
    B, T, _ = x.shape
    tm = min(T, 256)
    xspec = pl.BlockSpec((None, tm, D_MODEL), lambda b, i: (b, i, 0))
    zspec = lambda blk: pl.BlockSpec((None, tm, MIX_W), lambda b, i: (b, i, blk))
    dspec = pl.BlockSpec((2, None, tm, MIX_W), lambda b, i: (0, b, i, 0))
    const = lambda shape: pl.BlockSpec(shape, lambda b, i: (0,) * len(shape))
    return pl.pallas_call(
        _mix_kernel,
        grid=(B, T // tm),
        in_specs=[
            xspec,
            pl.BlockSpec((None, 1, D_MODEL), _mod_map(g1.shape[0])),
            const((1, D_MODEL)),
            pl.BlockSpec((None, tm, 4 * D_MODEL), lambda b, i: (b, i, 0)),
            dspec, zspec(ZB_RET + 3),
            dspec, zspec(ZB_S5),
            dspec, zspec(ZB_HG + 4),
            dspec, zspec(0), zspec(0),
            const((6, MIX_W)), const((MIX_W, MIX_W)),
            const((4, MIX_W, D_MODEL)), const((D_MODEL, D_MODEL)),
        ],
        out_specs=xspec,
        out_shape=jax.ShapeDtypeStruct((B, T, D_MODEL), F32),
        compiler_params=_cparams("parallel", "parallel"),
        name="mix_out",
    )(x, g1, nw, z, ret_o, z, s5_y, z, hg_o, z, rw_y, rw_bonus, rw_g, vecs, glu_w, w_branch, w_out)


def _ffn_kernel(x_ref, sc_ref, sh_ref, g2_ref, nw2_ref, nw3_ref, w1_ref, w2_ref, o_ref, h_ref, acc_ref):
    j = pl.program_id(2)

    @pl.when(j == 0)
    def _():
        h = _rms(x_ref[...], nw2_ref[...]) * (1.0 + sc_ref[...]) + sh_ref[...]
        h_ref[...] = h.astype(BF16)
        acc_ref[...] = jnp.zeros_like(acc_ref)

    a = jnp.maximum(_dot(h_ref[...], w1_ref[...]), 0.0)
    acc_ref[...] += _dot((a * a).astype(BF16), w2_ref[...])

    @pl.when(j == pl.num_programs(2) - 1)
    def _():
        o_ref[...] = x_ref[...] + g2_ref[...] * _rms(acc_ref[...], nw3_ref[...])


def _ffn(x, sc, sh, g2, nw2, nw3, w1, w2):
    B, T, _ = x.shape
    tm = min(T, 512)
    tf = 1024
    xspec = pl.BlockSpec((None, tm, D_MODEL), lambda b, i, j: (b, i, 0))
    mspec = lambda m: pl.BlockSpec((None, 1, D_MODEL), _mod_map(m.shape[0]))
    const = pl.BlockSpec((1, D_MODEL), lambda b, i, j: (0, 0))
    return pl.pallas_call(
        _ffn_kernel,
        grid=(B, T // tm, D_FF // tf),
        in_specs=[xspec, mspec(sc), mspec(sh), mspec(g2), const, const,
                  pl.BlockSpec((D_MODEL, tf), lambda b, i, j: (0, j)),
                  pl.BlockSpec((tf, D_MODEL), lambda b, i, j: (j, 0))],
        out_specs=xspec,
        out_shape=jax.ShapeDtypeStruct((B, T, D_MODEL), F32),
        scratch_shapes=[pltpu.VMEM((tm, D_MODEL), BF16), pltpu.VMEM((tm, D_MODEL), F32)],
        compiler_params=_cparams("parallel", "parallel", "arbitrary"),
        name="ffn",
    )(x, sc, sh, g2, nw2, nw3, w1, w2)


def _layer_params(l, p):
    w = p['w_in'][l]
    w_p = jnp.concatenate(
        [w[:, 3488:7584], w[:, 2560:3488], jnp.zeros((D_MODEL, Z_COLS - 7584), F32),
         w[:, 0:1024], w[:, 1024:1280], w[:, 1280:2560]], axis=1).astype(BF16)

    a_re, a_im, bb_re, bb_im = _s5_zoh(p['s5_lam_re'][l], p['s5_lam_im'][l], p['s5_log_dt'][l],
                                       p['s5_b_re'][l], p['s5_b_im'][l])
    eye = jnp.eye(S5_GROUPS, dtype=F32)
    bblk = lambda bb: jnp.einsum('dgph,gk->dghkp', bb, eye).reshape(2, MIX_W, S5_STATE).astype(BF16)
    cblk = lambda c: jnp.einsum('ghp,gk->gpkh', c, eye).reshape(S5_STATE, MIX_W).astype(BF16)

    def lora_pad(m, row0):
        return jnp.zeros((MIX_W, MIX_W), F32).at[row0:row0 + m.shape[0]].set(m)

    lora = jnp.stack([lora_pad(p['rw_w2'][l, 0], 0), lora_pad(p['rw_w2'][l, 1], 32),
                      lora_pad(p['rw_a2'][l], 64), lora_pad(p['rw_g2'][l], 96)]).astype(BF16)
    mu_p = jnp.concatenate([p['rw_mu'][l], jnp.zeros((4 * MIX_W - RW_COLS,), F32)]).reshape(1, 4 * MIX_W)
    return dict(
        w_p=w_p, nw=p['norm_w'][l],
        s5=(a_re, a_im, bblk(bb_re), bblk(bb_im), cblk(p['s5_c_re'][l]), cblk(p['s5_c_im'][l])),
        rw_mu=mu_p, rw_lora=lora, rw_w0=p['rw_w0'][l],
        rw_vecs=jnp.stack([p['rw_a0'][l], p['rw_k_k'][l], p['rw_k_a'][l], p['rw_r_k'][l]]),
        mix_vecs=jnp.stack([p['ret_gn_w'][l], p['s5_d'][l], p['s5_glu_b'][l], p['hg_norm_w'][l],
                            p['rw_ln_w'][l], p['rw_ln_b'][l]]),
        glu_w=p['s5_glu_w'][l].astype(BF16),
        w_branch=p['w_branch'][l].astype(BF16), w_out=p['w_out'][l].astype(BF16),
        ff_w1=p['ff_w1'][l].astype(BF16), ff_w2=p['ff_w2'][l].astype(BF16),
    )


def _trunk_layer(x, mod, init, grid_shift, lp, hg_lb):
    B = x.shape[0]
    sh1, sc1, g1, sh2, sc2, g2 = (m[:, None, :] for m in jnp.split(mod, 6, axis=-1))
    nw = lp['nw']
    z = _proj_in(x, sc1, sh1, nw[0:1], lp['w_p'])

    s_ret, s_s5r, s_s5i, s_hg, s_rw = init
    ret_o, f_ret = _retention(z, s_ret)

    s5_y, f_s5r, f_s5i = _s5(z, *lp['s5'], s_s5r.reshape(B, 2, 1, S5_STATE), s_s5i.reshape(B, 2, 1, S5_STATE))
    f_s5r = f_s5r.reshape(B, 2, S5_GROUPS, S5_P)
    f_s5i = f_s5i.reshape(B, 2, S5_GROUPS, S5_P)

    hq, hf, hkey = _hg_prep(z, hg_lb)
    hg_o, f_hg = _seq_scan_bidir((hf, 0), (hkey, 0), (hq, 0), (z, ZB_HG + 3), None, None,
                                 _pair_tiles(jnp.swapaxes(s_hg, -1, -2)))
    f_hg = jnp.swapaxes(_unpair_tiles(f_hg), -1, -2)

    r, k2, v, w, nkk, kka, rw_g, bonus = _rw_prep(z, lp['rw_mu'], lp['rw_vecs'], lp['rw_w0'],
                                                  lp['rw_lora'], grid_shift)
    rw_y, f_rw = _seq_scan_bidir((w, 0), (k2, 0), (r, 0), (v, 0), (nkk, 0), (kka, 0), _pair_tiles(s_rw))
    f_rw = _unpair_tiles(f_rw)

    x = _mix(x, g1, nw[1:2], z, ret_o, s5_y, hg_o, rw_y, bonus, rw_g, lp['mix_vecs'], lp['glu_w'],
             lp['w_branch'], lp['w_out'])
    x = _ffn(x, sc2, sh2, g2, nw[2:3], nw[3:4], lp['ff_w1'], lp['ff_w2'])
    return x, (f_ret, f_s5r, f_s5i, f_hg, f_rw)


def kernel(x_prompt, x_sample, state_ret, state_s5_re, state_s5_im, state_hgrn, state_rwkv, c, c_ctx, ada_w, ada_b, norm_w, w_in, ret_gn_w, s5_lam_re, s5_lam_im, s5_log_dt, s5_b_re, s5_b_im, s5_c_re, s5_c_im, s5_d, s5_glu_w, s5_glu_b, hg_lb, hg_norm_w, rw_mu, rw_w0, rw_w2, rw_a0, rw_a2, rw_g2, rw_k_k, rw_k_a, rw_r_k, rw_ln_w, rw_ln_b, w_branch, w_out, ff_w1, ff_w2):
    p = dict(norm_w=norm_w, w_in=w_in, ret_gn_w=ret_gn_w, s5_lam_re=s5_lam_re, s5_lam_im=s5_lam_im,
             s5_log_dt=s5_log_dt, s5_b_re=s5_b_re, s5_b_im=s5_b_im, s5_c_re=s5_c_re, s5_c_im=s5_c_im,
             s5_d=s5_d, s5_glu_w=s5_glu_w, s5_glu_b=s5_glu_b, hg_norm_w=hg_norm_w, rw_mu=rw_mu,
             rw_w0=rw_w0, rw_w2=rw_w2, rw_a0=rw_a0, rw_a2=rw_a2, rw_g2=rw_g2, rw_k_k=rw_k_k,
             rw_k_a=rw_k_a, rw_r_k=rw_r_k, rw_ln_w=rw_ln_w, rw_ln_b=rw_ln_b, w_branch=w_branch,
             w_out=w_out, ff_w1=ff_w1, ff_w2=ff_w2)
    depth = w_in.shape[0]
    n_ctx = x_prompt.shape[0]
    n_lat = x_sample.shape[0]
    assert 1 + n_lat <= 8

    lb_cum = jnp.cumsum(jax.nn.softmax(hg_lb.astype(F32), axis=0), axis=0)
    hg_lower = lb_cum - lb_cum[0]

    cond = jnp.concatenate([c_ctx[None, :], c, jnp.zeros((7 - n_lat, D_MODEL), F32)], axis=0)
    mod = _ada_mod(cond, ada_w, ada_b)

    zero_state = (jnp.zeros((n_ctx, 2, N_HEADS, HEAD_D, HEAD_D), F32),
                  jnp.zeros((n_ctx, 2, S5_GROUPS, S5_P), F32),
                  jnp.zeros((n_ctx, 2, S5_GROUPS, S5_P), F32),
                  jnp.zeros((n_ctx, 2, N_HEADS, HEAD_D, HEAD_D), F32),
                  jnp.zeros((n_ctx, 2, N_HEADS, HEAD_D, HEAD_D), F32))
    xp, xs = x_prompt, x_sample
    finals = []
    for l in range(depth):
        lp = _layer_params(l, p)
        xp, fin = _trunk_layer(xp, mod[l, 0:1], zero_state, False, lp, hg_lower[l])
        finals.append(fin)
        lat_init = (state_ret[:, l], state_s5_re[:, l], state_s5_im[:, l], state_hgrn[:, l], state_rwkv[:, l])
        xs, _ = _trunk_layer(xs, mod[l, 1:1 + n_lat], lat_init, True, lp, hg_lower[l])
    new_states = tuple(jnp.stack([f[i] for f in finals], axis=1) for i in range(5))
    return (xp, xs) + new_states
```

```python
import functools
import math

import jax
import jax.numpy as jnp
from jax import lax
from jax.experimental import pallas as pl
from jax.experimental.pallas import tpu as pltpu

F32 = jnp.float32
BF16 = jnp.bfloat16
HIGHEST = lax.Precision.HIGHEST

D_MODEL = 1024
GRID_W = 64
MIX_W = 256
N_HEADS = 4
HEAD_D = 64
HEAD_SHIFT = 6
S5_GROUPS = 16
S5_CH = 16
S5_P = 64
S5_STATE = S5_GROUPS * S5_P
D_FF = 4096
EPS = 1e-6
RW_LN_EPS = 64e-5
RW_COLS = 928

Z_COLS = 7680
ZB_GATE = 0
ZB_RW = 16
ZB_RET = 20
ZB_S5 = 24
ZB_HG = 25

VMEM_LIMIT = 56 * 1024 * 1024

LOG_GAMMA = tuple(
    tuple(math.log1p(-2.0 ** (-(5.0 + 0.5 * di) - h)) for h in range(N_HEADS)) for di in range(2))


def _cparams(*sem):
    return pltpu.CompilerParams(dimension_semantics=sem, vmem_limit_bytes=VMEM_LIMIT)


def _dot(a, b):
    return jnp.dot(a, b, preferred_element_type=F32)


def _dot_nt(a, b):
    return lax.dot_general(a, b, (((1,), (1,)), ((), ())), preferred_element_type=F32)


def _dot_tn(a, b):
    return lax.dot_general(a, b, (((0,), (0,)), ((), ())), preferred_element_type=F32)


def _head_ones():
    r = lax.broadcasted_iota(jnp.int32, (MIX_W, MIX_W), 0) >> HEAD_SHIFT
    c = lax.broadcasted_iota(jnp.int32, (MIX_W, MIX_W), 1) >> HEAD_SHIFT
    return jnp.where(r == c, 1.0, 0.0).astype(F32)


def _head_sum(x, ones_bd):
    return jnp.dot(x, ones_bd, precision=HIGHEST, preferred_element_type=F32)


def _rms(x, w):
    return x * lax.rsqrt(jnp.mean(x * x, axis=-1, keepdims=True) + EPS) * w


def _ada_kernel(c_ref, w_ref, b_ref, o_ref):
    c = c_ref[...]
    s = c * jax.nn.sigmoid(c)
    o_ref[...] = _dot(s.astype(BF16), w_ref[...].astype(BF16)) + b_ref[...]


def _ada_mod(cond, ada_w, ada_b):
    L = ada_w.shape[0]
    n = ada_w.shape[2]
    tn = 1536
    return pl.pallas_call(
        _ada_kernel,
        grid=(L, n // tn),
        in_specs=[
            pl.BlockSpec((8, D_MODEL), lambda l, j: (0, 0)),
            pl.BlockSpec((None, D_MODEL, tn), lambda l, j: (l, 0, j)),
            pl.BlockSpec((None, 1, tn), lambda l, j: (l, 0, j)),
        ],
        out_specs=pl.BlockSpec((None, 8, tn), lambda l, j: (l, 0, j)),
        out_shape=jax.ShapeDtypeStruct((L, 8, n), F32),
        compiler_params=_cparams("parallel", "parallel"),
        name="ada_mod",
    )(cond, ada_w, ada_b.reshape(L, 1, n))


def _proj_in_kernel(x_ref, sc_ref, sh_ref, nw_ref, w_ref, z_ref, h_ref):
    @pl.when(pl.program_id(2) == 0)
    def _():
        h = _rms(x_ref[...], nw_ref[...]) * (1.0 + sc_ref[...]) + sh_ref[...]
        h_ref[...] = h.astype(BF16)

    z_ref[...] = _dot(h_ref[...], w_ref[...])


def _mod_map(bm):
    if bm == 1:
        return lambda b, *_: (0, 0, 0)
    return lambda b, *_: (b, 0, 0)


def _proj_in(x, sc, sh, nw, w_p):
    B, T, _ = x.shape
    tm = min(T, 512)
    tn = 1920
    return pl.pallas_call(
        _proj_in_kernel,
        grid=(B, T // tm, Z_COLS // tn),
        in_specs=[
            pl.BlockSpec((None, tm, D_MODEL), lambda b, i, j: (b, i, 0)),
            pl.BlockSpec((None, 1, D_MODEL), _mod_map(sc.shape[0])),
            pl.BlockSpec((None, 1, D_MODEL), _mod_map(sh.shape[0])),
            pl.BlockSpec((1, D_MODEL), lambda b, i, j: (0, 0)),
            pl.BlockSpec((D_MODEL, tn), lambda b, i, j: (0, j)),
        ],
        out_specs=pl.BlockSpec((None, tm, tn), lambda b, i, j: (b, i, j)),
        out_shape=jax.ShapeDtypeStruct((B, T, Z_COLS), F32),
        scratch_shapes=[pltpu.VMEM((tm, D_MODEL), BF16)],
        compiler_params=_cparams("parallel", "parallel", "arbitrary"),
        name="proj_in",
    )(x, sc, sh, nw, w_p)


def _ret_kernel(q_ref, k_ref, v_ref, s0_ref, o_ref, fin_ref, s_ref, *, lc, nc):
    d = pl.program_id(0)
    c = pl.program_id(2)

    @pl.when(c == 0)
    def _():
        s_ref[...] = s0_ref[...]

    rows = lax.broadcasted_iota(jnp.int32, (lc, lc), 0)
    cols = lax.broadcasted_iota(jnp.int32, (lc, lc), 1)
    rel = jnp.where(d == 0, rows - cols, cols - rows).astype(F32)
    idx = lax.broadcasted_iota(jnp.int32, (lc, 1), 0)
    pos = jnp.where(d == 0, idx, lc - 1 - idx).astype(F32)
    for h in range(N_HEADS):
        lg = jnp.where(d == 0, LOG_GAMMA[0][h], LOG_GAMMA[1][h]).astype(F32)
        sl = slice(h * HEAD_D, (h + 1) * HEAD_D)
        q = q_ref[:, sl]
        k = k_ref[:, sl] * (HEAD_D ** -0.5)
        v = v_ref[:, sl].astype(BF16)
        att = _dot_nt(q.astype(BF16), k.astype(BF16))
        att = att * jnp.where(rel >= 0.0, jnp.exp(jnp.maximum(rel, 0.0) * lg), 0.0)
        qd = q * jnp.exp((pos + 1.0) * lg)
        kd = k * jnp.exp((lc - 1.0 - pos) * lg)
        s = s_ref[h]
        o_ref[:, sl] = _dot(att.astype(BF16), v) + _dot(qd.astype(BF16), s.astype(BF16))
        s_ref[h] = s * jnp.exp(lc * lg) + _dot_tn(kd.astype(BF16), v)

    @pl.when(c == nc - 1)
    def _():
        fin_ref[...] = s_ref[...]


def _retention(z, s0):
    B, T, _ = z.shape
    lc = min(T, 256)
    nc = T // lc

    def tmap(d, c):
        return jnp.where(d == 0, c, nc - 1 - c)

    def zspec(blk):
        return pl.BlockSpec((None, lc, MIX_W), lambda d, b, c: (b, tmap(d, c), blk))

    return pl.pallas_call(
        functools.partial(_ret_kernel, lc=lc, nc=nc),
        grid=(2, B, nc),
        in_specs=[
            zspec(ZB_RET), zspec(ZB_RET + 1), zspec(ZB_RET + 2),
            pl.BlockSpec((None, None, N_HEADS, HEAD_D, HEAD_D), lambda d, b, c: (b, d, 0, 0, 0)),
        ],
        out_specs=[
            pl.BlockSpec((None, None, lc, MIX_W), lambda d, b, c: (d, b, tmap(d, c), 0)),
            pl.BlockSpec((None, None, N_HEADS, HEAD_D, HEAD_D), lambda d, b, c: (b, d, 0, 0, 0)),
        ],
        out_shape=[
            jax.ShapeDtypeStruct((2, B, T, MIX_W), F32),
            jax.ShapeDtypeStruct((B, 2, N_HEADS, HEAD_D, HEAD_D), F32),
        ],
        scratch_shapes=[pltpu.VMEM((N_HEADS, HEAD_D, HEAD_D), F32)],
        compiler_params=_cparams("arbitrary", "arbitrary", "arbitrary"),
        name="retention",
    )(z, z, z, s0)


def _s5_zoh_kernel(lre_ref, lim_ref, ldt_ref, bre_ref, bim_ref, are_ref, aim_ref, bbre_ref, bbim_ref):
    lam_re = jnp.minimum(lre_ref[...], -1e-4)
    lam_im = lim_ref[...]
    dt = jnp.exp(ldt_ref[...])
    mag = jnp.exp(dt * lam_re)
    ang = dt * lam_im
    a_re = mag * jnp.cos(ang)
    a_im = mag * jnp.sin(ang)
    den = lam_re * lam_re + lam_im * lam_im
    f_re = ((a_re - 1.0) * lam_re + a_im * lam_im) / den
    f_im = (a_im * lam_re - (a_re - 1.0) * lam_im) / den
    b_re = bre_ref[...]
    b_im = bim_ref[...]
    are_ref[...] = a_re
    aim_ref[...] = a_im
    bbre_ref[...] = f_re * b_re - f_im * b_im
    bbim_ref[...] = f_re * b_im + f_im * b_re


def _s5_zoh(lam_re, lam_im, log_dt, b_re, b_im):
    n = 2 * S5_STATE
    col = lambda t: t.reshape(n, 1)
    ldt = jnp.broadcast_to(log_dt[:, :, None], (2, S5_GROUPS, S5_P))
    outs = pl.pallas_call(
        _s5_zoh_kernel,
        out_shape=[jax.ShapeDtypeStruct((n, 1), F32), jax.ShapeDtypeStruct((n, 1), F32),
                   jax.ShapeDtypeStruct((n, S5_CH), F32), jax.ShapeDtypeStruct((n, S5_CH), F32)],
        name="s5_zoh",
    )(col(lam_re), col(lam_im), col(ldt), b_re.reshape(n, S5_CH), b_im.reshape(n, S5_CH))
    a_re, a_im, bb_re, bb_im = outs
    shp = (2, S5_GROUPS, S5_P, S5_CH)
    return a_re.reshape(2, 1, S5_STATE), a_im.reshape(2, 1, S5_STATE), bb_re.reshape(shp), bb_im.reshape(shp)


def _s5_kernel(u_ref, are_ref, aim_ref, bre_ref, bim_ref, cre_ref, cim_ref, x0r_ref, x0i_ref,
               y_ref, fr_ref, fi_ref,
               xa_re, xa_im, xb_re, xb_im, p_re, p_im, car_re, car_im, *, L, nc):
    d = pl.program_id(0)
    c = pl.program_id(2)
    pad = L
    nsteps = L.bit_length() - 1
    rows = lax.broadcasted_iota(jnp.int32, (L, 1), 0)

    def scan(bu_re, bu_im, rev):
        bufs = ((xa_re, xa_im), (xb_re, xb_im))
        bufs[0][0][pad:pad + L, :] = bu_re
        bufs[0][1][pad:pad + L, :] = bu_im
        ar = are_ref[...]
        ai = aim_ref[...]
        for ks in range(nsteps):
            src, dst = bufs[ks % 2], bufs[(ks + 1) % 2]
            off = pad + (1 << ks) if rev else pad - (1 << ks)
            sre = src[0][off:off + L, :]
            sim = src[1][off:off + L, :]
            re = src[0][pad:pad + L, :]
            im = src[1][pad:pad + L, :]
            dst[0][pad:pad + L, :] = re + ar * sre - ai * sim
            dst[1][pad:pad + L, :] = im + ar * sim + ai * sre
            ar, ai = ar * ar - ai * ai, 2.0 * ar * ai
        fin = bufs[nsteps % 2]
        return fin[0][pad:pad + L, :], fin[1][pad:pad + L, :]

    def body(rev):
        @pl.when(c == 0)
        def _():
            zeros = jnp.zeros((pad, S5_STATE), F32)
            for buf in (xa_re, xa_im, xb_re, xb_im):
                buf[0:pad, :] = zeros
                buf[pad + L:2 * pad + L, :] = zeros
            first = rows == (L - 1 if rev else 0)
            pr, pi = scan(jnp.where(first, are_ref[...], 0.0), jnp.where(first, aim_ref[...], 0.0), rev)
            p_re[...] = pr
            p_im[...] = pi
            car_re[...] = x0r_ref[...]
            car_im[...] = x0i_ref[...]

        u = u_ref[...].astype(BF16)
        xr, xi = scan(_dot(u, bre_ref[...]), _dot(u, bim_ref[...]), rev)
        cr = car_re[...]
        ci = car_im[...]
        pr = p_re[...]
        pi = p_im[...]
        xr, xi = xr + pr * cr - pi * ci, xi + pr * ci + pi * cr
        y_ref[...] = _dot(xr.astype(BF16), cre_ref[...]) - _dot(xi.astype(BF16), cim_ref[...])
        last = 0 if rev else L - 1
        car_re[...] = xr[last:last + 1, :]
        car_im[...] = xi[last:last + 1, :]

    @pl.when(d == 0)
    def _():
        body(False)

    @pl.when(d == 1)
    def _():
        body(True)

    @pl.when(c == nc - 1)
    def _():
        fr_ref[...] = car_re[...]
        fi_ref[...] = car_im[...]


def _s5(z, a_re, a_im, bblk_re, bblk_im, cblk_re, cblk_im, x0_re, x0_im):
    B, T, _ = z.shape
    L = min(T, 128)
    nc = T // L

    def tmap(d, c):
        return jnp.where(d == 0, c, nc - 1 - c)

    dspec = lambda shape: pl.BlockSpec((None,) + shape, lambda d, b, c: (d, 0, 0))
    sspec = pl.BlockSpec((None, None, 1, S5_STATE), lambda d, b, c: (b, d, 0, 0))
    buf = pltpu.VMEM((3 * L, S5_STATE), F32)
    return pl.pallas_call(
        functools.partial(_s5_kernel, L=L, nc=nc),
        grid=(2, B, nc),
        in_specs=[
            pl.BlockSpec((None, L, MIX_W), lambda d, b, c: (b, tmap(d, c), ZB_S5)),
            dspec((1, S5_STATE)), dspec((1, S5_STATE)),
            dspec((MIX_W, S5_STATE)), dspec((MIX_W, S5_STATE)),
            pl.BlockSpec((S5_STATE, MIX_W), lambda d, b, c: (0, 0)),
            pl.BlockSpec((S5_STATE, MIX_W), lambda d, b, c: (0, 0)),
            sspec, sspec,
        ],
        out_specs=[
            pl.BlockSpec((None, None, L, MIX_W), lambda d, b, c: (d, b, tmap(d, c), 0)),
            sspec, sspec,
        ],
        out_shape=[
            jax.ShapeDtypeStruct((2, B, T, MIX_W), F32),
            jax.ShapeDtypeStruct((B, 2, 1, S5_STATE), F32),
            jax.ShapeDtypeStruct((B, 2, 1, S5_STATE), F32),
        ],
        scratch_shapes=[buf, buf, buf, buf,
                        pltpu.VMEM((L, S5_STATE), F32), pltpu.VMEM((L, S5_STATE), F32),
                        pltpu.VMEM((1, S5_STATE), F32), pltpu.VMEM((1, S5_STATE), F32)],
        compiler_params=_cparams("arbitrary", "arbitrary", "arbitrary"),
        name="s5_scan",
    )(z, a_re, a_im, bblk_re, bblk_im, cblk_re, cblk_im, x0_re, x0_im)


def _hg_prep_kernel(q_ref, zf_ref, zb_ref, lb_ref, qo_ref, f_ref, key_ref):
    q = q_ref[...]
    qo_ref[...] = q * jax.nn.sigmoid(q)
    for di, z_ref in enumerate((zf_ref, zb_ref)):
        lb = lb_ref[di:di + 1, :]
        zf = z_ref[...]
        f_ref[di] = lb + (1.0 - lb) * jax.nn.sigmoid(zf)
        key_ref[di] = (1.0 - lb) * jax.nn.sigmoid(-zf)


def _hg_prep(z, lb):
    B, T, _ = z.shape
    tm = min(T, 512)
    zspec = lambda blk: pl.BlockSpec((None, tm, MIX_W), lambda b, i: (b, i, blk))
    dspec = pl.BlockSpec((2, None, tm, MIX_W), lambda b, i: (0, b, i, 0))
    return pl.pallas_call(
        _hg_prep_kernel,
        grid=(B, T // tm),
        in_specs=[zspec(ZB_HG), zspec(ZB_HG + 1), zspec(ZB_HG + 2),
                  pl.BlockSpec((2, MIX_W), lambda b, i: (0, 0))],
        out_specs=[pl.BlockSpec((None, tm, MIX_W), lambda b, i: (b, i, 0)), dspec, dspec],
        out_shape=[jax.ShapeDtypeStruct((B, T, MIX_W), F32),
                   jax.ShapeDtypeStruct((2, B, T, MIX_W), F32),
                   jax.ShapeDtypeStruct((2, B, T, MIX_W), F32)],
        compiler_params=_cparams("parallel", "parallel"),
        name="hgrn_prep",
    )(z, z, z, lb)


def _rw_prep_kernel(*refs, grid_shift, tm):
    if grid_shift:
        (zc_ref, zu_ref, zd_ref, mu_ref, vec_ref, w0_ref, lora_ref,
         r_ref, k2_ref, v_ref, w_ref, nkk_ref, kka_ref, g_ref, bonus_ref, buf_ref) = refs
    else:
        (zc_ref, mu_ref, vec_ref, w0_ref, lora_ref,
         r_ref, k2_ref, v_ref, w_ref, nkk_ref, kka_ref, g_ref, bonus_ref, buf_ref) = refs
    i = pl.program_id(1)
    nt = pl.num_programs(1)
    halo = GRID_W
    width = 4 * MIX_W
    z = zc_ref[...]
    buf_ref[halo:halo + tm, :] = z
    lane = lax.broadcasted_iota(jnp.int32, (tm, width), 1)
    row = lax.broadcasted_iota(jnp.int32, (tm, width), 0)
    if grid_shift:
        buf_ref[0:halo, :] = jnp.where(i > 0, zu_ref[...], 0.0)
        buf_ref[halo + tm:2 * halo + tm, :] = jnp.where(i < nt - 1, zd_ref[...], 0.0)
        col = row & (GRID_W - 1)
        left = jnp.where(col > 0, buf_ref[halo - 1:halo - 1 + tm, :], 0.0)
        right = jnp.where(col < GRID_W - 1, buf_ref[halo + 1:halo + 1 + tm, :], 0.0)
        up = buf_ref[0:tm, :]
        down = buf_ref[2 * halo:2 * halo + tm, :]
        sel = lane & 3
        shifted = jnp.where(sel == 0, left, jnp.where(sel == 1, right, jnp.where(sel == 2, up, down)))
    else:
        zrow = jnp.zeros((1, width), F32)
        buf_ref[halo - 1:halo, :] = zrow
        buf_ref[halo + tm:halo + tm + 1, :] = zrow
        prev = buf_ref[halo - 1:halo - 1 + tm, :]
        nxt = buf_ref[halo + 1:halo + 1 + tm, :]
        shifted = jnp.where((lane & 1) == 0, prev, nxt)
    zs = z + mu_ref[...] * (shifted - z)
    r = zs[:, 0:MIX_W]
    k = zs[:, MIX_W:2 * MIX_W]
    v = zs[:, 2 * MIX_W:3 * MIX_W]
    sm = zs[:, 3 * MIX_W:4 * MIX_W]
    ones_bd = _head_ones()
    a0, k_k, k_a, r_k = (vec_ref[j:j + 1, :] for j in range(4))
    a = jax.nn.sigmoid(a0 + _dot(sm.astype(BF16), lora_ref[2]))
    g_ref[...] = _dot(jax.nn.sigmoid(sm).astype(BF16), lora_ref[3])
    kk = k * k_k
    kk = kk * lax.rsqrt(_head_sum(kk * kk, ones_bd) + 1e-12)
    k2 = k * (1.0 + (a - 1.0) * k_a)
    th = jnp.tanh(sm).astype(BF16)
    for di in range(2):
        log_w = -math.exp(-0.5) * jax.nn.sigmoid(w0_ref[di:di + 1, :] + _dot(th, lora_ref[di]))
        w_ref[di] = jnp.exp(log_w)
    r_ref[...] = r
    k2_ref[...] = k2
    v_ref[...] = v
    nkk_ref[...] = -kk
    kka_ref[...] = kk * a
    bonus_ref[...] = _head_sum(r * k2 * r_k, ones_bd) * v


def _rw_prep(z, mu_p, vecs, w0, lora, grid_shift):
    B, T, _ = z.shape
    width = 4 * MIX_W
    wblk = ZB_RW // 4
    if grid_shift:
        tm = min(T, 512)
        hb = tm // GRID_W
        nh = T // GRID_W
        z_specs = [
            pl.BlockSpec((None, tm, width), lambda b, i: (b, i, wblk)),
            pl.BlockSpec((None, GRID_W, width), lambda b, i: (b, jnp.maximum(i * hb - 1, 0), wblk)),
            pl.BlockSpec((None, GRID_W, width), lambda b, i: (b, jnp.minimum((i + 1) * hb, nh - 1), wblk)),
        ]
        z_args = (z, z, z)
    else:
        tm = T
        z_specs = [pl.BlockSpec((None, tm, width), lambda b, i: (b, i, wblk))]
        z_args = (z,)
    const = lambda shape: pl.BlockSpec(shape, lambda b, i: (0,) * len(shape))
    ospec = pl.BlockSpec((None, tm, MIX_W), lambda b, i: (b, i, 0))
    oshape = jax.ShapeDtypeStruct((B, T, MIX_W), F32)
    dspec = pl.BlockSpec((2, None, tm, MIX_W), lambda b, i: (0, b, i, 0))
    dshape = jax.ShapeDtypeStruct((2, B, T, MIX_W), F32)
    return pl.pallas_call(
        functools.partial(_rw_prep_kernel, grid_shift=grid_shift, tm=tm),
        grid=(B, T // tm),
        in_specs=z_specs + [const((1, width)), const((4, MIX_W)), const((2, MIX_W)),
                            const((4, MIX_W, MIX_W))],
        out_specs=[ospec, ospec, ospec, dspec, ospec, ospec, ospec, ospec],
        out_shape=[oshape, oshape, oshape, dshape, oshape, oshape, oshape, oshape],
        scratch_shapes=[pltpu.VMEM((tm + 2 * GRID_W, width), F32)],
        compiler_params=_cparams("parallel", "parallel"),
        name="rwkv_prep",
    )(*z_args, mu_p, vecs, w0, lora)


def _seq_kernel(*refs, use_sa, rev, nb, ub, tb, nt):
    if use_sa:
        w_ref, k_ref, r_ref, v_ref, a_ref, b_ref, s0_ref, y_ref, fin_ref, s_ref = refs
    else:
        w_ref, k_ref, r_ref, v_ref, s0_ref, y_ref, fin_ref, s_ref = refs
    j = pl.program_id(0)
    pair_w = 2 * HEAD_D
    group = 8

    @pl.when(j == 0)
    def _():
        s_ref[...] = s0_ref[...]

    lane = lax.broadcasted_iota(jnp.int32, (HEAD_D, pair_w), 1)
    sub = lax.broadcasted_iota(jnp.int32, (HEAD_D, pair_w), 0)
    lo = lane < HEAD_D
    diag = (lane & (HEAD_D - 1)) == sub
    r2 = lax.broadcasted_iota(jnp.int32, (2 * pair_w, pair_w), 0)
    c2 = lax.broadcasted_iota(jnp.int32, (2 * pair_w, pair_w), 1)
    ones2 = jnp.where(((r2 >> HEAD_SHIFT) & 1) == (c2 >> HEAD_SHIFT), 1.0, 0.0).astype(BF16)

    def split(x):
        hi = x.astype(BF16).astype(F32)
        return hi, (x - hi).astype(BF16).astype(F32)

    def seg_sum_mxu(hi, lo_):
        return _dot(jnp.concatenate([hi.astype(BF16), lo_.astype(BF16)], axis=1), ones2)

    def seg_sum_xlu(p):
        s_lo = jnp.sum(jnp.where(lo, p, 0.0), axis=1, keepdims=True)
        s_hi = jnp.sum(jnp.where(lo, 0.0, p), axis=1, keepdims=True)
        return jnp.where(lo, s_lo, s_hi)

    def chain(b, p, t0):
        sl = pl.ds(p * pair_w, pair_w)
        blk = lambda ref: ref[b, pl.ds(t0, group), sl]
        w, k, r, v = blk(w_ref), blk(k_ref), blk(r_ref), blk(v_ref)
        if use_sa:
            a, bb = blk(a_ref), blk(b_ref)
        v_hi, v_lo = split(v)
        s = s_ref[b, p]
        ys = [None] * group
        for i in range(group):
            t = group - 1 - i if rev else i
            row = lambda x: x[t:t + 1, :]
            vcol = seg_sum_mxu(jnp.where(diag, row(v_hi), 0.0), jnp.where(diag, row(v_lo), 0.0))
            if use_sa:
                s = s * row(w) + seg_sum_xlu(s * row(a)) * row(bb)
            else:
                s = s * row(w)
            s = s + vcol * row(k)
            yb = seg_sum_mxu(*split(s * row(r)))
            ys[t] = jnp.sum(jnp.where(diag, yb, 0.0), axis=0, keepdims=True)
        s_ref[b, p] = s
        y_ref[b, pl.ds(t0, group), sl] = jnp.concatenate(ys, axis=0)

    def group_step(g, carry):
        gg = tb // group - 1 - g if rev else g
        t0 = pl.multiple_of(gg * group, group)

        def batch_step(bg, c):
            for u in range(ub):
                for p in range(2):
                    chain(bg * ub + u, p, t0)
            return c

        return lax.fori_loop(0, nb // ub, batch_step, carry)

    lax.fori_loop(0, tb // group, group_step, 0)

    @pl.when(j == nt - 1)
    def _():
        fin_ref[...] = s_ref[...]


def _seq_scan(w, k, r, v, a, b, s0, rev):
    use_sa = a is not None
    B, T = v[0].shape[-3], v[0].shape[-2]
    tb = min(T, 128)
    nt = T // tb
    ub = 2 if B % 2 == 0 else 1
    tmap = (lambda j: nt - 1 - j) if rev else (lambda j: j)

    def spec(arr, lead, blk):
        if lead is not None:
            return pl.BlockSpec((None, B, tb, MIX_W), lambda j: (lead, 0, tmap(j), blk))
        return pl.BlockSpec((B, tb, MIX_W), lambda j: (0, tmap(j), blk))

    ins = [w, k, r, v] + ([a, b] if use_sa else [])
    sspec = pl.BlockSpec((B, 2, HEAD_D, 2 * HEAD_D), lambda j: (0, 0, 0, 0))
    return pl.pallas_call(
        functools.partial(_seq_kernel, use_sa=use_sa, rev=rev, nb=B, ub=ub, tb=tb, nt=nt),
        grid=(nt,),
        in_specs=[spec(*x) for x in ins] + [sspec],
        out_specs=[pl.BlockSpec((B, tb, MIX_W), lambda j: (0, tmap(j), 0)), sspec],
        out_shape=[jax.ShapeDtypeStruct((B, T, MIX_W), F32),
                   jax.ShapeDtypeStruct((B, 2, HEAD_D, 2 * HEAD_D), F32)],
        scratch_shapes=[pltpu.VMEM((B, 2, HEAD_D, 2 * HEAD_D), F32)],
        compiler_params=_cparams("arbitrary"),
        name=("rwkv_scan" if use_sa else "hgrn_scan") + ("_bwd" if rev else "_fwd"),
    )(*[x[0] for x in ins], s0)


def _seq_scan_bidir(w, k, r, v, a, b, s0):
    ys, fins = [], []
    for di in range(2):
        pick = lambda x: None if x is None else (x[0], di if x[0].ndim == 4 else None, x[1])
        y, fin = _seq_scan(pick(w), pick(k), pick(r), pick(v), pick(a), pick(b), s0[:, di], di == 1)
        ys.append(y)
        fins.append(fin)
    return jnp.stack(ys), jnp.stack(fins, axis=1)


def _pair_tiles(s):
    B = s.shape[0]
    s = s.reshape(B, 2, 2, 2, HEAD_D, HEAD_D).transpose(0, 1, 2, 4, 3, 5)
    return s.reshape(B, 2, 2, HEAD_D, 2 * HEAD_D)


def _unpair_tiles(s):
    B = s.shape[0]
    s = s.reshape(B, 2, 2, HEAD_D, 2, HEAD_D).transpose(0, 1, 2, 4, 3, 5)
    return s.reshape(B, 2, N_HEADS, HEAD_D, HEAD_D)


def _mix_kernel(x_ref, g1_ref, nw_ref, zg_ref,
                ret_ref, retg_ref, s5_ref, s5u_ref, hg_ref, hgg_ref, rw_ref, rwb_ref, rwg_ref,
                vec_ref, glu_w_ref, wbr_ref, wout_ref, o_ref):
    ones_bd = _head_ones()
    gn_w, s5_d, glu_b, hg_w, ln_w, ln_b = (vec_ref[j:j + 1, :] for j in range(6))

    def group_norm(o, eps):
        mu = _head_sum(o, ones_bd) * (1.0 / HEAD_D)
        oc = o - mu
        var = _head_sum(oc * oc, ones_bd) * (1.0 / HEAD_D)
        return oc * lax.rsqrt(var + eps)

    g = retg_ref[...]
    y_ret = group_norm(ret_ref[0] + ret_ref[1], EPS) * gn_w * (g * jax.nn.sigmoid(g))

    y = s5_d * s5u_ref[...] + s5_ref[0] + s5_ref[1]
    yg = jax.nn.gelu(y)
    y_s5 = yg * jax.nn.sigmoid(_dot(yg.astype(BF16), glu_w_ref[...]) + glu_b)

    o = hg_ref[0] + hg_ref[1]
    g = hgg_ref[...]
    ms = _head_sum(o * o, ones_bd) * (1.0 / HEAD_D)
    y_hg = o * lax.rsqrt(ms + EPS) * hg_w * (g * jax.nn.sigmoid(g))

    y = group_norm(rw_ref[0] + rw_ref[1], RW_LN_EPS) * ln_w + ln_b
    y_rw = (y + rwb_ref[...]) * rwg_ref[...]

    mixed = None
    for m, ym in enumerate((y_ret, y_s5, y_hg, y_rw)):
        br = _dot(ym.astype(BF16), wbr_ref[m])
        term = jax.nn.sigmoid(zg_ref[:, m * D_MODEL:(m + 1) * D_MODEL]) * br
        mixed = term if mixed is None else mixed + term
    mixed = _dot(mixed.astype(BF16), wout_ref[...])
    o_ref[...] = x_ref[...] + g1_ref[...] * _rms(mixed, nw_ref[...])


def _mix(x, g1, nw, z, ret_o, s5_y, hg_o, rw_y, rw_bonus, rw_g, vecs, glu_w, w_branch, w_out):
    B, T, _ = x.shape
    tm = min(T, 256)
    xspec = pl.BlockSpec((None, tm, D_MODEL), lambda b, i: (b, i, 0))
    zspec = lambda blk: pl.BlockSpec((None, tm, MIX_W), lambda b, i: (b, i, blk))
    dspec = pl.BlockSpec((2, None, tm, MIX_W), lambda b, i: (0, b, i, 0))
    const = lambda shape: pl.BlockSpec(shape, lambda b, i: (0,) * len(shape))
    return pl.pallas_call(
        _mix_kernel,
        grid=(B, T // tm),
        in_specs=[
            xspec,
            pl.BlockSpec((None, 1, D_MODEL), _mod_map(g1.shape[0])),
            const((1, D_MODEL)),
            pl.BlockSpec((None, tm, 4 * D_MODEL), lambda b, i: (b, i, 0)),
            dspec, zspec(ZB_RET + 3),
            dspec, zspec(ZB_S5),
            dspec, zspec(ZB_HG + 4),
            dspec, zspec(0), zspec(0),
            const((6, MIX_W)), const((MIX_W, MIX_W)),
            const((4, MIX_W, D_MODEL)), const((D_MODEL, D_MODEL)),
        ],
        out_specs=xspec,
        out_shape=jax.ShapeDtypeStruct((B, T, D_MODEL), F32),
        compiler_params=_cparams("parallel", "parallel"),
        name="mix_out",
    )(x, g1, nw, z, ret_o, z, s5_y, z, hg_o, z, rw_y, rw_bonus, rw_g, vecs, glu_w, w_branch, w_out)


def _ffn_kernel(x_ref, sc_ref, sh_ref, g2_ref, nw2_ref, nw3_ref, w1_ref, w2_ref, o_ref, h_ref, acc_ref):
    j = pl.program_id(2)

    @pl.when(j == 0)
    def _():
        h = _rms(x_ref[...], nw2_ref[...]) * (1.0 + sc_ref[...]) + sh_ref[...]
        h_ref[...] = h.astype(BF16)
        acc_ref[...] = jnp.zeros_like(acc_ref)

    a = jnp.maximum(_dot(h_ref[...], w1_ref[...]), 0.0)
    acc_ref[...] += _dot((a * a).astype(BF16), w2_ref[...])

    @pl.when(j == pl.num_programs(2) - 1)
    def _():
        o_ref[...] = x_ref[...] + g2_ref[...] * _rms(acc_ref[...], nw3_ref[...])


def _ffn(x, sc, sh, g2, nw2, nw3, w1, w2):
    B, T, _ = x.shape
    tm = min(T, 512)
    tf = 1024
    xspec = pl.BlockSpec((None, tm, D_MODEL), lambda b, i, j: (b, i, 0))
    mspec = lambda m: pl.BlockSpec((None, 1, D_MODEL), _mod_map(m.shape[0]))
    const = pl.BlockSpec((1, D_MODEL), lambda b, i, j: (0, 0))
    return pl.pallas_call(
        _ffn_kernel,
        grid=(B, T // tm, D_FF // tf),
        in_specs=[xspec, mspec(sc), mspec(sh), mspec(g2), const, const,
                  pl.BlockSpec((D_MODEL, tf), lambda b, i, j: (0, j)),
                  pl.BlockSpec((tf, D_MODEL), lambda b, i, j: (j, 0))],
        out_specs=xspec,
        out_shape=jax.ShapeDtypeStruct((B, T, D_MODEL), F32),
        scratch_shapes=[pltpu.VMEM((tm, D_MODEL), BF16), pltpu.VMEM((tm, D_MODEL), F32)],
        compiler_params=_cparams("parallel", "parallel", "arbitrary"),
        name="ffn",
    )(x, sc, sh, g2, nw2, nw3, w1, w2)


def _layer_params(l, p):
    w = p['w_in'][l]
    w_p = jnp.concatenate(
        [w[:, 3488:7584], w[:, 2560:3488], jnp.zeros((D_MODEL, Z_COLS - 7584), F32),
         w[:, 0:1024], w[:, 1024:1280], w[:, 1280:2560]], axis=1).astype(BF16)

    a_re, a_im, bb_re, bb_im = _s5_zoh(p['s5_lam_re'][l], p['s5_lam_im'][l], p['s5_log_dt'][l],
                                       p['s5_b_re'][l], p['s5_b_im'][l])
    eye = jnp.eye(S5_GROUPS, dtype=F32)
    bblk = lambda bb: jnp.einsum('dgph,gk->dghkp', bb, eye).reshape(2, MIX_W, S5_STATE).astype(BF16)
    cblk = lambda c: jnp.einsum('ghp,gk->gpkh', c, eye).reshape(S5_STATE, MIX_W).astype(BF16)

    def lora_pad(m, row0):
        return jnp.zeros((MIX_W, MIX_W), F32).at[row0:row0 + m.shape[0]].set(m)

    lora = jnp.stack([lora_pad(p['rw_w2'][l, 0], 0), lora_pad(p['rw_w2'][l, 1], 32),
                      lora_pad(p['rw_a2'][l], 64), lora_pad(p['rw_g2'][l], 96)]).astype(BF16)
    mu_p = jnp.concatenate([p['rw_mu'][l], jnp.zeros((4 * MIX_W - RW_COLS,), F32)]).reshape(1, 4 * MIX_W)
    return dict(
        w_p=w_p, nw=p['norm_w'][l],
        s5=(a_re, a_im, bblk(bb_re), bblk(bb_im), cblk(p['s5_c_re'][l]), cblk(p['s5_c_im'][l])),
        rw_mu=mu_p, rw_lora=lora, rw_w0=p['rw_w0'][l],
        rw_vecs=jnp.stack([p['rw_a0'][l], p['rw_k_k'][l], p['rw_k_a'][l], p['rw_r_k'][l]]),
        mix_vecs=jnp.stack([p['ret_gn_w'][l], p['s5_d'][l], p['s5_glu_b'][l], p['hg_norm_w'][l],
                            p['rw_ln_w'][l], p['rw_ln_b'][l]]),
        glu_w=p['s5_glu_w'][l].astype(BF16),
        w_branch=p['w_branch'][l].astype(BF16), w_out=p['w_out'][l].astype(BF16),
        ff_w1=p['ff_w1'][l].astype(BF16), ff_w2=p['ff_w2'][l].astype(BF16),
    )


def _trunk_layer(x, mod, init, grid_shift, lp, hg_lb):
    B = x.shape[0]
    sh1, sc1, g1, sh2, sc2, g2 = (m[:, None, :] for m in jnp.split(mod, 6, axis=-1))
    nw = lp['nw']
    z = _proj_in(x, sc1, sh1, nw[0:1], lp['w_p'])

    s_ret, s_s5r, s_s5i, s_hg, s_rw = init
    ret_o, f_ret = _retention(z, s_ret)

    s5_y, f_s5r, f_s5i = _s5(z, *lp['s5'], s_s5r.reshape(B, 2, 1, S5_STATE), s_s5i.reshape(B, 2, 1, S5_STATE))
    f_s5r = f_s5r.reshape(B, 2, S5_GROUPS, S5_P)
    f_s5i = f_s5i.reshape(B, 2, S5_GROUPS, S5_P)

    hq, hf, hkey = _hg_prep(z, hg_lb)
    hg_o, f_hg = _seq_scan_bidir((hf, 0), (hkey, 0), (hq, 0), (z, ZB_HG + 3), None, None,
                                 _pair_tiles(jnp.swapaxes(s_hg, -1, -2)))
    f_hg = jnp.swapaxes(_unpair_tiles(f_hg), -1, -2)

    r, k2, v, w, nkk, kka, rw_g, bonus = _rw_prep(z, lp['rw_mu'], lp['rw_vecs'], lp['rw_w0'],
                                                  lp['rw_lora'], grid_shift)
    rw_y, f_rw = _seq_scan_bidir((w, 0), (k2, 0), (r, 0), (v, 0), (nkk, 0), (kka, 0), _pair_tiles(s_rw))
    f_rw = _unpair_tiles(f_rw)

    x = _mix(x, g1, nw[1:2], z, ret_o, s5_y, hg_o, rw_y, bonus, rw_g, lp['mix_vecs'], lp['glu_w'],
             lp['w_branch'], lp['w_out'])
    x = _ffn(x, sc2, sh2, g2, nw[2:3], nw[3:4], lp['ff_w1'], lp['ff_w2'])
    return x, (f_ret, f_s5r, f_s5i, f_hg, f_rw)


def kernel(x_prompt, x_sample, state_ret, state_s5_re, state_s5_im, state_hgrn, state_rwkv, c, c_ctx, ada_w, ada_b, norm_w, w_in, ret_gn_w, s5_lam_re, s5_lam_im, s5_log_dt, s5_b_re, s5_b_im, s5_c_re, s5_c_im, s5_d, s5_glu_w, s5_glu_b, hg_lb, hg_norm_w, rw_mu, rw_w0, rw_w2, rw_a0, rw_a2, rw_g2, rw_k_k, rw_k_a, rw_r_k, rw_ln_w, rw_ln_b, w_branch, w_out, ff_w1, ff_w2):
    p = dict(norm_w=norm_w, w_in=w_in, ret_gn_w=ret_gn_w, s5_lam_re=s5_lam_re, s5_lam_im=s5_lam_im,
             s5_log_dt=s5_log_dt, s5_b_re=s5_b_re, s5_b_im=s5_b_im, s5_c_re=s5_c_re, s5_c_im=s5_c_im,
             s5_d=s5_d, s5_glu_w=s5_glu_w, s5_glu_b=s5_glu_b, hg_norm_w=hg_norm_w, rw_mu=rw_mu,
             rw_w0=rw_w0, rw_w2=rw_w2, rw_a0=rw_a0, rw_a2=rw_a2, rw_g2=rw_g2, rw_k_k=rw_k_k,
             rw_k_a=rw_k_a, rw_r_k=rw_r_k, rw_ln_w=rw_ln_w, rw_ln_b=rw_ln_b, w_branch=w_branch,
             w_out=w_out, ff_w1=ff_w1, ff_w2=ff_w2)
    depth = w_in.shape[0]
    n_ctx = x_prompt.shape[0]
    n_lat = x_sample.shape[0]
    assert 1 + n_lat <= 8

    lb_cum = jnp.cumsum(jax.nn.softmax(hg_lb.astype(F32), axis=0), axis=0)
    hg_lower = lb_cum - lb_cum[0]

    cond = jnp.concatenate([c_ctx[None, :], c, jnp.zeros((7 - n_lat, D_MODEL), F32)], axis=0)
    mod = _ada_mod(cond, ada_w, ada_b)

    zero_state = (jnp.zeros((n_ctx, 2, N_HEADS, HEAD_D, HEAD_D), F32),
                  jnp.zeros((n_ctx, 2, S5_GROUPS, S5_P), F32),
                  jnp.zeros((n_ctx, 2, S5_GROUPS, S5_P), F32),
                  jnp.zeros((n_ctx, 2, N_HEADS, HEAD_D, HEAD_D), F32),
                  jnp.zeros((n_ctx, 2, N_HEADS, HEAD_D, HEAD_D), F32))
    xp, xs = x_prompt, x_sample
    finals = []
    for l in range(depth):
        lp = _layer_params(l, p)
        xp, fin = _trunk_layer(xp, mod[l, 0:1], zero_state, False, lp, hg_lower[l])
        finals.append(fin)
        lat_init = (state_ret[:, l], state_s5_re[:, l], state_s5_im[:, l], state_hgrn[:, l], state_rwkv[:, l])
        xs, _ = _trunk_layer(xs, mod[l, 1:1 + n_lat], lat_init, True, lp, hg_lower[l])
    new_states = tuple(jnp.stack([f[i] for f in finals], axis=1) for i in range(5))
    return (xp, xs) + new_states
```

```python
import functools
import math

import jax
import jax.numpy as jnp
from jax import lax
from jax.experimental import pallas as pl
from jax.experimental.pallas import tpu as pltpu

F32 = jnp.float32
BF16 = jnp.bfloat16
HIGHEST = lax.Precision.HIGHEST

D_MODEL = 1024
GRID_W = 64
MIX_W = 256
N_HEADS = 4
HEAD_D = 64
HEAD_SHIFT = 6
S5_GROUPS = 16
S5_CH = 16
S5_P = 64
S5_STATE = S5_GROUPS * S5_P
D_FF = 4096
EPS = 1e-6
RW_LN_EPS = 64e-5
RW_COLS = 928

Z_COLS = 7680
ZB_GATE = 0
ZB_RW = 16
ZB_RET = 20
ZB_S5 = 24
ZB_HG = 25

VMEM_LIMIT = 56 * 1024 * 1024

LOG_GAMMA = tuple(
    tuple(math.log1p(-2.0 ** (-(5.0 + 0.5 * di) - h)) for h in range(N_HEADS)) for di in range(2))


def _cparams(*sem):
    return pltpu.CompilerParams(dimension_semantics=sem, vmem_limit_bytes=VMEM_LIMIT)


def _dot(a, b):
    return jnp.dot(a, b, preferred_element_type=F32)


def _dot_nt(a, b):
    return lax.dot_general(a, b, (((1,), (1,)), ((), ())), preferred_element_type=F32)


def _dot_tn(a, b):
    return lax.dot_general(a, b, (((0,), (0,)), ((), ())), preferred_element_type=F32)


def _head_ones():
    r = lax.broadcasted_iota(jnp.int32, (MIX_W, MIX_W), 0) >> HEAD_SHIFT
    c = lax.broadcasted_iota(jnp.int32, (MIX_W, MIX_W), 1) >> HEAD_SHIFT
    return jnp.where(r == c, 1.0, 0.0).astype(F32)


def _head_sum(x, ones_bd):
    return jnp.dot(x, ones_bd, precision=HIGHEST, preferred_element_type=F32)


def _rms(x, w):
    return x * lax.rsqrt(jnp.mean(x * x, axis=-1, keepdims=True) + EPS) * w


def _ada_kernel(c_ref, w_ref, b_ref, o_ref):
    c = c_ref[...]
    s = c * jax.nn.sigmoid(c)
    o_ref[...] = _dot(s.astype(BF16), w_ref[...].astype(BF16)) + b_ref[...]


def _ada_mod(cond, ada_w, ada_b):
    L = ada_w.shape[0]
    n = ada_w.shape[2]
    tn = 1536
    return pl.pallas_call(
        _ada_kernel,
        grid=(L, n // tn),
        in_specs=[
            pl.BlockSpec((8, D_MODEL), lambda l, j: (0, 0)),
            pl.BlockSpec((None, D_MODEL, tn), lambda l, j: (l, 0, j)),
            pl.BlockSpec((None, 1, tn), lambda l, j: (l, 0, j)),
        ],
        out_specs=pl.BlockSpec((None, 8, tn), lambda l, j: (l, 0, j)),
        out_shape=jax.ShapeDtypeStruct((L, 8, n), F32),
        compiler_params=_cparams("parallel", "parallel"),
        name="ada_mod",
    )(cond, ada_w, ada_b.reshape(L, 1, n))


def _proj_in_kernel(x_ref, sc_ref, sh_ref, nw_ref, w_ref, z_ref, h_ref):
    @pl.when(pl.program_id(2) == 0)
    def _():
        h = _rms(x_ref[...], nw_ref[...]) * (1.0 + sc_ref[...]) + sh_ref[...]
        h_ref[...] = h.astype(BF16)

    z_ref[...] = _dot(h_ref[...], w_ref[...])


def _mod_map(bm):
    if bm == 1:
        return lambda b, *_: (0, 0, 0)
    return lambda b, *_: (b, 0, 0)


def _proj_in(x, sc, sh, nw, w_p):
    B, T, _ = x.shape
    tm = min(T, 512)
    tn = 1920
    return pl.pallas_call(
        _proj_in_kernel,
        grid=(B, T // tm, Z_COLS // tn),
        in_specs=[
            pl.BlockSpec((None, tm, D_MODEL), lambda b, i, j: (b, i, 0)),
            pl.BlockSpec((None, 1, D_MODEL), _mod_map(sc.shape[0])),
            pl.BlockSpec((None, 1, D_MODEL), _mod_map(sh.shape[0])),
            pl.BlockSpec((1, D_MODEL), lambda b, i, j: (0, 0)),
            pl.BlockSpec((D_MODEL, tn), lambda b, i, j: (0, j)),
        ],
        out_specs=pl.BlockSpec((None, tm, tn), lambda b, i, j: (b, i, j)),
        out_shape=jax.ShapeDtypeStruct((B, T, Z_COLS), F32),
        scratch_shapes=[pltpu.VMEM((tm, D_MODEL), BF16)],
        compiler_params=_cparams("parallel", "parallel", "arbitrary"),
        name="proj_in",
    )(x, sc, sh, nw, w_p)


def _ret_kernel(q_ref, k_ref, v_ref, s0_ref, o_ref, fin_ref, s_ref, *, lc, nc):
    d = pl.program_id(0)
    c = pl.program_id(2)

    @pl.when(c == 0)
    def _():
        s_ref[...] = s0_ref[...]

    rows = lax.broadcasted_iota(jnp.int32, (lc, lc), 0)
    cols = lax.broadcasted_iota(jnp.int32, (lc, lc), 1)
    rel = jnp.where(d == 0, rows - cols, cols - rows).astype(F32)
    idx = lax.broadcasted_iota(jnp.int32, (lc, 1), 0)
    pos = jnp.where(d == 0, idx, lc - 1 - idx).astype(F32)
    for h in range(N_HEADS):
        lg = jnp.where(d == 0, LOG_GAMMA[0][h], LOG_GAMMA[1][h]).astype(F32)
        sl = slice(h * HEAD_D, (h + 1) * HEAD_D)
        q = q_ref[:, sl]
        k = k_ref[:, sl] * (HEAD_D ** -0.5)
        v = v_ref[:, sl].astype(BF16)
        att = _dot_nt(q.astype(BF16), k.astype(BF16))
        att = att * jnp.where(rel >= 0.0, jnp.exp(jnp.maximum(rel, 0.0) * lg), 0.0)
        qd = q * jnp.exp((pos + 1.0) * lg)
        kd = k * jnp.exp((lc - 1.0 - pos) * lg)
        s = s_ref[h]
        o_ref[:, sl] = _dot(att.astype(BF16), v) + _dot(qd.astype(BF16), s.astype(BF16))
        s_ref[h] = s * jnp.exp(lc * lg) + _dot_tn(kd.astype(BF16), v)

    @pl.when(c == nc - 1)
    def _():
        fin_ref[...] = s_ref[...]


def _retention(z, s0):
    B, T, _ = z.shape
    lc = min(T, 256)
    nc = T // lc

    def tmap(d, c):
        return jnp.where(d == 0, c, nc - 1 - c)

    def zspec(blk):
        return pl.BlockSpec((None, lc, MIX_W), lambda d, b, c: (b, tmap(d, c), blk))

    return pl.pallas_call(
        functools.partial(_ret_kernel, lc=lc, nc=nc),
        grid=(2, B, nc),
        in_specs=[
            zspec(ZB_RET), zspec(ZB_RET + 1), zspec(ZB_RET + 2),
            pl.BlockSpec((None, None, N_HEADS, HEAD_D, HEAD_D), lambda d, b, c: (b, d, 0, 0, 0)),
        ],
        out_specs=[
            pl.BlockSpec((None, None, lc, MIX_W), lambda d, b, c: (d, b, tmap(d, c), 0)),
            pl.BlockSpec((None, None, N_HEADS, HEAD_D, HEAD_D), lambda d, b, c: (b, d, 0, 0, 0)),
        ],
        out_shape=[
            jax.ShapeDtypeStruct((2, B, T, MIX_W), F32),
            jax.ShapeDtypeStruct((B, 2, N_HEADS, HEAD_D, HEAD_D), F32),
        ],
        scratch_shapes=[pltpu.VMEM((N_HEADS, HEAD_D, HEAD_D), F32)],
        compiler_params=_cparams("arbitrary", "arbitrary", "arbitrary"),
        name="retention",
    )(z, z, z, s0)


def _s5_zoh_kernel(lre_ref, lim_ref, ldt_ref, bre_ref, bim_ref, are_ref, aim_ref, bbre_ref, bbim_ref):
    lam_re = jnp.minimum(lre_ref[...], -1e-4)
    lam_im = lim_ref[...]
    dt = jnp.exp(ldt_ref[...])
    mag = jnp.exp(dt * lam_re)
    ang = dt * lam_im
    a_re = mag * jnp.cos(ang)
    a_im = mag * jnp.sin(ang)
    den = lam_re * lam_re + lam_im * lam_im
    f_re = ((a_re - 1.0) * lam_re + a_im * lam_im) / den
    f_im = (a_im * lam_re - (a_re - 1.0) * lam_im) / den
    b_re = bre_ref[...]
    b_im = bim_ref[...]
    are_ref[...] = a_re
    aim_ref[...] = a_im
    bbre_ref[...] = f_re * b_re - f_im * b_im
    bbim_ref[...] = f_re * b_im + f_im * b_re


def _s5_zoh(lam_re, lam_im, log_dt, b_re, b_im):
    n = 2 * S5_STATE
    col = lambda t: t.reshape(n, 1)
    ldt = jnp.broadcast_to(log_dt[:, :, None], (2, S5_GROUPS, S5_P))
    outs = pl.pallas_call(
        _s5_zoh_kernel,
        out_shape=[jax.ShapeDtypeStruct((n, 1), F32), jax.ShapeDtypeStruct((n, 1), F32),
                   jax.ShapeDtypeStruct((n, S5_CH), F32), jax.ShapeDtypeStruct((n, S5_CH), F32)],
        name="s5_zoh",
    )(col(lam_re), col(lam_im), col(ldt), b_re.reshape(n, S5_CH), b_im.reshape(n, S5_CH))
    a_re, a_im, bb_re, bb_im = outs
    shp = (2, S5_GROUPS, S5_P, S5_CH)
    return a_re.reshape(2, 1, S5_STATE), a_im.reshape(2, 1, S5_STATE), bb_re.reshape(shp), bb_im.reshape(shp)


def _s5_kernel(u_ref, are_ref, aim_ref, bre_ref, bim_ref, cre_ref, cim_ref, x0r_ref, x0i_ref,
               y_ref, fr_ref, fi_ref,
               xa_re, xa_im, xb_re, xb_im, p_re, p_im, car_re, car_im, *, L, nc):
    d = pl.program_id(0)
    c = pl.program_id(2)
    pad = L
    nsteps = L.bit_length() - 1
    rows = lax.broadcasted_iota(jnp.int32, (L, 1), 0)

    def scan(bu_re, bu_im, rev):
        bufs = ((xa_re, xa_im), (xb_re, xb_im))
        bufs[0][0][pad:pad + L, :] = bu_re
        bufs[0][1][pad:pad + L, :] = bu_im
        ar = are_ref[...]
        ai = aim_ref[...]
        for ks in range(nsteps):
            src, dst = bufs[ks % 2], bufs[(ks + 1) % 2]
            off = pad + (1 << ks) if rev else pad - (1 << ks)
            sre = src[0][off:off + L, :]
            sim = src[1][off:off + L, :]
            re = src[0][pad:pad + L, :]
            im = src[1][pad:pad + L, :]
            dst[0][pad:pad + L, :] = re + ar * sre - ai * sim
            dst[1][pad:pad + L, :] = im + ar * sim + ai * sre
            ar, ai = ar * ar - ai * ai, 2.0 * ar * ai
        fin = bufs[nsteps % 2]
        return fin[0][pad:pad + L, :], fin[1][pad:pad + L, :]

    def body(rev):
        @pl.when(c == 0)
        def _():
            zeros = jnp.zeros((pad, S5_STATE), F32)
            for buf in (xa_re, xa_im, xb_re, xb_im):
                buf[0:pad, :] = zeros
                buf[pad + L:2 * pad + L, :] = zeros
            first = rows == (L - 1 if rev else 0)
            pr, pi = scan(jnp.where(first, are_ref[...], 0.0), jnp.where(first, aim_ref[...], 0.0), rev)
            p_re[...] = pr
            p_im[...] = pi
            car_re[...] = x0r_ref[...]
            car_im[...] = x0i_ref[...]

        u = u_ref[...].astype(BF16)
        xr, xi = scan(_dot(u, bre_ref[...]), _dot(u, bim_ref[...]), rev)
        cr = car_re[...]
        ci = car_im[...]
        pr = p_re[...]
        pi = p_im[...]
        xr, xi = xr + pr * cr - pi * ci, xi + pr * ci + pi * cr
        y_ref[...] = _dot(xr.astype(BF16), cre_ref[...]) - _dot(xi.astype(BF16), cim_ref[...])
        last = 0 if rev else L - 1
        car_re[...] = xr[last:last + 1, :]
        car_im[...] = xi[last:last + 1, :]

    @pl.when(d == 0)
    def _():
        body(False)

    @pl.when(d == 1)
    def _():
        body(True)

    @pl.when(c == nc - 1)
    def _():
        fr_ref[...] = car_re[...]
        fi_ref[...] = car_im[...]


def _s5(z, a_re, a_im, bblk_re, bblk_im, cblk_re, cblk_im, x0_re, x0_im):
    B, T, _ = z.shape
    L = min(T, 128)
    nc = T // L

    def tmap(d, c):
        return jnp.where(d == 0, c, nc - 1 - c)

    dspec = lambda shape: pl.BlockSpec((None,) + shape, lambda d, b, c: (d, 0, 0))
    sspec = pl.BlockSpec((None, None, 1, S5_STATE), lambda d, b, c: (b, d, 0, 0))
    buf = pltpu.VMEM((3 * L, S5_STATE), F32)
    return pl.pallas_call(
        functools.partial(_s5_kernel, L=L, nc=nc),
        grid=(2, B, nc),
        in_specs=[
            pl.BlockSpec((None, L, MIX_W), lambda d, b, c: (b, tmap(d, c), ZB_S5)),
            dspec((1, S5_STATE)), dspec((1, S5_STATE)),
            dspec((MIX_W, S5_STATE)), dspec((MIX_W, S5_STATE)),
            pl.BlockSpec((S5_STATE, MIX_W), lambda d, b, c: (0, 0)),
            pl.BlockSpec((S5_STATE, MIX_W), lambda d, b, c: (0, 0)),
            sspec, sspec,
        ],
        out_specs=[
            pl.BlockSpec((None, None, L, MIX_W), lambda d, b, c: (d, b, tmap(d, c), 0)),
            sspec, sspec,
        ],
        out_shape=[
            jax.ShapeDtypeStruct((2, B, T, MIX_W), F32),
            jax.ShapeDtypeStruct((B, 2, 1, S5_STATE), F32),
            jax.ShapeDtypeStruct((B, 2, 1, S5_STATE), F32),
        ],
        scratch_shapes=[buf, buf, buf, buf,
                        pltpu.VMEM((L, S5_STATE), F32), pltpu.VMEM((L, S5_STATE), F32),
                        pltpu.VMEM((1, S5_STATE), F32), pltpu.VMEM((1, S5_STATE), F32)],
        compiler_params=_cparams("arbitrary", "arbitrary", "arbitrary"),
        name="s5_scan",
    )(z, a_re, a_im, bblk_re, bblk_im, cblk_re, cblk_im, x0_re, x0_im)


def _hg_prep_kernel(q_ref, zf_ref, zb_ref, lb_ref, qo_ref, f_ref, key_ref):
    q = q_ref[...]
    qo_ref[...] = q * jax.nn.sigmoid(q)
    for di, z_ref in enumerate((zf_ref, zb_ref)):
        lb = lb_ref[di:di + 1, :]
        zf = z_ref[...]
        f_ref[di] = lb + (1.0 - lb) * jax.nn.sigmoid(zf)
        key_ref[di] = (1.0 - lb) * jax.nn.sigmoid(-zf)


def _hg_prep(z, lb):
    B, T, _ = z.shape
    tm = min(T, 512)
    zspec = lambda blk: pl.BlockSpec((None, tm, MIX_W), lambda b, i: (b, i, blk))
    dspec = pl.BlockSpec((2, None, tm, MIX_W), lambda b, i: (0, b, i, 0))
    return pl.pallas_call(
        _hg_prep_kernel,
        grid=(B, T // tm),
        in_specs=[zspec(ZB_HG), zspec(ZB_HG + 1), zspec(ZB_HG + 2),
                  pl.BlockSpec((2, MIX_W), lambda b, i: (0, 0))],
        out_specs=[pl.BlockSpec((None, tm, MIX_W), lambda b, i: (b, i, 0)), dspec, dspec],
        out_shape=[jax.ShapeDtypeStruct((B, T, MIX_W), F32),
                   jax.ShapeDtypeStruct((2, B, T, MIX_W), F32),
                   jax.ShapeDtypeStruct((2, B, T, MIX_W), F32)],
        compiler_params=_cparams("parallel", "parallel"),
        name="hgrn_prep",
    )(z, z, z, lb)


def _rw_prep_kernel(*refs, grid_shift, tm):
    if grid_shift:
        (zc_ref, zu_ref, zd_ref, mu_ref, vec_ref, w0_ref, lora_ref,
         r_ref, k2_ref, v_ref, w_ref, nkk_ref, kka_ref, g_ref, bonus_ref, buf_ref) = refs
    else:
        (zc_ref, mu_ref, vec_ref, w0_ref, lora_ref,
         r_ref, k2_ref, v_ref, w_ref, nkk_ref, kka_ref, g_ref, bonus_ref, buf_ref) = refs
    i = pl.program_id(1)
    nt = pl.num_programs(1)
    halo = GRID_W
    width = 4 * MIX_W
    z = zc_ref[...]
    buf_ref[halo:halo + tm, :] = z
    lane = lax.broadcasted_iota(jnp.int32, (tm, width), 1)
    row = lax.broadcasted_iota(jnp.int32, (tm, width), 0)
    if grid_shift:
        buf_ref[0:halo, :] = jnp.where(i > 0, zu_ref[...], 0.0)
        buf_ref[halo + tm:2 * halo + tm, :] = jnp.where(i < nt - 1, zd_ref[...], 0.0)
        col = row & (GRID_W - 1)
        left = jnp.where(col > 0, buf_ref[halo - 1:halo - 1 + tm, :], 0.0)
        right = jnp.where(col < GRID_W - 1, buf_ref[halo + 1:halo + 1 + tm, :], 0.0)
        up = buf_ref[0:tm, :]
        down = buf_ref[2 * halo:2 * halo + tm, :]
        sel = lane & 3
        shifted = jnp.where(sel == 0, left, jnp.where(sel == 1, right, jnp.where(sel == 2, up, down)))
    else:
        zrow = jnp.zeros((1, width), F32)
        buf_ref[halo - 1:halo, :] = zrow
        buf_ref[halo + tm:halo + tm + 1, :] = zrow
        prev = buf_ref[halo - 1:halo - 1 + tm, :]
        nxt = buf_ref[halo + 1:halo + 1 + tm, :]
        shifted = jnp.where((lane & 1) == 0, prev, nxt)
    zs = z + mu_ref[...] * (shifted - z)
    r = zs[:, 0:MIX_W]
    k = zs[:, MIX_W:2 * MIX_W]
    v = zs[:, 2 * MIX_W:3 * MIX_W]
    sm = zs[:, 3 * MIX_W:4 * MIX_W]
    ones_bd = _head_ones()
    a0, k_k, k_a, r_k = (vec_ref[j:j + 1, :] for j in range(4))
    a = jax.nn.sigmoid(a0 + _dot(sm.astype(BF16), lora_ref[2]))
    g_ref[...] = _dot(jax.nn.sigmoid(sm).astype(BF16), lora_ref[3])
    kk = k * k_k
    kk = kk * lax.rsqrt(_head_sum(kk * kk, ones_bd) + 1e-12)
    k2 = k * (1.0 + (a - 1.0) * k_a)
    th = jnp.tanh(sm).astype(BF16)
    for di in range(2):
        log_w = -math.exp(-0.5) * jax.nn.sigmoid(w0_ref[di:di + 1, :] + _dot(th, lora_ref[di]))
        w_ref[di] = jnp.exp(log_w)
    r_ref[...] = r
    k2_ref[...] = k2
    v_ref[...] = v
    nkk_ref[...] = -kk
    kka_ref[...] = kk * a
    bonus_ref[...] = _head_sum(r * k2 * r_k, ones_bd) * v


def _rw_prep(z, mu_p, vecs, w0, lora, grid_shift):
    B, T, _ = z.shape
    width = 4 * MIX_W
    wblk = ZB_RW // 4
    if grid_shift:
        tm = min(T, 512)
        hb = tm // GRID_W
        nh = T // GRID_W
        z_specs = [
            pl.BlockSpec((None, tm, width), lambda b, i: (b, i, wblk)),
            pl.BlockSpec((None, GRID_W, width), lambda b, i: (b, jnp.maximum(i * hb - 1, 0), wblk)),
            pl.BlockSpec((None, GRID_W, width), lambda b, i: (b, jnp.minimum((i + 1) * hb, nh - 1), wblk)),
        ]
        z_args = (z, z, z)
    else:
        tm = T
        z_specs = [pl.BlockSpec((None, tm, width), lambda b, i: (b, i, wblk))]
        z_args = (z,)
    const = lambda shape: pl.BlockSpec(shape, lambda b, i: (0,) * len(shape))
    ospec = pl.BlockSpec((None, tm, MIX_W), lambda b, i: (b, i, 0))
    oshape = jax.ShapeDtypeStruct((B, T, MIX_W), F32)
    dspec = pl.BlockSpec((2, None, tm, MIX_W), lambda b, i: (0, b, i, 0))
    dshape = jax.ShapeDtypeStruct((2, B, T, MIX_W), F32)
    return pl.pallas_call(
        functools.partial(_rw_prep_kernel, grid_shift=grid_shift, tm=tm),
        grid=(B, T // tm),
        in_specs=z_specs + [const((1, width)), const((4, MIX_W)), const((2, MIX_W)),
                            const((4, MIX_W, MIX_W))],
        out_specs=[ospec, ospec, ospec, dspec, ospec, ospec, ospec, ospec],
        out_shape=[oshape, oshape, oshape, dshape, oshape, oshape, oshape, oshape],
        scratch_shapes=[pltpu.VMEM((tm + 2 * GRID_W, width), F32)],
        compiler_params=_cparams("parallel", "parallel"),
        name="rwkv_prep",
    )(*z_args, mu_p, vecs, w0, lora)


def _seq_kernel(*refs, streams, nb, ub, tb, nt):
    refs = list(refs)
    take = lambda n: [refs.pop(0) for _ in range(n)]
    in_refs = [take(6 if use_sa else 4) for use_sa, _ in streams]
    s0_refs = take(len(streams))
    y_refs = take(len(streams))
    fin_refs = take(len(streams))
    s_refs = take(len(streams))
    j = pl.program_id(0)
    pair_w = 2 * HEAD_D
    group = 8

    @pl.when(j == 0)
    def _():
        for s_ref, s0_ref in zip(s_refs, s0_refs):
            s_ref[...] = s0_ref[...]

    lane = lax.broadcasted_iota(jnp.int32, (HEAD_D, pair_w), 1)
    sub = lax.broadcasted_iota(jnp.int32, (HEAD_D, pair_w), 0)
    lo = lane < HEAD_D
    diag = (lane & (HEAD_D - 1)) == sub
    r2 = lax.broadcasted_iota(jnp.int32, (2 * pair_w, pair_w), 0)
    c2 = lax.broadcasted_iota(jnp.int32, (2 * pair_w, pair_w), 1)
    ones2 = jnp.where(((r2 >> HEAD_SHIFT) & 1) == (c2 >> HEAD_SHIFT), 1.0, 0.0).astype(BF16)

    def split(x):
        hi = x.astype(BF16).astype(F32)
        return hi, (x - hi).astype(BF16).astype(F32)

    def seg_sum_mxu(hi, lo_):
        return _dot(jnp.concatenate([hi.astype(BF16), lo_.astype(BF16)], axis=1), ones2)

    def seg_sum_xlu(p):
        s_lo = jnp.sum(jnp.where(lo, p, 0.0), axis=1, keepdims=True)
        s_hi = jnp.sum(jnp.where(lo, 0.0, p), axis=1, keepdims=True)
        return jnp.where(lo, s_lo, s_hi)

    def chain(si, b, p, g):
        use_sa, rev = streams[si]
        t0 = pl.multiple_of((tb // group - 1 - g if rev else g) * group, group)
        sl = pl.ds(p * pair_w, pair_w)
        blk = lambda ref: ref[b, pl.ds(t0, group), sl]
        w, k, r, v = (blk(ref) for ref in in_refs[si][:4])
        if use_sa:
            a, bb = blk(in_refs[si][4]), blk(in_refs[si][5])
            v_hi, v_lo = split(v)
        s = s_refs[si][b, p]
        ys = [None] * group
        for i in range(group):
            t = group - 1 - i if rev else i
            row = lambda x: x[t:t + 1, :]
            if use_sa:
                vcol = seg_sum_mxu(jnp.where(diag, row(v_hi), 0.0), jnp.where(diag, row(v_lo), 0.0))
                s = s * row(w) + seg_sum_xlu(s * row(a)) * row(bb)
            else:
                vcol = seg_sum_xlu(jnp.where(diag, row(v), 0.0))
                s = s * row(w)
            s = s + vcol * row(k)
            yb = seg_sum_mxu(*split(s * row(r)))
            ys[t] = jnp.sum(jnp.where(diag, yb, 0.0), axis=0, keepdims=True)
        s_refs[si][b, p] = s
        y_refs[si][b, pl.ds(t0, group), sl] = jnp.concatenate(ys, axis=0)

    def group_step(g, carry):
        def batch_step(bg, c):
            for si in range(len(streams)):
                for u in range(ub):
                    for p in range(2):
                        chain(si, bg * ub + u, p, g)
            return c

        return lax.fori_loop(0, nb // ub, batch_step, carry)

    lax.fori_loop(0, tb // group, group_step, 0)

    @pl.when(j == nt - 1)
    def _():
        for fin_ref, s_ref in zip(fin_refs, s_refs):
            fin_ref[...] = s_ref[...]


SCAN_BLOCK_ROWS = 512


def _seq_scan(mixers):
    B, T = mixers[0][3][0].shape[-3], mixers[0][3][0].shape[-2]
    tb = max(8, min(T, SCAN_BLOCK_ROWS // B))
    nt = T // tb
    ub = 2 if B % 2 == 0 else 1
    streams, args, in_specs, s0_args = [], [], [], []
    for w, k, r, v, a, b, s0 in mixers:
        for di in range(2):
            tmap = (lambda j: nt - 1 - j) if di else (lambda j: j)
            streams.append((a is not None, di == 1))
            for arr, blk in [x for x in (w, k, r, v, a, b) if x is not None]:
                args.append(arr)
                if arr.ndim == 4:
                    in_specs.append(pl.BlockSpec((None, B, tb, MIX_W),
                                                 lambda j, di=di, tmap=tmap, blk=blk: (di, 0, tmap(j), blk)))
                else:
                    in_specs.append(pl.BlockSpec((B, tb, MIX_W), lambda j, tmap=tmap, blk=blk: (0, tmap(j), blk)))
            s0_args.append(s0[:, di])
    n = len(streams)
    sspec = pl.BlockSpec((B, 2, HEAD_D, 2 * HEAD_D), lambda j: (0, 0, 0, 0))
    state = jax.ShapeDtypeStruct((B, 2, HEAD_D, 2 * HEAD_D), F32)
    y_specs = [pl.BlockSpec((B, tb, MIX_W), (lambda j: (0, nt - 1 - j, 0)) if rev else (lambda j: (0, j, 0)))
               for _, rev in streams]
    outs = pl.pallas_call(
        functools.partial(_seq_kernel, streams=tuple(streams), nb=B, ub=ub, tb=tb, nt=nt),
        grid=(nt,),
        in_specs=in_specs + [sspec] * n,
        out_specs=y_specs + [sspec] * n,
        out_shape=[jax.ShapeDtypeStruct((B, T, MIX_W), F32)] * n + [state] * n,
        scratch_shapes=[pltpu.VMEM((B, 2, HEAD_D, 2 * HEAD_D), F32)] * n,
        compiler_params=_cparams("arbitrary"),
        name="state_scan",
    )(*args, *s0_args)
    ys, fins = outs[:n], outs[n:]
    return [((ys[2 * m], ys[2 * m + 1]), jnp.stack([fins[2 * m], fins[2 * m + 1]], axis=1))
            for m in range(len(mixers))]


def _pair_tiles(s):
    B = s.shape[0]
    s = s.reshape(B, 2, 2, 2, HEAD_D, HEAD_D).transpose(0, 1, 2, 4, 3, 5)
    return s.reshape(B, 2, 2, HEAD_D, 2 * HEAD_D)


def _unpair_tiles(s):
    B = s.shape[0]
    s = s.reshape(B, 2, 2, HEAD_D, 2, HEAD_D).transpose(0, 1, 2, 4, 3, 5)
    return s.reshape(B, 2, N_HEADS, HEAD_D, HEAD_D)


def _mix_kernel(x_ref, g1_ref, nw_ref, zg_ref,
                ret_ref, retg_ref, s5_ref, s5u_ref, hgf_ref, hgb_ref, hgg_ref, rwf_ref, rwr_ref,
                rwb_ref, rwg_ref,
                vec_ref, glu_w_ref, wbr_ref, wout_ref, o_ref):
    ones_bd = _head_ones()
    gn_w, s5_d, glu_b, hg_w, ln_w, ln_b = (vec_ref[j:j + 1, :] for j in range(6))

    def group_norm(o, eps):
        mu = _head_sum(o, ones_bd) * (1.0 / HEAD_D)
        oc = o - mu
        var = _head_sum(oc * oc, ones_bd) * (1.0 / HEAD_D)
        return oc * lax.rsqrt(var + eps)

    g = retg_ref[...]
    y_ret = group_norm(ret_ref[0] + ret_ref[1], EPS) * gn_w * (g * jax.nn.sigmoid(g))

    y = s5_d * s5u_ref[...] + s5_ref[0] + s5_ref[1]
    yg = jax.nn.gelu(y)
    y_s5 = yg * jax.nn.sigmoid(_dot(yg.astype(BF16), glu_w_ref[...]) + glu_b)

    o = hgf_ref[...] + hgb_ref[...]
    g = hgg_ref[...]
    ms = _head_sum(o * o, ones_bd) * (1.0 / HEAD_D)
    y_hg = o * lax.rsqrt(ms + EPS) * hg_w * (g * jax.nn.sigmoid(g))

    y = group_norm(rwf_ref[...] + rwr_ref[...], RW_LN_EPS) * ln_w + ln_b
    y_rw = (y + rwb_ref[...]) * rwg_ref[...]

    mixed = None
    for m, ym in enumerate((y_ret, y_s5, y_hg, y_rw)):
        br = _dot(ym.astype(BF16), wbr_ref[m])
        term = jax.nn.sigmoid(zg_ref[:, m * D_MODEL:(m + 1) * D_MODEL]) * br
        mixed = term if mixed is None else mixed + term
    mixed = _dot(mixed.astype(BF16), wout_ref[...])
    o_ref[...] = x_ref[...] + g1_ref[...] * _rms(mixed, nw_ref[...])


def _mix(x, g1, nw, z, ret_o, s5_y, hg_o, rw_y, rw_bonus, rw_g, vecs, glu_w, w_branch, w_out):
    B, T, _ = x.shape
    tm = min(T, 256)
    xspec = pl.BlockSpec((None, tm, D_MODEL), lambda b, i: (b, i, 0))
    zspec = lambda blk: pl.BlockSpec((None, tm, MIX_W), lambda b, i: (b, i, blk))
    dspec = pl.BlockSpec((2, None, tm, MIX_W), lambda b, i: (0, b, i, 0))
    const = lambda shape: pl.BlockSpec(shape, lambda b, i: (0,) * len(shape))
    return pl.pallas_call(
        _mix_kernel,
        grid=(B, T // tm),
        in_specs=[
            xspec,
            pl.BlockSpec((None, 1, D_MODEL), _mod_map(g1.shape[0])),
            const((1, D_MODEL)),
            pl.BlockSpec((None, tm, 4 * D_MODEL), lambda b, i: (b, i, 0)),
            dspec, zspec(ZB_RET + 3),
            dspec, zspec(ZB_S5),
            zspec(0), zspec(0), zspec(ZB_HG + 4),
            zspec(0), zspec(0), zspec(0), zspec(0),
            const((6, MIX_W)), const((MIX_W, MIX_W)),
            const((4, MIX_W, D_MODEL)), const((D_MODEL, D_MODEL)),
        ],
        out_specs=xspec,
        out_shape=jax.ShapeDtypeStruct((B, T, D_MODEL), F32),
        compiler_params=_cparams("parallel", "parallel"),
        name="mix_out",
    )(x, g1, nw, z, ret_o, z, s5_y, z, hg_o[0], hg_o[1], z, rw_y[0], rw_y[1], rw_bonus, rw_g,
      vecs, glu_w, w_branch, w_out)


def _ffn_kernel(x_ref, sc_ref, sh_ref, g2_ref, nw2_ref, nw3_ref, w1_ref, w2_ref, o_ref, h_ref, acc_ref):
    j = pl.program_id(2)

    @pl.when(j == 0)
    def _():
        h = _rms(x_ref[...], nw2_ref[...]) * (1.0 + sc_ref[...]) + sh_ref[...]
        h_ref[...] = h.astype(BF16)
        acc_ref[...] = jnp.zeros_like(acc_ref)

    a = jnp.maximum(_dot(h_ref[...], w1_ref[...]), 0.0)
    acc_ref[...] += _dot((a * a).astype(BF16), w2_ref[...])

    @pl.when(j == pl.num_programs(2) - 1)
    def _():
        o_ref[...] = x_ref[...] + g2_ref[...] * _rms(acc_ref[...], nw3_ref[...])


def _ffn(x, sc, sh, g2, nw2, nw3, w1, w2):
    B, T, _ = x.shape
    tm = min(T, 512)
    tf = 1024
    xspec = pl.BlockSpec((None, tm, D_MODEL), lambda b, i, j: (b, i, 0))
    mspec = lambda m: pl.BlockSpec((None, 1, D_MODEL), _mod_map(m.shape[0]))
    const = pl.BlockSpec((1, D_MODEL), lambda b, i, j: (0, 0))
    return pl.pallas_call(
        _ffn_kernel,
        grid=(B, T // tm, D_FF // tf),
        in_specs=[xspec, mspec(sc), mspec(sh), mspec(g2), const, const,
                  pl.BlockSpec((D_MODEL, tf), lambda b, i, j: (0, j)),
                  pl.BlockSpec((tf, D_MODEL), lambda b, i, j: (j, 0))],
        out_specs=xspec,
        out_shape=jax.ShapeDtypeStruct((B, T, D_MODEL), F32),
        scratch_shapes=[pltpu.VMEM((tm, D_MODEL), BF16), pltpu.VMEM((tm, D_MODEL), F32)],
        compiler_params=_cparams("parallel", "parallel", "arbitrary"),
        name="ffn",
    )(x, sc, sh, g2, nw2, nw3, w1, w2)


def _layer_params(l, p):
    w = p['w_in'][l]
    w_p = jnp.concatenate(
        [w[:, 3488:7584], w[:, 2560:3488], jnp.zeros((D_MODEL, Z_COLS - 7584), F32),
         w[:, 0:1024], w[:, 1024:1280], w[:, 1280:2560]], axis=1).astype(BF16)

    a_re, a_im, bb_re, bb_im = _s5_zoh(p['s5_lam_re'][l], p['s5_lam_im'][l], p['s5_log_dt'][l],
                                       p['s5_b_re'][l], p['s5_b_im'][l])
    eye = jnp.eye(S5_GROUPS, dtype=F32)
    bblk = lambda bb: jnp.einsum('dgph,gk->dghkp', bb, eye).reshape(2, MIX_W, S5_STATE).astype(BF16)
    cblk = lambda c: jnp.einsum('ghp,gk->gpkh', c, eye).reshape(S5_STATE, MIX_W).astype(BF16)

    def lora_pad(m, row0):
        return jnp.zeros((MIX_W, MIX_W), F32).at[row0:row0 + m.shape[0]].set(m)

    lora = jnp.stack([lora_pad(p['rw_w2'][l, 0], 0), lora_pad(p['rw_w2'][l, 1], 32),
                      lora_pad(p['rw_a2'][l], 64), lora_pad(p['rw_g2'][l], 96)]).astype(BF16)
    mu_p = jnp.concatenate([p['rw_mu'][l], jnp.zeros((4 * MIX_W - RW_COLS,), F32)]).reshape(1, 4 * MIX_W)
    return dict(
        w_p=w_p, nw=p['norm_w'][l],
        s5=(a_re, a_im, bblk(bb_re), bblk(bb_im), cblk(p['s5_c_re'][l]), cblk(p['s5_c_im'][l])),
        rw_mu=mu_p, rw_lora=lora, rw_w0=p['rw_w0'][l],
        rw_vecs=jnp.stack([p['rw_a0'][l], p['rw_k_k'][l], p['rw_k_a'][l], p['rw_r_k'][l]]),
        mix_vecs=jnp.stack([p['ret_gn_w'][l], p['s5_d'][l], p['s5_glu_b'][l], p['hg_norm_w'][l],
                            p['rw_ln_w'][l], p['rw_ln_b'][l]]),
        glu_w=p['s5_glu_w'][l].astype(BF16),
        w_branch=p['w_branch'][l].astype(BF16), w_out=p['w_out'][l].astype(BF16),
        ff_w1=p['ff_w1'][l].astype(BF16), ff_w2=p['ff_w2'][l].astype(BF16),
    )


def _trunk_layer(x, mod, init, grid_shift, lp, hg_lb):
    B = x.shape[0]
    sh1, sc1, g1, sh2, sc2, g2 = (m[:, None, :] for m in jnp.split(mod, 6, axis=-1))
    nw = lp['nw']
    z = _proj_in(x, sc1, sh1, nw[0:1], lp['w_p'])

    s_ret, s_s5r, s_s5i, s_hg, s_rw = init
    ret_o, f_ret = _retention(z, s_ret)

    s5_y, f_s5r, f_s5i = _s5(z, *lp['s5'], s_s5r.reshape(B, 2, 1, S5_STATE), s_s5i.reshape(B, 2, 1, S5_STATE))
    f_s5r = f_s5r.reshape(B, 2, S5_GROUPS, S5_P)
    f_s5i = f_s5i.reshape(B, 2, S5_GROUPS, S5_P)

    hq, hf, hkey = _hg_prep(z, hg_lb)
    r, k2, v, w, nkk, kka, rw_g, bonus = _rw_prep(z, lp['rw_mu'], lp['rw_vecs'], lp['rw_w0'],
                                                  lp['rw_lora'], grid_shift)
    (hg_o, f_hg), (rw_y, f_rw) = _seq_scan([
        ((hf, 0), (hkey, 0), (hq, 0), (z, ZB_HG + 3), None, None, _pair_tiles(jnp.swapaxes(s_hg, -1, -2))),
        ((w, 0), (k2, 0), (r, 0), (v, 0), (nkk, 0), (kka, 0), _pair_tiles(s_rw))])
    f_hg = jnp.swapaxes(_unpair_tiles(f_hg), -1, -2)
    f_rw = _unpair_tiles(f_rw)

    x = _mix(x, g1, nw[1:2], z, ret_o, s5_y, hg_o, rw_y, bonus, rw_g, lp['mix_vecs'], lp['glu_w'],
             lp['w_branch'], lp['w_out'])
    x = _ffn(x, sc2, sh2, g2, nw[2:3], nw[3:4], lp['ff_w1'], lp['ff_w2'])
    return x, (f_ret, f_s5r, f_s5i, f_hg, f_rw)


def kernel(x_prompt, x_sample, state_ret, state_s5_re, state_s5_im, state_hgrn, state_rwkv, c, c_ctx, ada_w, ada_b, norm_w, w_in, ret_gn_w, s5_lam_re, s5_lam_im, s5_log_dt, s5_b_re, s5_b_im, s5_c_re, s5_c_im, s5_d, s5_glu_w, s5_glu_b, hg_lb, hg_norm_w, rw_mu, rw_w0, rw_w2, rw_a0, rw_a2, rw_g2, rw_k_k, rw_k_a, rw_r_k, rw_ln_w, rw_ln_b, w_branch, w_out, ff_w1, ff_w2):
    p = dict(norm_w=norm_w, w_in=w_in, ret_gn_w=ret_gn_w, s5_lam_re=s5_lam_re, s5_lam_im=s5_lam_im,
             s5_log_dt=s5_log_dt, s5_b_re=s5_b_re, s5_b_im=s5_b_im, s5_c_re=s5_c_re, s5_c_im=s5_c_im,
             s5_d=s5_d, s5_glu_w=s5_glu_w, s5_glu_b=s5_glu_b, hg_norm_w=hg_norm_w, rw_mu=rw_mu,
             rw_w0=rw_w0, rw_w2=rw_w2, rw_a0=rw_a0, rw_a2=rw_a2, rw_g2=rw_g2, rw_k_k=rw_k_k,
             rw_k_a=rw_k_a, rw_r_k=rw_r_k, rw_ln_w=rw_ln_w, rw_ln_b=rw_ln_b, w_branch=w_branch,
             w_out=w_out, ff_w1=ff_w1, ff_w2=ff_w2)
    depth = w_in.shape[0]
    n_ctx = x_prompt.shape[0]
    n_lat = x_sample.shape[0]
    assert 1 + n_lat <= 8

    lb_cum = jnp.cumsum(jax.nn.softmax(hg_lb.astype(F32), axis=0), axis=0)
    hg_lower = lb_cum - lb_cum[0]

    cond = jnp.concatenate([c_ctx[None, :], c, jnp.zeros((7 - n_lat, D_MODEL), F32)], axis=0)
    mod = _ada_mod(cond, ada_w, ada_b)

    zero_state = (jnp.zeros((n_ctx, 2, N_HEADS, HEAD_D, HEAD_D), F32),
                  jnp.zeros((n_ctx, 2, S5_GROUPS, S5_P), F32),
                  jnp.zeros((n_ctx, 2, S5_GROUPS, S5_P), F32),
                  jnp.zeros((n_ctx, 2, N_HEADS, HEAD_D, HEAD_D), F32),
                  jnp.zeros((n_ctx, 2, N_HEADS, HEAD_D, HEAD_D), F32))
    xp, xs = x_prompt, x_sample
    finals = []
    for l in range(depth):
        lp = _layer_params(l, p)
        xp, fin = _trunk_layer(xp, mod[l, 0:1], zero_state, False, lp, hg_lower[l])
        finals.append(fin)
        lat_init = (state_ret[:, l], state_s5_re[:, l], state_s5_im[:, l], state_hgrn[:, l], state_rwkv[:, l])
        xs, _ = _trunk_layer(xs, mod[l, 1:1 + n_lat], lat_init, True, lp, hg_lower[l])
    new_states = tuple(jnp.stack([f[i] for f in finals], axis=1) for i in range(5))
    return (xp, xs) + new_states
```

```python
import functools
import math

import jax
import jax.numpy as jnp
from jax import lax
from jax.experimental import pallas as pl
from jax.experimental.pallas import tpu as pltpu

F32 = jnp.float32
BF16 = jnp.bfloat16
HIGHEST = lax.Precision.HIGHEST

D_MODEL = 1024
GRID_W = 64
MIX_W = 256
N_HEADS = 4
HEAD_D = 64
HEAD_SHIFT = 6
S5_GROUPS = 16
S5_CH = 16
S5_P = 64
S5_STATE = S5_GROUPS * S5_P
D_FF = 4096
EPS = 1e-6
RW_LN_EPS = 64e-5
RW_COLS = 928

Z_COLS = 7680
ZB_GATE = 0
ZB_RW = 16
ZB_RET = 20
ZB_S5 = 24
ZB_HG = 25

VMEM_LIMIT = 56 * 1024 * 1024

LOG_GAMMA = tuple(
    tuple(math.log1p(-2.0 ** (-(5.0 + 0.5 * di) - h)) for h in range(N_HEADS)) for di in range(2))


def _cparams(*sem):
    return pltpu.CompilerParams(dimension_semantics=sem, vmem_limit_bytes=VMEM_LIMIT)


def _dot(a, b):
    return jnp.dot(a, b, preferred_element_type=F32)


def _dot_nt(a, b):
    return lax.dot_general(a, b, (((1,), (1,)), ((), ())), preferred_element_type=F32)


def _dot_tn(a, b):
    return lax.dot_general(a, b, (((0,), (0,)), ((), ())), preferred_element_type=F32)


def _head_ones():
    r = lax.broadcasted_iota(jnp.int32, (MIX_W, MIX_W), 0) >> HEAD_SHIFT
    c = lax.broadcasted_iota(jnp.int32, (MIX_W, MIX_W), 1) >> HEAD_SHIFT
    return jnp.where(r == c, 1.0, 0.0).astype(F32)


def _head_sum(x, ones_bd):
    return jnp.dot(x, ones_bd, precision=HIGHEST, preferred_element_type=F32)


def _rms(x, w):
    return x * lax.rsqrt(jnp.mean(x * x, axis=-1, keepdims=True) + EPS) * w


def _ada_kernel(c_ref, w_ref, b_ref, o_ref):
    c = c_ref[...]
    s = c * jax.nn.sigmoid(c)
    o_ref[...] = _dot(s.astype(BF16), w_ref[...].astype(BF16)) + b_ref[...]


def _ada_mod(cond, ada_w, ada_b):
    L = ada_w.shape[0]
    n = ada_w.shape[2]
    tn = 1536
    return pl.pallas_call(
        _ada_kernel,
        grid=(L, n // tn),
        in_specs=[
            pl.BlockSpec((8, D_MODEL), lambda l, j: (0, 0)),
            pl.BlockSpec((None, D_MODEL, tn), lambda l, j: (l, 0, j)),
            pl.BlockSpec((None, 1, tn), lambda l, j: (l, 0, j)),
        ],
        out_specs=pl.BlockSpec((None, 8, tn), lambda l, j: (l, 0, j)),
        out_shape=jax.ShapeDtypeStruct((L, 8, n), F32),
        compiler_params=_cparams("parallel", "parallel"),
        name="ada_mod",
    )(cond, ada_w, ada_b.reshape(L, 1, n))


def _proj_in_kernel(x_ref, sc_ref, sh_ref, nw_ref, w_ref, z_ref, h_ref):
    @pl.when(pl.program_id(2) == 0)
    def _():
        h = _rms(x_ref[...], nw_ref[...]) * (1.0 + sc_ref[...]) + sh_ref[...]
        h_ref[...] = h.astype(BF16)

    z_ref[...] = _dot(h_ref[...], w_ref[...])


def _mod_map(bm):
    if bm == 1:
        return lambda b, *_: (0, 0, 0)
    return lambda b, *_: (b, 0, 0)


def _proj_in(x, sc, sh, nw, w_p):
    B, T, _ = x.shape
    tm = min(T, 512)
    tn = 1920
    return pl.pallas_call(
        _proj_in_kernel,
        grid=(B, T // tm, Z_COLS // tn),
        in_specs=[
            pl.BlockSpec((None, tm, D_MODEL), lambda b, i, j: (b, i, 0)),
            pl.BlockSpec((None, 1, D_MODEL), _mod_map(sc.shape[0])),
            pl.BlockSpec((None, 1, D_MODEL), _mod_map(sh.shape[0])),
            pl.BlockSpec((1, D_MODEL), lambda b, i, j: (0, 0)),
            pl.BlockSpec((D_MODEL, tn), lambda b, i, j: (0, j)),
        ],
        out_specs=pl.BlockSpec((None, tm, tn), lambda b, i, j: (b, i, j)),
        out_shape=jax.ShapeDtypeStruct((B, T, Z_COLS), F32),
        scratch_shapes=[pltpu.VMEM((tm, D_MODEL), BF16)],
        compiler_params=_cparams("parallel", "parallel", "arbitrary"),
        name="proj_in",
    )(x, sc, sh, nw, w_p)


def _ret_kernel(q_ref, k_ref, v_ref, s0_ref, o_ref, fin_ref, s_ref, *, lc, nc):
    d = pl.program_id(0)
    c = pl.program_id(2)

    @pl.when(c == 0)
    def _():
        s_ref[...] = s0_ref[...]

    rows = lax.broadcasted_iota(jnp.int32, (lc, lc), 0)
    cols = lax.broadcasted_iota(jnp.int32, (lc, lc), 1)
    rel = jnp.where(d == 0, rows - cols, cols - rows).astype(F32)
    idx = lax.broadcasted_iota(jnp.int32, (lc, 1), 0)
    pos = jnp.where(d == 0, idx, lc - 1 - idx).astype(F32)
    for h in range(N_HEADS):
        lg = jnp.where(d == 0, LOG_GAMMA[0][h], LOG_GAMMA[1][h]).astype(F32)
        sl = slice(h * HEAD_D, (h + 1) * HEAD_D)
        q = q_ref[:, sl]
        k = k_ref[:, sl] * (HEAD_D ** -0.5)
        v = v_ref[:, sl].astype(BF16)
        att = _dot_nt(q.astype(BF16), k.astype(BF16))
        att = att * jnp.where(rel >= 0.0, jnp.exp(jnp.maximum(rel, 0.0) * lg), 0.0)
        qd = q * jnp.exp((pos + 1.0) * lg)
        kd = k * jnp.exp((lc - 1.0 - pos) * lg)
        s = s_ref[h]
        o_ref[:, sl] = _dot(att.astype(BF16), v) + _dot(qd.astype(BF16), s.astype(BF16))
        s_ref[h] = s * jnp.exp(lc * lg) + _dot_tn(kd.astype(BF16), v)

    @pl.when(c == nc - 1)
    def _():
        fin_ref[...] = s_ref[...]


def _retention(z, s0):
    B, T, _ = z.shape
    lc = min(T, 256)
    nc = T // lc

    def tmap(d, c):
        return jnp.where(d == 0, c, nc - 1 - c)

    def zspec(blk):
        return pl.BlockSpec((None, lc, MIX_W), lambda d, b, c: (b, tmap(d, c), blk))

    return pl.pallas_call(
        functools.partial(_ret_kernel, lc=lc, nc=nc),
        grid=(2, B, nc),
        in_specs=[
            zspec(ZB_RET), zspec(ZB_RET + 1), zspec(ZB_RET + 2),
            pl.BlockSpec((None, None, N_HEADS, HEAD_D, HEAD_D), lambda d, b, c: (b, d, 0, 0, 0)),
        ],
        out_specs=[
            pl.BlockSpec((None, None, lc, MIX_W), lambda d, b, c: (d, b, tmap(d, c), 0)),
            pl.BlockSpec((None, None, N_HEADS, HEAD_D, HEAD_D), lambda d, b, c: (b, d, 0, 0, 0)),
        ],
        out_shape=[
            jax.ShapeDtypeStruct((2, B, T, MIX_W), F32),
            jax.ShapeDtypeStruct((B, 2, N_HEADS, HEAD_D, HEAD_D), F32),
        ],
        scratch_shapes=[pltpu.VMEM((N_HEADS, HEAD_D, HEAD_D), F32)],
        compiler_params=_cparams("arbitrary", "arbitrary", "arbitrary"),
        name="retention",
    )(z, z, z, s0)


def _s5_zoh_kernel(lre_ref, lim_ref, ldt_ref, bre_ref, bim_ref, are_ref, aim_ref, bbre_ref, bbim_ref):
    lam_re = jnp.minimum(lre_ref[...], -1e-4)
    lam_im = lim_ref[...]
    dt = jnp.exp(ldt_ref[...])
    mag = jnp.exp(dt * lam_re)
    ang = dt * lam_im
    a_re = mag * jnp.cos(ang)
    a_im = mag * jnp.sin(ang)
    den = lam_re * lam_re + lam_im * lam_im
    f_re = ((a_re - 1.0) * lam_re + a_im * lam_im) / den
    f_im = (a_im * lam_re - (a_re - 1.0) * lam_im) / den
    b_re = bre_ref[...]
    b_im = bim_ref[...]
    are_ref[...] = a_re
    aim_ref[...] = a_im
    bbre_ref[...] = f_re * b_re - f_im * b_im
    bbim_ref[...] = f_re * b_im + f_im * b_re


def _s5_zoh(lam_re, lam_im, log_dt, b_re, b_im):
    n = 2 * S5_STATE
    col = lambda t: t.reshape(n, 1)
    ldt = jnp.broadcast_to(log_dt[:, :, None], (2, S5_GROUPS, S5_P))
    outs = pl.pallas_call(
        _s5_zoh_kernel,
        out_shape=[jax.ShapeDtypeStruct((n, 1), F32), jax.ShapeDtypeStruct((n, 1), F32),
                   jax.ShapeDtypeStruct((n, S5_CH), F32), jax.ShapeDtypeStruct((n, S5_CH), F32)],
        name="s5_zoh",
    )(col(lam_re), col(lam_im), col(ldt), b_re.reshape(n, S5_CH), b_im.reshape(n, S5_CH))
    a_re, a_im, bb_re, bb_im = outs
    shp = (2, S5_GROUPS, S5_P, S5_CH)
    return a_re.reshape(2, 1, S5_STATE), a_im.reshape(2, 1, S5_STATE), bb_re.reshape(shp), bb_im.reshape(shp)


def _s5_kernel(u_ref, are_ref, aim_ref, bre_ref, bim_ref, cre_ref, cim_ref, x0r_ref, x0i_ref,
               y_ref, fr_ref, fi_ref,
               xa_re, xa_im, xb_re, xb_im, p_re, p_im, car_re, car_im, *, L, nc):
    d = pl.program_id(0)
    c = pl.program_id(2)
    pad = L
    nsteps = L.bit_length() - 1
    rows = lax.broadcasted_iota(jnp.int32, (L, 1), 0)

    def scan(bu_re, bu_im, rev):
        bufs = ((xa_re, xa_im), (xb_re, xb_im))
        bufs[0][0][pad:pad + L, :] = bu_re
        bufs[0][1][pad:pad + L, :] = bu_im
        ar = are_ref[...]
        ai = aim_ref[...]
        for ks in range(nsteps):
            src, dst = bufs[ks % 2], bufs[(ks + 1) % 2]
            off = pad + (1 << ks) if rev else pad - (1 << ks)
            sre = src[0][off:off + L, :]
            sim = src[1][off:off + L, :]
            re = src[0][pad:pad + L, :]
            im = src[1][pad:pad + L, :]
            dst[0][pad:pad + L, :] = re + ar * sre - ai * sim
            dst[1][pad:pad + L, :] = im + ar * sim + ai * sre
            ar, ai = ar * ar - ai * ai, 2.0 * ar * ai
        fin = bufs[nsteps % 2]
        return fin[0][pad:pad + L, :], fin[1][pad:pad + L, :]

    def body(rev):
        @pl.when(c == 0)
        def _():
            zeros = jnp.zeros((pad, S5_STATE), F32)
            for buf in (xa_re, xa_im, xb_re, xb_im):
                buf[0:pad, :] = zeros
                buf[pad + L:2 * pad + L, :] = zeros
            first = rows == (L - 1 if rev else 0)
            pr, pi = scan(jnp.where(first, are_ref[...], 0.0), jnp.where(first, aim_ref[...], 0.0), rev)
            p_re[...] = pr
            p_im[...] = pi
            car_re[...] = x0r_ref[...]
            car_im[...] = x0i_ref[...]

        u = u_ref[...].astype(BF16)
        xr, xi = scan(_dot(u, bre_ref[...]), _dot(u, bim_ref[...]), rev)
        cr = car_re[...]
        ci = car_im[...]
        pr = p_re[...]
        pi = p_im[...]
        xr, xi = xr + pr * cr - pi * ci, xi + pr * ci + pi * cr
        y_ref[...] = _dot(xr.astype(BF16), cre_ref[...]) - _dot(xi.astype(BF16), cim_ref[...])
        last = 0 if rev else L - 1
        car_re[...] = xr[last:last + 1, :]
        car_im[...] = xi[last:last + 1, :]

    @pl.when(d == 0)
    def _():
        body(False)

    @pl.when(d == 1)
    def _():
        body(True)

    @pl.when(c == nc - 1)
    def _():
        fr_ref[...] = car_re[...]
        fi_ref[...] = car_im[...]


def _s5(z, a_re, a_im, bblk_re, bblk_im, cblk_re, cblk_im, x0_re, x0_im):
    B, T, _ = z.shape
    L = min(T, 128)
    nc = T // L

    def tmap(d, c):
        return jnp.where(d == 0, c, nc - 1 - c)

    dspec = lambda shape: pl.BlockSpec((None,) + shape, lambda d, b, c: (d, 0, 0))
    sspec = pl.BlockSpec((None, None, 1, S5_STATE), lambda d, b, c: (b, d, 0, 0))
    buf = pltpu.VMEM((3 * L, S5_STATE), F32)
    return pl.pallas_call(
        functools.partial(_s5_kernel, L=L, nc=nc),
        grid=(2, B, nc),
        in_specs=[
            pl.BlockSpec((None, L, MIX_W), lambda d, b, c: (b, tmap(d, c), ZB_S5)),
            dspec((1, S5_STATE)), dspec((1, S5_STATE)),
            dspec((MIX_W, S5_STATE)), dspec((MIX_W, S5_STATE)),
            pl.BlockSpec((S5_STATE, MIX_W), lambda d, b, c: (0, 0)),
            pl.BlockSpec((S5_STATE, MIX_W), lambda d, b, c: (0, 0)),
            sspec, sspec,
        ],
        out_specs=[
            pl.BlockSpec((None, None, L, MIX_W), lambda d, b, c: (d, b, tmap(d, c), 0)),
            sspec, sspec,
        ],
        out_shape=[
            jax.ShapeDtypeStruct((2, B, T, MIX_W), F32),
            jax.ShapeDtypeStruct((B, 2, 1, S5_STATE), F32),
            jax.ShapeDtypeStruct((B, 2, 1, S5_STATE), F32),
        ],
        scratch_shapes=[buf, buf, buf, buf,
                        pltpu.VMEM((L, S5_STATE), F32), pltpu.VMEM((L, S5_STATE), F32),
                        pltpu.VMEM((1, S5_STATE), F32), pltpu.VMEM((1, S5_STATE), F32)],
        compiler_params=_cparams("arbitrary", "arbitrary", "arbitrary"),
        name="s5_scan",
    )(z, a_re, a_im, bblk_re, bblk_im, cblk_re, cblk_im, x0_re, x0_im)


GLA_CHUNK = 16
GLA_CHUNK_SHIFT = 4
SUB = 8


def _gla_kernel(q_ref, zf_ref, v_ref, lb_ref, s0_ref, o_ref, fin_ref, s_ref, a_ref, qs_ref, key_ref,
                *, rev, tb, nt):
    jb = pl.program_id(1)
    ch = GLA_CHUNK
    half = 2 * HEAD_D

    @pl.when(jb == 0)
    def _():
        s_ref[...] = s0_ref[...]

    lb = lb_ref[...]
    zf = zf_ref[...]
    q = q_ref[...]
    qs_ref[...] = q * jax.nn.sigmoid(q)
    key_ref[...] = (1.0 - lb) * jax.nn.sigmoid(-zf)
    l1 = jnp.log(lb)
    l2 = jnp.log1p(-lb) + jnp.minimum(zf, 0.0) - jnp.log1p(jnp.exp(-jnp.abs(zf)))
    lf = jnp.maximum(l1, l2) + jnp.log1p(jnp.exp(-jnp.abs(l1 - l2)))
    r = lax.broadcasted_iota(jnp.int32, (tb, tb), 0)
    c = lax.broadcasted_iota(jnp.int32, (tb, tb), 1)
    same = (r >> GLA_CHUNK_SHIFT) == (c >> GLA_CHUNK_SHIFT)
    tri = jnp.where(same & ((c >= r) if rev else (c <= r)), 1.0, 0.0).astype(BF16)
    p1 = lf.astype(BF16)
    r1 = lf - p1.astype(F32)
    p2 = r1.astype(BF16)
    p3 = (r1 - p2.astype(F32)).astype(BF16)
    a_ref[...] = _dot(tri, p1) + _dot(tri, p2) + _dot(tri, p3)

    rows = lax.broadcasted_iota(jnp.int32, (SUB, half), 0)
    lo = lax.broadcasted_iota(jnp.int32, (SUB, half), 1) < HEAD_D
    same_head = ((lax.broadcasted_iota(jnp.int32, (half, half), 0) >> HEAD_SHIFT)
                 == (lax.broadcasted_iota(jnp.int32, (half, half), 1) >> HEAD_SHIFT))

    def pair_sums(p):
        s0 = jnp.sum(jnp.where(lo, p, 0.0), axis=1, keepdims=True)
        s1 = jnp.sum(jnp.where(lo, 0.0, p), axis=1, keepdims=True)
        return jnp.where(lo, s0, s1)

    def chunk(ci, carry):
        t0 = pl.multiple_of((tb // ch - 1 - ci if rev else ci) * ch, ch)
        win = pl.ds(t0, ch)
        a = a_ref[win, :]
        qc = qs_ref[win, :]
        kc = key_ref[win, :]
        vc = v_ref[win, :]
        last = 0 if rev else ch - 1
        a_last = a[last:last + 1, :]
        qe = (qc * jnp.exp(a)).astype(BF16)
        ke = (kc * jnp.exp(a_last - a)).astype(BF16)
        ea = jnp.exp(a_last)
        vb = vc.astype(BF16)
        nslab = ch // SUB
        acc = [[jnp.zeros((SUB, half), F32) for _ in range(2)] for _ in range(nslab)]
        for jj in range(ch):
            for sb in range(nslab):
                r0 = sb * SUB
                if (r0 > jj) if rev else (r0 + SUB - 1 < jj):
                    continue
                whole = (r0 + SUB - 1 <= jj) if rev else (r0 >= jj)
                rs = slice(r0, r0 + SUB)
                valid = (rows + r0 <= jj) if rev else (rows + r0 >= jj)
                for hp in range(2):
                    sl = slice(hp * half, (hp + 1) * half)
                    dec = jnp.exp(jnp.minimum(a[rs, sl] - a[jj:jj + 1, sl], 0.0))
                    p = qc[rs, sl] * dec * kc[jj:jj + 1, sl]
                    if not whole:
                        p = jnp.where(valid, p, 0.0)
                    acc[sb][hp] = acc[sb][hp] + pair_sums(p) * vc[jj:jj + 1, sl]
        for hp in range(2):
            sl = slice(hp * half, (hp + 1) * half)
            s = s_ref[hp]
            inter = _dot_nt(qe[:, sl], s.astype(BF16))
            intra = jnp.concatenate([acc[sb][hp] for sb in range(nslab)], axis=0)
            o_ref[win, sl] = inter + intra
            s_ref[hp] = s * ea[:, sl] + jnp.where(same_head, _dot_tn(vb[:, sl], ke[:, sl]), 0.0)
        return carry

    lax.fori_loop(0, tb // ch, chunk, 0, unroll=2)

    @pl.when(jb == nt - 1)
    def _():
        fin_ref[...] = s_ref[...]


def _gla(z, lb, s0, rev):
    B, T, _ = z.shape
    pair = 2 * HEAD_D
    s0 = s0.reshape(B, 2, 2, HEAD_D, HEAD_D)
    zero = jnp.zeros_like(s0[:, :, 0])
    s0 = jnp.concatenate([jnp.concatenate([s0[:, :, 0], zero], axis=-1),
                          jnp.concatenate([zero, s0[:, :, 1]], axis=-1)], axis=-2)
    tb = min(T, 256)
    nt = T // tb
    di = 1 if rev else 0
    tmap = (lambda j: nt - 1 - j) if rev else (lambda j: j)
    zspec = lambda blk: pl.BlockSpec((None, tb, MIX_W), lambda b, j: (b, tmap(j), blk))
    sspec = pl.BlockSpec((None, 2, pair, pair), lambda b, j: (b, 0, 0, 0))
    blk = pltpu.VMEM((tb, MIX_W), F32)
    o, fin = pl.pallas_call(
        functools.partial(_gla_kernel, rev=rev, tb=tb, nt=nt),
        grid=(B, nt),
        in_specs=[zspec(ZB_HG), zspec(ZB_HG + 1 + di), zspec(ZB_HG + 3),
                  pl.BlockSpec((1, MIX_W), lambda b, j: (0, 0)), sspec],
        out_specs=[pl.BlockSpec((None, tb, MIX_W), lambda b, j: (b, tmap(j), 0)), sspec],
        out_shape=[jax.ShapeDtypeStruct((B, T, MIX_W), F32),
                   jax.ShapeDtypeStruct((B, 2, pair, pair), F32)],
        scratch_shapes=[pltpu.VMEM((2, pair, pair), F32), blk, blk, blk],
        compiler_params=_cparams("parallel", "arbitrary"),
        name="hgrn_gla_bwd" if rev else "hgrn_gla_fwd",
    )(z, z, z, lb[di:di + 1], s0)
    fin = jnp.stack([fin[:, :, :HEAD_D, :HEAD_D], fin[:, :, HEAD_D:, HEAD_D:]], axis=2)
    return o, fin.reshape(B, N_HEADS, HEAD_D, HEAD_D)


def _rw_prep_kernel(*refs, grid_shift, tm):
    if grid_shift:
        (zc_ref, zu_ref, zd_ref, mu_ref, vec_ref, w0_ref, lora_ref,
         r_ref, k2_ref, v_ref, w_ref, nkk_ref, kka_ref, g_ref, bonus_ref, buf_ref) = refs
    else:
        (zc_ref, mu_ref, vec_ref, w0_ref, lora_ref,
         r_ref, k2_ref, v_ref, w_ref, nkk_ref, kka_ref, g_ref, bonus_ref, buf_ref) = refs
    i = pl.program_id(1)
    nt = pl.num_programs(1)
    halo = GRID_W
    width = 4 * MIX_W
    z = zc_ref[...]
    buf_ref[halo:halo + tm, :] = z
    lane = lax.broadcasted_iota(jnp.int32, (tm, width), 1)
    row = lax.broadcasted_iota(jnp.int32, (tm, width), 0)
    if grid_shift:
        buf_ref[0:halo, :] = jnp.where(i > 0, zu_ref[...], 0.0)
        buf_ref[halo + tm:2 * halo + tm, :] = jnp.where(i < nt - 1, zd_ref[...], 0.0)
        col = row & (GRID_W - 1)
        left = jnp.where(col > 0, buf_ref[halo - 1:halo - 1 + tm, :], 0.0)
        right = jnp.where(col < GRID_W - 1, buf_ref[halo + 1:halo + 1 + tm, :], 0.0)
        up = buf_ref[0:tm, :]
        down = buf_ref[2 * halo:2 * halo + tm, :]
        sel = lane & 3
        shifted = jnp.where(sel == 0, left, jnp.where(sel == 1, right, jnp.where(sel == 2, up, down)))
    else:
        zrow = jnp.zeros((1, width), F32)
        buf_ref[halo - 1:halo, :] = zrow
        buf_ref[halo + tm:halo + tm + 1, :] = zrow
        prev = buf_ref[halo - 1:halo - 1 + tm, :]
        nxt = buf_ref[halo + 1:halo + 1 + tm, :]
        shifted = jnp.where((lane & 1) == 0, prev, nxt)
    zs = z + mu_ref[...] * (shifted - z)
    r = zs[:, 0:MIX_W]
    k = zs[:, MIX_W:2 * MIX_W]
    v = zs[:, 2 * MIX_W:3 * MIX_W]
    sm = zs[:, 3 * MIX_W:4 * MIX_W]
    ones_bd = _head_ones()
    a0, k_k, k_a, r_k = (vec_ref[j:j + 1, :] for j in range(4))
    a = jax.nn.sigmoid(a0 + _dot(sm.astype(BF16), lora_ref[2]))
    g_ref[...] = _dot(jax.nn.sigmoid(sm).astype(BF16), lora_ref[3])
    kk = k * k_k
    kk = kk * lax.rsqrt(_head_sum(kk * kk, ones_bd) + 1e-12)
    k2 = k * (1.0 + (a - 1.0) * k_a)
    th = jnp.tanh(sm).astype(BF16)
    for di in range(2):
        log_w = -math.exp(-0.5) * jax.nn.sigmoid(w0_ref[di:di + 1, :] + _dot(th, lora_ref[di]))
        w_ref[di] = jnp.exp(log_w)
    r_ref[...] = r
    k2_ref[...] = k2
    v_ref[...] = v
    nkk_ref[...] = -kk
    kka_ref[...] = kk * a
    bonus_ref[...] = _head_sum(r * k2 * r_k, ones_bd) * v


def _rw_prep(z, mu_p, vecs, w0, lora, grid_shift):
    B, T, _ = z.shape
    width = 4 * MIX_W
    wblk = ZB_RW // 4
    if grid_shift:
        tm = min(T, 512)
        hb = tm // GRID_W
        nh = T // GRID_W
        z_specs = [
            pl.BlockSpec((None, tm, width), lambda b, i: (b, i, wblk)),
            pl.BlockSpec((None, GRID_W, width), lambda b, i: (b, jnp.maximum(i * hb - 1, 0), wblk)),
            pl.BlockSpec((None, GRID_W, width), lambda b, i: (b, jnp.minimum((i + 1) * hb, nh - 1), wblk)),
        ]
        z_args = (z, z, z)
    else:
        tm = T
        z_specs = [pl.BlockSpec((None, tm, width), lambda b, i: (b, i, wblk))]
        z_args = (z,)
    const = lambda shape: pl.BlockSpec(shape, lambda b, i: (0,) * len(shape))
    ospec = pl.BlockSpec((None, tm, MIX_W), lambda b, i: (b, i, 0))
    oshape = jax.ShapeDtypeStruct((B, T, MIX_W), F32)
    dspec = pl.BlockSpec((2, None, tm, MIX_W), lambda b, i: (0, b, i, 0))
    dshape = jax.ShapeDtypeStruct((2, B, T, MIX_W), F32)
    return pl.pallas_call(
        functools.partial(_rw_prep_kernel, grid_shift=grid_shift, tm=tm),
        grid=(B, T // tm),
        in_specs=z_specs + [const((1, width)), const((4, MIX_W)), const((2, MIX_W)),
                            const((4, MIX_W, MIX_W))],
        out_specs=[ospec, ospec, ospec, dspec, ospec, ospec, ospec, ospec],
        out_shape=[oshape, oshape, oshape, dshape, oshape, oshape, oshape, oshape],
        scratch_shapes=[pltpu.VMEM((tm + 2 * GRID_W, width), F32)],
        compiler_params=_cparams("parallel", "parallel"),
        name="rwkv_prep",
    )(*z_args, mu_p, vecs, w0, lora)


def _seq_kernel(*refs, streams, nb, ub, tb, nt):
    refs = list(refs)
    take = lambda n: [refs.pop(0) for _ in range(n)]
    in_refs = [take(6 if use_sa else 4) for use_sa, _ in streams]
    s0_refs = take(len(streams))
    y_refs = take(len(streams))
    fin_refs = take(len(streams))
    s_refs = take(len(streams))
    j = pl.program_id(0)
    pair_w = 2 * HEAD_D
    group = 8

    @pl.when(j == 0)
    def _():
        for s_ref, s0_ref in zip(s_refs, s0_refs):
            s_ref[...] = s0_ref[...]

    lane = lax.broadcasted_iota(jnp.int32, (HEAD_D, pair_w), 1)
    sub = lax.broadcasted_iota(jnp.int32, (HEAD_D, pair_w), 0)
    lo = lane < HEAD_D
    diag = (lane & (HEAD_D - 1)) == sub
    r2 = lax.broadcasted_iota(jnp.int32, (2 * pair_w, pair_w), 0)
    c2 = lax.broadcasted_iota(jnp.int32, (2 * pair_w, pair_w), 1)
    ones2 = jnp.where(((r2 >> HEAD_SHIFT) & 1) == (c2 >> HEAD_SHIFT), 1.0, 0.0).astype(BF16)

    def split(x):
        hi = x.astype(BF16).astype(F32)
        return hi, (x - hi).astype(BF16).astype(F32)

    def seg_sum_mxu(hi, lo_):
        return _dot(jnp.concatenate([hi.astype(BF16), lo_.astype(BF16)], axis=1), ones2)

    def seg_sum_xlu(p):
        s_lo = jnp.sum(jnp.where(lo, p, 0.0), axis=1, keepdims=True)
        s_hi = jnp.sum(jnp.where(lo, 0.0, p), axis=1, keepdims=True)
        return jnp.where(lo, s_lo, s_hi)

    def chain(si, b, p, g):
        use_sa, rev = streams[si]
        t0 = pl.multiple_of((tb // group - 1 - g if rev else g) * group, group)
        sl = pl.ds(p * pair_w, pair_w)
        blk = lambda ref: ref[b, pl.ds(t0, group), sl]
        w, k, r, v = (blk(ref) for ref in in_refs[si][:4])
        if use_sa:
            a, bb = blk(in_refs[si][4]), blk(in_refs[si][5])
            v_hi, v_lo = split(v)
        s = s_refs[si][b, p]
        ys = [None] * group
        for i in range(group):
            t = group - 1 - i if rev else i
            row = lambda x: x[t:t + 1, :]
            if use_sa:
                vcol = seg_sum_mxu(jnp.where(diag, row(v_hi), 0.0), jnp.where(diag, row(v_lo), 0.0))
                s = s * row(w) + seg_sum_xlu(s * row(a)) * row(bb)
            else:
                vcol = seg_sum_xlu(jnp.where(diag, row(v), 0.0))
                s = s * row(w)
            s = s + vcol * row(k)
            yb = seg_sum_mxu(*split(s * row(r)))
            ys[t] = jnp.sum(jnp.where(diag, yb, 0.0), axis=0, keepdims=True)
        s_refs[si][b, p] = s
        y_refs[si][b, pl.ds(t0, group), sl] = jnp.concatenate(ys, axis=0)

    def group_step(g, carry):
        def batch_step(bg, c):
            for si in range(len(streams)):
                for u in range(ub):
                    for p in range(2):
                        chain(si, bg * ub + u, p, g)
            return c

        return lax.fori_loop(0, nb // ub, batch_step, carry)

    lax.fori_loop(0, tb // group, group_step, 0)

    @pl.when(j == nt - 1)
    def _():
        for fin_ref, s_ref in zip(fin_refs, s_refs):
            fin_ref[...] = s_ref[...]


SCAN_BLOCK_ROWS = 512
SCAN_CHAINS = 16


def _seq_scan(mixers):
    B, T = mixers[0][3][0].shape[-3], mixers[0][3][0].shape[-2]
    tb = max(8, min(T, SCAN_BLOCK_ROWS // B))
    nt = T // tb
    ub = max(u for u in (1, 2, 4, 8) if B % u == 0 and u * 4 * len(mixers) <= max(SCAN_CHAINS, 4 * len(mixers)))
    streams, args, in_specs, s0_args = [], [], [], []
    for w, k, r, v, a, b, s0 in mixers:
        for di in range(2):
            tmap = (lambda j: nt - 1 - j) if di else (lambda j: j)
            streams.append((a is not None, di == 1))
            for arr, blk in [x for x in (w, k, r, v, a, b) if x is not None]:
                args.append(arr)
                if arr.ndim == 4:
                    in_specs.append(pl.BlockSpec((None, B, tb, MIX_W),
                                                 lambda j, di=di, tmap=tmap, blk=blk: (di, 0, tmap(j), blk)))
                else:
                    in_specs.append(pl.BlockSpec((B, tb, MIX_W), lambda j, tmap=tmap, blk=blk: (0, tmap(j), blk)))
            s0_args.append(s0[:, di])
    n = len(streams)
    sspec = pl.BlockSpec((B, 2, HEAD_D, 2 * HEAD_D), lambda j: (0, 0, 0, 0))
    state = jax.ShapeDtypeStruct((B, 2, HEAD_D, 2 * HEAD_D), F32)
    y_specs = [pl.BlockSpec((B, tb, MIX_W), (lambda j: (0, nt - 1 - j, 0)) if rev else (lambda j: (0, j, 0)))
               for _, rev in streams]
    outs = pl.pallas_call(
        functools.partial(_seq_kernel, streams=tuple(streams), nb=B, ub=ub, tb=tb, nt=nt),
        grid=(nt,),
        in_specs=in_specs + [sspec] * n,
        out_specs=y_specs + [sspec] * n,
        out_shape=[jax.ShapeDtypeStruct((B, T, MIX_W), F32)] * n + [state] * n,
        scratch_shapes=[pltpu.VMEM((B, 2, HEAD_D, 2 * HEAD_D), F32)] * n,
        compiler_params=_cparams("arbitrary"),
        name="state_scan",
    )(*args, *s0_args)
    ys, fins = outs[:n], outs[n:]
    return [((ys[2 * m], ys[2 * m + 1]), jnp.stack([fins[2 * m], fins[2 * m + 1]], axis=1))
            for m in range(len(mixers))]


def _pair_tiles(s):
    B = s.shape[0]
    s = s.reshape(B, 2, 2, 2, HEAD_D, HEAD_D).transpose(0, 1, 2, 4, 3, 5)
    return s.reshape(B, 2, 2, HEAD_D, 2 * HEAD_D)


def _unpair_tiles(s):
    B = s.shape[0]
    s = s.reshape(B, 2, 2, HEAD_D, 2, HEAD_D).transpose(0, 1, 2, 4, 3, 5)
    return s.reshape(B, 2, N_HEADS, HEAD_D, HEAD_D)


def _mix_kernel(x_ref, g1_ref, nw_ref, zg_ref,
                ret_ref, retg_ref, s5_ref, s5u_ref, hgf_ref, hgb_ref, hgg_ref, rwf_ref, rwr_ref,
                rwb_ref, rwg_ref,
                vec_ref, glu_w_ref, wbr_ref, wout_ref, o_ref):
    ones_bd = _head_ones()
    gn_w, s5_d, glu_b, hg_w, ln_w, ln_b = (vec_ref[j:j + 1, :] for j in range(6))

    def group_norm(o, eps):
        mu = _head_sum(o, ones_bd) * (1.0 / HEAD_D)
        oc = o - mu
        var = _head_sum(oc * oc, ones_bd) * (1.0 / HEAD_D)
        return oc * lax.rsqrt(var + eps)

    g = retg_ref[...]
    y_ret = group_norm(ret_ref[0] + ret_ref[1], EPS) * gn_w * (g * jax.nn.sigmoid(g))

    y = s5_d * s5u_ref[...] + s5_ref[0] + s5_ref[1]
    yg = jax.nn.gelu(y)
    y_s5 = yg * jax.nn.sigmoid(_dot(yg.astype(BF16), glu_w_ref[...]) + glu_b)

    o = hgf_ref[...] + hgb_ref[...]
    g = hgg_ref[...]
    ms = _head_sum(o * o, ones_bd) * (1.0 / HEAD_D)
    y_hg = o * lax.rsqrt(ms + EPS) * hg_w * (g * jax.nn.sigmoid(g))

    y = group_norm(rwf_ref[...] + rwr_ref[...], RW_LN_EPS) * ln_w + ln_b
    y_rw = (y + rwb_ref[...]) * rwg_ref[...]

    mixed = None
    for m, ym in enumerate((y_ret, y_s5, y_hg, y_rw)):
        br = _dot(ym.astype(BF16), wbr_ref[m])
        term = jax.nn.sigmoid(zg_ref[:, m * D_MODEL:(m + 1) * D_MODEL]) * br
        mixed = term if mixed is None else mixed + term
    mixed = _dot(mixed.astype(BF16), wout_ref[...])
    o_ref[...] = x_ref[...] + g1_ref[...] * _rms(mixed, nw_ref[...])


def _mix(x, g1, nw, z, ret_o, s5_y, hg_o, rw_y, rw_bonus, rw_g, vecs, glu_w, w_branch, w_out):
    B, T, _ = x.shape
    tm = min(T, 256)
    xspec = pl.BlockSpec((None, tm, D_MODEL), lambda b, i: (b, i, 0))
    zspec = lambda blk: pl.BlockSpec((None, tm, MIX_W), lambda b, i: (b, i, blk))
    dspec = pl.BlockSpec((2, None, tm, MIX_W), lambda b, i: (0, b, i, 0))
    const = lambda shape: pl.BlockSpec(shape, lambda b, i: (0,) * len(shape))
    return pl.pallas_call(
        _mix_kernel,
        grid=(B, T // tm),
        in_specs=[
            xspec,
            pl.BlockSpec((None, 1, D_MODEL), _mod_map(g1.shape[0])),
            const((1, D_MODEL)),
            pl.BlockSpec((None, tm, 4 * D_MODEL), lambda b, i: (b, i, 0)),
            dspec, zspec(ZB_RET + 3),
            dspec, zspec(ZB_S5),
            zspec(0), zspec(0), zspec(ZB_HG + 4),
            zspec(0), zspec(0), zspec(0), zspec(0),
            const((6, MIX_W)), const((MIX_W, MIX_W)),
            const((4, MIX_W, D_MODEL)), const((D_MODEL, D_MODEL)),
        ],
        out_specs=xspec,
        out_shape=jax.ShapeDtypeStruct((B, T, D_MODEL), F32),
        compiler_params=_cparams("parallel", "parallel"),
        name="mix_out",
    )(x, g1, nw, z, ret_o, z, s5_y, z, hg_o[0], hg_o[1], z, rw_y[0], rw_y[1], rw_bonus, rw_g,
      vecs, glu_w, w_branch, w_out)


def _ffn_kernel(x_ref, sc_ref, sh_ref, g2_ref, nw2_ref, nw3_ref, w1_ref, w2_ref, o_ref, h_ref, acc_ref):
    j = pl.program_id(2)

    @pl.when(j == 0)
    def _():
        h = _rms(x_ref[...], nw2_ref[...]) * (1.0 + sc_ref[...]) + sh_ref[...]
        h_ref[...] = h.astype(BF16)
        acc_ref[...] = jnp.zeros_like(acc_ref)

    a = jnp.maximum(_dot(h_ref[...], w1_ref[...]), 0.0)
    acc_ref[...] += _dot((a * a).astype(BF16), w2_ref[...])

    @pl.when(j == pl.num_programs(2) - 1)
    def _():
        o_ref[...] = x_ref[...] + g2_ref[...] * _rms(acc_ref[...], nw3_ref[...])


def _ffn(x, sc, sh, g2, nw2, nw3, w1, w2):
    B, T, _ = x.shape
    tm = min(T, 512)
    tf = 1024
    xspec = pl.BlockSpec((None, tm, D_MODEL), lambda b, i, j: (b, i, 0))
    mspec = lambda m: pl.BlockSpec((None, 1, D_MODEL), _mod_map(m.shape[0]))
    const = pl.BlockSpec((1, D_MODEL), lambda b, i, j: (0, 0))
    return pl.pallas_call(
        _ffn_kernel,
        grid=(B, T // tm, D_FF // tf),
        in_specs=[xspec, mspec(sc), mspec(sh), mspec(g2), const, const,
                  pl.BlockSpec((D_MODEL, tf), lambda b, i, j: (0, j)),
                  pl.BlockSpec((tf, D_MODEL), lambda b, i, j: (j, 0))],
        out_specs=xspec,
        out_shape=jax.ShapeDtypeStruct((B, T, D_MODEL), F32),
        scratch_shapes=[pltpu.VMEM((tm, D_MODEL), BF16), pltpu.VMEM((tm, D_MODEL), F32)],
        compiler_params=_cparams("parallel", "parallel", "arbitrary"),
        name="ffn",
    )(x, sc, sh, g2, nw2, nw3, w1, w2)


def _layer_params(l, p):
    w = p['w_in'][l]
    w_p = jnp.concatenate(
        [w[:, 3488:7584], w[:, 2560:3488], jnp.zeros((D_MODEL, Z_COLS - 7584), F32),
         w[:, 0:1024], w[:, 1024:1280], w[:, 1280:2560]], axis=1).astype(BF16)

    a_re, a_im, bb_re, bb_im = _s5_zoh(p['s5_lam_re'][l], p['s5_lam_im'][l], p['s5_log_dt'][l],
                                       p['s5_b_re'][l], p['s5_b_im'][l])
    eye = jnp.eye(S5_GROUPS, dtype=F32)
    bblk = lambda bb: jnp.einsum('dgph,gk->dghkp', bb, eye).reshape(2, MIX_W, S5_STATE).astype(BF16)
    cblk = lambda c: jnp.einsum('ghp,gk->gpkh', c, eye).reshape(S5_STATE, MIX_W).astype(BF16)

    def lora_pad(m, row0):
        return jnp.zeros((MIX_W, MIX_W), F32).at[row0:row0 + m.shape[0]].set(m)

    lora = jnp.stack([lora_pad(p['rw_w2'][l, 0], 0), lora_pad(p['rw_w2'][l, 1], 32),
                      lora_pad(p['rw_a2'][l], 64), lora_pad(p['rw_g2'][l], 96)]).astype(BF16)
    mu_p = jnp.concatenate([p['rw_mu'][l], jnp.zeros((4 * MIX_W - RW_COLS,), F32)]).reshape(1, 4 * MIX_W)
    return dict(
        w_p=w_p, nw=p['norm_w'][l],
        s5=(a_re, a_im, bblk(bb_re), bblk(bb_im), cblk(p['s5_c_re'][l]), cblk(p['s5_c_im'][l])),
        rw_mu=mu_p, rw_lora=lora, rw_w0=p['rw_w0'][l],
        rw_vecs=jnp.stack([p['rw_a0'][l], p['rw_k_k'][l], p['rw_k_a'][l], p['rw_r_k'][l]]),
        mix_vecs=jnp.stack([p['ret_gn_w'][l], p['s5_d'][l], p['s5_glu_b'][l], p['hg_norm_w'][l],
                            p['rw_ln_w'][l], p['rw_ln_b'][l]]),
        glu_w=p['s5_glu_w'][l].astype(BF16),
        w_branch=p['w_branch'][l].astype(BF16), w_out=p['w_out'][l].astype(BF16),
        ff_w1=p['ff_w1'][l].astype(BF16), ff_w2=p['ff_w2'][l].astype(BF16),
    )


def _trunk_layer(x, mod, init, grid_shift, lp, hg_lb):
    B = x.shape[0]
    sh1, sc1, g1, sh2, sc2, g2 = (m[:, None, :] for m in jnp.split(mod, 6, axis=-1))
    nw = lp['nw']
    z = _proj_in(x, sc1, sh1, nw[0:1], lp['w_p'])

    s_ret, s_s5r, s_s5i, s_hg, s_rw = init
    ret_o, f_ret = _retention(z, s_ret)

    s5_y, f_s5r, f_s5i = _s5(z, *lp['s5'], s_s5r.reshape(B, 2, 1, S5_STATE), s_s5i.reshape(B, 2, 1, S5_STATE))
    f_s5r = f_s5r.reshape(B, 2, S5_GROUPS, S5_P)
    f_s5i = f_s5i.reshape(B, 2, S5_GROUPS, S5_P)

    s_hg_t = jnp.swapaxes(s_hg, -1, -2)
    hg_f, f_hg_f = _gla(z, hg_lb, s_hg_t[:, 0], False)
    hg_b, f_hg_b = _gla(z, hg_lb, s_hg_t[:, 1], True)
    hg_o = (hg_f, hg_b)
    f_hg = jnp.swapaxes(jnp.stack([f_hg_f, f_hg_b], axis=1), -1, -2)

    r, k2, v, w, nkk, kka, rw_g, bonus = _rw_prep(z, lp['rw_mu'], lp['rw_vecs'], lp['rw_w0'],
                                                  lp['rw_lora'], grid_shift)
    ((rw_y, f_rw),) = _seq_scan([((w, 0), (k2, 0), (r, 0), (v, 0), (nkk, 0), (kka, 0), _pair_tiles(s_rw))])
    f_rw = _unpair_tiles(f_rw)

    x = _mix(x, g1, nw[1:2], z, ret_o, s5_y, hg_o, rw_y, bonus, rw_g, lp['mix_vecs'], lp['glu_w'],
             lp['w_branch'], lp['w_out'])
    x = _ffn(x, sc2, sh2, g2, nw[2:3], nw[3:4], lp['ff_w1'], lp['ff_w2'])
    return x, (f_ret, f_s5r, f_s5i, f_hg, f_rw)


def kernel(x_prompt, x_sample, state_ret, state_s5_re, state_s5_im, state_hgrn, state_rwkv, c, c_ctx, ada_w, ada_b, norm_w, w_in, ret_gn_w, s5_lam_re, s5_lam_im, s5_log_dt, s5_b_re, s5_b_im, s5_c_re, s5_c_im, s5_d, s5_glu_w, s5_glu_b, hg_lb, hg_norm_w, rw_mu, rw_w0, rw_w2, rw_a0, rw_a2, rw_g2, rw_k_k, rw_k_a, rw_r_k, rw_ln_w, rw_ln_b, w_branch, w_out, ff_w1, ff_w2):
    p = dict(norm_w=norm_w, w_in=w_in, ret_gn_w=ret_gn_w, s5_lam_re=s5_lam_re, s5_lam_im=s5_lam_im,
             s5_log_dt=s5_log_dt, s5_b_re=s5_b_re, s5_b_im=s5_b_im, s5_c_re=s5_c_re, s5_c_im=s5_c_im,
             s5_d=s5_d, s5_glu_w=s5_glu_w, s5_glu_b=s5_glu_b, hg_norm_w=hg_norm_w, rw_mu=rw_mu,
             rw_w0=rw_w0, rw_w2=rw_w2, rw_a0=rw_a0, rw_a2=rw_a2, rw_g2=rw_g2, rw_k_k=rw_k_k,
             rw_k_a=rw_k_a, rw_r_k=rw_r_k, rw_ln_w=rw_ln_w, rw_ln_b=rw_ln_b, w_branch=w_branch,
             w_out=w_out, ff_w1=ff_w1, ff_w2=ff_w2)
    depth = w_in.shape[0]
    n_ctx = x_prompt.shape[0]
    n_lat = x_sample.shape[0]
    assert 1 + n_lat <= 8

    lb_cum = jnp.cumsum(jax.nn.softmax(hg_lb.astype(F32), axis=0), axis=0)
    hg_lower = lb_cum - lb_cum[0]

    cond = jnp.concatenate([c_ctx[None, :], c, jnp.zeros((7 - n_lat, D_MODEL), F32)], axis=0)
    mod = _ada_mod(cond, ada_w, ada_b)

    zero_state = (jnp.zeros((n_ctx, 2, N_HEADS, HEAD_D, HEAD_D), F32),
                  jnp.zeros((n_ctx, 2, S5_GROUPS, S5_P), F32),
                  jnp.zeros((n_ctx, 2, S5_GROUPS, S5_P), F32),
                  jnp.zeros((n_ctx, 2, N_HEADS, HEAD_D, HEAD_D), F32),
                  jnp.zeros((n_ctx, 2, N_HEADS, HEAD_D, HEAD_D), F32))
    xp, xs = x_prompt, x_sample
    finals = []
    for l in range(depth):
        lp = _layer_params(l, p)
        xp, fin = _trunk_layer(xp, mod[l, 0:1], zero_state, False, lp, hg_lower[l])
        finals.append(fin)
        lat_init = (state_ret[:, l], state_s5_re[:, l], state_s5_im[:, l], state_hgrn[:, l], state_rwkv[:, l])
        xs, _ = _trunk_layer(xs, mod[l, 1:1 + n_lat], lat_init, True, lp, hg_lower[l])
    new_states = tuple(jnp.stack([f[i] for f in finals], axis=1) for i in range(5))
    return (xp, xs) + new_states
```

```python
import functools
import math

import jax
import jax.numpy as jnp
from jax import lax
from jax.experimental import pallas as pl
from jax.experimental.pallas import tpu as pltpu

F32 = jnp.float32
BF16 = jnp.bfloat16
HIGHEST = lax.Precision.HIGHEST

D_MODEL = 1024
GRID_W = 64
MIX_W = 256
N_HEADS = 4
HEAD_D = 64
HEAD_SHIFT = 6
S5_GROUPS = 16
S5_CH = 16
S5_P = 64
S5_STATE = S5_GROUPS * S5_P
D_FF = 4096
EPS = 1e-6
RW_LN_EPS = 64e-5
RW_COLS = 928

Z_COLS = 7680
ZB_GATE = 0
ZB_RW = 16
ZB_RET = 20
ZB_S5 = 24
ZB_HG = 25

VMEM_LIMIT = 56 * 1024 * 1024

LOG_GAMMA = tuple(
    tuple(math.log1p(-2.0 ** (-(5.0 + 0.5 * di) - h)) for h in range(N_HEADS)) for di in range(2))


def _cparams(*sem):
    return pltpu.CompilerParams(dimension_semantics=sem, vmem_limit_bytes=VMEM_LIMIT)


def _dot(a, b):
    return jnp.dot(a, b, preferred_element_type=F32)


def _dot_nt(a, b):
    return lax.dot_general(a, b, (((1,), (1,)), ((), ())), preferred_element_type=F32)


def _dot_tn(a, b):
    return lax.dot_general(a, b, (((0,), (0,)), ((), ())), preferred_element_type=F32)


def _head_ones():
    r = lax.broadcasted_iota(jnp.int32, (MIX_W, MIX_W), 0) >> HEAD_SHIFT
    c = lax.broadcasted_iota(jnp.int32, (MIX_W, MIX_W), 1) >> HEAD_SHIFT
    return jnp.where(r == c, 1.0, 0.0).astype(F32)


def _head_sum(x, ones_bd):
    return jnp.dot(x, ones_bd, precision=HIGHEST, preferred_element_type=F32)


def _rms(x, w):
    return x * lax.rsqrt(jnp.mean(x * x, axis=-1, keepdims=True) + EPS) * w


def _ada_kernel(c_ref, w_ref, b_ref, o_ref):
    c = c_ref[...]
    s = c * jax.nn.sigmoid(c)
    o_ref[...] = _dot(s.astype(BF16), w_ref[...].astype(BF16)) + b_ref[...]


def _ada_mod(cond, ada_w, ada_b):
    L = ada_w.shape[0]
    n = ada_w.shape[2]
    tn = 1536
    return pl.pallas_call(
        _ada_kernel,
        grid=(L, n // tn),
        in_specs=[
            pl.BlockSpec((8, D_MODEL), lambda l, j: (0, 0)),
            pl.BlockSpec((None, D_MODEL, tn), lambda l, j: (l, 0, j)),
            pl.BlockSpec((None, 1, tn), lambda l, j: (l, 0, j)),
        ],
        out_specs=pl.BlockSpec((None, 8, tn), lambda l, j: (l, 0, j)),
        out_shape=jax.ShapeDtypeStruct((L, 8, n), F32),
        compiler_params=_cparams("parallel", "parallel"),
        name="ada_mod",
    )(cond, ada_w, ada_b.reshape(L, 1, n))


def _proj_in_kernel(x_ref, sc_ref, sh_ref, nw_ref, w_ref, z_ref, h_ref):
    @pl.when(pl.program_id(2) == 0)
    def _():
        h = _rms(x_ref[...], nw_ref[...]) * (1.0 + sc_ref[...]) + sh_ref[...]
        h_ref[...] = h.astype(BF16)

    z_ref[...] = _dot(h_ref[...], w_ref[...])


def _mod_map(bm):
    if bm == 1:
        return lambda b, *_: (0, 0, 0)
    return lambda b, *_: (b, 0, 0)


def _proj_in(x, sc, sh, nw, w_p):
    B, T, _ = x.shape
    tm = min(T, 512)
    tn = 1920
    return pl.pallas_call(
        _proj_in_kernel,
        grid=(B, T // tm, Z_COLS // tn),
        in_specs=[
            pl.BlockSpec((None, tm, D_MODEL), lambda b, i, j: (b, i, 0)),
            pl.BlockSpec((None, 1, D_MODEL), _mod_map(sc.shape[0])),
            pl.BlockSpec((None, 1, D_MODEL), _mod_map(sh.shape[0])),
            pl.BlockSpec((1, D_MODEL), lambda b, i, j: (0, 0)),
            pl.BlockSpec((D_MODEL, tn), lambda b, i, j: (0, j)),
        ],
        out_specs=pl.BlockSpec((None, tm, tn), lambda b, i, j: (b, i, j)),
        out_shape=jax.ShapeDtypeStruct((B, T, Z_COLS), F32),
        scratch_shapes=[pltpu.VMEM((tm, D_MODEL), BF16)],
        compiler_params=_cparams("parallel", "parallel", "arbitrary"),
        name="proj_in",
    )(x, sc, sh, nw, w_p)


def _ret_kernel(q_ref, k_ref, v_ref, s0_ref, o_ref, fin_ref, s_ref, *, lc, nc):
    d = pl.program_id(0)
    c = pl.program_id(2)

    @pl.when(c == 0)
    def _():
        s_ref[...] = s0_ref[...]

    rows = lax.broadcasted_iota(jnp.int32, (lc, lc), 0)
    cols = lax.broadcasted_iota(jnp.int32, (lc, lc), 1)
    rel = jnp.where(d == 0, rows - cols, cols - rows).astype(F32)
    idx = lax.broadcasted_iota(jnp.int32, (lc, 1), 0)
    pos = jnp.where(d == 0, idx, lc - 1 - idx).astype(F32)
    for h in range(N_HEADS):
        lg = jnp.where(d == 0, LOG_GAMMA[0][h], LOG_GAMMA[1][h]).astype(F32)
        sl = slice(h * HEAD_D, (h + 1) * HEAD_D)
        q = q_ref[:, sl]
        k = k_ref[:, sl] * (HEAD_D ** -0.5)
        v = v_ref[:, sl].astype(BF16)
        att = _dot_nt(q.astype(BF16), k.astype(BF16))
        att = att * jnp.where(rel >= 0.0, jnp.exp(jnp.maximum(rel, 0.0) * lg), 0.0)
        qd = q * jnp.exp((pos + 1.0) * lg)
        kd = k * jnp.exp((lc - 1.0 - pos) * lg)
        s = s_ref[h]
        o_ref[:, sl] = _dot(att.astype(BF16), v) + _dot(qd.astype(BF16), s.astype(BF16))
        s_ref[h] = s * jnp.exp(lc * lg) + _dot_tn(kd.astype(BF16), v)

    @pl.when(c == nc - 1)
    def _():
        fin_ref[...] = s_ref[...]


def _retention(z, s0):
    B, T, _ = z.shape
    lc = min(T, 256)
    nc = T // lc

    def tmap(d, c):
        return jnp.where(d == 0, c, nc - 1 - c)

    def zspec(blk):
        return pl.BlockSpec((None, lc, MIX_W), lambda d, b, c: (b, tmap(d, c), blk))

    return pl.pallas_call(
        functools.partial(_ret_kernel, lc=lc, nc=nc),
        grid=(2, B, nc),
        in_specs=[
            zspec(ZB_RET), zspec(ZB_RET + 1), zspec(ZB_RET + 2),
            pl.BlockSpec((None, None, N_HEADS, HEAD_D, HEAD_D), lambda d, b, c: (b, d, 0, 0, 0)),
        ],
        out_specs=[
            pl.BlockSpec((None, None, lc, MIX_W), lambda d, b, c: (d, b, tmap(d, c), 0)),
            pl.BlockSpec((None, None, N_HEADS, HEAD_D, HEAD_D), lambda d, b, c: (b, d, 0, 0, 0)),
        ],
        out_shape=[
            jax.ShapeDtypeStruct((2, B, T, MIX_W), F32),
            jax.ShapeDtypeStruct((B, 2, N_HEADS, HEAD_D, HEAD_D), F32),
        ],
        scratch_shapes=[pltpu.VMEM((N_HEADS, HEAD_D, HEAD_D), F32)],
        compiler_params=_cparams("arbitrary", "arbitrary", "arbitrary"),
        name="retention",
    )(z, z, z, s0)


def _s5_zoh_kernel(lre_ref, lim_ref, ldt_ref, bre_ref, bim_ref, are_ref, aim_ref, bbre_ref, bbim_ref):
    lam_re = jnp.minimum(lre_ref[...], -1e-4)
    lam_im = lim_ref[...]
    dt = jnp.exp(ldt_ref[...])
    mag = jnp.exp(dt * lam_re)
    ang = dt * lam_im
    a_re = mag * jnp.cos(ang)
    a_im = mag * jnp.sin(ang)
    den = lam_re * lam_re + lam_im * lam_im
    f_re = ((a_re - 1.0) * lam_re + a_im * lam_im) / den
    f_im = (a_im * lam_re - (a_re - 1.0) * lam_im) / den
    b_re = bre_ref[...]
    b_im = bim_ref[...]
    are_ref[...] = a_re
    aim_ref[...] = a_im
    bbre_ref[...] = f_re * b_re - f_im * b_im
    bbim_ref[...] = f_re * b_im + f_im * b_re


def _s5_zoh(lam_re, lam_im, log_dt, b_re, b_im):
    n = 2 * S5_STATE
    col = lambda t: t.reshape(n, 1)
    ldt = jnp.broadcast_to(log_dt[:, :, None], (2, S5_GROUPS, S5_P))
    outs = pl.pallas_call(
        _s5_zoh_kernel,
        out_shape=[jax.ShapeDtypeStruct((n, 1), F32), jax.ShapeDtypeStruct((n, 1), F32),
                   jax.ShapeDtypeStruct((n, S5_CH), F32), jax.ShapeDtypeStruct((n, S5_CH), F32)],
        name="s5_zoh",
    )(col(lam_re), col(lam_im), col(ldt), b_re.reshape(n, S5_CH), b_im.reshape(n, S5_CH))
    a_re, a_im, bb_re, bb_im = outs
    shp = (2, S5_GROUPS, S5_P, S5_CH)
    return a_re.reshape(2, 1, S5_STATE), a_im.reshape(2, 1, S5_STATE), bb_re.reshape(shp), bb_im.reshape(shp)


def _s5_kernel(u_ref, are_ref, aim_ref, bre_ref, bim_ref, cre_ref, cim_ref, x0r_ref, x0i_ref,
               y_ref, fr_ref, fi_ref,
               xa_re, xa_im, xb_re, xb_im, p_re, p_im, car_re, car_im, *, L, nc):
    d = pl.program_id(0)
    c = pl.program_id(2)
    pad = L
    nsteps = L.bit_length() - 1
    rows = lax.broadcasted_iota(jnp.int32, (L, 1), 0)

    def scan(bu_re, bu_im, rev):
        bufs = ((xa_re, xa_im), (xb_re, xb_im))
        bufs[0][0][pad:pad + L, :] = bu_re
        bufs[0][1][pad:pad + L, :] = bu_im
        ar = are_ref[...]
        ai = aim_ref[...]
        for ks in range(nsteps):
            src, dst = bufs[ks % 2], bufs[(ks + 1) % 2]
            off = pad + (1 << ks) if rev else pad - (1 << ks)
            sre = src[0][off:off + L, :]
            sim = src[1][off:off + L, :]
            re = src[0][pad:pad + L, :]
            im = src[1][pad:pad + L, :]
            dst[0][pad:pad + L, :] = re + ar * sre - ai * sim
            dst[1][pad:pad + L, :] = im + ar * sim + ai * sre
            ar, ai = ar * ar - ai * ai, 2.0 * ar * ai
        fin = bufs[nsteps % 2]
        return fin[0][pad:pad + L, :], fin[1][pad:pad + L, :]

    def body(rev):
        @pl.when(c == 0)
        def _():
            zeros = jnp.zeros((pad, S5_STATE), F32)
            for buf in (xa_re, xa_im, xb_re, xb_im):
                buf[0:pad, :] = zeros
                buf[pad + L:2 * pad + L, :] = zeros
            first = rows == (L - 1 if rev else 0)
            pr, pi = scan(jnp.where(first, are_ref[...], 0.0), jnp.where(first, aim_ref[...], 0.0), rev)
            p_re[...] = pr
            p_im[...] = pi
            car_re[...] = x0r_ref[...]
            car_im[...] = x0i_ref[...]

        u = u_ref[...].astype(BF16)
        xr, xi = scan(_dot(u, bre_ref[...]), _dot(u, bim_ref[...]), rev)
        cr = car_re[...]
        ci = car_im[...]
        pr = p_re[...]
        pi = p_im[...]
        xr, xi = xr + pr * cr - pi * ci, xi + pr * ci + pi * cr
        y_ref[...] = _dot(xr.astype(BF16), cre_ref[...]) - _dot(xi.astype(BF16), cim_ref[...])
        last = 0 if rev else L - 1
        car_re[...] = xr[last:last + 1, :]
        car_im[...] = xi[last:last + 1, :]

    @pl.when(d == 0)
    def _():
        body(False)

    @pl.when(d == 1)
    def _():
        body(True)

    @pl.when(c == nc - 1)
    def _():
        fr_ref[...] = car_re[...]
        fi_ref[...] = car_im[...]


def _s5(z, a_re, a_im, bblk_re, bblk_im, cblk_re, cblk_im, x0_re, x0_im):
    B, T, _ = z.shape
    L = min(T, 128)
    nc = T // L

    def tmap(d, c):
        return jnp.where(d == 0, c, nc - 1 - c)

    dspec = lambda shape: pl.BlockSpec((None,) + shape, lambda d, b, c: (d, 0, 0))
    sspec = pl.BlockSpec((None, None, 1, S5_STATE), lambda d, b, c: (b, d, 0, 0))
    buf = pltpu.VMEM((3 * L, S5_STATE), F32)
    return pl.pallas_call(
        functools.partial(_s5_kernel, L=L, nc=nc),
        grid=(2, B, nc),
        in_specs=[
            pl.BlockSpec((None, L, MIX_W), lambda d, b, c: (b, tmap(d, c), ZB_S5)),
            dspec((1, S5_STATE)), dspec((1, S5_STATE)),
            dspec((MIX_W, S5_STATE)), dspec((MIX_W, S5_STATE)),
            pl.BlockSpec((S5_STATE, MIX_W), lambda d, b, c: (0, 0)),
            pl.BlockSpec((S5_STATE, MIX_W), lambda d, b, c: (0, 0)),
            sspec, sspec,
        ],
        out_specs=[
            pl.BlockSpec((None, None, L, MIX_W), lambda d, b, c: (d, b, tmap(d, c), 0)),
            sspec, sspec,
        ],
        out_shape=[
            jax.ShapeDtypeStruct((2, B, T, MIX_W), F32),
            jax.ShapeDtypeStruct((B, 2, 1, S5_STATE), F32),
            jax.ShapeDtypeStruct((B, 2, 1, S5_STATE), F32),
        ],
        scratch_shapes=[buf, buf, buf, buf,
                        pltpu.VMEM((L, S5_STATE), F32), pltpu.VMEM((L, S5_STATE), F32),
                        pltpu.VMEM((1, S5_STATE), F32), pltpu.VMEM((1, S5_STATE), F32)],
        compiler_params=_cparams("arbitrary", "arbitrary", "arbitrary"),
        name="s5_scan",
    )(z, a_re, a_im, bblk_re, bblk_im, cblk_re, cblk_im, x0_re, x0_im)


GLA_CHUNK = 16
GLA_CHUNK_SHIFT = 4
SUB = 8


def _gla_kernel(q_ref, zf_ref, v_ref, lb_ref, s0_ref, o_ref, fin_ref, s_ref, a_ref, qs_ref, key_ref,
                *, rev, tb, nt):
    jb = pl.program_id(1)
    ch = GLA_CHUNK
    half = 2 * HEAD_D

    @pl.when(jb == 0)
    def _():
        s_ref[...] = s0_ref[...]

    lb = lb_ref[...]
    zf = zf_ref[...]
    q = q_ref[...]
    qs_ref[...] = q * jax.nn.sigmoid(q)
    key_ref[...] = (1.0 - lb) * jax.nn.sigmoid(-zf)
    l1 = jnp.log(lb)
    l2 = jnp.log1p(-lb) + jnp.minimum(zf, 0.0) - jnp.log1p(jnp.exp(-jnp.abs(zf)))
    lf = jnp.maximum(l1, l2) + jnp.log1p(jnp.exp(-jnp.abs(l1 - l2)))
    r = lax.broadcasted_iota(jnp.int32, (tb, tb), 0)
    c = lax.broadcasted_iota(jnp.int32, (tb, tb), 1)
    same = (r >> GLA_CHUNK_SHIFT) == (c >> GLA_CHUNK_SHIFT)
    tri = jnp.where(same & ((c >= r) if rev else (c <= r)), 1.0, 0.0).astype(BF16)
    p1 = lf.astype(BF16)
    r1 = lf - p1.astype(F32)
    p2 = r1.astype(BF16)
    p3 = (r1 - p2.astype(F32)).astype(BF16)
    a_ref[...] = _dot(tri, p1) + _dot(tri, p2) + _dot(tri, p3)

    rows = lax.broadcasted_iota(jnp.int32, (SUB, half), 0)
    lo = lax.broadcasted_iota(jnp.int32, (SUB, half), 1) < HEAD_D
    same_head = ((lax.broadcasted_iota(jnp.int32, (half, half), 0) >> HEAD_SHIFT)
                 == (lax.broadcasted_iota(jnp.int32, (half, half), 1) >> HEAD_SHIFT))

    def pair_sums(p):
        s0 = jnp.sum(jnp.where(lo, p, 0.0), axis=1, keepdims=True)
        s1 = jnp.sum(jnp.where(lo, 0.0, p), axis=1, keepdims=True)
        return jnp.where(lo, s0, s1)

    def chunk(ci, carry):
        t0 = pl.multiple_of((tb // ch - 1 - ci if rev else ci) * ch, ch)
        win = pl.ds(t0, ch)
        a = a_ref[win, :]
        qc = qs_ref[win, :]
        kc = key_ref[win, :]
        vc = v_ref[win, :]
        last = 0 if rev else ch - 1
        a_last = a[last:last + 1, :]
        qe = (qc * jnp.exp(a)).astype(BF16)
        ke = (kc * jnp.exp(a_last - a)).astype(BF16)
        ea = jnp.exp(a_last)
        vb = vc.astype(BF16)
        nslab = ch // SUB
        acc = [[jnp.zeros((SUB, half), F32) for _ in range(2)] for _ in range(nslab)]
        for jj in range(ch):
            for sb in range(nslab):
                r0 = sb * SUB
                if (r0 > jj) if rev else (r0 + SUB - 1 < jj):
                    continue
                whole = (r0 + SUB - 1 <= jj) if rev else (r0 >= jj)
                rs = slice(r0, r0 + SUB)
                valid = (rows + r0 <= jj) if rev else (rows + r0 >= jj)
                for hp in range(2):
                    sl = slice(hp * half, (hp + 1) * half)
                    dec = jnp.exp(jnp.minimum(a[rs, sl] - a[jj:jj + 1, sl], 0.0))
                    p = qc[rs, sl] * dec * kc[jj:jj + 1, sl]
                    if not whole:
                        p = jnp.where(valid, p, 0.0)
                    acc[sb][hp] = acc[sb][hp] + pair_sums(p) * vc[jj:jj + 1, sl]
        for hp in range(2):
            sl = slice(hp * half, (hp + 1) * half)
            s = s_ref[hp]
            inter = _dot_nt(qe[:, sl], s.astype(BF16))
            intra = jnp.concatenate([acc[sb][hp] for sb in range(nslab)], axis=0)
            o_ref[win, sl] = inter + intra
            s_ref[hp] = s * ea[:, sl] + jnp.where(same_head, _dot_tn(vb[:, sl], ke[:, sl]), 0.0)
        return carry

    lax.fori_loop(0, tb // ch, chunk, 0, unroll=2)

    @pl.when(jb == nt - 1)
    def _():
        fin_ref[...] = s_ref[...]


def _gla(z, lb, s0, rev):
    B, T, _ = z.shape
    pair = 2 * HEAD_D
    s0 = s0.reshape(B, 2, 2, HEAD_D, HEAD_D)
    zero = jnp.zeros_like(s0[:, :, 0])
    s0 = jnp.concatenate([jnp.concatenate([s0[:, :, 0], zero], axis=-1),
                          jnp.concatenate([zero, s0[:, :, 1]], axis=-1)], axis=-2)
    tb = min(T, 256)
    nt = T // tb
    di = 1 if rev else 0
    tmap = (lambda j: nt - 1 - j) if rev else (lambda j: j)
    zspec = lambda blk: pl.BlockSpec((None, tb, MIX_W), lambda b, j: (b, tmap(j), blk))
    sspec = pl.BlockSpec((None, 2, pair, pair), lambda b, j: (b, 0, 0, 0))
    blk = pltpu.VMEM((tb, MIX_W), F32)
    o, fin = pl.pallas_call(
        functools.partial(_gla_kernel, rev=rev, tb=tb, nt=nt),
        grid=(B, nt),
        in_specs=[zspec(ZB_HG), zspec(ZB_HG + 1 + di), zspec(ZB_HG + 3),
                  pl.BlockSpec((1, MIX_W), lambda b, j: (0, 0)), sspec],
        out_specs=[pl.BlockSpec((None, tb, MIX_W), lambda b, j: (b, tmap(j), 0)), sspec],
        out_shape=[jax.ShapeDtypeStruct((B, T, MIX_W), F32),
                   jax.ShapeDtypeStruct((B, 2, pair, pair), F32)],
        scratch_shapes=[pltpu.VMEM((2, pair, pair), F32), blk, blk, blk],
        compiler_params=_cparams("parallel", "arbitrary"),
        name="hgrn_gla_bwd" if rev else "hgrn_gla_fwd",
    )(z, z, z, lb[di:di + 1], s0)
    fin = jnp.stack([fin[:, :, :HEAD_D, :HEAD_D], fin[:, :, HEAD_D:, HEAD_D:]], axis=2)
    return o, fin.reshape(B, N_HEADS, HEAD_D, HEAD_D)


def _rw_prep_kernel(*refs, grid_shift, tm):
    if grid_shift:
        (zc_ref, zu_ref, zd_ref, mu_ref, vec_ref, w0_ref, lora_ref,
         r_ref, k2_ref, v_ref, w_ref, nkk_ref, kka_ref, g_ref, bonus_ref, buf_ref) = refs
    else:
        (zc_ref, mu_ref, vec_ref, w0_ref, lora_ref,
         r_ref, k2_ref, v_ref, w_ref, nkk_ref, kka_ref, g_ref, bonus_ref, buf_ref) = refs
    i = pl.program_id(1)
    nt = pl.num_programs(1)
    halo = GRID_W
    width = 4 * MIX_W
    z = zc_ref[...]
    buf_ref[halo:halo + tm, :] = z
    lane = lax.broadcasted_iota(jnp.int32, (tm, width), 1)
    row = lax.broadcasted_iota(jnp.int32, (tm, width), 0)
    if grid_shift:
        buf_ref[0:halo, :] = jnp.where(i > 0, zu_ref[...], 0.0)
        buf_ref[halo + tm:2 * halo + tm, :] = jnp.where(i < nt - 1, zd_ref[...], 0.0)
        col = row & (GRID_W - 1)
        left = jnp.where(col > 0, buf_ref[halo - 1:halo - 1 + tm, :], 0.0)
        right = jnp.where(col < GRID_W - 1, buf_ref[halo + 1:halo + 1 + tm, :], 0.0)
        up = buf_ref[0:tm, :]
        down = buf_ref[2 * halo:2 * halo + tm, :]
        sel = lane & 3
        shifted = jnp.where(sel == 0, left, jnp.where(sel == 1, right, jnp.where(sel == 2, up, down)))
    else:
        zrow = jnp.zeros((1, width), F32)
        buf_ref[halo - 1:halo, :] = zrow
        buf_ref[halo + tm:halo + tm + 1, :] = zrow
        prev = buf_ref[halo - 1:halo - 1 + tm, :]
        nxt = buf_ref[halo + 1:halo + 1 + tm, :]
        shifted = jnp.where((lane & 1) == 0, prev, nxt)
    zs = z + mu_ref[...] * (shifted - z)
    r = zs[:, 0:MIX_W]
    k = zs[:, MIX_W:2 * MIX_W]
    v = zs[:, 2 * MIX_W:3 * MIX_W]
    sm = zs[:, 3 * MIX_W:4 * MIX_W]
    ones_bd = _head_ones()
    a0, k_k, k_a, r_k = (vec_ref[j:j + 1, :] for j in range(4))
    a = jax.nn.sigmoid(a0 + _dot(sm.astype(BF16), lora_ref[2]))
    g_ref[...] = _dot(jax.nn.sigmoid(sm).astype(BF16), lora_ref[3])
    kk = k * k_k
    kk = kk * lax.rsqrt(_head_sum(kk * kk, ones_bd) + 1e-12)
    k2 = k * (1.0 + (a - 1.0) * k_a)
    th = jnp.tanh(sm).astype(BF16)
    for di in range(2):
        w_ref[di] = -math.exp(-0.5) * jax.nn.sigmoid(w0_ref[di:di + 1, :] + _dot(th, lora_ref[di]))
    r_ref[...] = r
    k2_ref[...] = k2
    v_ref[...] = v
    nkk_ref[...] = -kk
    kka_ref[...] = kk * a
    bonus_ref[...] = _head_sum(r * k2 * r_k, ones_bd) * v


def _rw_prep(z, mu_p, vecs, w0, lora, grid_shift):
    B, T, _ = z.shape
    width = 4 * MIX_W
    wblk = ZB_RW // 4
    if grid_shift:
        tm = min(T, 512)
        hb = tm // GRID_W
        nh = T // GRID_W
        z_specs = [
            pl.BlockSpec((None, tm, width), lambda b, i: (b, i, wblk)),
            pl.BlockSpec((None, GRID_W, width), lambda b, i: (b, jnp.maximum(i * hb - 1, 0), wblk)),
            pl.BlockSpec((None, GRID_W, width), lambda b, i: (b, jnp.minimum((i + 1) * hb, nh - 1), wblk)),
        ]
        z_args = (z, z, z)
    else:
        tm = T
        z_specs = [pl.BlockSpec((None, tm, width), lambda b, i: (b, i, wblk))]
        z_args = (z,)
    const = lambda shape: pl.BlockSpec(shape, lambda b, i: (0,) * len(shape))
    ospec = pl.BlockSpec((None, tm, MIX_W), lambda b, i: (b, i, 0))
    oshape = jax.ShapeDtypeStruct((B, T, MIX_W), F32)
    dspec = pl.BlockSpec((2, None, tm, MIX_W), lambda b, i: (0, b, i, 0))
    dshape = jax.ShapeDtypeStruct((2, B, T, MIX_W), F32)
    return pl.pallas_call(
        functools.partial(_rw_prep_kernel, grid_shift=grid_shift, tm=tm),
        grid=(B, T // tm),
        in_specs=z_specs + [const((1, width)), const((4, MIX_W)), const((2, MIX_W)),
                            const((4, MIX_W, MIX_W))],
        out_specs=[ospec, ospec, ospec, dspec, ospec, ospec, ospec, ospec],
        out_shape=[oshape, oshape, oshape, dshape, oshape, oshape, oshape, oshape],
        scratch_shapes=[pltpu.VMEM((tm + 2 * GRID_W, width), F32)],
        compiler_params=_cparams("parallel", "parallel"),
        name="rwkv_prep",
    )(*z_args, mu_p, vecs, w0, lora)


def _seq_kernel(*refs, streams, nb, ub, tb, nt):
    refs = list(refs)
    take = lambda n: [refs.pop(0) for _ in range(n)]
    in_refs = [take(6 if use_sa else 4) for use_sa, _ in streams]
    s0_refs = take(len(streams))
    y_refs = take(len(streams))
    fin_refs = take(len(streams))
    s_refs = take(len(streams))
    j = pl.program_id(0)
    pair_w = 2 * HEAD_D
    group = 8

    @pl.when(j == 0)
    def _():
        for s_ref, s0_ref in zip(s_refs, s0_refs):
            s_ref[...] = s0_ref[...]

    lane = lax.broadcasted_iota(jnp.int32, (HEAD_D, pair_w), 1)
    sub = lax.broadcasted_iota(jnp.int32, (HEAD_D, pair_w), 0)
    lo = lane < HEAD_D
    diag = (lane & (HEAD_D - 1)) == sub
    r2 = lax.broadcasted_iota(jnp.int32, (2 * pair_w, pair_w), 0)
    c2 = lax.broadcasted_iota(jnp.int32, (2 * pair_w, pair_w), 1)
    ones2 = jnp.where(((r2 >> HEAD_SHIFT) & 1) == (c2 >> HEAD_SHIFT), 1.0, 0.0).astype(BF16)

    def split(x):
        hi = x.astype(BF16).astype(F32)
        return hi, (x - hi).astype(BF16).astype(F32)

    def seg_sum_mxu(hi, lo_):
        return _dot(jnp.concatenate([hi.astype(BF16), lo_.astype(BF16)], axis=1), ones2)

    def seg_sum_xlu(p):
        s_lo = jnp.sum(jnp.where(lo, p, 0.0), axis=1, keepdims=True)
        s_hi = jnp.sum(jnp.where(lo, 0.0, p), axis=1, keepdims=True)
        return jnp.where(lo, s_lo, s_hi)

    def chain(si, b, p, g):
        use_sa, rev = streams[si]
        t0 = pl.multiple_of((tb // group - 1 - g if rev else g) * group, group)
        sl = pl.ds(p * pair_w, pair_w)
        blk = lambda ref: ref[b, pl.ds(t0, group), sl]
        w, k, r, v = (blk(ref) for ref in in_refs[si][:4])
        if use_sa:
            a, bb = blk(in_refs[si][4]), blk(in_refs[si][5])
            v_hi, v_lo = split(v)
        s = s_refs[si][b, p]
        ys = [None] * group
        for i in range(group):
            t = group - 1 - i if rev else i
            row = lambda x: x[t:t + 1, :]
            if use_sa:
                vcol = seg_sum_mxu(jnp.where(diag, row(v_hi), 0.0), jnp.where(diag, row(v_lo), 0.0))
                s = s * row(w) + seg_sum_xlu(s * row(a)) * row(bb)
            else:
                vcol = seg_sum_xlu(jnp.where(diag, row(v), 0.0))
                s = s * row(w)
            s = s + vcol * row(k)
            yb = seg_sum_mxu(*split(s * row(r)))
            ys[t] = jnp.sum(jnp.where(diag, yb, 0.0), axis=0, keepdims=True)
        s_refs[si][b, p] = s
        y_refs[si][b, pl.ds(t0, group), sl] = jnp.concatenate(ys, axis=0)

    def group_step(g, carry):
        def batch_step(bg, c):
            for si in range(len(streams)):
                for u in range(ub):
                    for p in range(2):
                        chain(si, bg * ub + u, p, g)
            return c

        return lax.fori_loop(0, nb // ub, batch_step, carry)

    lax.fori_loop(0, tb // group, group_step, 0)

    @pl.when(j == nt - 1)
    def _():
        for fin_ref, s_ref in zip(fin_refs, s_refs):
            fin_ref[...] = s_ref[...]


SCAN_BLOCK_ROWS = 512
SCAN_CHAINS = 16


def _seq_scan(mixers):
    B, T = mixers[0][3][0].shape[-3], mixers[0][3][0].shape[-2]
    tb = max(8, min(T, SCAN_BLOCK_ROWS // B))
    nt = T // tb
    ub = max(u for u in (1, 2, 4, 8) if B % u == 0 and u * 4 * len(mixers) <= max(SCAN_CHAINS, 4 * len(mixers)))
    streams, args, in_specs, s0_args = [], [], [], []
    for w, k, r, v, a, b, s0 in mixers:
        for di in range(2):
            tmap = (lambda j: nt - 1 - j) if di else (lambda j: j)
            streams.append((a is not None, di == 1))
            for arr, blk in [x for x in (w, k, r, v, a, b) if x is not None]:
                args.append(arr)
                if arr.ndim == 4:
                    in_specs.append(pl.BlockSpec((None, B, tb, MIX_W),
                                                 lambda j, di=di, tmap=tmap, blk=blk: (di, 0, tmap(j), blk)))
                else:
                    in_specs.append(pl.BlockSpec((B, tb, MIX_W), lambda j, tmap=tmap, blk=blk: (0, tmap(j), blk)))
            s0_args.append(s0[:, di])
    n = len(streams)
    sspec = pl.BlockSpec((B, 2, HEAD_D, 2 * HEAD_D), lambda j: (0, 0, 0, 0))
    state = jax.ShapeDtypeStruct((B, 2, HEAD_D, 2 * HEAD_D), F32)
    y_specs = [pl.BlockSpec((B, tb, MIX_W), (lambda j: (0, nt - 1 - j, 0)) if rev else (lambda j: (0, j, 0)))
               for _, rev in streams]
    outs = pl.pallas_call(
        functools.partial(_seq_kernel, streams=tuple(streams), nb=B, ub=ub, tb=tb, nt=nt),
        grid=(nt,),
        in_specs=in_specs + [sspec] * n,
        out_specs=y_specs + [sspec] * n,
        out_shape=[jax.ShapeDtypeStruct((B, T, MIX_W), F32)] * n + [state] * n,
        scratch_shapes=[pltpu.VMEM((B, 2, HEAD_D, 2 * HEAD_D), F32)] * n,
        compiler_params=_cparams("arbitrary"),
        name="state_scan",
    )(*args, *s0_args)
    ys, fins = outs[:n], outs[n:]
    return [((ys[2 * m], ys[2 * m + 1]), jnp.stack([fins[2 * m], fins[2 * m + 1]], axis=1))
            for m in range(len(mixers))]


def _pair_tiles(s):
    B = s.shape[0]
    s = s.reshape(B, 2, 2, 2, HEAD_D, HEAD_D).transpose(0, 1, 2, 4, 3, 5)
    return s.reshape(B, 2, 2, HEAD_D, 2 * HEAD_D)


def _unpair_tiles(s):
    B = s.shape[0]
    s = s.reshape(B, 2, 2, HEAD_D, 2, HEAD_D).transpose(0, 1, 2, 4, 3, 5)
    return s.reshape(B, 2, N_HEADS, HEAD_D, HEAD_D)


RW_CHUNK = 64
RW_CHUNK_SHIFT = 6


def _mm(a, b):
    return jnp.dot(a.astype(BF16), b.astype(BF16), preferred_element_type=F32)


def _rwkv_kernel(lw_ref, a_ref, b_ref, k_ref, r_ref, v_ref, s0_ref, y_ref, fin_ref, s_ref, g_ref,
                 *, rev, tb, nt):
    jb = pl.program_id(1)
    ch = RW_CHUNK
    pair = 2 * HEAD_D

    @pl.when(jb == 0)
    def _():
        s_ref[...] = s0_ref[...]

    lw = lw_ref[...]
    rr = lax.broadcasted_iota(jnp.int32, (tb, tb), 0)
    cc = lax.broadcasted_iota(jnp.int32, (tb, tb), 1)
    same = (rr >> RW_CHUNK_SHIFT) == (cc >> RW_CHUNK_SHIFT)
    tri = jnp.where(same & ((cc >= rr) if rev else (cc <= rr)), 1.0, 0.0).astype(BF16)
    p1 = lw.astype(BF16)
    r1 = lw - p1.astype(F32)
    p2 = r1.astype(BF16)
    p3 = (r1 - p2.astype(F32)).astype(BF16)
    g_ref[...] = _dot(tri, p1) + _dot(tri, p2) + _dot(tri, p3)

    si = lax.broadcasted_iota(jnp.int32, (ch, ch), 0)
    ri = lax.broadcasted_iota(jnp.int32, (ch, ch), 1)
    if rev:
        si, ri = ch - 1 - si, ch - 1 - ri
    strict = ri < si
    incl = ri <= si
    eye = jnp.where(ri == si, 1.0, 0.0).astype(F32)
    levels = [((si >> (lv + 1)) == (ri >> (lv + 1))) & (((si >> lv) & 1) == 1) & (((ri >> lv) & 1) == 0)
              for lv in range(RW_CHUNK_SHIFT)]
    lo = lax.broadcasted_iota(jnp.int32, (ch, pair), 1) < HEAD_D
    same_head = ((lax.broadcasted_iota(jnp.int32, (pair, pair), 0) >> HEAD_SHIFT)
                 == (lax.broadcasted_iota(jnp.int32, (pair, pair), 1) >> HEAD_SHIFT))

    nch = tb // ch
    g = g_ref[...]
    e_g = jnp.exp(g)
    e_ng = jnp.exp(-g)
    at_f = a_ref[...] * jnp.exp(g - lw)
    rt_f = r_ref[...] * e_g
    at = at_f.astype(BF16)
    rt = rt_f.astype(BF16)
    first = (lax.broadcasted_iota(jnp.int32, (tb, MIX_W), 1) & HEAD_D) == 0
    at_h = [jnp.where(first, at_f, 0.0).astype(BF16), jnp.where(first, 0.0, at_f).astype(BF16)]
    rt_h = [jnp.where(first, rt_f, 0.0).astype(BF16), jnp.where(first, 0.0, rt_f).astype(BF16)]
    bt = (b_ref[...] * e_ng).astype(BF16)
    kt = (k_ref[...] * e_ng).astype(BF16)
    vb = v_ref[...].astype(BF16)

    items = [(c, hp, hh) for c in range(nch) for hp in range(2) for hh in range(2)]
    rows = lambda c: slice(c * ch, (c + 1) * ch)
    lanes = lambda hp: slice(hp * pair, (hp + 1) * pair)
    n_m, p_m, m_m, q_m = {}, {}, {}, {}
    for it in items:
        c, hp, hh = it
        ar = jnp.concatenate([at_h[hh][rows(c), lanes(hp)], rt_h[hh][rows(c), lanes(hp)]], axis=0)
        np_ = _dot_nt(ar, bt[rows(c), lanes(hp)])
        mq = _dot_nt(ar, kt[rows(c), lanes(hp)])
        n_m[it] = jnp.where(strict, np_[:ch], 0.0).astype(BF16)
        p_m[it] = jnp.where(incl, np_[ch:], 0.0).astype(BF16)
        m_m[it] = jnp.where(strict, mq[:ch], 0.0).astype(BF16)
        q_m[it] = jnp.where(incl, mq[ch:], 0.0).astype(BF16)
    t_m = {it: eye + jnp.where(levels[0], n_m[it].astype(F32), 0.0) for it in items}
    for lv in range(1, RW_CHUNK_SHIFT):
        tn = {it: _mm(t_m[it], n_m[it]) for it in items}
        t_m = {it: t_m[it] + jnp.where(levels[lv], _mm(tn[it], t_m[it]), 0.0) for it in items}
    mv = {it: _mm(m_m[it], vb[rows(it[0]), lanes(it[1])]) for it in items}
    qv = {it: _mm(q_m[it], vb[rows(it[0]), lanes(it[1])]) for it in items}
    y2 = {it: _mm(t_m[it], jnp.concatenate([at[rows(it[0]), lanes(it[1])], mv[it].astype(BF16)], axis=1))
          for it in items}

    last = 0 if rev else ch - 1
    for ci in range(nch):
        c = nch - 1 - ci if rev else ci
        g_c = g[c * ch + last:c * ch + last + 1, :]
        e_gc = jnp.exp(g_c - g[rows(c), :])
        bh = (b_ref[rows(c), :] * e_gc).astype(BF16)
        kh = (k_ref[rows(c), :] * e_gc).astype(BF16)
        dec_c = jnp.exp(g_c)
        xs, sas, ss = [], [], []
        for hp in range(2):
            s = s_ref[hp]
            w2 = jnp.where(lo, y2[(c, hp, 0)][:, :pair], y2[(c, hp, 1)][:, :pair]).astype(BF16)
            xs.append(_dot_nt(jnp.concatenate([w2, rt[rows(c), lanes(hp)]], axis=0), s.astype(BF16)))
            ss.append(s)
        for hp in range(2):
            w1 = jnp.where(lo, y2[(c, hp, 0)][:, pair:], y2[(c, hp, 1)][:, pair:])
            sas.append(w1 + xs[hp][:ch])
        for hp in range(2):
            sa = sas[hp]
            y_ref[rows(c), lanes(hp)] = xs[hp][ch:] + jnp.where(
                lo, _mm(p_m[(c, hp, 0)], sa) + qv[(c, hp, 0)], _mm(p_m[(c, hp, 1)], sa) + qv[(c, hp, 1)])
            upd = _dot_tn(jnp.concatenate([sa.astype(BF16), vb[rows(c), lanes(hp)]], axis=0),
                          jnp.concatenate([bh[:, lanes(hp)], kh[:, lanes(hp)]], axis=0))
            s_ref[hp] = ss[hp] * dec_c[:, lanes(hp)] + jnp.where(same_head, upd, 0.0)

    @pl.when(jb == nt - 1)
    def _():
        fin_ref[...] = s_ref[...]


def _head_pair_blockdiag(s):
    B = s.shape[0]
    s = s.reshape(B, 2, 2, HEAD_D, HEAD_D)
    zero = jnp.zeros_like(s[:, :, 0])
    return jnp.concatenate([jnp.concatenate([s[:, :, 0], zero], axis=-1),
                            jnp.concatenate([zero, s[:, :, 1]], axis=-1)], axis=-2)


def _head_pair_blocks(t):
    B = t.shape[0]
    t = jnp.stack([t[:, :, :HEAD_D, :HEAD_D], t[:, :, HEAD_D:, HEAD_D:]], axis=2)
    return t.reshape(B, N_HEADS, HEAD_D, HEAD_D)


def _rwkv(lw, a, b, k, r, v, s0, rev):
    B, T, _ = v.shape
    pair = 2 * HEAD_D
    tb = min(T, 256)
    nt = T // tb
    di = 1 if rev else 0
    tmap = (lambda j: nt - 1 - j) if rev else (lambda j: j)
    spec = pl.BlockSpec((None, tb, MIX_W), lambda bi, j: (bi, tmap(j), 0))
    sspec = pl.BlockSpec((None, 2, pair, pair), lambda bi, j: (bi, 0, 0, 0))
    y, fin = pl.pallas_call(
        functools.partial(_rwkv_kernel, rev=rev, tb=tb, nt=nt),
        grid=(B, nt),
        in_specs=[pl.BlockSpec((None, None, tb, MIX_W), lambda bi, j: (di, bi, tmap(j), 0)),
                  spec, spec, spec, spec, spec, sspec],
        out_specs=[spec, sspec],
        out_shape=[jax.ShapeDtypeStruct((B, T, MIX_W), F32),
                   jax.ShapeDtypeStruct((B, 2, pair, pair), F32)],
        scratch_shapes=[pltpu.VMEM((2, pair, pair), F32), pltpu.VMEM((tb, MIX_W), F32)],
        compiler_params=_cparams("parallel", "arbitrary"),
        name="rwkv_chunk_bwd" if rev else "rwkv_chunk_fwd",
    )(lw, a, b, k, r, v, _head_pair_blockdiag(s0))
    return y, _head_pair_blocks(fin)


def _mix_kernel(x_ref, g1_ref, nw_ref, zg_ref,
                ret_ref, retg_ref, s5_ref, s5u_ref, hgf_ref, hgb_ref, hgg_ref, rwf_ref, rwr_ref,
                rwb_ref, rwg_ref,
                vec_ref, glu_w_ref, wbr_ref, wout_ref, o_ref):
    ones_bd = _head_ones()
    gn_w, s5_d, glu_b, hg_w, ln_w, ln_b = (vec_ref[j:j + 1, :] for j in range(6))

    def group_norm(o, eps):
        mu = _head_sum(o, ones_bd) * (1.0 / HEAD_D)
        oc = o - mu
        var = _head_sum(oc * oc, ones_bd) * (1.0 / HEAD_D)
        return oc * lax.rsqrt(var + eps)

    g = retg_ref[...]
    y_ret = group_norm(ret_ref[0] + ret_ref[1], EPS) * gn_w * (g * jax.nn.sigmoid(g))

    y = s5_d * s5u_ref[...] + s5_ref[0] + s5_ref[1]
    yg = jax.nn.gelu(y)
    y_s5 = yg * jax.nn.sigmoid(_dot(yg.astype(BF16), glu_w_ref[...]) + glu_b)

    o = hgf_ref[...] + hgb_ref[...]
    g = hgg_ref[...]
    ms = _head_sum(o * o, ones_bd) * (1.0 / HEAD_D)
    y_hg = o * lax.rsqrt(ms + EPS) * hg_w * (g * jax.nn.sigmoid(g))

    y = group_norm(rwf_ref[...] + rwr_ref[...], RW_LN_EPS) * ln_w + ln_b
    y_rw = (y + rwb_ref[...]) * rwg_ref[...]

    mixed = None
    for m, ym in enumerate((y_ret, y_s5, y_hg, y_rw)):
        br = _dot(ym.astype(BF16), wbr_ref[m])
        term = jax.nn.sigmoid(zg_ref[:, m * D_MODEL:(m + 1) * D_MODEL]) * br
        mixed = term if mixed is None else mixed + term
    mixed = _dot(mixed.astype(BF16), wout_ref[...])
    o_ref[...] = x_ref[...] + g1_ref[...] * _rms(mixed, nw_ref[...])


def _mix(x, g1, nw, z, ret_o, s5_y, hg_o, rw_y, rw_bonus, rw_g, vecs, glu_w, w_branch, w_out):
    B, T, _ = x.shape
    tm = min(T, 256)
    xspec = pl.BlockSpec((None, tm, D_MODEL), lambda b, i: (b, i, 0))
    zspec = lambda blk: pl.BlockSpec((None, tm, MIX_W), lambda b, i: (b, i, blk))
    dspec = pl.BlockSpec((2, None, tm, MIX_W), lambda b, i: (0, b, i, 0))
    const = lambda shape: pl.BlockSpec(shape, lambda b, i: (0,) * len(shape))
    return pl.pallas_call(
        _mix_kernel,
        grid=(B, T // tm),
        in_specs=[
            xspec,
            pl.BlockSpec((None, 1, D_MODEL), _mod_map(g1.shape[0])),
            const((1, D_MODEL)),
            pl.BlockSpec((None, tm, 4 * D_MODEL), lambda b, i: (b, i, 0)),
            dspec, zspec(ZB_RET + 3),
            dspec, zspec(ZB_S5),
            zspec(0), zspec(0), zspec(ZB_HG + 4),
            zspec(0), zspec(0), zspec(0), zspec(0),
            const((6, MIX_W)), const((MIX_W, MIX_W)),
            const((4, MIX_W, D_MODEL)), const((D_MODEL, D_MODEL)),
        ],
        out_specs=xspec,
        out_shape=jax.ShapeDtypeStruct((B, T, D_MODEL), F32),
        compiler_params=_cparams("parallel", "parallel"),
        name="mix_out",
    )(x, g1, nw, z, ret_o, z, s5_y, z, hg_o[0], hg_o[1], z, rw_y[0], rw_y[1], rw_bonus, rw_g,
      vecs, glu_w, w_branch, w_out)


def _ffn_kernel(x_ref, sc_ref, sh_ref, g2_ref, nw2_ref, nw3_ref, w1_ref, w2_ref, o_ref, h_ref, acc_ref):
    j = pl.program_id(2)

    @pl.when(j == 0)
    def _():
        h = _rms(x_ref[...], nw2_ref[...]) * (1.0 + sc_ref[...]) + sh_ref[...]
        h_ref[...] = h.astype(BF16)
        acc_ref[...] = jnp.zeros_like(acc_ref)

    a = jnp.maximum(_dot(h_ref[...], w1_ref[...]), 0.0)
    acc_ref[...] += _dot((a * a).astype(BF16), w2_ref[...])

    @pl.when(j == pl.num_programs(2) - 1)
    def _():
        o_ref[...] = x_ref[...] + g2_ref[...] * _rms(acc_ref[...], nw3_ref[...])


def _ffn(x, sc, sh, g2, nw2, nw3, w1, w2):
    B, T, _ = x.shape
    tm = min(T, 512)
    tf = 1024
    xspec = pl.BlockSpec((None, tm, D_MODEL), lambda b, i, j: (b, i, 0))
    mspec = lambda m: pl.BlockSpec((None, 1, D_MODEL), _mod_map(m.shape[0]))
    const = pl.BlockSpec((1, D_MODEL), lambda b, i, j: (0, 0))
    return pl.pallas_call(
        _ffn_kernel,
        grid=(B, T // tm, D_FF // tf),
        in_specs=[xspec, mspec(sc), mspec(sh), mspec(g2), const, const,
                  pl.BlockSpec((D_MODEL, tf), lambda b, i, j: (0, j)),
                  pl.BlockSpec((tf, D_MODEL), lambda b, i, j: (j, 0))],
        out_specs=xspec,
        out_shape=jax.ShapeDtypeStruct((B, T, D_MODEL), F32),
        scratch_shapes=[pltpu.VMEM((tm, D_MODEL), BF16), pltpu.VMEM((tm, D_MODEL), F32)],
        compiler_params=_cparams("parallel", "parallel", "arbitrary"),
        name="ffn",
    )(x, sc, sh, g2, nw2, nw3, w1, w2)


def _layer_params(l, p):
    w = p['w_in'][l]
    w_p = jnp.concatenate(
        [w[:, 3488:7584], w[:, 2560:3488], jnp.zeros((D_MODEL, Z_COLS - 7584), F32),
         w[:, 0:1024], w[:, 1024:1280], w[:, 1280:2560]], axis=1).astype(BF16)

    a_re, a_im, bb_re, bb_im = _s5_zoh(p['s5_lam_re'][l], p['s5_lam_im'][l], p['s5_log_dt'][l],
                                       p['s5_b_re'][l], p['s5_b_im'][l])
    eye = jnp.eye(S5_GROUPS, dtype=F32)
    bblk = lambda bb: jnp.einsum('dgph,gk->dghkp', bb, eye).reshape(2, MIX_W, S5_STATE).astype(BF16)
    cblk = lambda c: jnp.einsum('ghp,gk->gpkh', c, eye).reshape(S5_STATE, MIX_W).astype(BF16)

    def lora_pad(m, row0):
        return jnp.zeros((MIX_W, MIX_W), F32).at[row0:row0 + m.shape[0]].set(m)

    lora = jnp.stack([lora_pad(p['rw_w2'][l, 0], 0), lora_pad(p['rw_w2'][l, 1], 32),
                      lora_pad(p['rw_a2'][l], 64), lora_pad(p['rw_g2'][l], 96)]).astype(BF16)
    mu_p = jnp.concatenate([p['rw_mu'][l], jnp.zeros((4 * MIX_W - RW_COLS,), F32)]).reshape(1, 4 * MIX_W)
    return dict(
        w_p=w_p, nw=p['norm_w'][l],
        s5=(a_re, a_im, bblk(bb_re), bblk(bb_im), cblk(p['s5_c_re'][l]), cblk(p['s5_c_im'][l])),
        rw_mu=mu_p, rw_lora=lora, rw_w0=p['rw_w0'][l],
        rw_vecs=jnp.stack([p['rw_a0'][l], p['rw_k_k'][l], p['rw_k_a'][l], p['rw_r_k'][l]]),
        mix_vecs=jnp.stack([p['ret_gn_w'][l], p['s5_d'][l], p['s5_glu_b'][l], p['hg_norm_w'][l],
                            p['rw_ln_w'][l], p['rw_ln_b'][l]]),
        glu_w=p['s5_glu_w'][l].astype(BF16),
        w_branch=p['w_branch'][l].astype(BF16), w_out=p['w_out'][l].astype(BF16),
        ff_w1=p['ff_w1'][l].astype(BF16), ff_w2=p['ff_w2'][l].astype(BF16),
    )


def _trunk_layer(x, mod, init, grid_shift, lp, hg_lb):
    B = x.shape[0]
    sh1, sc1, g1, sh2, sc2, g2 = (m[:, None, :] for m in jnp.split(mod, 6, axis=-1))
    nw = lp['nw']
    z = _proj_in(x, sc1, sh1, nw[0:1], lp['w_p'])

    s_ret, s_s5r, s_s5i, s_hg, s_rw = init
    ret_o, f_ret = _retention(z, s_ret)

    s5_y, f_s5r, f_s5i = _s5(z, *lp['s5'], s_s5r.reshape(B, 2, 1, S5_STATE), s_s5i.reshape(B, 2, 1, S5_STATE))
    f_s5r = f_s5r.reshape(B, 2, S5_GROUPS, S5_P)
    f_s5i = f_s5i.reshape(B, 2, S5_GROUPS, S5_P)

    s_hg_t = jnp.swapaxes(s_hg, -1, -2)
    hg_f, f_hg_f = _gla(z, hg_lb, s_hg_t[:, 0], False)
    hg_b, f_hg_b = _gla(z, hg_lb, s_hg_t[:, 1], True)
    hg_o = (hg_f, hg_b)
    f_hg = jnp.swapaxes(jnp.stack([f_hg_f, f_hg_b], axis=1), -1, -2)

    r, k2, v, w, nkk, kka, rw_g, bonus = _rw_prep(z, lp['rw_mu'], lp['rw_vecs'], lp['rw_w0'],
                                                  lp['rw_lora'], grid_shift)
    rw_f, f_rw_f = _rwkv(w, nkk, kka, k2, r, v, s_rw[:, 0], False)
    rw_b, f_rw_b = _rwkv(w, nkk, kka, k2, r, v, s_rw[:, 1], True)
    rw_y = (rw_f, rw_b)
    f_rw = jnp.stack([f_rw_f, f_rw_b], axis=1)

    x = _mix(x, g1, nw[1:2], z, ret_o, s5_y, hg_o, rw_y, bonus, rw_g, lp['mix_vecs'], lp['glu_w'],
             lp['w_branch'], lp['w_out'])
    x = _ffn(x, sc2, sh2, g2, nw[2:3], nw[3:4], lp['ff_w1'], lp['ff_w2'])
    return x, (f_ret, f_s5r, f_s5i, f_hg, f_rw)


def kernel(x_prompt, x_sample, state_ret, state_s5_re, state_s5_im, state_hgrn, state_rwkv, c, c_ctx, ada_w, ada_b, norm_w, w_in, ret_gn_w, s5_lam_re, s5_lam_im, s5_log_dt, s5_b_re, s5_b_im, s5_c_re, s5_c_im, s5_d, s5_glu_w, s5_glu_b, hg_lb, hg_norm_w, rw_mu, rw_w0, rw_w2, rw_a0, rw_a2, rw_g2, rw_k_k, rw_k_a, rw_r_k, rw_ln_w, rw_ln_b, w_branch, w_out, ff_w1, ff_w2):
    p = dict(norm_w=norm_w, w_in=w_in, ret_gn_w=ret_gn_w, s5_lam_re=s5_lam_re, s5_lam_im=s5_lam_im,
             s5_log_dt=s5_log_dt, s5_b_re=s5_b_re, s5_b_im=s5_b_im, s5_c_re=s5_c_re, s5_c_im=s5_c_im,
             s5_d=s5_d, s5_glu_w=s5_glu_w, s5_glu_b=s5_glu_b, hg_norm_w=hg_norm_w, rw_mu=rw_mu,
             rw_w0=rw_w0, rw_w2=rw_w2, rw_a0=rw_a0, rw_a2=rw_a2, rw_g2=rw_g2, rw_k_k=rw_k_k,
             rw_k_a=rw_k_a, rw_r_k=rw_r_k, rw_ln_w=rw_ln_w, rw_ln_b=rw_ln_b, w_branch=w_branch,
             w_out=w_out, ff_w1=ff_w1, ff_w2=ff_w2)
    depth = w_in.shape[0]
    n_ctx = x_prompt.shape[0]
    n_lat = x_sample.shape[0]
    assert 1 + n_lat <= 8

    lb_cum = jnp.cumsum(jax.nn.softmax(hg_lb.astype(F32), axis=0), axis=0)
    hg_lower = lb_cum - lb_cum[0]

    cond = jnp.concatenate([c_ctx[None, :], c, jnp.zeros((7 - n_lat, D_MODEL), F32)], axis=0)
    mod = _ada_mod(cond, ada_w, ada_b)

    zero_state = (jnp.zeros((n_ctx, 2, N_HEADS, HEAD_D, HEAD_D), F32),
                  jnp.zeros((n_ctx, 2, S5_GROUPS, S5_P), F32),
                  jnp.zeros((n_ctx, 2, S5_GROUPS, S5_P), F32),
                  jnp.zeros((n_ctx, 2, N_HEADS, HEAD_D, HEAD_D), F32),
                  jnp.zeros((n_ctx, 2, N_HEADS, HEAD_D, HEAD_D), F32))
    xp, xs = x_prompt, x_sample
    finals = []
    for l in range(depth):
        lp = _layer_params(l, p)
        xp, fin = _trunk_layer(xp, mod[l, 0:1], zero_state, False, lp, hg_lower[l])
        finals.append(fin)
        lat_init = (state_ret[:, l], state_s5_re[:, l], state_s5_im[:, l], state_hgrn[:, l], state_rwkv[:, l])
        xs, _ = _trunk_layer(xs, mod[l, 1:1 + n_lat], lat_init, True, lp, hg_lower[l])
    new_states = tuple(jnp.stack([f[i] for f in finals], axis=1) for i in range(5))
    return (xp, xs) + new_states
```

```python
import functools
import math

import jax
import jax.numpy as jnp
from jax import lax
from jax.experimental import pallas as pl
from jax.experimental.pallas import tpu as pltpu

F32 = jnp.float32
BF16 = jnp.bfloat16
HIGHEST = lax.Precision.HIGHEST

D_MODEL = 1024
GRID_W = 64
MIX_W = 256
N_HEADS = 4
HEAD_D = 64
HEAD_SHIFT = 6
LANES = 128
S5_GROUPS = 16
S5_CH = 16
S5_P = 64
S5_STATE = S5_GROUPS * S5_P
D_FF = 4096
EPS = 1e-6
RW_LN_EPS = 64e-5
RW_COLS = 928

Z_COLS = 3584
ZB_RW = 0
ZB_RET = 4
ZB_S5 = 8
ZB_HG = 9

VMEM_LIMIT = 56 * 1024 * 1024

LOG_GAMMA = tuple(
    tuple(math.log1p(-2.0 ** (-(5.0 + 0.5 * di) - h)) for h in range(N_HEADS)) for di in range(2))


def _cparams(*sem):
    return pltpu.CompilerParams(dimension_semantics=sem, vmem_limit_bytes=VMEM_LIMIT)


def _dot(a, b):
    return jnp.dot(a, b, preferred_element_type=F32)


def _dot_nt(a, b):
    return lax.dot_general(a, b, (((1,), (1,)), ((), ())), preferred_element_type=F32)


def _dot_tn(a, b):
    return lax.dot_general(a, b, (((0,), (0,)), ((), ())), preferred_element_type=F32)


def _head_ones():
    r = lax.broadcasted_iota(jnp.int32, (MIX_W, MIX_W), 0) >> HEAD_SHIFT
    c = lax.broadcasted_iota(jnp.int32, (MIX_W, MIX_W), 1) >> HEAD_SHIFT
    return jnp.where(r == c, 1.0, 0.0).astype(F32)


def _head_sum(x, ones_bd):
    return jnp.dot(x, ones_bd, precision=HIGHEST, preferred_element_type=F32)


def _rms(x, w):
    return x * lax.rsqrt(jnp.mean(x * x, axis=-1, keepdims=True) + EPS) * w


def _ada_kernel(c_ref, w_ref, b_ref, o_ref):
    c = c_ref[...]
    s = c * jax.nn.sigmoid(c)
    o_ref[...] = _dot(s.astype(BF16), w_ref[...].astype(BF16)) + b_ref[...]


def _ada_mod(cond, ada_w, ada_b):
    L = ada_w.shape[0]
    n = ada_w.shape[2]
    tn = 1536
    return pl.pallas_call(
        _ada_kernel,
        grid=(L, n // tn),
        in_specs=[
            pl.BlockSpec((8, D_MODEL), lambda l, j: (0, 0)),
            pl.BlockSpec((None, D_MODEL, tn), lambda l, j: (l, 0, j)),
            pl.BlockSpec((None, 1, tn), lambda l, j: (l, 0, j)),
        ],
        out_specs=pl.BlockSpec((None, 8, tn), lambda l, j: (l, 0, j)),
        out_shape=jax.ShapeDtypeStruct((L, 8, n), F32),
        compiler_params=_cparams("parallel", "parallel"),
        name="ada_mod",
    )(cond, ada_w, ada_b.reshape(L, 1, n))


def _proj_in_kernel(x_ref, sc_ref, sh_ref, nw_ref, w_ref, z_ref, h_ref):
    @pl.when(pl.program_id(2) == 0)
    def _():
        h = _rms(x_ref[...], nw_ref[...]) * (1.0 + sc_ref[...]) + sh_ref[...]
        h_ref[...] = h.astype(BF16)

    z_ref[...] = _dot(h_ref[...], w_ref[...])


def _mod_map(bm):
    if bm == 1:
        return lambda b, *_: (0, 0, 0)
    return lambda b, *_: (b, 0, 0)


def _proj_in(x, sc, sh, nw, w_p):
    B, T, _ = x.shape
    tm = min(T, 512)
    tn = 1792
    return pl.pallas_call(
        _proj_in_kernel,
        grid=(B, T // tm, Z_COLS // tn),
        in_specs=[
            pl.BlockSpec((None, tm, D_MODEL), lambda b, i, j: (b, i, 0)),
            pl.BlockSpec((None, 1, D_MODEL), _mod_map(sc.shape[0])),
            pl.BlockSpec((None, 1, D_MODEL), _mod_map(sh.shape[0])),
            pl.BlockSpec((1, D_MODEL), lambda b, i, j: (0, 0)),
            pl.BlockSpec((D_MODEL, tn), lambda b, i, j: (0, j)),
        ],
        out_specs=pl.BlockSpec((None, tm, tn), lambda b, i, j: (b, i, j)),
        out_shape=jax.ShapeDtypeStruct((B, T, Z_COLS), F32),
        scratch_shapes=[pltpu.VMEM((tm, D_MODEL), BF16)],
        compiler_params=_cparams("parallel", "parallel", "arbitrary"),
        name="proj_in",
    )(x, sc, sh, nw, w_p)


def _ret_kernel(q_ref, k_ref, v_ref, s0_ref, o_ref, fin_ref, s_ref, *, lc, nc):
    d = pl.program_id(0)
    c = pl.program_id(2)

    @pl.when(c == 0)
    def _():
        s_ref[...] = s0_ref[...]

    rows = lax.broadcasted_iota(jnp.int32, (lc, lc), 0)
    cols = lax.broadcasted_iota(jnp.int32, (lc, lc), 1)
    rel = jnp.where(d == 0, rows - cols, cols - rows).astype(F32)
    idx = lax.broadcasted_iota(jnp.int32, (lc, 1), 0)
    pos = jnp.where(d == 0, idx, lc - 1 - idx).astype(F32)
    for h in range(N_HEADS):
        lg = jnp.where(d == 0, LOG_GAMMA[0][h], LOG_GAMMA[1][h]).astype(F32)
        sl = slice(h * HEAD_D, (h + 1) * HEAD_D)
        q = q_ref[:, sl]
        k = k_ref[:, sl] * (HEAD_D ** -0.5)
        v = v_ref[:, sl].astype(BF16)
        att = _dot_nt(q.astype(BF16), k.astype(BF16))
        att = att * jnp.where(rel >= 0.0, jnp.exp(jnp.maximum(rel, 0.0) * lg), 0.0)
        qd = q * jnp.exp((pos + 1.0) * lg)
        kd = k * jnp.exp((lc - 1.0 - pos) * lg)
        s = s_ref[h]
        o_ref[:, sl] = _dot(att.astype(BF16), v) + _dot(qd.astype(BF16), s.astype(BF16))
        s_ref[h] = s * jnp.exp(lc * lg) + _dot_tn(kd.astype(BF16), v)

    @pl.when(c == nc - 1)
    def _():
        fin_ref[...] = s_ref[...]


def _retention(z, s0):
    B, T, _ = z.shape
    lc = min(T, 256)
    nc = T // lc

    def tmap(d, c):
        return jnp.where(d == 0, c, nc - 1 - c)

    def zspec(blk):
        return pl.BlockSpec((None, lc, MIX_W), lambda d, b, c: (b, tmap(d, c), blk))

    return pl.pallas_call(
        functools.partial(_ret_kernel, lc=lc, nc=nc),
        grid=(2, B, nc),
        in_specs=[
            zspec(ZB_RET), zspec(ZB_RET + 1), zspec(ZB_RET + 2),
            pl.BlockSpec((None, None, N_HEADS, HEAD_D, HEAD_D), lambda d, b, c: (b, d, 0, 0, 0)),
        ],
        out_specs=[
            pl.BlockSpec((None, None, lc, MIX_W), lambda d, b, c: (d, b, tmap(d, c), 0)),
            pl.BlockSpec((None, None, N_HEADS, HEAD_D, HEAD_D), lambda d, b, c: (b, d, 0, 0, 0)),
        ],
        out_shape=[
            jax.ShapeDtypeStruct((2, B, T, MIX_W), F32),
            jax.ShapeDtypeStruct((B, 2, N_HEADS, HEAD_D, HEAD_D), F32),
        ],
        scratch_shapes=[pltpu.VMEM((N_HEADS, HEAD_D, HEAD_D), F32)],
        compiler_params=_cparams("arbitrary", "arbitrary", "arbitrary"),
        name="retention",
    )(z, z, z, s0)


def _s5_zoh_kernel(lre_ref, lim_ref, ldt_ref, bre_ref, bim_ref, are_ref, aim_ref, bbre_ref, bbim_ref):
    lam_re = jnp.minimum(lre_ref[...], -1e-4)
    lam_im = lim_ref[...]
    dt = jnp.exp(ldt_ref[...])
    mag = jnp.exp(dt * lam_re)
    ang = dt * lam_im
    a_re = mag * jnp.cos(ang)
    a_im = mag * jnp.sin(ang)
    den = lam_re * lam_re + lam_im * lam_im
    f_re = ((a_re - 1.0) * lam_re + a_im * lam_im) / den
    f_im = (a_im * lam_re - (a_re - 1.0) * lam_im) / den
    b_re = bre_ref[...]
    b_im = bim_ref[...]
    are_ref[...] = a_re
    aim_ref[...] = a_im
    bbre_ref[...] = f_re * b_re - f_im * b_im
    bbim_ref[...] = f_re * b_im + f_im * b_re


def _s5_zoh(lam_re, lam_im, log_dt, b_re, b_im):
    n = 2 * S5_STATE
    col = lambda t: t.reshape(n, 1)
    ldt = jnp.broadcast_to(log_dt[:, :, None], (2, S5_GROUPS, S5_P))
    outs = pl.pallas_call(
        _s5_zoh_kernel,
        out_shape=[jax.ShapeDtypeStruct((n, 1), F32), jax.ShapeDtypeStruct((n, 1), F32),
                   jax.ShapeDtypeStruct((n, S5_CH), F32), jax.ShapeDtypeStruct((n, S5_CH), F32)],
        name="s5_zoh",
    )(col(lam_re), col(lam_im), col(ldt), b_re.reshape(n, S5_CH), b_im.reshape(n, S5_CH))
    a_re, a_im, bb_re, bb_im = outs
    shp = (2, S5_GROUPS, S5_P, S5_CH)
    return a_re.reshape(2, 1, S5_STATE), a_im.reshape(2, 1, S5_STATE), bb_re.reshape(shp), bb_im.reshape(shp)


S5_RADIX = 16


def _s5_kernel(u_ref, are_ref, aim_ref, bre_ref, bim_ref, cre_ref, cim_ref, x0r_ref, x0i_ref,
               y_ref, fr_ref, fi_ref, x_re, x_im, car_re, car_im, *, L, nc):
    d = pl.program_id(0)
    c = pl.program_id(2)
    R = S5_RADIX
    G = L // R

    def cmul(pr, pi, qr, qi):
        return pr * qr - pi * qi, pr * qi + pi * qr

    def body(rev):
        @pl.when(c == 0)
        def _():
            car_re[...] = x0r_ref[...]
            car_im[...] = x0i_ref[...]

        u = u_ref[...].astype(BF16)
        bu_re = _dot(u, bre_ref[...])
        bu_im = _dot(u, bim_ref[...])
        order = list(range(R - 1, -1, -1)) if rev else list(range(R))
        slab = lambda j: pl.ds(j, G, stride=R)
        for q in range(S5_STATE // LANES):
            ql = slice(q * LANES, (q + 1) * LANES)
            x_re[q] = bu_re[:, ql]
            x_im[q] = bu_im[:, ql]
            ar = are_ref[:, ql]
            ai = aim_ref[:, ql]

            er = x_re[q, slab(order[0]), :]
            ei = x_im[q, slab(order[0]), :]
            for j in order[1:]:
                tr, ti = cmul(ar, ai, er, ei)
                er = tr + x_re[q, slab(j), :]
                ei = ti + x_im[q, slab(j), :]

            a_r, a_i = ar, ai
            for _ in range(R.bit_length() - 1):
                a_r, a_i = cmul(a_r, a_i, a_r, a_i)
            zr = car_re[:, ql]
            zi = car_im[:, ql]
            cin_r = [None] * G
            cin_i = [None] * G
            for k in (range(G - 1, -1, -1) if rev else range(G)):
                cin_r[k] = zr
                cin_i[k] = zi
                tr, ti = cmul(a_r, a_i, zr, zi)
                zr = tr + er[k:k + 1, :]
                zi = ti + ei[k:k + 1, :]
            car_re[:, ql] = zr
            car_im[:, ql] = zi

            xr = jnp.concatenate(cin_r, axis=0)
            xi = jnp.concatenate(cin_i, axis=0)
            for j in order:
                tr, ti = cmul(ar, ai, xr, xi)
                xr = tr + x_re[q, slab(j), :]
                xi = ti + x_im[q, slab(j), :]
                x_re[q, slab(j), :] = xr
                x_im[q, slab(j), :] = xi
        nq = S5_STATE // LANES
        xr_all = jnp.concatenate([x_re[q] for q in range(nq)], axis=1).astype(BF16)
        xi_all = jnp.concatenate([x_im[q] for q in range(nq)], axis=1).astype(BF16)
        y_ref[...] = _dot(xr_all, cre_ref[...]) - _dot(xi_all, cim_ref[...])

    @pl.when(d == 0)
    def _():
        body(False)

    @pl.when(d == 1)
    def _():
        body(True)

    @pl.when(c == nc - 1)
    def _():
        fr_ref[...] = car_re[...]
        fi_ref[...] = car_im[...]


def _s5(z, a_re, a_im, bblk_re, bblk_im, cblk_re, cblk_im, x0_re, x0_im):
    B, T, _ = z.shape
    L = min(T, 128)
    nc = T // L

    def tmap(d, c):
        return jnp.where(d == 0, c, nc - 1 - c)

    dspec = lambda shape: pl.BlockSpec((None,) + shape, lambda d, b, c: (d, 0, 0))
    sspec = pl.BlockSpec((None, None, 1, S5_STATE), lambda d, b, c: (b, d, 0, 0))
    buf = pltpu.VMEM((S5_STATE // LANES, L, LANES), F32)
    return pl.pallas_call(
        functools.partial(_s5_kernel, L=L, nc=nc),
        grid=(2, B, nc),
        in_specs=[
            pl.BlockSpec((None, L, MIX_W), lambda d, b, c: (b, tmap(d, c), ZB_S5)),
            dspec((1, S5_STATE)), dspec((1, S5_STATE)),
            dspec((MIX_W, S5_STATE)), dspec((MIX_W, S5_STATE)),
            pl.BlockSpec((S5_STATE, MIX_W), lambda d, b, c: (0, 0)),
            pl.BlockSpec((S5_STATE, MIX_W), lambda d, b, c: (0, 0)),
            sspec, sspec,
        ],
        out_specs=[
            pl.BlockSpec((None, None, L, MIX_W), lambda d, b, c: (d, b, tmap(d, c), 0)),
            sspec, sspec,
        ],
        out_shape=[
            jax.ShapeDtypeStruct((2, B, T, MIX_W), F32),
            jax.ShapeDtypeStruct((B, 2, 1, S5_STATE), F32),
            jax.ShapeDtypeStruct((B, 2, 1, S5_STATE), F32),
        ],
        scratch_shapes=[buf, buf, pltpu.VMEM((1, S5_STATE), F32), pltpu.VMEM((1, S5_STATE), F32)],
        compiler_params=_cparams("arbitrary", "arbitrary", "arbitrary"),
        name="s5_scan",
    )(z, a_re, a_im, bblk_re, bblk_im, cblk_re, cblk_im, x0_re, x0_im)


GLA_CHUNK = 16
GLA_CHUNK_SHIFT = 4
SUB = 8


def _gla_kernel(q_ref, zf_ref, v_ref, lb_ref, s0_ref, o_ref, fin_ref, s_ref, a_ref, qs_ref, key_ref,
                *, rev, tb, nt):
    jb = pl.program_id(1)
    ch = GLA_CHUNK
    half = 2 * HEAD_D

    @pl.when(jb == 0)
    def _():
        s_ref[...] = s0_ref[...]

    lb = lb_ref[...]
    zf = zf_ref[...]
    q = q_ref[...]
    qs_ref[...] = q * jax.nn.sigmoid(q)
    key_ref[...] = (1.0 - lb) * jax.nn.sigmoid(-zf)
    l1 = jnp.log(lb)
    l2 = jnp.log1p(-lb) + jnp.minimum(zf, 0.0) - jnp.log1p(jnp.exp(-jnp.abs(zf)))
    lf = jnp.maximum(l1, l2) + jnp.log1p(jnp.exp(-jnp.abs(l1 - l2)))
    r = lax.broadcasted_iota(jnp.int32, (tb, tb), 0)
    c = lax.broadcasted_iota(jnp.int32, (tb, tb), 1)
    same = (r >> GLA_CHUNK_SHIFT) == (c >> GLA_CHUNK_SHIFT)
    tri = jnp.where(same & ((c >= r) if rev else (c <= r)), 1.0, 0.0).astype(BF16)
    p1 = lf.astype(BF16)
    r1 = lf - p1.astype(F32)
    p2 = r1.astype(BF16)
    p3 = (r1 - p2.astype(F32)).astype(BF16)
    a_ref[...] = _dot(tri, p1) + _dot(tri, p2) + _dot(tri, p3)

    rows = lax.broadcasted_iota(jnp.int32, (SUB, half), 0)
    lo = lax.broadcasted_iota(jnp.int32, (SUB, half), 1) < HEAD_D
    same_head = ((lax.broadcasted_iota(jnp.int32, (half, half), 0) >> HEAD_SHIFT)
                 == (lax.broadcasted_iota(jnp.int32, (half, half), 1) >> HEAD_SHIFT))

    def pair_sums(p):
        s0 = jnp.sum(jnp.where(lo, p, 0.0), axis=1, keepdims=True)
        s1 = jnp.sum(jnp.where(lo, 0.0, p), axis=1, keepdims=True)
        return jnp.where(lo, s0, s1)

    def chunk(ci, carry):
        t0 = pl.multiple_of((tb // ch - 1 - ci if rev else ci) * ch, ch)
        win = pl.ds(t0, ch)
        a = a_ref[win, :]
        qc = qs_ref[win, :]
        kc = key_ref[win, :]
        vc = v_ref[win, :]
        last = 0 if rev else ch - 1
        a_last = a[last:last + 1, :]
        qe = (qc * jnp.exp(a)).astype(BF16)
        ke = (kc * jnp.exp(a_last - a)).astype(BF16)
        ea = jnp.exp(a_last)
        vb = vc.astype(BF16)
        nslab = ch // SUB
        acc = [[jnp.zeros((SUB, half), F32) for _ in range(2)] for _ in range(nslab)]
        for jj in range(ch):
            for sb in range(nslab):
                r0 = sb * SUB
                if (r0 > jj) if rev else (r0 + SUB - 1 < jj):
                    continue
                whole = (r0 + SUB - 1 <= jj) if rev else (r0 >= jj)
                rs = slice(r0, r0 + SUB)
                valid = (rows + r0 <= jj) if rev else (rows + r0 >= jj)
                for hp in range(2):
                    sl = slice(hp * half, (hp + 1) * half)
                    dec = jnp.exp(jnp.minimum(a[rs, sl] - a[jj:jj + 1, sl], 0.0))
                    p = qc[rs, sl] * dec * kc[jj:jj + 1, sl]
                    if not whole:
                        p = jnp.where(valid, p, 0.0)
                    acc[sb][hp] = acc[sb][hp] + pair_sums(p) * vc[jj:jj + 1, sl]
        for hp in range(2):
            sl = slice(hp * half, (hp + 1) * half)
            s = s_ref[hp]
            inter = _dot_nt(qe[:, sl], s.astype(BF16))
            intra = jnp.concatenate([acc[sb][hp] for sb in range(nslab)], axis=0)
            o_ref[win, sl] = inter + intra
            s_ref[hp] = s * ea[:, sl] + jnp.where(same_head, _dot_tn(vb[:, sl], ke[:, sl]), 0.0)
        return carry

    lax.fori_loop(0, tb // ch, chunk, 0, unroll=2)

    @pl.when(jb == nt - 1)
    def _():
        fin_ref[...] = s_ref[...]


def _gla(z, lb, s0, rev):
    B, T, _ = z.shape
    pair = 2 * HEAD_D
    s0 = s0.reshape(B, 2, 2, HEAD_D, HEAD_D)
    zero = jnp.zeros_like(s0[:, :, 0])
    s0 = jnp.concatenate([jnp.concatenate([s0[:, :, 0], zero], axis=-1),
                          jnp.concatenate([zero, s0[:, :, 1]], axis=-1)], axis=-2)
    tb = min(T, 256)
    nt = T // tb
    di = 1 if rev else 0
    tmap = (lambda j: nt - 1 - j) if rev else (lambda j: j)
    zspec = lambda blk: pl.BlockSpec((None, tb, MIX_W), lambda b, j: (b, tmap(j), blk))
    sspec = pl.BlockSpec((None, 2, pair, pair), lambda b, j: (b, 0, 0, 0))
    blk = pltpu.VMEM((tb, MIX_W), F32)
    o, fin = pl.pallas_call(
        functools.partial(_gla_kernel, rev=rev, tb=tb, nt=nt),
        grid=(B, nt),
        in_specs=[zspec(ZB_HG), zspec(ZB_HG + 1 + di), zspec(ZB_HG + 3),
                  pl.BlockSpec((1, MIX_W), lambda b, j: (0, 0)), sspec],
        out_specs=[pl.BlockSpec((None, tb, MIX_W), lambda b, j: (b, tmap(j), 0)), sspec],
        out_shape=[jax.ShapeDtypeStruct((B, T, MIX_W), F32),
                   jax.ShapeDtypeStruct((B, 2, pair, pair), F32)],
        scratch_shapes=[pltpu.VMEM((2, pair, pair), F32), blk, blk, blk],
        compiler_params=_cparams("parallel", "arbitrary"),
        name="hgrn_gla_bwd" if rev else "hgrn_gla_fwd",
    )(z, z, z, lb[di:di + 1], s0)
    fin = jnp.stack([fin[:, :, :HEAD_D, :HEAD_D], fin[:, :, HEAD_D:, HEAD_D:]], axis=2)
    return o, fin.reshape(B, N_HEADS, HEAD_D, HEAD_D)


def _rw_prep_kernel(*refs, grid_shift, tm):
    if grid_shift:
        (zc_ref, zu_ref, zd_ref, mu_ref, vec_ref, w0_ref, lora_ref,
         r_ref, k2_ref, v_ref, w_ref, nkk_ref, kka_ref, g_ref, bonus_ref, buf_ref) = refs
    else:
        (zc_ref, mu_ref, vec_ref, w0_ref, lora_ref,
         r_ref, k2_ref, v_ref, w_ref, nkk_ref, kka_ref, g_ref, bonus_ref, buf_ref) = refs
    i = pl.program_id(1)
    nt = pl.num_programs(1)
    halo = GRID_W
    width = 4 * MIX_W
    z = zc_ref[...]
    buf_ref[halo:halo + tm, :] = z
    lane = lax.broadcasted_iota(jnp.int32, (tm, width), 1)
    row = lax.broadcasted_iota(jnp.int32, (tm, width), 0)
    if grid_shift:
        buf_ref[0:halo, :] = jnp.where(i > 0, zu_ref[...], 0.0)
        buf_ref[halo + tm:2 * halo + tm, :] = jnp.where(i < nt - 1, zd_ref[...], 0.0)
        col = row & (GRID_W - 1)
        left = jnp.where(col > 0, buf_ref[halo - 1:halo - 1 + tm, :], 0.0)
        right = jnp.where(col < GRID_W - 1, buf_ref[halo + 1:halo + 1 + tm, :], 0.0)
        up = buf_ref[0:tm, :]
        down = buf_ref[2 * halo:2 * halo + tm, :]
        sel = lane & 3
        shifted = jnp.where(sel == 0, left, jnp.where(sel == 1, right, jnp.where(sel == 2, up, down)))
    else:
        zrow = jnp.zeros((1, width), F32)
        buf_ref[halo - 1:halo, :] = zrow
        buf_ref[halo + tm:halo + tm + 1, :] = zrow
        prev = buf_ref[halo - 1:halo - 1 + tm, :]
        nxt = buf_ref[halo + 1:halo + 1 + tm, :]
        shifted = jnp.where((lane & 1) == 0, prev, nxt)
    zs = z + mu_ref[...] * (shifted - z)
    r = zs[:, 0:MIX_W]
    k = zs[:, MIX_W:2 * MIX_W]
    v = zs[:, 2 * MIX_W:3 * MIX_W]
    sm = zs[:, 3 * MIX_W:4 * MIX_W]
    ones_bd = _head_ones()
    a0, k_k, k_a, r_k = (vec_ref[j:j + 1, :] for j in range(4))
    a = jax.nn.sigmoid(a0 + _dot(sm.astype(BF16), lora_ref[2]))
    g_ref[...] = _dot(jax.nn.sigmoid(sm).astype(BF16), lora_ref[3])
    kk = k * k_k
    kk = kk * lax.rsqrt(_head_sum(kk * kk, ones_bd) + 1e-12)
    k2 = k * (1.0 + (a - 1.0) * k_a)
    th = jnp.tanh(sm).astype(BF16)
    for di in range(2):
        w_ref[di] = -math.exp(-0.5) * jax.nn.sigmoid(w0_ref[di:di + 1, :] + _dot(th, lora_ref[di]))
    r_ref[...] = r
    k2_ref[...] = k2
    v_ref[...] = v
    nkk_ref[...] = -kk
    kka_ref[...] = kk * a
    bonus_ref[...] = _head_sum(r * k2 * r_k, ones_bd) * v


def _rw_prep(z, mu_p, vecs, w0, lora, grid_shift):
    B, T, _ = z.shape
    width = 4 * MIX_W
    wblk = ZB_RW // 4
    if grid_shift:
        tm = min(T, 512)
        hb = tm // GRID_W
        nh = T // GRID_W
        z_specs = [
            pl.BlockSpec((None, tm, width), lambda b, i: (b, i, wblk)),
            pl.BlockSpec((None, GRID_W, width), lambda b, i: (b, jnp.maximum(i * hb - 1, 0), wblk)),
            pl.BlockSpec((None, GRID_W, width), lambda b, i: (b, jnp.minimum((i + 1) * hb, nh - 1), wblk)),
        ]
        z_args = (z, z, z)
    else:
        tm = T
        z_specs = [pl.BlockSpec((None, tm, width), lambda b, i: (b, i, wblk))]
        z_args = (z,)
    const = lambda shape: pl.BlockSpec(shape, lambda b, i: (0,) * len(shape))
    ospec = pl.BlockSpec((None, tm, MIX_W), lambda b, i: (b, i, 0))
    oshape = jax.ShapeDtypeStruct((B, T, MIX_W), F32)
    dspec = pl.BlockSpec((2, None, tm, MIX_W), lambda b, i: (0, b, i, 0))
    dshape = jax.ShapeDtypeStruct((2, B, T, MIX_W), F32)
    return pl.pallas_call(
        functools.partial(_rw_prep_kernel, grid_shift=grid_shift, tm=tm),
        grid=(B, T // tm),
        in_specs=z_specs + [const((1, width)), const((4, MIX_W)), const((2, MIX_W)),
                            const((4, MIX_W, MIX_W))],
        out_specs=[ospec, ospec, ospec, dspec, ospec, ospec, ospec, ospec],
        out_shape=[oshape, oshape, oshape, dshape, oshape, oshape, oshape, oshape],
        scratch_shapes=[pltpu.VMEM((tm + 2 * GRID_W, width), F32)],
        compiler_params=_cparams("parallel", "parallel"),
        name="rwkv_prep",
    )(*z_args, mu_p, vecs, w0, lora)


def _seq_kernel(*refs, streams, nb, ub, tb, nt):
    refs = list(refs)
    take = lambda n: [refs.pop(0) for _ in range(n)]
    in_refs = [take(6 if use_sa else 4) for use_sa, _ in streams]
    s0_refs = take(len(streams))
    y_refs = take(len(streams))
    fin_refs = take(len(streams))
    s_refs = take(len(streams))
    j = pl.program_id(0)
    pair_w = 2 * HEAD_D
    group = 8

    @pl.when(j == 0)
    def _():
        for s_ref, s0_ref in zip(s_refs, s0_refs):
            s_ref[...] = s0_ref[...]

    lane = lax.broadcasted_iota(jnp.int32, (HEAD_D, pair_w), 1)
    sub = lax.broadcasted_iota(jnp.int32, (HEAD_D, pair_w), 0)
    lo = lane < HEAD_D
    diag = (lane & (HEAD_D - 1)) == sub
    r2 = lax.broadcasted_iota(jnp.int32, (2 * pair_w, pair_w), 0)
    c2 = lax.broadcasted_iota(jnp.int32, (2 * pair_w, pair_w), 1)
    ones2 = jnp.where(((r2 >> HEAD_SHIFT) & 1) == (c2 >> HEAD_SHIFT), 1.0, 0.0).astype(BF16)

    def split(x):
        hi = x.astype(BF16).astype(F32)
        return hi, (x - hi).astype(BF16).astype(F32)

    def seg_sum_mxu(hi, lo_):
        return _dot(jnp.concatenate([hi.astype(BF16), lo_.astype(BF16)], axis=1), ones2)

    def seg_sum_xlu(p):
        s_lo = jnp.sum(jnp.where(lo, p, 0.0), axis=1, keepdims=True)
        s_hi = jnp.sum(jnp.where(lo, 0.0, p), axis=1, keepdims=True)
        return jnp.where(lo, s_lo, s_hi)

    def chain(si, b, p, g):
        use_sa, rev = streams[si]
        t0 = pl.multiple_of((tb // group - 1 - g if rev else g) * group, group)
        sl = pl.ds(p * pair_w, pair_w)
        blk = lambda ref: ref[b, pl.ds(t0, group), sl]
        w, k, r, v = (blk(ref) for ref in in_refs[si][:4])
        if use_sa:
            a, bb = blk(in_refs[si][4]), blk(in_refs[si][5])
            v_hi, v_lo = split(v)
        s = s_refs[si][b, p]
        ys = [None] * group
        for i in range(group):
            t = group - 1 - i if rev else i
            row = lambda x: x[t:t + 1, :]
            if use_sa:
                vcol = seg_sum_mxu(jnp.where(diag, row(v_hi), 0.0), jnp.where(diag, row(v_lo), 0.0))
                s = s * row(w) + seg_sum_xlu(s * row(a)) * row(bb)
            else:
                vcol = seg_sum_xlu(jnp.where(diag, row(v), 0.0))
                s = s * row(w)
            s = s + vcol * row(k)
            yb = seg_sum_mxu(*split(s * row(r)))
            ys[t] = jnp.sum(jnp.where(diag, yb, 0.0), axis=0, keepdims=True)
        s_refs[si][b, p] = s
        y_refs[si][b, pl.ds(t0, group), sl] = jnp.concatenate(ys, axis=0)

    def group_step(g, carry):
        def batch_step(bg, c):
            for si in range(len(streams)):
                for u in range(ub):
                    for p in range(2):
                        chain(si, bg * ub + u, p, g)
            return c

        return lax.fori_loop(0, nb // ub, batch_step, carry)

    lax.fori_loop(0, tb // group, group_step, 0)

    @pl.when(j == nt - 1)
    def _():
        for fin_ref, s_ref in zip(fin_refs, s_refs):
            fin_ref[...] = s_ref[...]


SCAN_BLOCK_ROWS = 512
SCAN_CHAINS = 16


def _seq_scan(mixers):
    B, T = mixers[0][3][0].shape[-3], mixers[0][3][0].shape[-2]
    tb = max(8, min(T, SCAN_BLOCK_ROWS // B))
    nt = T // tb
    ub = max(u for u in (1, 2, 4, 8) if B % u == 0 and u * 4 * len(mixers) <= max(SCAN_CHAINS, 4 * len(mixers)))
    streams, args, in_specs, s0_args = [], [], [], []
    for w, k, r, v, a, b, s0 in mixers:
        for di in range(2):
            tmap = (lambda j: nt - 1 - j) if di else (lambda j: j)
            streams.append((a is not None, di == 1))
            for arr, blk in [x for x in (w, k, r, v, a, b) if x is not None]:
                args.append(arr)
                if arr.ndim == 4:
                    in_specs.append(pl.BlockSpec((None, B, tb, MIX_W),
                                                 lambda j, di=di, tmap=tmap, blk=blk: (di, 0, tmap(j), blk)))
                else:
                    in_specs.append(pl.BlockSpec((B, tb, MIX_W), lambda j, tmap=tmap, blk=blk: (0, tmap(j), blk)))
            s0_args.append(s0[:, di])
    n = len(streams)
    sspec = pl.BlockSpec((B, 2, HEAD_D, 2 * HEAD_D), lambda j: (0, 0, 0, 0))
    state = jax.ShapeDtypeStruct((B, 2, HEAD_D, 2 * HEAD_D), F32)
    y_specs = [pl.BlockSpec((B, tb, MIX_W), (lambda j: (0, nt - 1 - j, 0)) if rev else (lambda j: (0, j, 0)))
               for _, rev in streams]
    outs = pl.pallas_call(
        functools.partial(_seq_kernel, streams=tuple(streams), nb=B, ub=ub, tb=tb, nt=nt),
        grid=(nt,),
        in_specs=in_specs + [sspec] * n,
        out_specs=y_specs + [sspec] * n,
        out_shape=[jax.ShapeDtypeStruct((B, T, MIX_W), F32)] * n + [state] * n,
        scratch_shapes=[pltpu.VMEM((B, 2, HEAD_D, 2 * HEAD_D), F32)] * n,
        compiler_params=_cparams("arbitrary"),
        name="state_scan",
    )(*args, *s0_args)
    ys, fins = outs[:n], outs[n:]
    return [((ys[2 * m], ys[2 * m + 1]), jnp.stack([fins[2 * m], fins[2 * m + 1]], axis=1))
            for m in range(len(mixers))]


def _pair_tiles(s):
    B = s.shape[0]
    s = s.reshape(B, 2, 2, 2, HEAD_D, HEAD_D).transpose(0, 1, 2, 4, 3, 5)
    return s.reshape(B, 2, 2, HEAD_D, 2 * HEAD_D)


def _unpair_tiles(s):
    B = s.shape[0]
    s = s.reshape(B, 2, 2, HEAD_D, 2, HEAD_D).transpose(0, 1, 2, 4, 3, 5)
    return s.reshape(B, 2, N_HEADS, HEAD_D, HEAD_D)


RW_CHUNK = 64
RW_CHUNK_SHIFT = 6


def _mm(a, b):
    return jnp.dot(a.astype(BF16), b.astype(BF16), preferred_element_type=F32)


def _rwkv_kernel(lw_ref, a_ref, b_ref, k_ref, r_ref, v_ref, s0_ref, y_ref, fin_ref, s_ref, g_ref,
                 *, rev, tb, nt):
    jb = pl.program_id(1)
    ch = RW_CHUNK
    pair = 2 * HEAD_D

    @pl.when(jb == 0)
    def _():
        s_ref[...] = s0_ref[...]

    lw = lw_ref[...]
    rr = lax.broadcasted_iota(jnp.int32, (tb, tb), 0)
    cc = lax.broadcasted_iota(jnp.int32, (tb, tb), 1)
    same = (rr >> RW_CHUNK_SHIFT) == (cc >> RW_CHUNK_SHIFT)
    tri = jnp.where(same & ((cc >= rr) if rev else (cc <= rr)), 1.0, 0.0).astype(BF16)
    p1 = lw.astype(BF16)
    r1 = lw - p1.astype(F32)
    p2 = r1.astype(BF16)
    p3 = (r1 - p2.astype(F32)).astype(BF16)
    g_ref[...] = _dot(tri, p1) + _dot(tri, p2) + _dot(tri, p3)

    si = lax.broadcasted_iota(jnp.int32, (ch, ch), 0)
    ri = lax.broadcasted_iota(jnp.int32, (ch, ch), 1)
    if rev:
        si, ri = ch - 1 - si, ch - 1 - ri
    strict = ri < si
    incl = ri <= si
    eye = jnp.where(ri == si, 1.0, 0.0).astype(F32)
    levels = [((si >> (lv + 1)) == (ri >> (lv + 1))) & (((si >> lv) & 1) == 1) & (((ri >> lv) & 1) == 0)
              for lv in range(RW_CHUNK_SHIFT)]
    lo = lax.broadcasted_iota(jnp.int32, (ch, pair), 1) < HEAD_D
    same_head = ((lax.broadcasted_iota(jnp.int32, (pair, pair), 0) >> HEAD_SHIFT)
                 == (lax.broadcasted_iota(jnp.int32, (pair, pair), 1) >> HEAD_SHIFT))

    nch = tb // ch
    g = g_ref[...]
    e_g = jnp.exp(g)
    e_ng = jnp.exp(-g)
    at_f = a_ref[...] * jnp.exp(g - lw)
    rt_f = r_ref[...] * e_g
    at = at_f.astype(BF16)
    rt = rt_f.astype(BF16)
    first = (lax.broadcasted_iota(jnp.int32, (tb, MIX_W), 1) & HEAD_D) == 0
    at_h = [jnp.where(first, at_f, 0.0).astype(BF16), jnp.where(first, 0.0, at_f).astype(BF16)]
    rt_h = [jnp.where(first, rt_f, 0.0).astype(BF16), jnp.where(first, 0.0, rt_f).astype(BF16)]
    bt = (b_ref[...] * e_ng).astype(BF16)
    kt = (k_ref[...] * e_ng).astype(BF16)
    vb = v_ref[...].astype(BF16)

    items = [(c, hp, hh) for c in range(nch) for hp in range(2) for hh in range(2)]
    rows = lambda c: slice(c * ch, (c + 1) * ch)
    lanes = lambda hp: slice(hp * pair, (hp + 1) * pair)
    n_m, p_m, m_m, q_m = {}, {}, {}, {}
    for it in items:
        c, hp, hh = it
        ar = jnp.concatenate([at_h[hh][rows(c), lanes(hp)], rt_h[hh][rows(c), lanes(hp)]], axis=0)
        np_ = _dot_nt(ar, bt[rows(c), lanes(hp)])
        mq = _dot_nt(ar, kt[rows(c), lanes(hp)])
        n_m[it] = jnp.where(strict, np_[:ch], 0.0).astype(BF16)
        p_m[it] = jnp.where(incl, np_[ch:], 0.0).astype(BF16)
        m_m[it] = jnp.where(strict, mq[:ch], 0.0).astype(BF16)
        q_m[it] = jnp.where(incl, mq[ch:], 0.0).astype(BF16)
    t_m = {it: eye + jnp.where(levels[0], n_m[it].astype(F32), 0.0) for it in items}
    for lv in range(1, RW_CHUNK_SHIFT):
        tn = {it: _mm(t_m[it], n_m[it]) for it in items}
        t_m = {it: t_m[it] + jnp.where(levels[lv], _mm(tn[it], t_m[it]), 0.0) for it in items}
    mv = {it: _mm(m_m[it], vb[rows(it[0]), lanes(it[1])]) for it in items}
    qv = {it: _mm(q_m[it], vb[rows(it[0]), lanes(it[1])]) for it in items}
    y2 = {it: _mm(t_m[it], jnp.concatenate([at[rows(it[0]), lanes(it[1])], mv[it].astype(BF16)], axis=1))
          for it in items}

    last = 0 if rev else ch - 1
    for ci in range(nch):
        c = nch - 1 - ci if rev else ci
        g_c = g[c * ch + last:c * ch + last + 1, :]
        e_gc = jnp.exp(g_c - g[rows(c), :])
        bh = (b_ref[rows(c), :] * e_gc).astype(BF16)
        kh = (k_ref[rows(c), :] * e_gc).astype(BF16)
        dec_c = jnp.exp(g_c)
        xs, sas, ss = [], [], []
        for hp in range(2):
            s = s_ref[hp]
            w2 = jnp.where(lo, y2[(c, hp, 0)][:, :pair], y2[(c, hp, 1)][:, :pair]).astype(BF16)
            xs.append(_dot_nt(jnp.concatenate([w2, rt[rows(c), lanes(hp)]], axis=0), s.astype(BF16)))
            ss.append(s)
        for hp in range(2):
            w1 = jnp.where(lo, y2[(c, hp, 0)][:, pair:], y2[(c, hp, 1)][:, pair:])
            sas.append(w1 + xs[hp][:ch])
        for hp in range(2):
            sa = sas[hp]
            y_ref[rows(c), lanes(hp)] = xs[hp][ch:] + jnp.where(
                lo, _mm(p_m[(c, hp, 0)], sa) + qv[(c, hp, 0)], _mm(p_m[(c, hp, 1)], sa) + qv[(c, hp, 1)])
            upd = _dot_tn(jnp.concatenate([sa.astype(BF16), vb[rows(c), lanes(hp)]], axis=0),
                          jnp.concatenate([bh[:, lanes(hp)], kh[:, lanes(hp)]], axis=0))
            s_ref[hp] = ss[hp] * dec_c[:, lanes(hp)] + jnp.where(same_head, upd, 0.0)

    @pl.when(jb == nt - 1)
    def _():
        fin_ref[...] = s_ref[...]


def _head_pair_blockdiag(s):
    B = s.shape[0]
    s = s.reshape(B, 2, 2, HEAD_D, HEAD_D)
    zero = jnp.zeros_like(s[:, :, 0])
    return jnp.concatenate([jnp.concatenate([s[:, :, 0], zero], axis=-1),
                            jnp.concatenate([zero, s[:, :, 1]], axis=-1)], axis=-2)


def _head_pair_blocks(t):
    B = t.shape[0]
    t = jnp.stack([t[:, :, :HEAD_D, :HEAD_D], t[:, :, HEAD_D:, HEAD_D:]], axis=2)
    return t.reshape(B, N_HEADS, HEAD_D, HEAD_D)


def _rwkv(lw, a, b, k, r, v, s0, rev):
    B, T, _ = v.shape
    pair = 2 * HEAD_D
    tb = min(T, 256)
    nt = T // tb
    di = 1 if rev else 0
    tmap = (lambda j: nt - 1 - j) if rev else (lambda j: j)
    spec = pl.BlockSpec((None, tb, MIX_W), lambda bi, j: (bi, tmap(j), 0))
    sspec = pl.BlockSpec((None, 2, pair, pair), lambda bi, j: (bi, 0, 0, 0))
    y, fin = pl.pallas_call(
        functools.partial(_rwkv_kernel, rev=rev, tb=tb, nt=nt),
        grid=(B, nt),
        in_specs=[pl.BlockSpec((None, None, tb, MIX_W), lambda bi, j: (di, bi, tmap(j), 0)),
                  spec, spec, spec, spec, spec, sspec],
        out_specs=[spec, sspec],
        out_shape=[jax.ShapeDtypeStruct((B, T, MIX_W), F32),
                   jax.ShapeDtypeStruct((B, 2, pair, pair), F32)],
        scratch_shapes=[pltpu.VMEM((2, pair, pair), F32), pltpu.VMEM((tb, MIX_W), F32)],
        compiler_params=_cparams("parallel", "arbitrary"),
        name="rwkv_chunk_bwd" if rev else "rwkv_chunk_fwd",
    )(lw, a, b, k, r, v, _head_pair_blockdiag(s0))
    return y, _head_pair_blocks(fin)


def _mix_kernel(x_ref, sc_ref, sh_ref, g1_ref, nw0_ref, nw_ref, wg_ref,
                ret_ref, retg_ref, s5_ref, s5u_ref, hgf_ref, hgb_ref, hgg_ref, rwf_ref, rwr_ref,
                rwb_ref, rwg_ref,
                vec_ref, glu_w_ref, wbr_ref, wout_ref, o_ref):
    ones_bd = _head_ones()
    h = (_rms(x_ref[...], nw0_ref[...]) * (1.0 + sc_ref[...]) + sh_ref[...]).astype(BF16)
    gn_w, s5_d, glu_b, hg_w, ln_w, ln_b = (vec_ref[j:j + 1, :] for j in range(6))

    def group_norm(o, eps):
        mu = _head_sum(o, ones_bd) * (1.0 / HEAD_D)
        oc = o - mu
        var = _head_sum(oc * oc, ones_bd) * (1.0 / HEAD_D)
        return oc * lax.rsqrt(var + eps)

    g = retg_ref[...]
    y_ret = group_norm(ret_ref[0] + ret_ref[1], EPS) * gn_w * (g * jax.nn.sigmoid(g))

    y = s5_d * s5u_ref[...] + s5_ref[0] + s5_ref[1]
    yg = jax.nn.gelu(y)
    y_s5 = yg * jax.nn.sigmoid(_dot(yg.astype(BF16), glu_w_ref[...]) + glu_b)

    o = hgf_ref[...] + hgb_ref[...]
    g = hgg_ref[...]
    ms = _head_sum(o * o, ones_bd) * (1.0 / HEAD_D)
    y_hg = o * lax.rsqrt(ms + EPS) * hg_w * (g * jax.nn.sigmoid(g))

    y = group_norm(rwf_ref[...] + rwr_ref[...], RW_LN_EPS) * ln_w + ln_b
    y_rw = (y + rwb_ref[...]) * rwg_ref[...]

    mixed = None
    for m, ym in enumerate((y_ret, y_s5, y_hg, y_rw)):
        br = _dot(ym.astype(BF16), wbr_ref[m])
        term = jax.nn.sigmoid(_dot(h, wg_ref[:, m * D_MODEL:(m + 1) * D_MODEL])) * br
        mixed = term if mixed is None else mixed + term
    mixed = _dot(mixed.astype(BF16), wout_ref[...])
    o_ref[...] = x_ref[...] + g1_ref[...] * _rms(mixed, nw_ref[...])


def _mix(x, sc, sh, g1, nw0, nw, w_gate, z, ret_o, s5_y, hg_o, rw_y, rw_bonus, rw_g, vecs, glu_w,
         w_branch, w_out):
    B, T, _ = x.shape
    tm = min(T, 256)
    xspec = pl.BlockSpec((None, tm, D_MODEL), lambda b, i: (b, i, 0))
    zspec = lambda blk: pl.BlockSpec((None, tm, MIX_W), lambda b, i: (b, i, blk))
    dspec = pl.BlockSpec((2, None, tm, MIX_W), lambda b, i: (0, b, i, 0))
    const = lambda shape: pl.BlockSpec(shape, lambda b, i: (0,) * len(shape))
    mspec = lambda m: pl.BlockSpec((None, 1, D_MODEL), _mod_map(m.shape[0]))
    return pl.pallas_call(
        _mix_kernel,
        grid=(B, T // tm),
        in_specs=[
            xspec, mspec(sc), mspec(sh), mspec(g1),
            const((1, D_MODEL)), const((1, D_MODEL)),
            const((D_MODEL, 4 * D_MODEL)),
            dspec, zspec(ZB_RET + 3),
            dspec, zspec(ZB_S5),
            zspec(0), zspec(0), zspec(ZB_HG + 4),
            zspec(0), zspec(0), zspec(0), zspec(0),
            const((6, MIX_W)), const((MIX_W, MIX_W)),
            const((4, MIX_W, D_MODEL)), const((D_MODEL, D_MODEL)),
        ],
        out_specs=xspec,
        out_shape=jax.ShapeDtypeStruct((B, T, D_MODEL), F32),
        compiler_params=_cparams("parallel", "parallel"),
        name="mix_out",
    )(x, sc, sh, g1, nw0, nw, w_gate, ret_o, z, s5_y, z, hg_o[0], hg_o[1], z, rw_y[0], rw_y[1],
      rw_bonus, rw_g, vecs, glu_w, w_branch, w_out)


def _ffn_kernel(x_ref, sc_ref, sh_ref, g2_ref, nw2_ref, nw3_ref, w1_ref, w2_ref, o_ref, h_ref, acc_ref):
    j = pl.program_id(2)

    @pl.when(j == 0)
    def _():
        h = _rms(x_ref[...], nw2_ref[...]) * (1.0 + sc_ref[...]) + sh_ref[...]
        h_ref[...] = h.astype(BF16)
        acc_ref[...] = jnp.zeros_like(acc_ref)

    a = jnp.maximum(_dot(h_ref[...], w1_ref[...]), 0.0)
    acc_ref[...] += _dot((a * a).astype(BF16), w2_ref[...])

    @pl.when(j == pl.num_programs(2) - 1)
    def _():
        o_ref[...] = x_ref[...] + g2_ref[...] * _rms(acc_ref[...], nw3_ref[...])


def _ffn(x, sc, sh, g2, nw2, nw3, w1, w2):
    B, T, _ = x.shape
    tm = min(T, 512)
    tf = 1024
    xspec = pl.BlockSpec((None, tm, D_MODEL), lambda b, i, j: (b, i, 0))
    mspec = lambda m: pl.BlockSpec((None, 1, D_MODEL), _mod_map(m.shape[0]))
    const = pl.BlockSpec((1, D_MODEL), lambda b, i, j: (0, 0))
    return pl.pallas_call(
        _ffn_kernel,
        grid=(B, T // tm, D_FF // tf),
        in_specs=[xspec, mspec(sc), mspec(sh), mspec(g2), const, const,
                  pl.BlockSpec((D_MODEL, tf), lambda b, i, j: (0, j)),
                  pl.BlockSpec((tf, D_MODEL), lambda b, i, j: (j, 0))],
        out_specs=xspec,
        out_shape=jax.ShapeDtypeStruct((B, T, D_MODEL), F32),
        scratch_shapes=[pltpu.VMEM((tm, D_MODEL), BF16), pltpu.VMEM((tm, D_MODEL), F32)],
        compiler_params=_cparams("parallel", "parallel", "arbitrary"),
        name="ffn",
    )(x, sc, sh, g2, nw2, nw3, w1, w2)


def _layer_params(l, p):
    w = p['w_in'][l]
    w_p = jnp.concatenate(
        [w[:, 2560:3488], jnp.zeros((D_MODEL, 4 * MIX_W - RW_COLS), F32),
         w[:, 0:1024], w[:, 1024:1280], w[:, 1280:2560]], axis=1).astype(BF16)
    w_gate = w[:, 3488:7584].astype(BF16)

    a_re, a_im, bb_re, bb_im = _s5_zoh(p['s5_lam_re'][l], p['s5_lam_im'][l], p['s5_log_dt'][l],
                                       p['s5_b_re'][l], p['s5_b_im'][l])
    eye = jnp.eye(S5_GROUPS, dtype=F32)
    bblk = lambda bb: jnp.einsum('dgph,gk->dghkp', bb, eye).reshape(2, MIX_W, S5_STATE).astype(BF16)
    cblk = lambda c: jnp.einsum('ghp,gk->gpkh', c, eye).reshape(S5_STATE, MIX_W).astype(BF16)

    def lora_pad(m, row0):
        return jnp.zeros((MIX_W, MIX_W), F32).at[row0:row0 + m.shape[0]].set(m)

    lora = jnp.stack([lora_pad(p['rw_w2'][l, 0], 0), lora_pad(p['rw_w2'][l, 1], 32),
                      lora_pad(p['rw_a2'][l], 64), lora_pad(p['rw_g2'][l], 96)]).astype(BF16)
    mu_p = jnp.concatenate([p['rw_mu'][l], jnp.zeros((4 * MIX_W - RW_COLS,), F32)]).reshape(1, 4 * MIX_W)
    return dict(
        w_p=w_p, w_gate=w_gate, nw=p['norm_w'][l],
        s5=(a_re, a_im, bblk(bb_re), bblk(bb_im), cblk(p['s5_c_re'][l]), cblk(p['s5_c_im'][l])),
        rw_mu=mu_p, rw_lora=lora, rw_w0=p['rw_w0'][l],
        rw_vecs=jnp.stack([p['rw_a0'][l], p['rw_k_k'][l], p['rw_k_a'][l], p['rw_r_k'][l]]),
        mix_vecs=jnp.stack([p['ret_gn_w'][l], p['s5_d'][l], p['s5_glu_b'][l], p['hg_norm_w'][l],
                            p['rw_ln_w'][l], p['rw_ln_b'][l]]),
        glu_w=p['s5_glu_w'][l].astype(BF16),
        w_branch=p['w_branch'][l].astype(BF16), w_out=p['w_out'][l].astype(BF16),
        ff_w1=p['ff_w1'][l].astype(BF16), ff_w2=p['ff_w2'][l].astype(BF16),
    )


def _trunk_layer(x, mod, init, grid_shift, lp, hg_lb):
    B = x.shape[0]
    sh1, sc1, g1, sh2, sc2, g2 = (m[:, None, :] for m in jnp.split(mod, 6, axis=-1))
    nw = lp['nw']
    z = _proj_in(x, sc1, sh1, nw[0:1], lp['w_p'])

    s_ret, s_s5r, s_s5i, s_hg, s_rw = init
    ret_o, f_ret = _retention(z, s_ret)

    s5_y, f_s5r, f_s5i = _s5(z, *lp['s5'], s_s5r.reshape(B, 2, 1, S5_STATE), s_s5i.reshape(B, 2, 1, S5_STATE))
    f_s5r = f_s5r.reshape(B, 2, S5_GROUPS, S5_P)
    f_s5i = f_s5i.reshape(B, 2, S5_GROUPS, S5_P)

    s_hg_t = jnp.swapaxes(s_hg, -1, -2)
    hg_f, f_hg_f = _gla(z, hg_lb, s_hg_t[:, 0], False)
    hg_b, f_hg_b = _gla(z, hg_lb, s_hg_t[:, 1], True)
    hg_o = (hg_f, hg_b)
    f_hg = jnp.swapaxes(jnp.stack([f_hg_f, f_hg_b], axis=1), -1, -2)

    r, k2, v, w, nkk, kka, rw_g, bonus = _rw_prep(z, lp['rw_mu'], lp['rw_vecs'], lp['rw_w0'],
                                                  lp['rw_lora'], grid_shift)
    rw_f, f_rw_f = _rwkv(w, nkk, kka, k2, r, v, s_rw[:, 0], False)
    rw_b, f_rw_b = _rwkv(w, nkk, kka, k2, r, v, s_rw[:, 1], True)
    rw_y = (rw_f, rw_b)
    f_rw = jnp.stack([f_rw_f, f_rw_b], axis=1)

    x = _mix(x, sc1, sh1, g1, nw[0:1], nw[1:2], lp['w_gate'], z, ret_o, s5_y, hg_o, rw_y, bonus, rw_g,
             lp['mix_vecs'], lp['glu_w'], lp['w_branch'], lp['w_out'])
    x = _ffn(x, sc2, sh2, g2, nw[2:3], nw[3:4], lp['ff_w1'], lp['ff_w2'])
    return x, (f_ret, f_s5r, f_s5i, f_hg, f_rw)


def kernel(x_prompt, x_sample, state_ret, state_s5_re, state_s5_im, state_hgrn, state_rwkv, c, c_ctx, ada_w, ada_b, norm_w, w_in, ret_gn_w, s5_lam_re, s5_lam_im, s5_log_dt, s5_b_re, s5_b_im, s5_c_re, s5_c_im, s5_d, s5_glu_w, s5_glu_b, hg_lb, hg_norm_w, rw_mu, rw_w0, rw_w2, rw_a0, rw_a2, rw_g2, rw_k_k, rw_k_a, rw_r_k, rw_ln_w, rw_ln_b, w_branch, w_out, ff_w1, ff_w2):
    p = dict(norm_w=norm_w, w_in=w_in, ret_gn_w=ret_gn_w, s5_lam_re=s5_lam_re, s5_lam_im=s5_lam_im,
             s5_log_dt=s5_log_dt, s5_b_re=s5_b_re, s5_b_im=s5_b_im, s5_c_re=s5_c_re, s5_c_im=s5_c_im,
             s5_d=s5_d, s5_glu_w=s5_glu_w, s5_glu_b=s5_glu_b, hg_norm_w=hg_norm_w, rw_mu=rw_mu,
             rw_w0=rw_w0, rw_w2=rw_w2, rw_a0=rw_a0, rw_a2=rw_a2, rw_g2=rw_g2, rw_k_k=rw_k_k,
             rw_k_a=rw_k_a, rw_r_k=rw_r_k, rw_ln_w=rw_ln_w, rw_ln_b=rw_ln_b, w_branch=w_branch,
             w_out=w_out, ff_w1=ff_w1, ff_w2=ff_w2)
    depth = w_in.shape[0]
    n_ctx = x_prompt.shape[0]
    n_lat = x_sample.shape[0]
    assert 1 + n_lat <= 8

    lb_cum = jnp.cumsum(jax.nn.softmax(hg_lb.astype(F32), axis=0), axis=0)
    hg_lower = lb_cum - lb_cum[0]

    cond = jnp.concatenate([c_ctx[None, :], c, jnp.zeros((7 - n_lat, D_MODEL), F32)], axis=0)
    mod = _ada_mod(cond, ada_w, ada_b)

    zero_state = (jnp.zeros((n_ctx, 2, N_HEADS, HEAD_D, HEAD_D), F32),
                  jnp.zeros((n_ctx, 2, S5_GROUPS, S5_P), F32),
                  jnp.zeros((n_ctx, 2, S5_GROUPS, S5_P), F32),
                  jnp.zeros((n_ctx, 2, N_HEADS, HEAD_D, HEAD_D), F32),
                  jnp.zeros((n_ctx, 2, N_HEADS, HEAD_D, HEAD_D), F32))
    xp, xs = x_prompt, x_sample
    finals = []
    for l in range(depth):
        lp = _layer_params(l, p)
        xp, fin = _trunk_layer(xp, mod[l, 0:1], zero_state, False, lp, hg_lower[l])
        finals.append(fin)
        lat_init = (state_ret[:, l], state_s5_re[:, l], state_s5_im[:, l], state_hgrn[:, l], state_rwkv[:, l])
        xs, _ = _trunk_layer(xs, mod[l, 1:1 + n_lat], lat_init, True, lp, hg_lower[l])
    new_states = tuple(jnp.stack([f[i] for f in finals], axis=1) for i in range(5))
    return (xp, xs) + new_states
```

```python
import functools
import math

import jax
import jax.numpy as jnp
from jax import lax
from jax.experimental import pallas as pl
from jax.experimental.pallas import tpu as pltpu

F32 = jnp.float32
BF16 = jnp.bfloat16
HIGHEST = lax.Precision.HIGHEST

D_MODEL = 1024
GRID_W = 64
MIX_W = 256
N_HEADS = 4
HEAD_D = 64
HEAD_SHIFT = 6
LANES = 128
S5_GROUPS = 16
S5_CH = 16
S5_P = 64
S5_STATE = S5_GROUPS * S5_P
D_FF = 4096
EPS = 1e-6
RW_LN_EPS = 64e-5
RW_COLS = 928

Z_COLS = 3584
ZB_RW = 0
ZB_RET = 4
ZB_S5 = 8
ZB_HG = 9

VMEM_LIMIT = 56 * 1024 * 1024

LOG_GAMMA = tuple(
    tuple(math.log1p(-2.0 ** (-(5.0 + 0.5 * di) - h)) for h in range(N_HEADS)) for di in range(2))


def _cparams(*sem):
    return pltpu.CompilerParams(dimension_semantics=sem, vmem_limit_bytes=VMEM_LIMIT)


def _dot(a, b):
    return jnp.dot(a, b, preferred_element_type=F32)


def _dot_nt(a, b):
    return lax.dot_general(a, b, (((1,), (1,)), ((), ())), preferred_element_type=F32)


def _dot_tn(a, b):
    return lax.dot_general(a, b, (((0,), (0,)), ((), ())), preferred_element_type=F32)


def _head_ones():
    r = lax.broadcasted_iota(jnp.int32, (MIX_W, MIX_W), 0) >> HEAD_SHIFT
    c = lax.broadcasted_iota(jnp.int32, (MIX_W, MIX_W), 1) >> HEAD_SHIFT
    return jnp.where(r == c, 1.0, 0.0).astype(F32)


def _head_sum(x, ones_bd):
    return jnp.dot(x, ones_bd, precision=HIGHEST, preferred_element_type=F32)


def _rms(x, w):
    return x * lax.rsqrt(jnp.mean(x * x, axis=-1, keepdims=True) + EPS) * w


def _ada_kernel(c_ref, w_ref, b_ref, o_ref):
    c = c_ref[...]
    s = c * jax.nn.sigmoid(c)
    o_ref[...] = _dot(s.astype(BF16), w_ref[...].astype(BF16)) + b_ref[...]


def _ada_mod(cond, ada_w, ada_b):
    L = ada_w.shape[0]
    n = ada_w.shape[2]
    tn = 1536
    return pl.pallas_call(
        _ada_kernel,
        grid=(L, n // tn),
        in_specs=[
            pl.BlockSpec((8, D_MODEL), lambda l, j: (0, 0)),
            pl.BlockSpec((None, D_MODEL, tn), lambda l, j: (l, 0, j)),
            pl.BlockSpec((None, 1, tn), lambda l, j: (l, 0, j)),
        ],
        out_specs=pl.BlockSpec((None, 8, tn), lambda l, j: (l, 0, j)),
        out_shape=jax.ShapeDtypeStruct((L, 8, n), F32),
        compiler_params=_cparams("parallel", "parallel"),
        name="ada_mod",
    )(cond, ada_w, ada_b.reshape(L, 1, n))


def _proj_in_kernel(x_ref, sc_ref, sh_ref, nw_ref, w_ref, z_ref, h_ref):
    @pl.when(pl.program_id(2) == 0)
    def _():
        h = _rms(x_ref[...], nw_ref[...]) * (1.0 + sc_ref[...]) + sh_ref[...]
        h_ref[...] = h.astype(BF16)

    z_ref[...] = _dot(h_ref[...], w_ref[...])


def _mod_map(bm):
    if bm == 1:
        return lambda b, *_: (0, 0, 0)
    return lambda b, *_: (b, 0, 0)


def _proj_in(x, sc, sh, nw, w_p):
    B, T, _ = x.shape
    tm = min(T, 512)
    tn = 1792
    return pl.pallas_call(
        _proj_in_kernel,
        grid=(B, T // tm, Z_COLS // tn),
        in_specs=[
            pl.BlockSpec((None, tm, D_MODEL), lambda b, i, j: (b, i, 0)),
            pl.BlockSpec((None, 1, D_MODEL), _mod_map(sc.shape[0])),
            pl.BlockSpec((None, 1, D_MODEL), _mod_map(sh.shape[0])),
            pl.BlockSpec((1, D_MODEL), lambda b, i, j: (0, 0)),
            pl.BlockSpec((D_MODEL, tn), lambda b, i, j: (0, j)),
        ],
        out_specs=pl.BlockSpec((None, tm, tn), lambda b, i, j: (b, i, j)),
        out_shape=jax.ShapeDtypeStruct((B, T, Z_COLS), F32),
        scratch_shapes=[pltpu.VMEM((tm, D_MODEL), BF16)],
        compiler_params=_cparams("parallel", "parallel", "arbitrary"),
        name="proj_in",
    )(x, sc, sh, nw, w_p)


def _ret_kernel(q_ref, k_ref, v_ref, s0_ref, o_ref, fin_ref, s_ref, *, lc, nc):
    d = pl.program_id(0)
    c = pl.program_id(2)

    @pl.when(c == 0)
    def _():
        s_ref[...] = s0_ref[...]

    rows = lax.broadcasted_iota(jnp.int32, (lc, lc), 0)
    cols = lax.broadcasted_iota(jnp.int32, (lc, lc), 1)
    rel = jnp.where(d == 0, rows - cols, cols - rows).astype(F32)
    idx = lax.broadcasted_iota(jnp.int32, (lc, 1), 0)
    pos = jnp.where(d == 0, idx, lc - 1 - idx).astype(F32)
    for h in range(N_HEADS):
        lg = jnp.where(d == 0, LOG_GAMMA[0][h], LOG_GAMMA[1][h]).astype(F32)
        sl = slice(h * HEAD_D, (h + 1) * HEAD_D)
        q = q_ref[:, sl]
        k = k_ref[:, sl] * (HEAD_D ** -0.5)
        v = v_ref[:, sl].astype(BF16)
        att = _dot_nt(q.astype(BF16), k.astype(BF16))
        att = att * jnp.where(rel >= 0.0, jnp.exp(jnp.maximum(rel, 0.0) * lg), 0.0)
        qd = q * jnp.exp((pos + 1.0) * lg)
        kd = k * jnp.exp((lc - 1.0 - pos) * lg)
        s = s_ref[h]
        o_ref[:, sl] = _dot(att.astype(BF16), v) + _dot(qd.astype(BF16), s.astype(BF16))
        s_ref[h] = s * jnp.exp(lc * lg) + _dot_tn(kd.astype(BF16), v)

    @pl.when(c == nc - 1)
    def _():
        fin_ref[...] = s_ref[...]


def _retention(z, s0):
    B, T, _ = z.shape
    lc = min(T, 256)
    nc = T // lc

    def tmap(d, c):
        return jnp.where(d == 0, c, nc - 1 - c)

    def zspec(blk):
        return pl.BlockSpec((None, lc, MIX_W), lambda d, b, c: (b, tmap(d, c), blk))

    return pl.pallas_call(
        functools.partial(_ret_kernel, lc=lc, nc=nc),
        grid=(2, B, nc),
        in_specs=[
            zspec(ZB_RET), zspec(ZB_RET + 1), zspec(ZB_RET + 2),
            pl.BlockSpec((None, None, N_HEADS, HEAD_D, HEAD_D), lambda d, b, c: (b, d, 0, 0, 0)),
        ],
        out_specs=[
            pl.BlockSpec((None, None, lc, MIX_W), lambda d, b, c: (d, b, tmap(d, c), 0)),
            pl.BlockSpec((None, None, N_HEADS, HEAD_D, HEAD_D), lambda d, b, c: (b, d, 0, 0, 0)),
        ],
        out_shape=[
            jax.ShapeDtypeStruct((2, B, T, MIX_W), F32),
            jax.ShapeDtypeStruct((B, 2, N_HEADS, HEAD_D, HEAD_D), F32),
        ],
        scratch_shapes=[pltpu.VMEM((N_HEADS, HEAD_D, HEAD_D), F32)],
        compiler_params=_cparams("arbitrary", "arbitrary", "arbitrary"),
        name="retention",
    )(z, z, z, s0)


def _s5_zoh_kernel(lre_ref, lim_ref, ldt_ref, bre_ref, bim_ref, are_ref, aim_ref, bbre_ref, bbim_ref):
    lam_re = jnp.minimum(lre_ref[...], -1e-4)
    lam_im = lim_ref[...]
    dt = jnp.exp(ldt_ref[...])
    mag = jnp.exp(dt * lam_re)
    ang = dt * lam_im
    a_re = mag * jnp.cos(ang)
    a_im = mag * jnp.sin(ang)
    den = lam_re * lam_re + lam_im * lam_im
    f_re = ((a_re - 1.0) * lam_re + a_im * lam_im) / den
    f_im = (a_im * lam_re - (a_re - 1.0) * lam_im) / den
    b_re = bre_ref[...]
    b_im = bim_ref[...]
    are_ref[...] = a_re
    aim_ref[...] = a_im
    bbre_ref[...] = f_re * b_re - f_im * b_im
    bbim_ref[...] = f_re * b_im + f_im * b_re


def _s5_zoh(lam_re, lam_im, log_dt, b_re, b_im):
    n = 2 * S5_STATE
    col = lambda t: t.reshape(n, 1)
    ldt = jnp.broadcast_to(log_dt[:, :, None], (2, S5_GROUPS, S5_P))
    outs = pl.pallas_call(
        _s5_zoh_kernel,
        out_shape=[jax.ShapeDtypeStruct((n, 1), F32), jax.ShapeDtypeStruct((n, 1), F32),
                   jax.ShapeDtypeStruct((n, S5_CH), F32), jax.ShapeDtypeStruct((n, S5_CH), F32)],
        name="s5_zoh",
    )(col(lam_re), col(lam_im), col(ldt), b_re.reshape(n, S5_CH), b_im.reshape(n, S5_CH))
    a_re, a_im, bb_re, bb_im = outs
    shp = (2, S5_GROUPS, S5_P, S5_CH)
    return a_re.reshape(2, 1, S5_STATE), a_im.reshape(2, 1, S5_STATE), bb_re.reshape(shp), bb_im.reshape(shp)


S5_RADIX = 16


def _s5_kernel(u_ref, are_ref, aim_ref, bre_ref, bim_ref, cre_ref, cim_ref, x0r_ref, x0i_ref,
               y_ref, fr_ref, fi_ref, x_re, x_im, o_re, o_im, car_re, car_im, *, L, nc):
    d = pl.program_id(0)
    c = pl.program_id(2)
    R = S5_RADIX
    G = L // R

    def cmul(pr, pi, qr, qi):
        return pr * qr - pi * qi, pr * qi + pi * qr

    def body(rev):
        @pl.when(c == 0)
        def _():
            car_re[...] = x0r_ref[...]
            car_im[...] = x0i_ref[...]

        rr = lax.broadcasted_iota(jnp.int32, (L, L), 0)
        cc = lax.broadcasted_iota(jnp.int32, (L, L), 1)
        lg, lr = G.bit_length() - 1, R.bit_length() - 1
        perm = jnp.where(cc == ((rr & (G - 1)) << lr) + (rr >> lg), 1.0, 0.0).astype(BF16)
        unperm = jnp.where(cc == ((rr & (R - 1)) << lg) + (rr >> lr), 1.0, 0.0).astype(BF16)
        u = _dot(perm, u_ref[...].astype(BF16)).astype(BF16)
        x_re[...] = _dot(u, bre_ref[...])
        x_im[...] = _dot(u, bim_ref[...])
        ar = are_ref[...]
        ai = aim_ref[...]
        order = list(range(R - 1, -1, -1)) if rev else list(range(R))
        slab = lambda j: slice(j * G, (j + 1) * G)

        er = x_re[slab(order[0]), :]
        ei = x_im[slab(order[0]), :]
        for j in order[1:]:
            tr, ti = cmul(ar, ai, er, ei)
            er = tr + x_re[slab(j), :]
            ei = ti + x_im[slab(j), :]

        a_r, a_i = ar, ai
        for _ in range(R.bit_length() - 1):
            a_r, a_i = cmul(a_r, a_i, a_r, a_i)
        zr = car_re[...]
        zi = car_im[...]
        cin_r = [None] * G
        cin_i = [None] * G
        for k in (range(G - 1, -1, -1) if rev else range(G)):
            cin_r[k] = zr
            cin_i[k] = zi
            tr, ti = cmul(a_r, a_i, zr, zi)
            zr = tr + er[k:k + 1, :]
            zi = ti + ei[k:k + 1, :]
        car_re[...] = zr
        car_im[...] = zi

        xr = jnp.concatenate(cin_r, axis=0)
        xi = jnp.concatenate(cin_i, axis=0)
        for j in order:
            tr, ti = cmul(ar, ai, xr, xi)
            xr = tr + x_re[slab(j), :]
            xi = ti + x_im[slab(j), :]
            o_re[slab(j), :] = xr
            o_im[slab(j), :] = xi
        xr_all = _dot(unperm, o_re[...].astype(BF16)).astype(BF16)
        xi_all = _dot(unperm, o_im[...].astype(BF16)).astype(BF16)
        y_ref[...] = _dot(xr_all, cre_ref[...]) - _dot(xi_all, cim_ref[...])

    @pl.when(d == 0)
    def _():
        body(False)

    @pl.when(d == 1)
    def _():
        body(True)

    @pl.when(c == nc - 1)
    def _():
        fr_ref[...] = car_re[...]
        fi_ref[...] = car_im[...]


def _s5(z, a_re, a_im, bblk_re, bblk_im, cblk_re, cblk_im, x0_re, x0_im):
    B, T, _ = z.shape
    L = min(T, 128)
    nc = T // L

    def tmap(d, c):
        return jnp.where(d == 0, c, nc - 1 - c)

    dspec = lambda shape: pl.BlockSpec((None,) + shape, lambda d, b, c: (d, 0, 0))
    sspec = pl.BlockSpec((None, None, 1, S5_STATE), lambda d, b, c: (b, d, 0, 0))
    assert L % S5_RADIX == 0 and (L // S5_RADIX) & (L // S5_RADIX - 1) == 0
    buf = pltpu.VMEM((L, S5_STATE), F32)
    return pl.pallas_call(
        functools.partial(_s5_kernel, L=L, nc=nc),
        grid=(2, B, nc),
        in_specs=[
            pl.BlockSpec((None, L, MIX_W), lambda d, b, c: (b, tmap(d, c), ZB_S5)),
            dspec((1, S5_STATE)), dspec((1, S5_STATE)),
            dspec((MIX_W, S5_STATE)), dspec((MIX_W, S5_STATE)),
            pl.BlockSpec((S5_STATE, MIX_W), lambda d, b, c: (0, 0)),
            pl.BlockSpec((S5_STATE, MIX_W), lambda d, b, c: (0, 0)),
            sspec, sspec,
        ],
        out_specs=[
            pl.BlockSpec((None, None, L, MIX_W), lambda d, b, c: (d, b, tmap(d, c), 0)),
            sspec, sspec,
        ],
        out_shape=[
            jax.ShapeDtypeStruct((2, B, T, MIX_W), F32),
            jax.ShapeDtypeStruct((B, 2, 1, S5_STATE), F32),
            jax.ShapeDtypeStruct((B, 2, 1, S5_STATE), F32),
        ],
        scratch_shapes=[buf, buf, buf, buf, pltpu.VMEM((1, S5_STATE), F32), pltpu.VMEM((1, S5_STATE), F32)],
        compiler_params=_cparams("arbitrary", "arbitrary", "arbitrary"),
        name="s5_scan",
    )(z, a_re, a_im, bblk_re, bblk_im, cblk_re, cblk_im, x0_re, x0_im)


GLA_CHUNK = 16
GLA_CHUNK_SHIFT = 4
SUB = 8


def _gla_kernel(q_ref, zf_ref, v_ref, lb_ref, s0_ref, o_ref, fin_ref, s_ref, a_ref, qs_ref, key_ref,
                *, rev, tb, nt):
    jb = pl.program_id(1)
    ch = GLA_CHUNK
    half = 2 * HEAD_D

    @pl.when(jb == 0)
    def _():
        s_ref[...] = s0_ref[...]

    lb = lb_ref[...]
    zf = zf_ref[...]
    q = q_ref[...]
    qs_ref[...] = q * jax.nn.sigmoid(q)
    key_ref[...] = (1.0 - lb) * jax.nn.sigmoid(-zf)
    l1 = jnp.log(lb)
    l2 = jnp.log1p(-lb) + jnp.minimum(zf, 0.0) - jnp.log1p(jnp.exp(-jnp.abs(zf)))
    lf = jnp.maximum(l1, l2) + jnp.log1p(jnp.exp(-jnp.abs(l1 - l2)))
    r = lax.broadcasted_iota(jnp.int32, (tb, tb), 0)
    c = lax.broadcasted_iota(jnp.int32, (tb, tb), 1)
    same = (r >> GLA_CHUNK_SHIFT) == (c >> GLA_CHUNK_SHIFT)
    tri = jnp.where(same & ((c >= r) if rev else (c <= r)), 1.0, 0.0).astype(BF16)
    p1 = lf.astype(BF16)
    r1 = lf - p1.astype(F32)
    p2 = r1.astype(BF16)
    p3 = (r1 - p2.astype(F32)).astype(BF16)
    a_ref[...] = _dot(tri, p1) + _dot(tri, p2) + _dot(tri, p3)

    rows = lax.broadcasted_iota(jnp.int32, (SUB, half), 0)
    lo = lax.broadcasted_iota(jnp.int32, (SUB, half), 1) < HEAD_D
    same_head = ((lax.broadcasted_iota(jnp.int32, (half, half), 0) >> HEAD_SHIFT)
                 == (lax.broadcasted_iota(jnp.int32, (half, half), 1) >> HEAD_SHIFT))

    def pair_sums(p):
        s0 = jnp.sum(jnp.where(lo, p, 0.0), axis=1, keepdims=True)
        s1 = jnp.sum(jnp.where(lo, 0.0, p), axis=1, keepdims=True)
        return jnp.where(lo, s0, s1)

    def chunk(ci, carry):
        t0 = pl.multiple_of((tb // ch - 1 - ci if rev else ci) * ch, ch)
        win = pl.ds(t0, ch)
        a = a_ref[win, :]
        qc = qs_ref[win, :]
        kc = key_ref[win, :]
        vc = v_ref[win, :]
        last = 0 if rev else ch - 1
        a_last = a[last:last + 1, :]
        qe = (qc * jnp.exp(a)).astype(BF16)
        ke = (kc * jnp.exp(a_last - a)).astype(BF16)
        ea = jnp.exp(a_last)
        vb = vc.astype(BF16)
        nslab = ch // SUB
        acc = [[jnp.zeros((SUB, half), F32) for _ in range(2)] for _ in range(nslab)]
        for jj in range(ch):
            for sb in range(nslab):
                r0 = sb * SUB
                if (r0 > jj) if rev else (r0 + SUB - 1 < jj):
                    continue
                whole = (r0 + SUB - 1 <= jj) if rev else (r0 >= jj)
                rs = slice(r0, r0 + SUB)
                valid = (rows + r0 <= jj) if rev else (rows + r0 >= jj)
                for hp in range(2):
                    sl = slice(hp * half, (hp + 1) * half)
                    dec = jnp.exp(jnp.minimum(a[rs, sl] - a[jj:jj + 1, sl], 0.0))
                    p = qc[rs, sl] * dec * kc[jj:jj + 1, sl]
                    if not whole:
                        p = jnp.where(valid, p, 0.0)
                    acc[sb][hp] = acc[sb][hp] + pair_sums(p) * vc[jj:jj + 1, sl]
        for hp in range(2):
            sl = slice(hp * half, (hp + 1) * half)
            s = s_ref[hp]
            inter = _dot_nt(qe[:, sl], s.astype(BF16))
            intra = jnp.concatenate([acc[sb][hp] for sb in range(nslab)], axis=0)
            o_ref[win, sl] = inter + intra
            s_ref[hp] = s * ea[:, sl] + jnp.where(same_head, _dot_tn(vb[:, sl], ke[:, sl]), 0.0)
        return carry

    lax.fori_loop(0, tb // ch, chunk, 0, unroll=2)

    @pl.when(jb == nt - 1)
    def _():
        fin_ref[...] = s_ref[...]


def _gla(z, lb, s0, rev):
    B, T, _ = z.shape
    pair = 2 * HEAD_D
    s0 = s0.reshape(B, 2, 2, HEAD_D, HEAD_D)
    zero = jnp.zeros_like(s0[:, :, 0])
    s0 = jnp.concatenate([jnp.concatenate([s0[:, :, 0], zero], axis=-1),
                          jnp.concatenate([zero, s0[:, :, 1]], axis=-1)], axis=-2)
    tb = min(T, 256)
    nt = T // tb
    di = 1 if rev else 0
    tmap = (lambda j: nt - 1 - j) if rev else (lambda j: j)
    zspec = lambda blk: pl.BlockSpec((None, tb, MIX_W), lambda b, j: (b, tmap(j), blk))
    sspec = pl.BlockSpec((None, 2, pair, pair), lambda b, j: (b, 0, 0, 0))
    blk = pltpu.VMEM((tb, MIX_W), F32)
    o, fin = pl.pallas_call(
        functools.partial(_gla_kernel, rev=rev, tb=tb, nt=nt),
        grid=(B, nt),
        in_specs=[zspec(ZB_HG), zspec(ZB_HG + 1 + di), zspec(ZB_HG + 3),
                  pl.BlockSpec((1, MIX_W), lambda b, j: (0, 0)), sspec],
        out_specs=[pl.BlockSpec((None, tb, MIX_W), lambda b, j: (b, tmap(j), 0)), sspec],
        out_shape=[jax.ShapeDtypeStruct((B, T, MIX_W), F32),
                   jax.ShapeDtypeStruct((B, 2, pair, pair), F32)],
        scratch_shapes=[pltpu.VMEM((2, pair, pair), F32), blk, blk, blk],
        compiler_params=_cparams("parallel", "arbitrary"),
        name="hgrn_gla_bwd" if rev else "hgrn_gla_fwd",
    )(z, z, z, lb[di:di + 1], s0)
    fin = jnp.stack([fin[:, :, :HEAD_D, :HEAD_D], fin[:, :, HEAD_D:, HEAD_D:]], axis=2)
    return o, fin.reshape(B, N_HEADS, HEAD_D, HEAD_D)


def _rw_prep_kernel(*refs, grid_shift, tm):
    if grid_shift:
        (zc_ref, zu_ref, zd_ref, mu_ref, vec_ref, w0_ref, lora_ref,
         r_ref, k2_ref, v_ref, w_ref, nkk_ref, kka_ref, g_ref, bonus_ref, buf_ref) = refs
    else:
        (zc_ref, mu_ref, vec_ref, w0_ref, lora_ref,
         r_ref, k2_ref, v_ref, w_ref, nkk_ref, kka_ref, g_ref, bonus_ref, buf_ref) = refs
    i = pl.program_id(1)
    nt = pl.num_programs(1)
    halo = GRID_W
    width = 4 * MIX_W
    z = zc_ref[...]
    buf_ref[halo:halo + tm, :] = z
    lane = lax.broadcasted_iota(jnp.int32, (tm, width), 1)
    row = lax.broadcasted_iota(jnp.int32, (tm, width), 0)
    if grid_shift:
        buf_ref[0:halo, :] = jnp.where(i > 0, zu_ref[...], 0.0)
        buf_ref[halo + tm:2 * halo + tm, :] = jnp.where(i < nt - 1, zd_ref[...], 0.0)
        col = row & (GRID_W - 1)
        left = jnp.where(col > 0, buf_ref[halo - 1:halo - 1 + tm, :], 0.0)
        right = jnp.where(col < GRID_W - 1, buf_ref[halo + 1:halo + 1 + tm, :], 0.0)
        up = buf_ref[0:tm, :]
        down = buf_ref[2 * halo:2 * halo + tm, :]
        sel = lane & 3
        shifted = jnp.where(sel == 0, left, jnp.where(sel == 1, right, jnp.where(sel == 2, up, down)))
    else:
        zrow = jnp.zeros((1, width), F32)
        buf_ref[halo - 1:halo, :] = zrow
        buf_ref[halo + tm:halo + tm + 1, :] = zrow
        prev = buf_ref[halo - 1:halo - 1 + tm, :]
        nxt = buf_ref[halo + 1:halo + 1 + tm, :]
        shifted = jnp.where((lane & 1) == 0, prev, nxt)
    zs = z + mu_ref[...] * (shifted - z)
    r = zs[:, 0:MIX_W]
    k = zs[:, MIX_W:2 * MIX_W]
    v = zs[:, 2 * MIX_W:3 * MIX_W]
    sm = zs[:, 3 * MIX_W:4 * MIX_W]
    ones_bd = _head_ones()
    a0, k_k, k_a, r_k = (vec_ref[j:j + 1, :] for j in range(4))
    a = jax.nn.sigmoid(a0 + _dot(sm.astype(BF16), lora_ref[2]))
    g_ref[...] = _dot(jax.nn.sigmoid(sm).astype(BF16), lora_ref[3])
    kk = k * k_k
    kk = kk * lax.rsqrt(_head_sum(kk * kk, ones_bd) + 1e-12)
    k2 = k * (1.0 + (a - 1.0) * k_a)
    th = jnp.tanh(sm).astype(BF16)
    for di in range(2):
        w_ref[di] = -math.exp(-0.5) * jax.nn.sigmoid(w0_ref[di:di + 1, :] + _dot(th, lora_ref[di]))
    r_ref[...] = r
    k2_ref[...] = k2
    v_ref[...] = v
    nkk_ref[...] = -kk
    kka_ref[...] = kk * a
    bonus_ref[...] = _head_sum(r * k2 * r_k, ones_bd) * v


def _rw_prep(z, mu_p, vecs, w0, lora, grid_shift):
    B, T, _ = z.shape
    width = 4 * MIX_W
    wblk = ZB_RW // 4
    if grid_shift:
        tm = min(T, 512)
        hb = tm // GRID_W
        nh = T // GRID_W
        z_specs = [
            pl.BlockSpec((None, tm, width), lambda b, i: (b, i, wblk)),
            pl.BlockSpec((None, GRID_W, width), lambda b, i: (b, jnp.maximum(i * hb - 1, 0), wblk)),
            pl.BlockSpec((None, GRID_W, width), lambda b, i: (b, jnp.minimum((i + 1) * hb, nh - 1), wblk)),
        ]
        z_args = (z, z, z)
    else:
        tm = T
        z_specs = [pl.BlockSpec((None, tm, width), lambda b, i: (b, i, wblk))]
        z_args = (z,)
    const = lambda shape: pl.BlockSpec(shape, lambda b, i: (0,) * len(shape))
    ospec = pl.BlockSpec((None, tm, MIX_W), lambda b, i: (b, i, 0))
    oshape = jax.ShapeDtypeStruct((B, T, MIX_W), F32)
    dspec = pl.BlockSpec((2, None, tm, MIX_W), lambda b, i: (0, b, i, 0))
    dshape = jax.ShapeDtypeStruct((2, B, T, MIX_W), F32)
    return pl.pallas_call(
        functools.partial(_rw_prep_kernel, grid_shift=grid_shift, tm=tm),
        grid=(B, T // tm),
        in_specs=z_specs + [const((1, width)), const((4, MIX_W)), const((2, MIX_W)),
                            const((4, MIX_W, MIX_W))],
        out_specs=[ospec, ospec, ospec, dspec, ospec, ospec, ospec, ospec],
        out_shape=[oshape, oshape, oshape, dshape, oshape, oshape, oshape, oshape],
        scratch_shapes=[pltpu.VMEM((tm + 2 * GRID_W, width), F32)],
        compiler_params=_cparams("parallel", "parallel"),
        name="rwkv_prep",
    )(*z_args, mu_p, vecs, w0, lora)


def _seq_kernel(*refs, streams, nb, ub, tb, nt):
    refs = list(refs)
    take = lambda n: [refs.pop(0) for _ in range(n)]
    in_refs = [take(6 if use_sa else 4) for use_sa, _ in streams]
    s0_refs = take(len(streams))
    y_refs = take(len(streams))
    fin_refs = take(len(streams))
    s_refs = take(len(streams))
    j = pl.program_id(0)
    pair_w = 2 * HEAD_D
    group = 8

    @pl.when(j == 0)
    def _():
        for s_ref, s0_ref in zip(s_refs, s0_refs):
            s_ref[...] = s0_ref[...]

    lane = lax.broadcasted_iota(jnp.int32, (HEAD_D, pair_w), 1)
    sub = lax.broadcasted_iota(jnp.int32, (HEAD_D, pair_w), 0)
    lo = lane < HEAD_D
    diag = (lane & (HEAD_D - 1)) == sub
    r2 = lax.broadcasted_iota(jnp.int32, (2 * pair_w, pair_w), 0)
    c2 = lax.broadcasted_iota(jnp.int32, (2 * pair_w, pair_w), 1)
    ones2 = jnp.where(((r2 >> HEAD_SHIFT) & 1) == (c2 >> HEAD_SHIFT), 1.0, 0.0).astype(BF16)

    def split(x):
        hi = x.astype(BF16).astype(F32)
        return hi, (x - hi).astype(BF16).astype(F32)

    def seg_sum_mxu(hi, lo_):
        return _dot(jnp.concatenate([hi.astype(BF16), lo_.astype(BF16)], axis=1), ones2)

    def seg_sum_xlu(p):
        s_lo = jnp.sum(jnp.where(lo, p, 0.0), axis=1, keepdims=True)
        s_hi = jnp.sum(jnp.where(lo, 0.0, p), axis=1, keepdims=True)
        return jnp.where(lo, s_lo, s_hi)

    def chain(si, b, p, g):
        use_sa, rev = streams[si]
        t0 = pl.multiple_of((tb // group - 1 - g if rev else g) * group, group)
        sl = pl.ds(p * pair_w, pair_w)
        blk = lambda ref: ref[b, pl.ds(t0, group), sl]
        w, k, r, v = (blk(ref) for ref in in_refs[si][:4])
        if use_sa:
            a, bb = blk(in_refs[si][4]), blk(in_refs[si][5])
            v_hi, v_lo = split(v)
        s = s_refs[si][b, p]
        ys = [None] * group
        for i in range(group):
            t = group - 1 - i if rev else i
            row = lambda x: x[t:t + 1, :]
            if use_sa:
                vcol = seg_sum_mxu(jnp.where(diag, row(v_hi), 0.0), jnp.where(diag, row(v_lo), 0.0))
                s = s * row(w) + seg_sum_xlu(s * row(a)) * row(bb)
            else:
                vcol = seg_sum_xlu(jnp.where(diag, row(v), 0.0))
                s = s * row(w)
            s = s + vcol * row(k)
            yb = seg_sum_mxu(*split(s * row(r)))
            ys[t] = jnp.sum(jnp.where(diag, yb, 0.0), axis=0, keepdims=True)
        s_refs[si][b, p] = s
        y_refs[si][b, pl.ds(t0, group), sl] = jnp.concatenate(ys, axis=0)

    def group_step(g, carry):
        def batch_step(bg, c):
            for si in range(len(streams)):
                for u in range(ub):
                    for p in range(2):
                        chain(si, bg * ub + u, p, g)
            return c

        return lax.fori_loop(0, nb // ub, batch_step, carry)

    lax.fori_loop(0, tb // group, group_step, 0)

    @pl.when(j == nt - 1)
    def _():
        for fin_ref, s_ref in zip(fin_refs, s_refs):
            fin_ref[...] = s_ref[...]


SCAN_BLOCK_ROWS = 512
SCAN_CHAINS = 16


def _seq_scan(mixers):
    B, T = mixers[0][3][0].shape[-3], mixers[0][3][0].shape[-2]
    tb = max(8, min(T, SCAN_BLOCK_ROWS // B))
    nt = T // tb
    ub = max(u for u in (1, 2, 4, 8) if B % u == 0 and u * 4 * len(mixers) <= max(SCAN_CHAINS, 4 * len(mixers)))
    streams, args, in_specs, s0_args = [], [], [], []
    for w, k, r, v, a, b, s0 in mixers:
        for di in range(2):
            tmap = (lambda j: nt - 1 - j) if di else (lambda j: j)
            streams.append((a is not None, di == 1))
            for arr, blk in [x for x in (w, k, r, v, a, b) if x is not None]:
                args.append(arr)
                if arr.ndim == 4:
                    in_specs.append(pl.BlockSpec((None, B, tb, MIX_W),
                                                 lambda j, di=di, tmap=tmap, blk=blk: (di, 0, tmap(j), blk)))
                else:
                    in_specs.append(pl.BlockSpec((B, tb, MIX_W), lambda j, tmap=tmap, blk=blk: (0, tmap(j), blk)))
            s0_args.append(s0[:, di])
    n = len(streams)
    sspec = pl.BlockSpec((B, 2, HEAD_D, 2 * HEAD_D), lambda j: (0, 0, 0, 0))
    state = jax.ShapeDtypeStruct((B, 2, HEAD_D, 2 * HEAD_D), F32)
    y_specs = [pl.BlockSpec((B, tb, MIX_W), (lambda j: (0, nt - 1 - j, 0)) if rev else (lambda j: (0, j, 0)))
               for _, rev in streams]
    outs = pl.pallas_call(
        functools.partial(_seq_kernel, streams=tuple(streams), nb=B, ub=ub, tb=tb, nt=nt),
        grid=(nt,),
        in_specs=in_specs + [sspec] * n,
        out_specs=y_specs + [sspec] * n,
        out_shape=[jax.ShapeDtypeStruct((B, T, MIX_W), F32)] * n + [state] * n,
        scratch_shapes=[pltpu.VMEM((B, 2, HEAD_D, 2 * HEAD_D), F32)] * n,
        compiler_params=_cparams("arbitrary"),
        name="state_scan",
    )(*args, *s0_args)
    ys, fins = outs[:n], outs[n:]
    return [((ys[2 * m], ys[2 * m + 1]), jnp.stack([fins[2 * m], fins[2 * m + 1]], axis=1))
            for m in range(len(mixers))]


def _pair_tiles(s):
    B = s.shape[0]
    s = s.reshape(B, 2, 2, 2, HEAD_D, HEAD_D).transpose(0, 1, 2, 4, 3, 5)
    return s.reshape(B, 2, 2, HEAD_D, 2 * HEAD_D)


def _unpair_tiles(s):
    B = s.shape[0]
    s = s.reshape(B, 2, 2, HEAD_D, 2, HEAD_D).transpose(0, 1, 2, 4, 3, 5)
    return s.reshape(B, 2, N_HEADS, HEAD_D, HEAD_D)


RW_CHUNK = 64
RW_CHUNK_SHIFT = 6


def _mm(a, b):
    return jnp.dot(a.astype(BF16), b.astype(BF16), preferred_element_type=F32)


def _rwkv_kernel(lw_ref, a_ref, b_ref, k_ref, r_ref, v_ref, s0_ref, y_ref, fin_ref, s_ref, g_ref,
                 *, rev, tb, nt):
    jb = pl.program_id(1)
    ch = RW_CHUNK
    pair = 2 * HEAD_D

    @pl.when(jb == 0)
    def _():
        s_ref[...] = s0_ref[...]

    lw = lw_ref[...]
    rr = lax.broadcasted_iota(jnp.int32, (tb, tb), 0)
    cc = lax.broadcasted_iota(jnp.int32, (tb, tb), 1)
    same = (rr >> RW_CHUNK_SHIFT) == (cc >> RW_CHUNK_SHIFT)
    tri = jnp.where(same & ((cc >= rr) if rev else (cc <= rr)), 1.0, 0.0).astype(BF16)
    p1 = lw.astype(BF16)
    r1 = lw - p1.astype(F32)
    p2 = r1.astype(BF16)
    p3 = (r1 - p2.astype(F32)).astype(BF16)
    g_ref[...] = _dot(tri, p1) + _dot(tri, p2) + _dot(tri, p3)

    si = lax.broadcasted_iota(jnp.int32, (ch, ch), 0)
    ri = lax.broadcasted_iota(jnp.int32, (ch, ch), 1)
    if rev:
        si, ri = ch - 1 - si, ch - 1 - ri
    strict = ri < si
    incl = ri <= si
    eye = jnp.where(ri == si, 1.0, 0.0).astype(F32)
    levels = [((si >> (lv + 1)) == (ri >> (lv + 1))) & (((si >> lv) & 1) == 1) & (((ri >> lv) & 1) == 0)
              for lv in range(RW_CHUNK_SHIFT)]
    lo = lax.broadcasted_iota(jnp.int32, (ch, pair), 1) < HEAD_D
    same_head = ((lax.broadcasted_iota(jnp.int32, (pair, pair), 0) >> HEAD_SHIFT)
                 == (lax.broadcasted_iota(jnp.int32, (pair, pair), 1) >> HEAD_SHIFT))

    nch = tb // ch
    g = g_ref[...]
    e_g = jnp.exp(g)
    e_ng = jnp.exp(-g)
    at_f = a_ref[...] * jnp.exp(g - lw)
    rt_f = r_ref[...] * e_g
    at = at_f.astype(BF16)
    rt = rt_f.astype(BF16)
    first = (lax.broadcasted_iota(jnp.int32, (tb, MIX_W), 1) & HEAD_D) == 0
    at_h = [jnp.where(first, at_f, 0.0).astype(BF16), jnp.where(first, 0.0, at_f).astype(BF16)]
    rt_h = [jnp.where(first, rt_f, 0.0).astype(BF16), jnp.where(first, 0.0, rt_f).astype(BF16)]
    bt = (b_ref[...] * e_ng).astype(BF16)
    kt = (k_ref[...] * e_ng).astype(BF16)
    vb = v_ref[...].astype(BF16)

    items = [(c, hp, hh) for c in range(nch) for hp in range(2) for hh in range(2)]
    rows = lambda c: slice(c * ch, (c + 1) * ch)
    lanes = lambda hp: slice(hp * pair, (hp + 1) * pair)
    n_m, p_m, m_m, q_m = {}, {}, {}, {}
    for it in items:
        c, hp, hh = it
        ar = jnp.concatenate([at_h[hh][rows(c), lanes(hp)], rt_h[hh][rows(c), lanes(hp)]], axis=0)
        np_ = _dot_nt(ar, bt[rows(c), lanes(hp)])
        mq = _dot_nt(ar, kt[rows(c), lanes(hp)])
        n_m[it] = jnp.where(strict, np_[:ch], 0.0).astype(BF16)
        p_m[it] = jnp.where(incl, np_[ch:], 0.0).astype(BF16)
        m_m[it] = jnp.where(strict, mq[:ch], 0.0).astype(BF16)
        q_m[it] = jnp.where(incl, mq[ch:], 0.0).astype(BF16)
    t_m = {it: eye + jnp.where(levels[0], n_m[it].astype(F32), 0.0) for it in items}
    for lv in range(1, RW_CHUNK_SHIFT):
        tn = {it: _mm(t_m[it], n_m[it]) for it in items}
        t_m = {it: t_m[it] + jnp.where(levels[lv], _mm(tn[it], t_m[it]), 0.0) for it in items}
    mv = {it: _mm(m_m[it], vb[rows(it[0]), lanes(it[1])]) for it in items}
    qv = {it: _mm(q_m[it], vb[rows(it[0]), lanes(it[1])]) for it in items}
    y2 = {it: _mm(t_m[it], jnp.concatenate([at[rows(it[0]), lanes(it[1])], mv[it].astype(BF16)], axis=1))
          for it in items}

    last = 0 if rev else ch - 1
    for ci in range(nch):
        c = nch - 1 - ci if rev else ci
        g_c = g[c * ch + last:c * ch + last + 1, :]
        e_gc = jnp.exp(g_c - g[rows(c), :])
        bh = (b_ref[rows(c), :] * e_gc).astype(BF16)
        kh = (k_ref[rows(c), :] * e_gc).astype(BF16)
        dec_c = jnp.exp(g_c)
        xs, sas, ss = [], [], []
        for hp in range(2):
            s = s_ref[hp]
            w2 = jnp.where(lo, y2[(c, hp, 0)][:, :pair], y2[(c, hp, 1)][:, :pair]).astype(BF16)
            xs.append(_dot_nt(jnp.concatenate([w2, rt[rows(c), lanes(hp)]], axis=0), s.astype(BF16)))
            ss.append(s)
        for hp in range(2):
            w1 = jnp.where(lo, y2[(c, hp, 0)][:, pair:], y2[(c, hp, 1)][:, pair:])
            sas.append(w1 + xs[hp][:ch])
        for hp in range(2):
            sa = sas[hp]
            y_ref[rows(c), lanes(hp)] = xs[hp][ch:] + jnp.where(
                lo, _mm(p_m[(c, hp, 0)], sa) + qv[(c, hp, 0)], _mm(p_m[(c, hp, 1)], sa) + qv[(c, hp, 1)])
            upd = _dot_tn(jnp.concatenate([sa.astype(BF16), vb[rows(c), lanes(hp)]], axis=0),
                          jnp.concatenate([bh[:, lanes(hp)], kh[:, lanes(hp)]], axis=0))
            s_ref[hp] = ss[hp] * dec_c[:, lanes(hp)] + jnp.where(same_head, upd, 0.0)

    @pl.when(jb == nt - 1)
    def _():
        fin_ref[...] = s_ref[...]


def _head_pair_blockdiag(s):
    B = s.shape[0]
    s = s.reshape(B, 2, 2, HEAD_D, HEAD_D)
    zero = jnp.zeros_like(s[:, :, 0])
    return jnp.concatenate([jnp.concatenate([s[:, :, 0], zero], axis=-1),
                            jnp.concatenate([zero, s[:, :, 1]], axis=-1)], axis=-2)


def _head_pair_blocks(t):
    B = t.shape[0]
    t = jnp.stack([t[:, :, :HEAD_D, :HEAD_D], t[:, :, HEAD_D:, HEAD_D:]], axis=2)
    return t.reshape(B, N_HEADS, HEAD_D, HEAD_D)


def _rwkv(lw, a, b, k, r, v, s0, rev):
    B, T, _ = v.shape
    pair = 2 * HEAD_D
    tb = min(T, 256)
    nt = T // tb
    di = 1 if rev else 0
    tmap = (lambda j: nt - 1 - j) if rev else (lambda j: j)
    spec = pl.BlockSpec((None, tb, MIX_W), lambda bi, j: (bi, tmap(j), 0))
    sspec = pl.BlockSpec((None, 2, pair, pair), lambda bi, j: (bi, 0, 0, 0))
    y, fin = pl.pallas_call(
        functools.partial(_rwkv_kernel, rev=rev, tb=tb, nt=nt),
        grid=(B, nt),
        in_specs=[pl.BlockSpec((None, None, tb, MIX_W), lambda bi, j: (di, bi, tmap(j), 0)),
                  spec, spec, spec, spec, spec, sspec],
        out_specs=[spec, sspec],
        out_shape=[jax.ShapeDtypeStruct((B, T, MIX_W), F32),
                   jax.ShapeDtypeStruct((B, 2, pair, pair), F32)],
        scratch_shapes=[pltpu.VMEM((2, pair, pair), F32), pltpu.VMEM((tb, MIX_W), F32)],
        compiler_params=_cparams("parallel", "arbitrary"),
        name="rwkv_chunk_bwd" if rev else "rwkv_chunk_fwd",
    )(lw, a, b, k, r, v, _head_pair_blockdiag(s0))
    return y, _head_pair_blocks(fin)


def _mix_kernel(x_ref, sc_ref, sh_ref, g1_ref, nw0_ref, nw_ref, wg_ref,
                ret_ref, retg_ref, s5_ref, s5u_ref, hgf_ref, hgb_ref, hgg_ref, rwf_ref, rwr_ref,
                rwb_ref, rwg_ref,
                vec_ref, glu_w_ref, wbr_ref, wout_ref, o_ref):
    ones_bd = _head_ones()
    h = (_rms(x_ref[...], nw0_ref[...]) * (1.0 + sc_ref[...]) + sh_ref[...]).astype(BF16)
    gn_w, s5_d, glu_b, hg_w, ln_w, ln_b = (vec_ref[j:j + 1, :] for j in range(6))

    def group_norm(o, eps):
        mu = _head_sum(o, ones_bd) * (1.0 / HEAD_D)
        oc = o - mu
        var = _head_sum(oc * oc, ones_bd) * (1.0 / HEAD_D)
        return oc * lax.rsqrt(var + eps)

    g = retg_ref[...]
    y_ret = group_norm(ret_ref[0] + ret_ref[1], EPS) * gn_w * (g * jax.nn.sigmoid(g))

    y = s5_d * s5u_ref[...] + s5_ref[0] + s5_ref[1]
    yg = jax.nn.gelu(y)
    y_s5 = yg * jax.nn.sigmoid(_dot(yg.astype(BF16), glu_w_ref[...]) + glu_b)

    o = hgf_ref[...] + hgb_ref[...]
    g = hgg_ref[...]
    ms = _head_sum(o * o, ones_bd) * (1.0 / HEAD_D)
    y_hg = o * lax.rsqrt(ms + EPS) * hg_w * (g * jax.nn.sigmoid(g))

    y = group_norm(rwf_ref[...] + rwr_ref[...], RW_LN_EPS) * ln_w + ln_b
    y_rw = (y + rwb_ref[...]) * rwg_ref[...]

    mixed = None
    for m, ym in enumerate((y_ret, y_s5, y_hg, y_rw)):
        br = _dot(ym.astype(BF16), wbr_ref[m])
        term = jax.nn.sigmoid(_dot(h, wg_ref[:, m * D_MODEL:(m + 1) * D_MODEL])) * br
        mixed = term if mixed is None else mixed + term
    mixed = _dot(mixed.astype(BF16), wout_ref[...])
    o_ref[...] = x_ref[...] + g1_ref[...] * _rms(mixed, nw_ref[...])


def _mix(x, sc, sh, g1, nw0, nw, w_gate, z, ret_o, s5_y, hg_o, rw_y, rw_bonus, rw_g, vecs, glu_w,
         w_branch, w_out):
    B, T, _ = x.shape
    tm = min(T, 256)
    xspec = pl.BlockSpec((None, tm, D_MODEL), lambda b, i: (b, i, 0))
    zspec = lambda blk: pl.BlockSpec((None, tm, MIX_W), lambda b, i: (b, i, blk))
    dspec = pl.BlockSpec((2, None, tm, MIX_W), lambda b, i: (0, b, i, 0))
    const = lambda shape: pl.BlockSpec(shape, lambda b, i: (0,) * len(shape))
    mspec = lambda m: pl.BlockSpec((None, 1, D_MODEL), _mod_map(m.shape[0]))
    return pl.pallas_call(
        _mix_kernel,
        grid=(B, T // tm),
        in_specs=[
            xspec, mspec(sc), mspec(sh), mspec(g1),
            const((1, D_MODEL)), const((1, D_MODEL)),
            const((D_MODEL, 4 * D_MODEL)),
            dspec, zspec(ZB_RET + 3),
            dspec, zspec(ZB_S5),
            zspec(0), zspec(0), zspec(ZB_HG + 4),
            zspec(0), zspec(0), zspec(0), zspec(0),
            const((6, MIX_W)), const((MIX_W, MIX_W)),
            const((4, MIX_W, D_MODEL)), const((D_MODEL, D_MODEL)),
        ],
        out_specs=xspec,
        out_shape=jax.ShapeDtypeStruct((B, T, D_MODEL), F32),
        compiler_params=_cparams("parallel", "parallel"),
        name="mix_out",
    )(x, sc, sh, g1, nw0, nw, w_gate, ret_o, z, s5_y, z, hg_o[0], hg_o[1], z, rw_y[0], rw_y[1],
      rw_bonus, rw_g, vecs, glu_w, w_branch, w_out)


def _ffn_kernel(x_ref, sc_ref, sh_ref, g2_ref, nw2_ref, nw3_ref, w1_ref, w2_ref, o_ref, h_ref, acc_ref):
    j = pl.program_id(2)

    @pl.when(j == 0)
    def _():
        h = _rms(x_ref[...], nw2_ref[...]) * (1.0 + sc_ref[...]) + sh_ref[...]
        h_ref[...] = h.astype(BF16)
        acc_ref[...] = jnp.zeros_like(acc_ref)

    a = jnp.maximum(_dot(h_ref[...], w1_ref[...]), 0.0)
    acc_ref[...] += _dot((a * a).astype(BF16), w2_ref[...])

    @pl.when(j == pl.num_programs(2) - 1)
    def _():
        o_ref[...] = x_ref[...] + g2_ref[...] * _rms(acc_ref[...], nw3_ref[...])


def _ffn(x, sc, sh, g2, nw2, nw3, w1, w2):
    B, T, _ = x.shape
    tm = min(T, 512)
    tf = 1024
    xspec = pl.BlockSpec((None, tm, D_MODEL), lambda b, i, j: (b, i, 0))
    mspec = lambda m: pl.BlockSpec((None, 1, D_MODEL), _mod_map(m.shape[0]))
    const = pl.BlockSpec((1, D_MODEL), lambda b, i, j: (0, 0))
    return pl.pallas_call(
        _ffn_kernel,
        grid=(B, T // tm, D_FF // tf),
        in_specs=[xspec, mspec(sc), mspec(sh), mspec(g2), const, const,
                  pl.BlockSpec((D_MODEL, tf), lambda b, i, j: (0, j)),
                  pl.BlockSpec((tf, D_MODEL), lambda b, i, j: (j, 0))],
        out_specs=xspec,
        out_shape=jax.ShapeDtypeStruct((B, T, D_MODEL), F32),
        scratch_shapes=[pltpu.VMEM((tm, D_MODEL), BF16), pltpu.VMEM((tm, D_MODEL), F32)],
        compiler_params=_cparams("parallel", "parallel", "arbitrary"),
        name="ffn",
    )(x, sc, sh, g2, nw2, nw3, w1, w2)


def _layer_params(l, p):
    w = p['w_in'][l]
    w_p = jnp.concatenate(
        [w[:, 2560:3488], jnp.zeros((D_MODEL, 4 * MIX_W - RW_COLS), F32),
         w[:, 0:1024], w[:, 1024:1280], w[:, 1280:2560]], axis=1).astype(BF16)
    w_gate = w[:, 3488:7584].astype(BF16)

    a_re, a_im, bb_re, bb_im = _s5_zoh(p['s5_lam_re'][l], p['s5_lam_im'][l], p['s5_log_dt'][l],
                                       p['s5_b_re'][l], p['s5_b_im'][l])
    eye = jnp.eye(S5_GROUPS, dtype=F32)
    bblk = lambda bb: jnp.einsum('dgph,gk->dghkp', bb, eye).reshape(2, MIX_W, S5_STATE).astype(BF16)
    cblk = lambda c: jnp.einsum('ghp,gk->gpkh', c, eye).reshape(S5_STATE, MIX_W).astype(BF16)

    def lora_pad(m, row0):
        return jnp.zeros((MIX_W, MIX_W), F32).at[row0:row0 + m.shape[0]].set(m)

    lora = jnp.stack([lora_pad(p['rw_w2'][l, 0], 0), lora_pad(p['rw_w2'][l, 1], 32),
                      lora_pad(p['rw_a2'][l], 64), lora_pad(p['rw_g2'][l], 96)]).astype(BF16)
    mu_p = jnp.concatenate([p['rw_mu'][l], jnp.zeros((4 * MIX_W - RW_COLS,), F32)]).reshape(1, 4 * MIX_W)
    return dict(
        w_p=w_p, w_gate=w_gate, nw=p['norm_w'][l],
        s5=(a_re, a_im, bblk(bb_re), bblk(bb_im), cblk(p['s5_c_re'][l]), cblk(p['s5_c_im'][l])),
        rw_mu=mu_p, rw_lora=lora, rw_w0=p['rw_w0'][l],
        rw_vecs=jnp.stack([p['rw_a0'][l], p['rw_k_k'][l], p['rw_k_a'][l], p['rw_r_k'][l]]),
        mix_vecs=jnp.stack([p['ret_gn_w'][l], p['s5_d'][l], p['s5_glu_b'][l], p['hg_norm_w'][l],
                            p['rw_ln_w'][l], p['rw_ln_b'][l]]),
        glu_w=p['s5_glu_w'][l].astype(BF16),
        w_branch=p['w_branch'][l].astype(BF16), w_out=p['w_out'][l].astype(BF16),
        ff_w1=p['ff_w1'][l].astype(BF16), ff_w2=p['ff_w2'][l].astype(BF16),
    )


def _trunk_layer(x, mod, init, grid_shift, lp, hg_lb):
    B, T, _ = x.shape
    sh1, sc1, g1, sh2, sc2, g2 = (m[:, None, :] for m in jnp.split(mod, 6, axis=-1))
    nw = lp['nw']
    if mod.shape[0] == 1:
        tok = lambda t: t.reshape(t.shape[:-3] + (1, B * T, t.shape[-1]))
    else:
        tok = lambda t: t
    z = _proj_in(tok(x), sc1, sh1, nw[0:1], lp['w_p']).reshape(B, T, Z_COLS)

    s_ret, s_s5r, s_s5i, s_hg, s_rw = init
    ret_o, f_ret = _retention(z, s_ret)

    s5_y, f_s5r, f_s5i = _s5(z, *lp['s5'], s_s5r.reshape(B, 2, 1, S5_STATE), s_s5i.reshape(B, 2, 1, S5_STATE))
    f_s5r = f_s5r.reshape(B, 2, S5_GROUPS, S5_P)
    f_s5i = f_s5i.reshape(B, 2, S5_GROUPS, S5_P)

    s_hg_t = jnp.swapaxes(s_hg, -1, -2)
    hg_f, f_hg_f = _gla(z, hg_lb, s_hg_t[:, 0], False)
    hg_b, f_hg_b = _gla(z, hg_lb, s_hg_t[:, 1], True)
    hg_o = (hg_f, hg_b)
    f_hg = jnp.swapaxes(jnp.stack([f_hg_f, f_hg_b], axis=1), -1, -2)

    r, k2, v, w, nkk, kka, rw_g, bonus = _rw_prep(z, lp['rw_mu'], lp['rw_vecs'], lp['rw_w0'],
                                                  lp['rw_lora'], grid_shift)
    rw_f, f_rw_f = _rwkv(w, nkk, kka, k2, r, v, s_rw[:, 0], False)
    rw_b, f_rw_b = _rwkv(w, nkk, kka, k2, r, v, s_rw[:, 1], True)
    rw_y = (rw_f, rw_b)
    f_rw = jnp.stack([f_rw_f, f_rw_b], axis=1)

    x = _mix(tok(x), sc1, sh1, g1, nw[0:1], nw[1:2], lp['w_gate'], tok(z), tok(ret_o), tok(s5_y),
             (tok(hg_f), tok(hg_b)), (tok(rw_f), tok(rw_b)), tok(bonus), tok(rw_g),
             lp['mix_vecs'], lp['glu_w'], lp['w_branch'], lp['w_out'])
    x = _ffn(x, sc2, sh2, g2, nw[2:3], nw[3:4], lp['ff_w1'], lp['ff_w2'])
    return x.reshape(B, T, D_MODEL), (f_ret, f_s5r, f_s5i, f_hg, f_rw)


def kernel(x_prompt, x_sample, state_ret, state_s5_re, state_s5_im, state_hgrn, state_rwkv, c, c_ctx, ada_w, ada_b, norm_w, w_in, ret_gn_w, s5_lam_re, s5_lam_im, s5_log_dt, s5_b_re, s5_b_im, s5_c_re, s5_c_im, s5_d, s5_glu_w, s5_glu_b, hg_lb, hg_norm_w, rw_mu, rw_w0, rw_w2, rw_a0, rw_a2, rw_g2, rw_k_k, rw_k_a, rw_r_k, rw_ln_w, rw_ln_b, w_branch, w_out, ff_w1, ff_w2):
    p = dict(norm_w=norm_w, w_in=w_in, ret_gn_w=ret_gn_w, s5_lam_re=s5_lam_re, s5_lam_im=s5_lam_im,
             s5_log_dt=s5_log_dt, s5_b_re=s5_b_re, s5_b_im=s5_b_im, s5_c_re=s5_c_re, s5_c_im=s5_c_im,
             s5_d=s5_d, s5_glu_w=s5_glu_w, s5_glu_b=s5_glu_b, hg_norm_w=hg_norm_w, rw_mu=rw_mu,
             rw_w0=rw_w0, rw_w2=rw_w2, rw_a0=rw_a0, rw_a2=rw_a2, rw_g2=rw_g2, rw_k_k=rw_k_k,
             rw_k_a=rw_k_a, rw_r_k=rw_r_k, rw_ln_w=rw_ln_w, rw_ln_b=rw_ln_b, w_branch=w_branch,
             w_out=w_out, ff_w1=ff_w1, ff_w2=ff_w2)
    depth = w_in.shape[0]
    n_ctx = x_prompt.shape[0]
    n_lat = x_sample.shape[0]
    assert 1 + n_lat <= 8

    lb_cum = jnp.cumsum(jax.nn.softmax(hg_lb.astype(F32), axis=0), axis=0)
    hg_lower = lb_cum - lb_cum[0]

    cond = jnp.concatenate([c_ctx[None, :], c, jnp.zeros((7 - n_lat, D_MODEL), F32)], axis=0)
    mod = _ada_mod(cond, ada_w, ada_b)

    zero_state = (jnp.zeros((n_ctx, 2, N_HEADS, HEAD_D, HEAD_D), F32),
                  jnp.zeros((n_ctx, 2, S5_GROUPS, S5_P), F32),
                  jnp.zeros((n_ctx, 2, S5_GROUPS, S5_P), F32),
                  jnp.zeros((n_ctx, 2, N_HEADS, HEAD_D, HEAD_D), F32),
                  jnp.zeros((n_ctx, 2, N_HEADS, HEAD_D, HEAD_D), F32))
    xp, xs = x_prompt, x_sample
    finals = []
    for l in range(depth):
        lp = _layer_params(l, p)
        xp, fin = _trunk_layer(xp, mod[l, 0:1], zero_state, False, lp, hg_lower[l])
        finals.append(fin)
        lat_init = (state_ret[:, l], state_s5_re[:, l], state_s5_im[:, l], state_hgrn[:, l], state_rwkv[:, l])
        xs, _ = _trunk_layer(xs, mod[l, 1:1 + n_lat], lat_init, True, lp, hg_lower[l])
    new_states = tuple(jnp.stack([f[i] for f in finals], axis=1) for i in range(5))
    return (xp, xs) + new_states
```

```python
import functools
import math

import jax
import jax.numpy as jnp
from jax import lax
from jax.experimental import pallas as pl
from jax.experimental.pallas import tpu as pltpu

F32 = jnp.float32
BF16 = jnp.bfloat16

D_MODEL = 1024
GRID_W = 64
MIX_W = 256
N_HEADS = 4
HEAD_D = 64
HEAD_SHIFT = 6
LANES = 128
S5_GROUPS = 16
S5_CH = 16
S5_P = 64
S5_STATE = S5_GROUPS * S5_P
D_FF = 4096
EPS = 1e-6
RW_LN_EPS = 64e-5
RW_COLS = 928

Z_COLS = 3584
ZB_RW = 0
ZB_RET = 4
ZB_S5 = 8
ZB_HG = 9

VMEM_LIMIT = 56 * 1024 * 1024

LOG_GAMMA = tuple(
    tuple(math.log1p(-2.0 ** (-(5.0 + 0.5 * di) - h)) for h in range(N_HEADS)) for di in range(2))


def _cparams(*sem):
    return pltpu.CompilerParams(dimension_semantics=sem, vmem_limit_bytes=VMEM_LIMIT)


def _dot(a, b):
    return jnp.dot(a, b, preferred_element_type=F32)


def _dot_nt(a, b):
    return lax.dot_general(a, b, (((1,), (1,)), ((), ())), preferred_element_type=F32)


def _dot_tn(a, b):
    return lax.dot_general(a, b, (((0,), (0,)), ((), ())), preferred_element_type=F32)


def _head_ones():
    r = lax.broadcasted_iota(jnp.int32, (MIX_W, MIX_W), 0) >> HEAD_SHIFT
    c = lax.broadcasted_iota(jnp.int32, (MIX_W, MIX_W), 1) >> HEAD_SHIFT
    return jnp.where(r == c, 1.0, 0.0).astype(F32)


def _head_sum(x, ones_bd):
    ones = ones_bd.astype(BF16)
    p1 = x.astype(BF16)
    r1 = x - p1.astype(F32)
    p2 = r1.astype(BF16)
    p3 = (r1 - p2.astype(F32)).astype(BF16)
    return _dot(p1, ones) + _dot(p2, ones) + _dot(p3, ones)


def _rms(x, w):
    return x * lax.rsqrt(jnp.mean(x * x, axis=-1, keepdims=True) + EPS) * w


def _ada_kernel(c_ref, w_ref, b_ref, o_ref):
    c = c_ref[...]
    s = c * jax.nn.sigmoid(c)
    o_ref[...] = _dot(s.astype(BF16), w_ref[...].astype(BF16)) + b_ref[...]


def _ada_mod(cond, ada_w, ada_b):
    L = ada_w.shape[0]
    n = ada_w.shape[2]
    tn = 1536
    return pl.pallas_call(
        _ada_kernel,
        grid=(L, n // tn),
        in_specs=[
            pl.BlockSpec((8, D_MODEL), lambda l, j: (0, 0)),
            pl.BlockSpec((None, D_MODEL, tn), lambda l, j: (l, 0, j)),
            pl.BlockSpec((None, 1, tn), lambda l, j: (l, 0, j)),
        ],
        out_specs=pl.BlockSpec((None, 8, tn), lambda l, j: (l, 0, j)),
        out_shape=jax.ShapeDtypeStruct((L, 8, n), F32),
        compiler_params=_cparams("parallel", "parallel"),
        name="ada_mod",
    )(cond, ada_w, ada_b.reshape(L, 1, n))


def _proj_in_kernel(x_ref, sc_ref, sh_ref, nw_ref, w_ref, z_ref):
    h = _rms(x_ref[...], nw_ref[...]) * (1.0 + sc_ref[...]) + sh_ref[...]
    z_ref[...] = _dot(h.astype(BF16), w_ref[...])


def _mod_map(bm):
    if bm == 1:
        return lambda b, *_: (0, 0, 0)
    return lambda b, *_: (b, 0, 0)


def _proj_in(x, sc, sh, nw, w_p):
    B, T, _ = x.shape
    tm = min(T, 512)
    return pl.pallas_call(
        _proj_in_kernel,
        grid=(B, T // tm),
        in_specs=[
            pl.BlockSpec((None, tm, D_MODEL), lambda b, i: (b, i, 0)),
            pl.BlockSpec((None, 1, D_MODEL), _mod_map(sc.shape[0])),
            pl.BlockSpec((None, 1, D_MODEL), _mod_map(sh.shape[0])),
            pl.BlockSpec((1, D_MODEL), lambda b, i: (0, 0)),
            pl.BlockSpec((D_MODEL, Z_COLS), lambda b, i: (0, 0)),
        ],
        out_specs=pl.BlockSpec((None, tm, Z_COLS), lambda b, i: (b, i, 0)),
        out_shape=jax.ShapeDtypeStruct((B, T, Z_COLS), F32),
        compiler_params=_cparams("parallel", "parallel"),
        name="proj_in",
    )(x, sc, sh, nw, w_p)


def _ret_kernel(q_ref, k_ref, v_ref, s0_ref, o_ref, fin_ref, s_ref, *, lc, nc):
    d = pl.program_id(0)
    c = pl.program_id(2)

    @pl.when(c == 0)
    def _():
        s_ref[...] = s0_ref[...]

    rows = lax.broadcasted_iota(jnp.int32, (lc, lc), 0)
    cols = lax.broadcasted_iota(jnp.int32, (lc, lc), 1)
    rel = jnp.where(d == 0, rows - cols, cols - rows).astype(F32)
    idx = lax.broadcasted_iota(jnp.int32, (lc, 1), 0)
    pos = jnp.where(d == 0, idx, lc - 1 - idx).astype(F32)
    for h in range(N_HEADS):
        lg = jnp.where(d == 0, LOG_GAMMA[0][h], LOG_GAMMA[1][h]).astype(F32)
        sl = slice(h * HEAD_D, (h + 1) * HEAD_D)
        q = q_ref[:, sl]
        k = k_ref[:, sl] * (HEAD_D ** -0.5)
        v = v_ref[:, sl].astype(BF16)
        att = _dot_nt(q.astype(BF16), k.astype(BF16))
        att = att * jnp.where(rel >= 0.0, jnp.exp(jnp.maximum(rel, 0.0) * lg), 0.0)
        qd = q * jnp.exp((pos + 1.0) * lg)
        kd = k * jnp.exp((lc - 1.0 - pos) * lg)
        s = s_ref[h]
        o_ref[:, sl] = _dot(att.astype(BF16), v) + _dot(qd.astype(BF16), s.astype(BF16))
        s_ref[h] = s * jnp.exp(lc * lg) + _dot_tn(kd.astype(BF16), v)

    @pl.when(c == nc - 1)
    def _():
        fin_ref[...] = s_ref[...]


def _retention(z, s0):
    B, T, _ = z.shape
    lc = min(T, 256)
    nc = T // lc

    def tmap(d, c):
        return jnp.where(d == 0, c, nc - 1 - c)

    def zspec(blk):
        return pl.BlockSpec((None, lc, MIX_W), lambda d, b, c: (b, tmap(d, c), blk))

    return pl.pallas_call(
        functools.partial(_ret_kernel, lc=lc, nc=nc),
        grid=(2, B, nc),
        in_specs=[
            zspec(ZB_RET), zspec(ZB_RET + 1), zspec(ZB_RET + 2),
            pl.BlockSpec((None, None, N_HEADS, HEAD_D, HEAD_D), lambda d, b, c: (b, d, 0, 0, 0)),
        ],
        out_specs=[
            pl.BlockSpec((None, None, lc, MIX_W), lambda d, b, c: (d, b, tmap(d, c), 0)),
            pl.BlockSpec((None, None, N_HEADS, HEAD_D, HEAD_D), lambda d, b, c: (b, d, 0, 0, 0)),
        ],
        out_shape=[
            jax.ShapeDtypeStruct((2, B, T, MIX_W), F32),
            jax.ShapeDtypeStruct((B, 2, N_HEADS, HEAD_D, HEAD_D), F32),
        ],
        scratch_shapes=[pltpu.VMEM((N_HEADS, HEAD_D, HEAD_D), F32)],
        compiler_params=_cparams("arbitrary", "arbitrary", "arbitrary"),
        name="retention",
    )(z, z, z, s0)


def _s5_zoh_kernel(lre_ref, lim_ref, ldt_ref, bre_ref, bim_ref, are_ref, aim_ref, bbre_ref, bbim_ref):
    lam_re = jnp.minimum(lre_ref[...], -1e-4)
    lam_im = lim_ref[...]
    dt = jnp.exp(ldt_ref[...])
    mag = jnp.exp(dt * lam_re)
    ang = dt * lam_im
    a_re = mag * jnp.cos(ang)
    a_im = mag * jnp.sin(ang)
    den = lam_re * lam_re + lam_im * lam_im
    f_re = ((a_re - 1.0) * lam_re + a_im * lam_im) / den
    f_im = (a_im * lam_re - (a_re - 1.0) * lam_im) / den
    b_re = bre_ref[...]
    b_im = bim_ref[...]
    are_ref[...] = a_re
    aim_ref[...] = a_im
    bbre_ref[...] = f_re * b_re - f_im * b_im
    bbim_ref[...] = f_re * b_im + f_im * b_re


def _s5_zoh(lam_re, lam_im, log_dt, b_re, b_im):
    n = 2 * S5_STATE
    col = lambda t: t.reshape(n, 1)
    ldt = jnp.broadcast_to(log_dt[:, :, None], (2, S5_GROUPS, S5_P))
    outs = pl.pallas_call(
        _s5_zoh_kernel,
        out_shape=[jax.ShapeDtypeStruct((n, 1), F32), jax.ShapeDtypeStruct((n, 1), F32),
                   jax.ShapeDtypeStruct((n, S5_CH), F32), jax.ShapeDtypeStruct((n, S5_CH), F32)],
        name="s5_zoh",
    )(col(lam_re), col(lam_im), col(ldt), b_re.reshape(n, S5_CH), b_im.reshape(n, S5_CH))
    a_re, a_im, bb_re, bb_im = outs
    shp = (2, S5_GROUPS, S5_P, S5_CH)
    return a_re.reshape(2, 1, S5_STATE), a_im.reshape(2, 1, S5_STATE), bb_re.reshape(shp), bb_im.reshape(shp)


S5_RADIX = 16


def _s5_kernel(u_ref, are_ref, aim_ref, bre_ref, bim_ref, cre_ref, cim_ref, x0r_ref, x0i_ref,
               y_ref, fr_ref, fi_ref, x_re, x_im, o_re, o_im, car_re, car_im, *, L, nc):
    d = pl.program_id(0)
    c = pl.program_id(2)
    R = S5_RADIX
    G = L // R

    def cmul(pr, pi, qr, qi):
        return pr * qr - pi * qi, pr * qi + pi * qr

    def body(rev):
        @pl.when(c == 0)
        def _():
            car_re[...] = x0r_ref[...]
            car_im[...] = x0i_ref[...]

        rr = lax.broadcasted_iota(jnp.int32, (L, L), 0)
        cc = lax.broadcasted_iota(jnp.int32, (L, L), 1)
        lg, lr = G.bit_length() - 1, R.bit_length() - 1
        perm = jnp.where(cc == ((rr & (G - 1)) << lr) + (rr >> lg), 1.0, 0.0).astype(BF16)
        unperm = jnp.where(cc == ((rr & (R - 1)) << lg) + (rr >> lr), 1.0, 0.0).astype(BF16)
        u = _dot(perm, u_ref[...].astype(BF16)).astype(BF16)
        x_re[...] = _dot(u, bre_ref[...])
        x_im[...] = _dot(u, bim_ref[...])
        ar = are_ref[...]
        ai = aim_ref[...]
        order = list(range(R - 1, -1, -1)) if rev else list(range(R))
        slab = lambda j: slice(j * G, (j + 1) * G)

        er = x_re[slab(order[0]), :]
        ei = x_im[slab(order[0]), :]
        for j in order[1:]:
            tr, ti = cmul(ar, ai, er, ei)
            er = tr + x_re[slab(j), :]
            ei = ti + x_im[slab(j), :]

        a_r, a_i = ar, ai
        for _ in range(R.bit_length() - 1):
            a_r, a_i = cmul(a_r, a_i, a_r, a_i)
        zr = car_re[...]
        zi = car_im[...]
        cin_r = [None] * G
        cin_i = [None] * G
        for k in (range(G - 1, -1, -1) if rev else range(G)):
            cin_r[k] = zr
            cin_i[k] = zi
            tr, ti = cmul(a_r, a_i, zr, zi)
            zr = tr + er[k:k + 1, :]
            zi = ti + ei[k:k + 1, :]
        car_re[...] = zr
        car_im[...] = zi

        xr = jnp.concatenate(cin_r, axis=0)
        xi = jnp.concatenate(cin_i, axis=0)
        for j in order:
            tr, ti = cmul(ar, ai, xr, xi)
            xr = tr + x_re[slab(j), :]
            xi = ti + x_im[slab(j), :]
            o_re[slab(j), :] = xr
            o_im[slab(j), :] = xi
        xr_all = _dot(unperm, o_re[...].astype(BF16)).astype(BF16)
        xi_all = _dot(unperm, o_im[...].astype(BF16)).astype(BF16)
        y_ref[...] = _dot(xr_all, cre_ref[...]) - _dot(xi_all, cim_ref[...])

    @pl.when(d == 0)
    def _():
        body(False)

    @pl.when(d == 1)
    def _():
        body(True)

    @pl.when(c == nc - 1)
    def _():
        fr_ref[...] = car_re[...]
        fi_ref[...] = car_im[...]


def _s5(z, a_re, a_im, bblk_re, bblk_im, cblk_re, cblk_im, x0_re, x0_im):
    B, T, _ = z.shape
    L = min(T, 128)
    nc = T // L

    def tmap(d, c):
        return jnp.where(d == 0, c, nc - 1 - c)

    dspec = lambda shape: pl.BlockSpec((None,) + shape, lambda d, b, c: (d, 0, 0))
    sspec = pl.BlockSpec((None, None, 1, S5_STATE), lambda d, b, c: (b, d, 0, 0))
    assert L % S5_RADIX == 0 and (L // S5_RADIX) & (L // S5_RADIX - 1) == 0
    buf = pltpu.VMEM((L, S5_STATE), F32)
    return pl.pallas_call(
        functools.partial(_s5_kernel, L=L, nc=nc),
        grid=(2, B, nc),
        in_specs=[
            pl.BlockSpec((None, L, MIX_W), lambda d, b, c: (b, tmap(d, c), ZB_S5)),
            dspec((1, S5_STATE)), dspec((1, S5_STATE)),
            dspec((MIX_W, S5_STATE)), dspec((MIX_W, S5_STATE)),
            pl.BlockSpec((S5_STATE, MIX_W), lambda d, b, c: (0, 0)),
            pl.BlockSpec((S5_STATE, MIX_W), lambda d, b, c: (0, 0)),
            sspec, sspec,
        ],
        out_specs=[
            pl.BlockSpec((None, None, L, MIX_W), lambda d, b, c: (d, b, tmap(d, c), 0)),
            sspec, sspec,
        ],
        out_shape=[
            jax.ShapeDtypeStruct((2, B, T, MIX_W), F32),
            jax.ShapeDtypeStruct((B, 2, 1, S5_STATE), F32),
            jax.ShapeDtypeStruct((B, 2, 1, S5_STATE), F32),
        ],
        scratch_shapes=[buf, buf, buf, buf, pltpu.VMEM((1, S5_STATE), F32), pltpu.VMEM((1, S5_STATE), F32)],
        compiler_params=_cparams("arbitrary", "arbitrary", "arbitrary"),
        name="s5_scan",
    )(z, a_re, a_im, bblk_re, bblk_im, cblk_re, cblk_im, x0_re, x0_im)


GLA_CHUNK = 16
GLA_CHUNK_SHIFT = 4
SUB = 8


def _gla_kernel(q_ref, zf_ref, v_ref, lb_ref, s0_ref, o_ref, fin_ref, s_ref, a_ref, qs_ref, key_ref,
                *, rev, tb, nt):
    jb = pl.program_id(1)
    ch = GLA_CHUNK
    half = 2 * HEAD_D

    @pl.when(jb == 0)
    def _():
        s_ref[...] = s0_ref[...]

    lb = lb_ref[...]
    zf = zf_ref[...]
    q = q_ref[...]
    qs_ref[...] = q * jax.nn.sigmoid(q)
    key_ref[...] = (1.0 - lb) * jax.nn.sigmoid(-zf)
    l1 = jnp.log(lb)
    l2 = jnp.log1p(-lb) + jnp.minimum(zf, 0.0) - jnp.log1p(jnp.exp(-jnp.abs(zf)))
    lf = jnp.maximum(l1, l2) + jnp.log1p(jnp.exp(-jnp.abs(l1 - l2)))
    r = lax.broadcasted_iota(jnp.int32, (tb, tb), 0)
    c = lax.broadcasted_iota(jnp.int32, (tb, tb), 1)
    same = (r >> GLA_CHUNK_SHIFT) == (c >> GLA_CHUNK_SHIFT)
    tri = jnp.where(same & ((c >= r) if rev else (c <= r)), 1.0, 0.0).astype(BF16)
    p1 = lf.astype(BF16)
    r1 = lf - p1.astype(F32)
    p2 = r1.astype(BF16)
    p3 = (r1 - p2.astype(F32)).astype(BF16)
    a_ref[...] = _dot(tri, p1) + _dot(tri, p2) + _dot(tri, p3)

    rows = lax.broadcasted_iota(jnp.int32, (SUB, half), 0)
    lo = lax.broadcasted_iota(jnp.int32, (SUB, half), 1) < HEAD_D
    same_head = ((lax.broadcasted_iota(jnp.int32, (half, half), 0) >> HEAD_SHIFT)
                 == (lax.broadcasted_iota(jnp.int32, (half, half), 1) >> HEAD_SHIFT))

    def pair_sums(p):
        s0 = jnp.sum(jnp.where(lo, p, 0.0), axis=1, keepdims=True)
        s1 = jnp.sum(jnp.where(lo, 0.0, p), axis=1, keepdims=True)
        return jnp.where(lo, s0, s1)

    def chunk(ci, carry):
        t0 = pl.multiple_of((tb // ch - 1 - ci if rev else ci) * ch, ch)
        win = pl.ds(t0, ch)
        a = a_ref[win, :]
        qc = qs_ref[win, :]
        kc = key_ref[win, :]
        vc = v_ref[win, :]
        last = 0 if rev else ch - 1
        a_last = a[last:last + 1, :]
        qe = (qc * jnp.exp(a)).astype(BF16)
        ke = (kc * jnp.exp(a_last - a)).astype(BF16)
        ea = jnp.exp(a_last)
        vb = vc.astype(BF16)
        nslab = ch // SUB
        acc = [[jnp.zeros((SUB, half), F32) for _ in range(2)] for _ in range(nslab)]
        for jj in range(ch):
            for sb in range(nslab):
                r0 = sb * SUB
                if (r0 > jj) if rev else (r0 + SUB - 1 < jj):
                    continue
                whole = (r0 + SUB - 1 <= jj) if rev else (r0 >= jj)
                rs = slice(r0, r0 + SUB)
                valid = (rows + r0 <= jj) if rev else (rows + r0 >= jj)
                for hp in range(2):
                    sl = slice(hp * half, (hp + 1) * half)
                    dec = jnp.exp(jnp.minimum(a[rs, sl] - a[jj:jj + 1, sl], 0.0))
                    p = qc[rs, sl] * dec * kc[jj:jj + 1, sl]
                    if not whole:
                        p = jnp.where(valid, p, 0.0)
                    acc[sb][hp] = acc[sb][hp] + pair_sums(p) * vc[jj:jj + 1, sl]
        for hp in range(2):
            sl = slice(hp * half, (hp + 1) * half)
            s = s_ref[hp]
            inter = _dot_nt(qe[:, sl], s.astype(BF16))
            intra = jnp.concatenate([acc[sb][hp] for sb in range(nslab)], axis=0)
            o_ref[win, sl] = inter + intra
            s_ref[hp] = s * ea[:, sl] + jnp.where(same_head, _dot_tn(vb[:, sl], ke[:, sl]), 0.0)
        return carry

    lax.fori_loop(0, tb // ch, chunk, 0, unroll=2)

    @pl.when(jb == nt - 1)
    def _():
        fin_ref[...] = s_ref[...]


def _gla(z, lb, s0, rev):
    B, T, _ = z.shape
    pair = 2 * HEAD_D
    s0 = s0.reshape(B, 2, 2, HEAD_D, HEAD_D)
    zero = jnp.zeros_like(s0[:, :, 0])
    s0 = jnp.concatenate([jnp.concatenate([s0[:, :, 0], zero], axis=-1),
                          jnp.concatenate([zero, s0[:, :, 1]], axis=-1)], axis=-2)
    tb = min(T, 256)
    nt = T // tb
    di = 1 if rev else 0
    tmap = (lambda j: nt - 1 - j) if rev else (lambda j: j)
    zspec = lambda blk: pl.BlockSpec((None, tb, MIX_W), lambda b, j: (b, tmap(j), blk))
    sspec = pl.BlockSpec((None, 2, pair, pair), lambda b, j: (b, 0, 0, 0))
    blk = pltpu.VMEM((tb, MIX_W), F32)
    o, fin = pl.pallas_call(
        functools.partial(_gla_kernel, rev=rev, tb=tb, nt=nt),
        grid=(B, nt),
        in_specs=[zspec(ZB_HG), zspec(ZB_HG + 1 + di), zspec(ZB_HG + 3),
                  pl.BlockSpec((1, MIX_W), lambda b, j: (0, 0)), sspec],
        out_specs=[pl.BlockSpec((None, tb, MIX_W), lambda b, j: (b, tmap(j), 0)), sspec],
        out_shape=[jax.ShapeDtypeStruct((B, T, MIX_W), F32),
                   jax.ShapeDtypeStruct((B, 2, pair, pair), F32)],
        scratch_shapes=[pltpu.VMEM((2, pair, pair), F32), blk, blk, blk],
        compiler_params=_cparams("parallel", "arbitrary"),
        name="hgrn_gla_bwd" if rev else "hgrn_gla_fwd",
    )(z, z, z, lb[di:di + 1], s0)
    fin = jnp.stack([fin[:, :, :HEAD_D, :HEAD_D], fin[:, :, HEAD_D:, HEAD_D:]], axis=2)
    return o, fin.reshape(B, N_HEADS, HEAD_D, HEAD_D)


def _rw_prep_kernel(*refs, grid_shift, tm):
    if grid_shift:
        (zc_ref, zu_ref, zd_ref, mu_ref, vec_ref, w0_ref, lora_ref,
         r_ref, k2_ref, v_ref, w_ref, nkk_ref, kka_ref, g_ref, bonus_ref, buf_ref) = refs
    else:
        (zc_ref, mu_ref, vec_ref, w0_ref, lora_ref,
         r_ref, k2_ref, v_ref, w_ref, nkk_ref, kka_ref, g_ref, bonus_ref, buf_ref) = refs
    i = pl.program_id(1)
    nt = pl.num_programs(1)
    halo = GRID_W
    width = 4 * MIX_W
    z = zc_ref[...]
    buf_ref[halo:halo + tm, :] = z
    lane = lax.broadcasted_iota(jnp.int32, (tm, width), 1)
    row = lax.broadcasted_iota(jnp.int32, (tm, width), 0)
    if grid_shift:
        buf_ref[0:halo, :] = jnp.where(i > 0, zu_ref[...], 0.0)
        buf_ref[halo + tm:2 * halo + tm, :] = jnp.where(i < nt - 1, zd_ref[...], 0.0)
        col = row & (GRID_W - 1)
        left = jnp.where(col > 0, buf_ref[halo - 1:halo - 1 + tm, :], 0.0)
        right = jnp.where(col < GRID_W - 1, buf_ref[halo + 1:halo + 1 + tm, :], 0.0)
        up = buf_ref[0:tm, :]
        down = buf_ref[2 * halo:2 * halo + tm, :]
        sel = lane & 3
        shifted = jnp.where(sel == 0, left, jnp.where(sel == 1, right, jnp.where(sel == 2, up, down)))
    else:
        zrow = jnp.zeros((1, width), F32)
        buf_ref[halo - 1:halo, :] = zrow
        buf_ref[halo + tm:halo + tm + 1, :] = zrow
        prev = buf_ref[halo - 1:halo - 1 + tm, :]
        nxt = buf_ref[halo + 1:halo + 1 + tm, :]
        shifted = jnp.where((lane & 1) == 0, prev, nxt)
    zs = z + mu_ref[...] * (shifted - z)
    r = zs[:, 0:MIX_W]
    k = zs[:, MIX_W:2 * MIX_W]
    v = zs[:, 2 * MIX_W:3 * MIX_W]
    sm = zs[:, 3 * MIX_W:4 * MIX_W]
    ones_bd = _head_ones()
    a0, k_k, k_a, r_k = (vec_ref[j:j + 1, :] for j in range(4))
    a = jax.nn.sigmoid(a0 + _dot(sm.astype(BF16), lora_ref[2]))
    g_ref[...] = _dot(jax.nn.sigmoid(sm).astype(BF16), lora_ref[3])
    kk = k * k_k
    kk = kk * lax.rsqrt(_head_sum(kk * kk, ones_bd) + 1e-12)
    k2 = k * (1.0 + (a - 1.0) * k_a)
    th = jnp.tanh(sm).astype(BF16)
    for di in range(2):
        w_ref[di] = -math.exp(-0.5) * jax.nn.sigmoid(w0_ref[di:di + 1, :] + _dot(th, lora_ref[di]))
    r_ref[...] = r
    k2_ref[...] = k2
    v_ref[...] = v
    nkk_ref[...] = -kk
    kka_ref[...] = kk * a
    bonus_ref[...] = _head_sum(r * k2 * r_k, ones_bd) * v


def _rw_prep(z, mu_p, vecs, w0, lora, grid_shift):
    B, T, _ = z.shape
    width = 4 * MIX_W
    wblk = ZB_RW // 4
    if grid_shift:
        tm = min(T, 512)
        hb = tm // GRID_W
        nh = T // GRID_W
        z_specs = [
            pl.BlockSpec((None, tm, width), lambda b, i: (b, i, wblk)),
            pl.BlockSpec((None, GRID_W, width), lambda b, i: (b, jnp.maximum(i * hb - 1, 0), wblk)),
            pl.BlockSpec((None, GRID_W, width), lambda b, i: (b, jnp.minimum((i + 1) * hb, nh - 1), wblk)),
        ]
        z_args = (z, z, z)
    else:
        tm = T
        z_specs = [pl.BlockSpec((None, tm, width), lambda b, i: (b, i, wblk))]
        z_args = (z,)
    const = lambda shape: pl.BlockSpec(shape, lambda b, i: (0,) * len(shape))
    ospec = pl.BlockSpec((None, tm, MIX_W), lambda b, i: (b, i, 0))
    oshape = jax.ShapeDtypeStruct((B, T, MIX_W), F32)
    dspec = pl.BlockSpec((2, None, tm, MIX_W), lambda b, i: (0, b, i, 0))
    dshape = jax.ShapeDtypeStruct((2, B, T, MIX_W), F32)
    return pl.pallas_call(
        functools.partial(_rw_prep_kernel, grid_shift=grid_shift, tm=tm),
        grid=(B, T // tm),
        in_specs=z_specs + [const((1, width)), const((4, MIX_W)), const((2, MIX_W)),
                            const((4, MIX_W, MIX_W))],
        out_specs=[ospec, ospec, ospec, dspec, ospec, ospec, ospec, ospec],
        out_shape=[oshape, oshape, oshape, dshape, oshape, oshape, oshape, oshape],
        scratch_shapes=[pltpu.VMEM((tm + 2 * GRID_W, width), F32)],
        compiler_params=_cparams("parallel", "parallel"),
        name="rwkv_prep",
    )(*z_args, mu_p, vecs, w0, lora)


RW_CHUNK = 64
RW_CHUNK_SHIFT = 6


def _mm(a, b):
    return jnp.dot(a.astype(BF16), b.astype(BF16), preferred_element_type=F32)


def _rwkv_kernel(lw_ref, a_ref, b_ref, k_ref, r_ref, v_ref, s0_ref, y_ref, fin_ref, s_ref, g_ref,
                 *, rev, tb, nt):
    jb = pl.program_id(1)
    ch = RW_CHUNK
    pair = 2 * HEAD_D

    @pl.when(jb == 0)
    def _():
        s_ref[...] = s0_ref[...]

    lw = lw_ref[...]
    rr = lax.broadcasted_iota(jnp.int32, (tb, tb), 0)
    cc = lax.broadcasted_iota(jnp.int32, (tb, tb), 1)
    same = (rr >> RW_CHUNK_SHIFT) == (cc >> RW_CHUNK_SHIFT)
    tri = jnp.where(same & ((cc >= rr) if rev else (cc <= rr)), 1.0, 0.0).astype(BF16)
    p1 = lw.astype(BF16)
    r1 = lw - p1.astype(F32)
    p2 = r1.astype(BF16)
    p3 = (r1 - p2.astype(F32)).astype(BF16)
    g_ref[...] = _dot(tri, p1) + _dot(tri, p2) + _dot(tri, p3)

    si = lax.broadcasted_iota(jnp.int32, (ch, ch), 0)
    ri = lax.broadcasted_iota(jnp.int32, (ch, ch), 1)
    if rev:
        si, ri = ch - 1 - si, ch - 1 - ri
    strict = ri < si
    incl = ri <= si
    eye = jnp.where(ri == si, 1.0, 0.0).astype(F32)
    levels = [((si >> (lv + 1)) == (ri >> (lv + 1))) & (((si >> lv) & 1) == 1) & (((ri >> lv) & 1) == 0)
              for lv in range(RW_CHUNK_SHIFT)]
    lo = lax.broadcasted_iota(jnp.int32, (ch, pair), 1) < HEAD_D
    same_head = ((lax.broadcasted_iota(jnp.int32, (pair, pair), 0) >> HEAD_SHIFT)
                 == (lax.broadcasted_iota(jnp.int32, (pair, pair), 1) >> HEAD_SHIFT))

    nch = tb // ch
    g = g_ref[...]
    e_g = jnp.exp(g)
    e_ng = jnp.exp(-g)
    at_f = a_ref[...] * jnp.exp(g - lw)
    rt_f = r_ref[...] * e_g
    at = at_f.astype(BF16)
    rt = rt_f.astype(BF16)
    first = (lax.broadcasted_iota(jnp.int32, (tb, MIX_W), 1) & HEAD_D) == 0
    at_h = [jnp.where(first, at_f, 0.0).astype(BF16), jnp.where(first, 0.0, at_f).astype(BF16)]
    rt_h = [jnp.where(first, rt_f, 0.0).astype(BF16), jnp.where(first, 0.0, rt_f).astype(BF16)]
    bt = (b_ref[...] * e_ng).astype(BF16)
    kt = (k_ref[...] * e_ng).astype(BF16)
    vb = v_ref[...].astype(BF16)

    items = [(c, hp, hh) for c in range(nch) for hp in range(2) for hh in range(2)]
    rows = lambda c: slice(c * ch, (c + 1) * ch)
    lanes = lambda hp: slice(hp * pair, (hp + 1) * pair)
    n_m, p_m, m_m, q_m = {}, {}, {}, {}
    for it in items:
        c, hp, hh = it
        ar = jnp.concatenate([at_h[hh][rows(c), lanes(hp)], rt_h[hh][rows(c), lanes(hp)]], axis=0)
        np_ = _dot_nt(ar, bt[rows(c), lanes(hp)])
        mq = _dot_nt(ar, kt[rows(c), lanes(hp)])
        n_m[it] = jnp.where(strict, np_[:ch], 0.0).astype(BF16)
        p_m[it] = jnp.where(incl, np_[ch:], 0.0).astype(BF16)
        m_m[it] = jnp.where(strict, mq[:ch], 0.0).astype(BF16)
        q_m[it] = jnp.where(incl, mq[ch:], 0.0).astype(BF16)
    t_m = {it: eye + jnp.where(levels[0], n_m[it].astype(F32), 0.0) for it in items}
    for lv in range(1, RW_CHUNK_SHIFT):
        tn = {it: _mm(t_m[it], n_m[it]) for it in items}
        t_m = {it: t_m[it] + jnp.where(levels[lv], _mm(tn[it], t_m[it]), 0.0) for it in items}
    mv = {it: _mm(m_m[it], vb[rows(it[0]), lanes(it[1])]) for it in items}
    qv = {it: _mm(q_m[it], vb[rows(it[0]), lanes(it[1])]) for it in items}
    y2 = {it: _mm(t_m[it], jnp.concatenate([at[rows(it[0]), lanes(it[1])], mv[it].astype(BF16)], axis=1))
          for it in items}

    last = 0 if rev else ch - 1
    for ci in range(nch):
        c = nch - 1 - ci if rev else ci
        g_c = g[c * ch + last:c * ch + last + 1, :]
        e_gc = jnp.exp(g_c - g[rows(c), :])
        bh = (b_ref[rows(c), :] * e_gc).astype(BF16)
        kh = (k_ref[rows(c), :] * e_gc).astype(BF16)
        dec_c = jnp.exp(g_c)
        xs, sas, ss = [], [], []
        for hp in range(2):
            s = s_ref[hp]
            w2 = jnp.where(lo, y2[(c, hp, 0)][:, :pair], y2[(c, hp, 1)][:, :pair]).astype(BF16)
            xs.append(_dot_nt(jnp.concatenate([w2, rt[rows(c), lanes(hp)]], axis=0), s.astype(BF16)))
            ss.append(s)
        for hp in range(2):
            w1 = jnp.where(lo, y2[(c, hp, 0)][:, pair:], y2[(c, hp, 1)][:, pair:])
            sas.append(w1 + xs[hp][:ch])
        for hp in range(2):
            sa = sas[hp]
            y_ref[rows(c), lanes(hp)] = xs[hp][ch:] + jnp.where(
                lo, _mm(p_m[(c, hp, 0)], sa) + qv[(c, hp, 0)], _mm(p_m[(c, hp, 1)], sa) + qv[(c, hp, 1)])
            upd = _dot_tn(jnp.concatenate([sa.astype(BF16), vb[rows(c), lanes(hp)]], axis=0),
                          jnp.concatenate([bh[:, lanes(hp)], kh[:, lanes(hp)]], axis=0))
            s_ref[hp] = ss[hp] * dec_c[:, lanes(hp)] + jnp.where(same_head, upd, 0.0)

    @pl.when(jb == nt - 1)
    def _():
        fin_ref[...] = s_ref[...]


def _head_pair_blockdiag(s):
    B = s.shape[0]
    s = s.reshape(B, 2, 2, HEAD_D, HEAD_D)
    zero = jnp.zeros_like(s[:, :, 0])
    return jnp.concatenate([jnp.concatenate([s[:, :, 0], zero], axis=-1),
                            jnp.concatenate([zero, s[:, :, 1]], axis=-1)], axis=-2)


def _head_pair_blocks(t):
    B = t.shape[0]
    t = jnp.stack([t[:, :, :HEAD_D, :HEAD_D], t[:, :, HEAD_D:, HEAD_D:]], axis=2)
    return t.reshape(B, N_HEADS, HEAD_D, HEAD_D)


def _rwkv(lw, a, b, k, r, v, s0, rev):
    B, T, _ = v.shape
    pair = 2 * HEAD_D
    tb = min(T, 256)
    nt = T // tb
    di = 1 if rev else 0
    tmap = (lambda j: nt - 1 - j) if rev else (lambda j: j)
    spec = pl.BlockSpec((None, tb, MIX_W), lambda bi, j: (bi, tmap(j), 0))
    sspec = pl.BlockSpec((None, 2, pair, pair), lambda bi, j: (bi, 0, 0, 0))
    y, fin = pl.pallas_call(
        functools.partial(_rwkv_kernel, rev=rev, tb=tb, nt=nt),
        grid=(B, nt),
        in_specs=[pl.BlockSpec((None, None, tb, MIX_W), lambda bi, j: (di, bi, tmap(j), 0)),
                  spec, spec, spec, spec, spec, sspec],
        out_specs=[spec, sspec],
        out_shape=[jax.ShapeDtypeStruct((B, T, MIX_W), F32),
                   jax.ShapeDtypeStruct((B, 2, pair, pair), F32)],
        scratch_shapes=[pltpu.VMEM((2, pair, pair), F32), pltpu.VMEM((tb, MIX_W), F32)],
        compiler_params=_cparams("parallel", "arbitrary"),
        name="rwkv_chunk_bwd" if rev else "rwkv_chunk_fwd",
    )(lw, a, b, k, r, v, _head_pair_blockdiag(s0))
    return y, _head_pair_blocks(fin)


def _mix_kernel(x_ref, sc_ref, sh_ref, g1_ref, nw0_ref, nw_ref, wg_ref,
                ret_ref, retg_ref, s5_ref, s5u_ref, hgf_ref, hgb_ref, hgg_ref, rwf_ref, rwr_ref,
                rwb_ref, rwg_ref,
                vec_ref, glu_w_ref, wbr_ref, wout_ref, o_ref):
    ones_bd = _head_ones()
    h = (_rms(x_ref[...], nw0_ref[...]) * (1.0 + sc_ref[...]) + sh_ref[...]).astype(BF16)
    gn_w, s5_d, glu_b, hg_w, ln_w, ln_b = (vec_ref[j:j + 1, :] for j in range(6))

    def group_norm(o, eps):
        mu = _head_sum(o, ones_bd) * (1.0 / HEAD_D)
        oc = o - mu
        var = _head_sum(oc * oc, ones_bd) * (1.0 / HEAD_D)
        return oc * lax.rsqrt(var + eps)

    g = retg_ref[...]
    y_ret = group_norm(ret_ref[0] + ret_ref[1], EPS) * gn_w * (g * jax.nn.sigmoid(g))

    y = s5_d * s5u_ref[...] + s5_ref[0] + s5_ref[1]
    yg = jax.nn.gelu(y)
    y_s5 = yg * jax.nn.sigmoid(_dot(yg.astype(BF16), glu_w_ref[...]) + glu_b)

    o = hgf_ref[...] + hgb_ref[...]
    g = hgg_ref[...]
    ms = _head_sum(o * o, ones_bd) * (1.0 / HEAD_D)
    y_hg = o * lax.rsqrt(ms + EPS) * hg_w * (g * jax.nn.sigmoid(g))

    y = group_norm(rwf_ref[...] + rwr_ref[...], RW_LN_EPS) * ln_w + ln_b
    y_rw = (y + rwb_ref[...]) * rwg_ref[...]

    mixed = None
    for m, ym in enumerate((y_ret, y_s5, y_hg, y_rw)):
        br = _dot(ym.astype(BF16), wbr_ref[m])
        term = jax.nn.sigmoid(_dot(h, wg_ref[:, m * D_MODEL:(m + 1) * D_MODEL])) * br
        mixed = term if mixed is None else mixed + term
    mixed = _dot(mixed.astype(BF16), wout_ref[...])
    o_ref[...] = x_ref[...] + g1_ref[...] * _rms(mixed, nw_ref[...])


def _mix(x, sc, sh, g1, nw0, nw, w_gate, z, ret_o, s5_y, hg_o, rw_y, rw_bonus, rw_g, vecs, glu_w,
         w_branch, w_out):
    B, T, _ = x.shape
    tm = min(T, 256)
    xspec = pl.BlockSpec((None, tm, D_MODEL), lambda b, i: (b, i, 0))
    zspec = lambda blk: pl.BlockSpec((None, tm, MIX_W), lambda b, i: (b, i, blk))
    dspec = pl.BlockSpec((2, None, tm, MIX_W), lambda b, i: (0, b, i, 0))
    const = lambda shape: pl.BlockSpec(shape, lambda b, i: (0,) * len(shape))
    mspec = lambda m: pl.BlockSpec((None, 1, D_MODEL), _mod_map(m.shape[0]))
    return pl.pallas_call(
        _mix_kernel,
        grid=(B, T // tm),
        in_specs=[
            xspec, mspec(sc), mspec(sh), mspec(g1),
            const((1, D_MODEL)), const((1, D_MODEL)),
            const((D_MODEL, 4 * D_MODEL)),
            dspec, zspec(ZB_RET + 3),
            dspec, zspec(ZB_S5),
            zspec(0), zspec(0), zspec(ZB_HG + 4),
            zspec(0), zspec(0), zspec(0), zspec(0),
            const((6, MIX_W)), const((MIX_W, MIX_W)),
            const((4, MIX_W, D_MODEL)), const((D_MODEL, D_MODEL)),
        ],
        out_specs=xspec,
        out_shape=jax.ShapeDtypeStruct((B, T, D_MODEL), F32),
        compiler_params=_cparams("parallel", "parallel"),
        name="mix_out",
    )(x, sc, sh, g1, nw0, nw, w_gate, ret_o, z, s5_y, z, hg_o[0], hg_o[1], z, rw_y[0], rw_y[1],
      rw_bonus, rw_g, vecs, glu_w, w_branch, w_out)


FFN_CHUNK = 1024


def _ffn_kernel(x_ref, sc_ref, sh_ref, g2_ref, nw2_ref, nw3_ref, w1_ref, w2_ref, o_ref):
    x = x_ref[...]
    h = (_rms(x, nw2_ref[...]) * (1.0 + sc_ref[...]) + sh_ref[...]).astype(BF16)
    acc = None
    for j in range(D_FF // FFN_CHUNK):
        cols = slice(j * FFN_CHUNK, (j + 1) * FFN_CHUNK)
        a = jnp.maximum(_dot(h, w1_ref[:, cols]), 0.0)
        part = _dot((a * a).astype(BF16), w2_ref[cols, :])
        acc = part if acc is None else acc + part
    o_ref[...] = x + g2_ref[...] * _rms(acc, nw3_ref[...])


def _ffn(x, sc, sh, g2, nw2, nw3, w1, w2):
    B, T, _ = x.shape
    tm = min(T, 512)
    xspec = pl.BlockSpec((None, tm, D_MODEL), lambda b, i: (b, i, 0))
    mspec = lambda m: pl.BlockSpec((None, 1, D_MODEL), _mod_map(m.shape[0]))
    const = pl.BlockSpec((1, D_MODEL), lambda b, i: (0, 0))
    return pl.pallas_call(
        _ffn_kernel,
        grid=(B, T // tm),
        in_specs=[xspec, mspec(sc), mspec(sh), mspec(g2), const, const,
                  pl.BlockSpec((D_MODEL, D_FF), lambda b, i: (0, 0)),
                  pl.BlockSpec((D_FF, D_MODEL), lambda b, i: (0, 0))],
        out_specs=xspec,
        out_shape=jax.ShapeDtypeStruct((B, T, D_MODEL), F32),
        compiler_params=_cparams("parallel", "parallel"),
        name="ffn",
    )(x, sc, sh, g2, nw2, nw3, w1, w2)


def _layer_params(l, p):
    w = p['w_in'][l]
    w_p = jnp.concatenate(
        [w[:, 2560:3488], jnp.zeros((D_MODEL, 4 * MIX_W - RW_COLS), F32),
         w[:, 0:1024], w[:, 1024:1280], w[:, 1280:2560]], axis=1).astype(BF16)
    w_gate = w[:, 3488:7584].astype(BF16)

    a_re, a_im, bb_re, bb_im = _s5_zoh(p['s5_lam_re'][l], p['s5_lam_im'][l], p['s5_log_dt'][l],
                                       p['s5_b_re'][l], p['s5_b_im'][l])
    eye = jnp.eye(S5_GROUPS, dtype=F32)
    bblk = lambda bb: jnp.einsum('dgph,gk->dghkp', bb, eye).reshape(2, MIX_W, S5_STATE).astype(BF16)
    cblk = lambda c: jnp.einsum('ghp,gk->gpkh', c, eye).reshape(S5_STATE, MIX_W).astype(BF16)

    def lora_pad(m, row0):
        return jnp.zeros((MIX_W, MIX_W), F32).at[row0:row0 + m.shape[0]].set(m)

    lora = jnp.stack([lora_pad(p['rw_w2'][l, 0], 0), lora_pad(p['rw_w2'][l, 1], 32),
                      lora_pad(p['rw_a2'][l], 64), lora_pad(p['rw_g2'][l], 96)]).astype(BF16)
    mu_p = jnp.concatenate([p['rw_mu'][l], jnp.zeros((4 * MIX_W - RW_COLS,), F32)]).reshape(1, 4 * MIX_W)
    return dict(
        w_p=w_p, w_gate=w_gate, nw=p['norm_w'][l],
        s5=(a_re, a_im, bblk(bb_re), bblk(bb_im), cblk(p['s5_c_re'][l]), cblk(p['s5_c_im'][l])),
        rw_mu=mu_p, rw_lora=lora, rw_w0=p['rw_w0'][l],
        rw_vecs=jnp.stack([p['rw_a0'][l], p['rw_k_k'][l], p['rw_k_a'][l], p['rw_r_k'][l]]),
        mix_vecs=jnp.stack([p['ret_gn_w'][l], p['s5_d'][l], p['s5_glu_b'][l], p['hg_norm_w'][l],
                            p['rw_ln_w'][l], p['rw_ln_b'][l]]),
        glu_w=p['s5_glu_w'][l].astype(BF16),
        w_branch=p['w_branch'][l].astype(BF16), w_out=p['w_out'][l].astype(BF16),
        ff_w1=p['ff_w1'][l].astype(BF16), ff_w2=p['ff_w2'][l].astype(BF16),
    )


def _trunk_layer(x, mod, init, grid_shift, lp, hg_lb):
    B, T, _ = x.shape
    sh1, sc1, g1, sh2, sc2, g2 = (m[:, None, :] for m in jnp.split(mod, 6, axis=-1))
    nw = lp['nw']
    if mod.shape[0] == 1:
        tok = lambda t: t.reshape(t.shape[:-3] + (1, B * T, t.shape[-1]))
    else:
        tok = lambda t: t
    z = _proj_in(tok(x), sc1, sh1, nw[0:1], lp['w_p']).reshape(B, T, Z_COLS)

    s_ret, s_s5r, s_s5i, s_hg, s_rw = init
    ret_o, f_ret = _retention(z, s_ret)

    s5_y, f_s5r, f_s5i = _s5(z, *lp['s5'], s_s5r.reshape(B, 2, 1, S5_STATE), s_s5i.reshape(B, 2, 1, S5_STATE))
    f_s5r = f_s5r.reshape(B, 2, S5_GROUPS, S5_P)
    f_s5i = f_s5i.reshape(B, 2, S5_GROUPS, S5_P)

    s_hg_t = jnp.swapaxes(s_hg, -1, -2)
    hg_f, f_hg_f = _gla(z, hg_lb, s_hg_t[:, 0], False)
    hg_b, f_hg_b = _gla(z, hg_lb, s_hg_t[:, 1], True)
    f_hg = jnp.swapaxes(jnp.stack([f_hg_f, f_hg_b], axis=1), -1, -2)

    r, k2, v, w, nkk, kka, rw_g, bonus = _rw_prep(z, lp['rw_mu'], lp['rw_vecs'], lp['rw_w0'],
                                                  lp['rw_lora'], grid_shift)
    rw_f, f_rw_f = _rwkv(w, nkk, kka, k2, r, v, s_rw[:, 0], False)
    rw_b, f_rw_b = _rwkv(w, nkk, kka, k2, r, v, s_rw[:, 1], True)
    f_rw = jnp.stack([f_rw_f, f_rw_b], axis=1)

    x = _mix(tok(x), sc1, sh1, g1, nw[0:1], nw[1:2], lp['w_gate'], tok(z), tok(ret_o), tok(s5_y),
             (tok(hg_f), tok(hg_b)), (tok(rw_f), tok(rw_b)), tok(bonus), tok(rw_g),
             lp['mix_vecs'], lp['glu_w'], lp['w_branch'], lp['w_out'])
    x = _ffn(x, sc2, sh2, g2, nw[2:3], nw[3:4], lp['ff_w1'], lp['ff_w2'])
    return x.reshape(B, T, D_MODEL), (f_ret, f_s5r, f_s5i, f_hg, f_rw)


def kernel(x_prompt, x_sample, state_ret, state_s5_re, state_s5_im, state_hgrn, state_rwkv, c, c_ctx, ada_w, ada_b, norm_w, w_in, ret_gn_w, s5_lam_re, s5_lam_im, s5_log_dt, s5_b_re, s5_b_im, s5_c_re, s5_c_im, s5_d, s5_glu_w, s5_glu_b, hg_lb, hg_norm_w, rw_mu, rw_w0, rw_w2, rw_a0, rw_a2, rw_g2, rw_k_k, rw_k_a, rw_r_k, rw_ln_w, rw_ln_b, w_branch, w_out, ff_w1, ff_w2):
    p = dict(norm_w=norm_w, w_in=w_in, ret_gn_w=ret_gn_w, s5_lam_re=s5_lam_re, s5_lam_im=s5_lam_im,
             s5_log_dt=s5_log_dt, s5_b_re=s5_b_re, s5_b_im=s5_b_im, s5_c_re=s5_c_re, s5_c_im=s5_c_im,
             s5_d=s5_d, s5_glu_w=s5_glu_w, s5_glu_b=s5_glu_b, hg_norm_w=hg_norm_w, rw_mu=rw_mu,
             rw_w0=rw_w0, rw_w2=rw_w2, rw_a0=rw_a0, rw_a2=rw_a2, rw_g2=rw_g2, rw_k_k=rw_k_k,
             rw_k_a=rw_k_a, rw_r_k=rw_r_k, rw_ln_w=rw_ln_w, rw_ln_b=rw_ln_b, w_branch=w_branch,
             w_out=w_out, ff_w1=ff_w1, ff_w2=ff_w2)
    depth = w_in.shape[0]
    n_ctx = x_prompt.shape[0]
    n_lat = x_sample.shape[0]
    assert 1 + n_lat <= 8

    lb_cum = jnp.cumsum(jax.nn.softmax(hg_lb.astype(F32), axis=0), axis=0)
    hg_lower = lb_cum - lb_cum[0]

    cond = jnp.concatenate([c_ctx[None, :], c, jnp.zeros((7 - n_lat, D_MODEL), F32)], axis=0)
    mod = _ada_mod(cond, ada_w, ada_b)

    zero_state = (jnp.zeros((n_ctx, 2, N_HEADS, HEAD_D, HEAD_D), F32),
                  jnp.zeros((n_ctx, 2, S5_GROUPS, S5_P), F32),
                  jnp.zeros((n_ctx, 2, S5_GROUPS, S5_P), F32),
                  jnp.zeros((n_ctx, 2, N_HEADS, HEAD_D, HEAD_D), F32),
                  jnp.zeros((n_ctx, 2, N_HEADS, HEAD_D, HEAD_D), F32))
    xp, xs = x_prompt, x_sample
    finals = []
    for l in range(depth):
        lp = _layer_params(l, p)
        xp, fin = _trunk_layer(xp, mod[l, 0:1], zero_state, False, lp, hg_lower[l])
        finals.append(fin)
        lat_init = (state_ret[:, l], state_s5_re[:, l], state_s5_im[:, l], state_hgrn[:, l], state_rwkv[:, l])
        xs, _ = _trunk_layer(xs, mod[l, 1:1 + n_lat], lat_init, True, lp, hg_lower[l])
    new_states = tuple(jnp.stack([f[i] for f in finals], axis=1) for i in range(5))
    return (xp, xs) + new_states
```

```python
import functools
import math

import jax
import jax.numpy as jnp
from jax import lax
from jax.experimental import pallas as pl
from jax.experimental.pallas import tpu as pltpu

F32 = jnp.float32
BF16 = jnp.bfloat16

D_MODEL = 1024
GRID_W = 64
MIX_W = 256
N_HEADS = 4
HEAD_D = 64
HEAD_SHIFT = 6
LANES = 128
S5_GROUPS = 16
S5_CH = 16
S5_P = 64
S5_STATE = S5_GROUPS * S5_P
D_FF = 4096
EPS = 1e-6
RW_LN_EPS = 64e-5
RW_COLS = 928

Z_COLS = 3584
ZB_RW = 0
ZB_RET = 4
ZB_S5 = 8
ZB_HG = 9

VMEM_LIMIT = 56 * 1024 * 1024

LOG_GAMMA = tuple(
    tuple(math.log1p(-2.0 ** (-(5.0 + 0.5 * di) - h)) for h in range(N_HEADS)) for di in range(2))


def _cparams(*sem):
    return pltpu.CompilerParams(dimension_semantics=sem, vmem_limit_bytes=VMEM_LIMIT)


def _dot(a, b):
    return jnp.dot(a, b, preferred_element_type=F32)


def _dot_nt(a, b):
    return lax.dot_general(a, b, (((1,), (1,)), ((), ())), preferred_element_type=F32)


def _dot_tn(a, b):
    return lax.dot_general(a, b, (((0,), (0,)), ((), ())), preferred_element_type=F32)


def _head_ones():
    r = lax.broadcasted_iota(jnp.int32, (MIX_W, MIX_W), 0) >> HEAD_SHIFT
    c = lax.broadcasted_iota(jnp.int32, (MIX_W, MIX_W), 1) >> HEAD_SHIFT
    return jnp.where(r == c, 1.0, 0.0).astype(F32)


def _head_sum(x, ones_bd):
    ones = ones_bd.astype(BF16)
    p1 = x.astype(BF16)
    r1 = x - p1.astype(F32)
    p2 = r1.astype(BF16)
    p3 = (r1 - p2.astype(F32)).astype(BF16)
    return _dot(p1, ones) + _dot(p2, ones) + _dot(p3, ones)


def _rms(x, w):
    return x * lax.rsqrt(jnp.mean(x * x, axis=-1, keepdims=True) + EPS) * w


def _ada_kernel(c_ref, w_ref, b_ref, o_ref):
    c = c_ref[...]
    s = c * jax.nn.sigmoid(c)
    o_ref[...] = _dot(s.astype(BF16), w_ref[...].astype(BF16)) + b_ref[...]


def _ada_mod(cond, ada_w, ada_b):
    L = ada_w.shape[0]
    n = ada_w.shape[2]
    tn = 1536
    return pl.pallas_call(
        _ada_kernel,
        grid=(L, n // tn),
        in_specs=[
            pl.BlockSpec((8, D_MODEL), lambda l, j: (0, 0)),
            pl.BlockSpec((None, D_MODEL, tn), lambda l, j: (l, 0, j)),
            pl.BlockSpec((None, 1, tn), lambda l, j: (l, 0, j)),
        ],
        out_specs=pl.BlockSpec((None, 8, tn), lambda l, j: (l, 0, j)),
        out_shape=jax.ShapeDtypeStruct((L, 8, n), F32),
        compiler_params=_cparams("parallel", "parallel"),
        name="ada_mod",
    )(cond, ada_w, ada_b.reshape(L, 1, n))


def _proj_in_kernel(x_ref, sc_ref, sh_ref, nw_ref, w_ref, z_ref):
    h = _rms(x_ref[...], nw_ref[...]) * (1.0 + sc_ref[...]) + sh_ref[...]
    z_ref[...] = _dot(h.astype(BF16), w_ref[...])


def _mod_map(bm):
    if bm == 1:
        return lambda b, *_: (0, 0, 0)
    return lambda b, *_: (b, 0, 0)


def _proj_in(x, sc, sh, nw, w_p):
    B, T, _ = x.shape
    tm = min(T, 512)
    return pl.pallas_call(
        _proj_in_kernel,
        grid=(B, T // tm),
        in_specs=[
            pl.BlockSpec((None, tm, D_MODEL), lambda b, i: (b, i, 0)),
            pl.BlockSpec((None, 1, D_MODEL), _mod_map(sc.shape[0])),
            pl.BlockSpec((None, 1, D_MODEL), _mod_map(sh.shape[0])),
            pl.BlockSpec((1, D_MODEL), lambda b, i: (0, 0)),
            pl.BlockSpec((D_MODEL, Z_COLS), lambda b, i: (0, 0)),
        ],
        out_specs=pl.BlockSpec((None, tm, Z_COLS), lambda b, i: (b, i, 0)),
        out_shape=jax.ShapeDtypeStruct((B, T, Z_COLS), F32),
        compiler_params=_cparams("parallel", "parallel"),
        name="proj_in",
    )(x, sc, sh, nw, w_p)


def _ret_kernel(q_ref, k_ref, v_ref, s0_ref, o_ref, fin_ref, s_ref, *, lc, nc):
    d = pl.program_id(0)
    c = pl.program_id(2)

    @pl.when(c == 0)
    def _():
        s_ref[...] = s0_ref[...]

    rows = lax.broadcasted_iota(jnp.int32, (lc, lc), 0)
    cols = lax.broadcasted_iota(jnp.int32, (lc, lc), 1)
    rel = jnp.where(d == 0, rows - cols, cols - rows).astype(F32)
    idx = lax.broadcasted_iota(jnp.int32, (lc, 1), 0)
    pos = jnp.where(d == 0, idx, lc - 1 - idx).astype(F32)
    for h in range(N_HEADS):
        lg = jnp.where(d == 0, LOG_GAMMA[0][h], LOG_GAMMA[1][h]).astype(F32)
        sl = slice(h * HEAD_D, (h + 1) * HEAD_D)
        q = q_ref[:, sl]
        k = k_ref[:, sl] * (HEAD_D ** -0.5)
        v = v_ref[:, sl].astype(BF16)
        att = _dot_nt(q.astype(BF16), k.astype(BF16))
        att = att * jnp.where(rel >= 0.0, jnp.exp(jnp.maximum(rel, 0.0) * lg), 0.0)
        qd = q * jnp.exp((pos + 1.0) * lg)
        kd = k * jnp.exp((lc - 1.0 - pos) * lg)
        s = s_ref[h]
        o_ref[:, sl] = _dot(att.astype(BF16), v) + _dot(qd.astype(BF16), s.astype(BF16))
        s_ref[h] = s * jnp.exp(lc * lg) + _dot_tn(kd.astype(BF16), v)

    @pl.when(c == nc - 1)
    def _():
        fin_ref[...] = s_ref[...]


def _retention(z, s0):
    B, T, _ = z.shape
    lc = min(T, 256)
    nc = T // lc

    def tmap(d, c):
        return jnp.where(d == 0, c, nc - 1 - c)

    def zspec(blk):
        return pl.BlockSpec((None, lc, MIX_W), lambda d, b, c: (b, tmap(d, c), blk))

    return pl.pallas_call(
        functools.partial(_ret_kernel, lc=lc, nc=nc),
        grid=(2, B, nc),
        in_specs=[
            zspec(ZB_RET), zspec(ZB_RET + 1), zspec(ZB_RET + 2),
            pl.BlockSpec((None, None, N_HEADS, HEAD_D, HEAD_D), lambda d, b, c: (b, d, 0, 0, 0)),
        ],
        out_specs=[
            pl.BlockSpec((None, None, lc, MIX_W), lambda d, b, c: (d, b, tmap(d, c), 0)),
            pl.BlockSpec((None, None, N_HEADS, HEAD_D, HEAD_D), lambda d, b, c: (b, d, 0, 0, 0)),
        ],
        out_shape=[
            jax.ShapeDtypeStruct((2, B, T, MIX_W), F32),
            jax.ShapeDtypeStruct((B, 2, N_HEADS, HEAD_D, HEAD_D), F32),
        ],
        scratch_shapes=[pltpu.VMEM((N_HEADS, HEAD_D, HEAD_D), F32)],
        compiler_params=_cparams("arbitrary", "arbitrary", "arbitrary"),
        name="retention",
    )(z, z, z, s0)


def _s5_zoh_kernel(lre_ref, lim_ref, ldt_ref, bre_ref, bim_ref, are_ref, aim_ref, bbre_ref, bbim_ref):
    lam_re = jnp.minimum(lre_ref[...], -1e-4)
    lam_im = lim_ref[...]
    dt = jnp.exp(ldt_ref[...])
    mag = jnp.exp(dt * lam_re)
    ang = dt * lam_im
    a_re = mag * jnp.cos(ang)
    a_im = mag * jnp.sin(ang)
    den = lam_re * lam_re + lam_im * lam_im
    f_re = ((a_re - 1.0) * lam_re + a_im * lam_im) / den
    f_im = (a_im * lam_re - (a_re - 1.0) * lam_im) / den
    b_re = bre_ref[...]
    b_im = bim_ref[...]
    are_ref[...] = a_re
    aim_ref[...] = a_im
    bbre_ref[...] = f_re * b_re - f_im * b_im
    bbim_ref[...] = f_re * b_im + f_im * b_re


def _s5_zoh(lam_re, lam_im, log_dt, b_re, b_im):
    n = 2 * S5_STATE
    col = lambda t: t.reshape(n, 1)
    ldt = jnp.broadcast_to(log_dt[:, :, None], (2, S5_GROUPS, S5_P))
    outs = pl.pallas_call(
        _s5_zoh_kernel,
        out_shape=[jax.ShapeDtypeStruct((n, 1), F32), jax.ShapeDtypeStruct((n, 1), F32),
                   jax.ShapeDtypeStruct((n, S5_CH), F32), jax.ShapeDtypeStruct((n, S5_CH), F32)],
        name="s5_zoh",
    )(col(lam_re), col(lam_im), col(ldt), b_re.reshape(n, S5_CH), b_im.reshape(n, S5_CH))
    a_re, a_im, bb_re, bb_im = outs
    shp = (2, S5_GROUPS, S5_P, S5_CH)
    return a_re.reshape(2, 1, S5_STATE), a_im.reshape(2, 1, S5_STATE), bb_re.reshape(shp), bb_im.reshape(shp)


S5_RADIX = 16


def _s5_kernel(u_ref, are_ref, aim_ref, bre_ref, bim_ref, cre_ref, cim_ref, x0r_ref, x0i_ref,
               y_ref, fr_ref, fi_ref, x_re, x_im, o_re, o_im, car_re, car_im, *, L, nc):
    d = pl.program_id(0)
    c = pl.program_id(2)
    R = S5_RADIX
    G = L // R

    def cmul(pr, pi, qr, qi):
        return pr * qr - pi * qi, pr * qi + pi * qr

    def body(rev):
        @pl.when(c == 0)
        def _():
            car_re[...] = x0r_ref[...]
            car_im[...] = x0i_ref[...]

        rr = lax.broadcasted_iota(jnp.int32, (L, L), 0)
        cc = lax.broadcasted_iota(jnp.int32, (L, L), 1)
        lg, lr = G.bit_length() - 1, R.bit_length() - 1
        perm = jnp.where(cc == ((rr & (G - 1)) << lr) + (rr >> lg), 1.0, 0.0).astype(BF16)
        unperm = jnp.where(cc == ((rr & (R - 1)) << lg) + (rr >> lr), 1.0, 0.0).astype(BF16)
        u = _dot(perm, u_ref[...].astype(BF16)).astype(BF16)
        x_re[...] = _dot(u, bre_ref[...])
        x_im[...] = _dot(u, bim_ref[...])
        ar = are_ref[...]
        ai = aim_ref[...]
        order = list(range(R - 1, -1, -1)) if rev else list(range(R))
        slab = lambda j: slice(j * G, (j + 1) * G)

        er = x_re[slab(order[0]), :]
        ei = x_im[slab(order[0]), :]
        for j in order[1:]:
            tr, ti = cmul(ar, ai, er, ei)
            er = tr + x_re[slab(j), :]
            ei = ti + x_im[slab(j), :]

        a_r, a_i = ar, ai
        for _ in range(R.bit_length() - 1):
            a_r, a_i = cmul(a_r, a_i, a_r, a_i)
        zr = car_re[...]
        zi = car_im[...]
        cin_r = [None] * G
        cin_i = [None] * G
        for k in (range(G - 1, -1, -1) if rev else range(G)):
            cin_r[k] = zr
            cin_i[k] = zi
            tr, ti = cmul(a_r, a_i, zr, zi)
            zr = tr + er[k:k + 1, :]
            zi = ti + ei[k:k + 1, :]
        car_re[...] = zr
        car_im[...] = zi

        xr = jnp.concatenate(cin_r, axis=0)
        xi = jnp.concatenate(cin_i, axis=0)
        for j in order:
            tr, ti = cmul(ar, ai, xr, xi)
            xr = tr + x_re[slab(j), :]
            xi = ti + x_im[slab(j), :]
            o_re[slab(j), :] = xr
            o_im[slab(j), :] = xi
        xr_all = _dot(unperm, o_re[...].astype(BF16)).astype(BF16)
        xi_all = _dot(unperm, o_im[...].astype(BF16)).astype(BF16)
        y_ref[...] = _dot(xr_all, cre_ref[...]) - _dot(xi_all, cim_ref[...])

    @pl.when(d == 0)
    def _():
        body(False)

    @pl.when(d == 1)
    def _():
        body(True)

    @pl.when(c == nc - 1)
    def _():
        fr_ref[...] = car_re[...]
        fi_ref[...] = car_im[...]


def _s5(z, a_re, a_im, bblk_re, bblk_im, cblk_re, cblk_im, x0_re, x0_im):
    B, T, _ = z.shape
    L = min(T, 128)
    nc = T // L

    def tmap(d, c):
        return jnp.where(d == 0, c, nc - 1 - c)

    dspec = lambda shape: pl.BlockSpec((None,) + shape, lambda d, b, c: (d, 0, 0))
    sspec = pl.BlockSpec((None, None, 1, S5_STATE), lambda d, b, c: (b, d, 0, 0))
    assert L % S5_RADIX == 0 and (L // S5_RADIX) & (L // S5_RADIX - 1) == 0
    buf = pltpu.VMEM((L, S5_STATE), F32)
    return pl.pallas_call(
        functools.partial(_s5_kernel, L=L, nc=nc),
        grid=(2, B, nc),
        in_specs=[
            pl.BlockSpec((None, L, MIX_W), lambda d, b, c: (b, tmap(d, c), ZB_S5)),
            dspec((1, S5_STATE)), dspec((1, S5_STATE)),
            dspec((MIX_W, S5_STATE)), dspec((MIX_W, S5_STATE)),
            pl.BlockSpec((S5_STATE, MIX_W), lambda d, b, c: (0, 0)),
            pl.BlockSpec((S5_STATE, MIX_W), lambda d, b, c: (0, 0)),
            sspec, sspec,
        ],
        out_specs=[
            pl.BlockSpec((None, None, L, MIX_W), lambda d, b, c: (d, b, tmap(d, c), 0)),
            sspec, sspec,
        ],
        out_shape=[
            jax.ShapeDtypeStruct((2, B, T, MIX_W), F32),
            jax.ShapeDtypeStruct((B, 2, 1, S5_STATE), F32),
            jax.ShapeDtypeStruct((B, 2, 1, S5_STATE), F32),
        ],
        scratch_shapes=[buf, buf, buf, buf, pltpu.VMEM((1, S5_STATE), F32), pltpu.VMEM((1, S5_STATE), F32)],
        compiler_params=_cparams("arbitrary", "arbitrary", "arbitrary"),
        name="s5_scan",
    )(z, a_re, a_im, bblk_re, bblk_im, cblk_re, cblk_im, x0_re, x0_im)


GLA_CHUNK = 16
GLA_CHUNK_SHIFT = 4
SUB = 8


def _gla_steps(q_ref, zf_ref, v_ref, lb_ref, s0_ref, o_ref, fin_ref, s_ref, a_ref, qs_ref, key_ref,
               *, rev, tb, nt, unrolled):
    jb = pl.program_id(1)
    ch = GLA_CHUNK
    half = 2 * HEAD_D

    @pl.when(jb == 0)
    def _():
        s_ref[...] = s0_ref[...]

    lb = lb_ref[...]
    zf = zf_ref[...]
    q = q_ref[...]
    qs_ref[...] = q * jax.nn.sigmoid(q)
    key_ref[...] = (1.0 - lb) * jax.nn.sigmoid(-zf)
    l1 = jnp.log(lb)
    l2 = jnp.log1p(-lb) + jnp.minimum(zf, 0.0) - jnp.log1p(jnp.exp(-jnp.abs(zf)))
    lf = jnp.maximum(l1, l2) + jnp.log1p(jnp.exp(-jnp.abs(l1 - l2)))
    r = lax.broadcasted_iota(jnp.int32, (tb, tb), 0)
    c = lax.broadcasted_iota(jnp.int32, (tb, tb), 1)
    same = (r >> GLA_CHUNK_SHIFT) == (c >> GLA_CHUNK_SHIFT)
    tri = jnp.where(same & ((c >= r) if rev else (c <= r)), 1.0, 0.0).astype(BF16)
    p1 = lf.astype(BF16)
    r1 = lf - p1.astype(F32)
    p2 = r1.astype(BF16)
    p3 = (r1 - p2.astype(F32)).astype(BF16)
    a_ref[...] = _dot(tri, p1) + _dot(tri, p2) + _dot(tri, p3)

    rows = lax.broadcasted_iota(jnp.int32, (SUB, half), 0)
    lo = lax.broadcasted_iota(jnp.int32, (SUB, half), 1) < HEAD_D
    same_head = ((lax.broadcasted_iota(jnp.int32, (half, half), 0) >> HEAD_SHIFT)
                 == (lax.broadcasted_iota(jnp.int32, (half, half), 1) >> HEAD_SHIFT))

    def pair_sums(p):
        s0 = jnp.sum(jnp.where(lo, p, 0.0), axis=1, keepdims=True)
        s1 = jnp.sum(jnp.where(lo, 0.0, p), axis=1, keepdims=True)
        return jnp.where(lo, s0, s1)

    def chunk(ci, carry):
        t0 = (tb // ch - 1 - ci if rev else ci) * ch
        win = pl.ds(t0 if isinstance(ci, int) else pl.multiple_of(t0, ch), ch)
        a = a_ref[win, :]
        qc = qs_ref[win, :]
        kc = key_ref[win, :]
        vc = v_ref[win, :]
        last = 0 if rev else ch - 1
        a_last = a[last:last + 1, :]
        qe = (qc * jnp.exp(a)).astype(BF16)
        ke = (kc * jnp.exp(a_last - a)).astype(BF16)
        ea = jnp.exp(a_last)
        vb = vc.astype(BF16)
        nslab = ch // SUB
        acc = [[jnp.zeros((SUB, half), F32) for _ in range(2)] for _ in range(nslab)]
        for jj in range(ch):
            for sb in range(nslab):
                r0 = sb * SUB
                if (r0 > jj) if rev else (r0 + SUB - 1 < jj):
                    continue
                whole = (r0 + SUB - 1 <= jj) if rev else (r0 >= jj)
                rs = slice(r0, r0 + SUB)
                valid = (rows + r0 <= jj) if rev else (rows + r0 >= jj)
                for hp in range(2):
                    sl = slice(hp * half, (hp + 1) * half)
                    dec = jnp.exp(jnp.minimum(a[rs, sl] - a[jj:jj + 1, sl], 0.0))
                    p = qc[rs, sl] * dec * kc[jj:jj + 1, sl]
                    if not whole:
                        p = jnp.where(valid, p, 0.0)
                    acc[sb][hp] = acc[sb][hp] + pair_sums(p) * vc[jj:jj + 1, sl]
        for hp in range(2):
            sl = slice(hp * half, (hp + 1) * half)
            s = s_ref[hp]
            inter = _dot_nt(qe[:, sl], s.astype(BF16))
            intra = jnp.concatenate([acc[sb][hp] for sb in range(nslab)], axis=0)
            o_ref[win, sl] = inter + intra
            s_ref[hp] = s * ea[:, sl] + jnp.where(same_head, _dot_tn(vb[:, sl], ke[:, sl]), 0.0)
        return carry

    yield
    if unrolled:
        for ci in range(tb // ch):
            chunk(ci, 0)
            yield
    else:
        lax.fori_loop(0, tb // ch, chunk, 0, unroll=2)

    @pl.when(jb == nt - 1)
    def _():
        fin_ref[...] = s_ref[...]


def _gla_kernel(*refs, rev, tb, nt):
    for _ in _gla_steps(*refs, rev=rev, tb=tb, nt=nt, unrolled=False):
        pass


def _gla(z, lb, s0, rev):
    B, T, _ = z.shape
    pair = 2 * HEAD_D
    s0 = s0.reshape(B, 2, 2, HEAD_D, HEAD_D)
    zero = jnp.zeros_like(s0[:, :, 0])
    s0 = jnp.concatenate([jnp.concatenate([s0[:, :, 0], zero], axis=-1),
                          jnp.concatenate([zero, s0[:, :, 1]], axis=-1)], axis=-2)
    tb = min(T, 256)
    nt = T // tb
    di = 1 if rev else 0
    tmap = (lambda j: nt - 1 - j) if rev else (lambda j: j)
    zspec = lambda blk: pl.BlockSpec((None, tb, MIX_W), lambda b, j: (b, tmap(j), blk))
    sspec = pl.BlockSpec((None, 2, pair, pair), lambda b, j: (b, 0, 0, 0))
    blk = pltpu.VMEM((tb, MIX_W), F32)
    o, fin = pl.pallas_call(
        functools.partial(_gla_kernel, rev=rev, tb=tb, nt=nt),
        grid=(B, nt),
        in_specs=[zspec(ZB_HG), zspec(ZB_HG + 1 + di), zspec(ZB_HG + 3),
                  pl.BlockSpec((1, MIX_W), lambda b, j: (0, 0)), sspec],
        out_specs=[pl.BlockSpec((None, tb, MIX_W), lambda b, j: (b, tmap(j), 0)), sspec],
        out_shape=[jax.ShapeDtypeStruct((B, T, MIX_W), F32),
                   jax.ShapeDtypeStruct((B, 2, pair, pair), F32)],
        scratch_shapes=[pltpu.VMEM((2, pair, pair), F32), blk, blk, blk],
        compiler_params=_cparams("parallel", "arbitrary"),
        name="hgrn_gla_bwd" if rev else "hgrn_gla_fwd",
    )(z, z, z, lb[di:di + 1], s0)
    fin = jnp.stack([fin[:, :, :HEAD_D, :HEAD_D], fin[:, :, HEAD_D:, HEAD_D:]], axis=2)
    return o, fin.reshape(B, N_HEADS, HEAD_D, HEAD_D)


def _rw_prep_kernel(*refs, grid_shift, tm):
    if grid_shift:
        (zc_ref, zu_ref, zd_ref, mu_ref, vec_ref, w0_ref, lora_ref,
         r_ref, k2_ref, v_ref, w_ref, nkk_ref, kka_ref, g_ref, bonus_ref, buf_ref) = refs
    else:
        (zc_ref, mu_ref, vec_ref, w0_ref, lora_ref,
         r_ref, k2_ref, v_ref, w_ref, nkk_ref, kka_ref, g_ref, bonus_ref, buf_ref) = refs
    i = pl.program_id(1)
    nt = pl.num_programs(1)
    halo = GRID_W
    width = 4 * MIX_W
    z = zc_ref[...]
    buf_ref[halo:halo + tm, :] = z
    lane = lax.broadcasted_iota(jnp.int32, (tm, width), 1)
    row = lax.broadcasted_iota(jnp.int32, (tm, width), 0)
    if grid_shift:
        buf_ref[0:halo, :] = jnp.where(i > 0, zu_ref[...], 0.0)
        buf_ref[halo + tm:2 * halo + tm, :] = jnp.where(i < nt - 1, zd_ref[...], 0.0)
        col = row & (GRID_W - 1)
        left = jnp.where(col > 0, buf_ref[halo - 1:halo - 1 + tm, :], 0.0)
        right = jnp.where(col < GRID_W - 1, buf_ref[halo + 1:halo + 1 + tm, :], 0.0)
        up = buf_ref[0:tm, :]
        down = buf_ref[2 * halo:2 * halo + tm, :]
        sel = lane & 3
        shifted = jnp.where(sel == 0, left, jnp.where(sel == 1, right, jnp.where(sel == 2, up, down)))
    else:
        zrow = jnp.zeros((1, width), F32)
        buf_ref[halo - 1:halo, :] = zrow
        buf_ref[halo + tm:halo + tm + 1, :] = zrow
        prev = buf_ref[halo - 1:halo - 1 + tm, :]
        nxt = buf_ref[halo + 1:halo + 1 + tm, :]
        shifted = jnp.where((lane & 1) == 0, prev, nxt)
    zs = z + mu_ref[...] * (shifted - z)
    r = zs[:, 0:MIX_W]
    k = zs[:, MIX_W:2 * MIX_W]
    v = zs[:, 2 * MIX_W:3 * MIX_W]
    sm = zs[:, 3 * MIX_W:4 * MIX_W]
    ones_bd = _head_ones()
    a0, k_k, k_a, r_k = (vec_ref[j:j + 1, :] for j in range(4))
    a = jax.nn.sigmoid(a0 + _dot(sm.astype(BF16), lora_ref[2]))
    g_ref[...] = _dot(jax.nn.sigmoid(sm).astype(BF16), lora_ref[3])
    kk = k * k_k
    kk = kk * lax.rsqrt(_head_sum(kk * kk, ones_bd) + 1e-12)
    k2 = k * (1.0 + (a - 1.0) * k_a)
    th = jnp.tanh(sm).astype(BF16)
    for di in range(2):
        w_ref[di] = -math.exp(-0.5) * jax.nn.sigmoid(w0_ref[di:di + 1, :] + _dot(th, lora_ref[di]))
    r_ref[...] = r
    k2_ref[...] = k2
    v_ref[...] = v
    nkk_ref[...] = -kk
    kka_ref[...] = kk * a
    bonus_ref[...] = _head_sum(r * k2 * r_k, ones_bd) * v


def _rw_prep(z, mu_p, vecs, w0, lora, grid_shift):
    B, T, _ = z.shape
    width = 4 * MIX_W
    wblk = ZB_RW // 4
    if grid_shift:
        tm = min(T, 512)
        hb = tm // GRID_W
        nh = T // GRID_W
        z_specs = [
            pl.BlockSpec((None, tm, width), lambda b, i: (b, i, wblk)),
            pl.BlockSpec((None, GRID_W, width), lambda b, i: (b, jnp.maximum(i * hb - 1, 0), wblk)),
            pl.BlockSpec((None, GRID_W, width), lambda b, i: (b, jnp.minimum((i + 1) * hb, nh - 1), wblk)),
        ]
        z_args = (z, z, z)
    else:
        tm = T
        z_specs = [pl.BlockSpec((None, tm, width), lambda b, i: (b, i, wblk))]
        z_args = (z,)
    const = lambda shape: pl.BlockSpec(shape, lambda b, i: (0,) * len(shape))
    ospec = pl.BlockSpec((None, tm, MIX_W), lambda b, i: (b, i, 0))
    oshape = jax.ShapeDtypeStruct((B, T, MIX_W), F32)
    dspec = pl.BlockSpec((2, None, tm, MIX_W), lambda b, i: (0, b, i, 0))
    dshape = jax.ShapeDtypeStruct((2, B, T, MIX_W), F32)
    return pl.pallas_call(
        functools.partial(_rw_prep_kernel, grid_shift=grid_shift, tm=tm),
        grid=(B, T // tm),
        in_specs=z_specs + [const((1, width)), const((4, MIX_W)), const((2, MIX_W)),
                            const((4, MIX_W, MIX_W))],
        out_specs=[ospec, ospec, ospec, dspec, ospec, ospec, ospec, ospec],
        out_shape=[oshape, oshape, oshape, dshape, oshape, oshape, oshape, oshape],
        scratch_shapes=[pltpu.VMEM((tm + 2 * GRID_W, width), F32)],
        compiler_params=_cparams("parallel", "parallel"),
        name="rwkv_prep",
    )(*z_args, mu_p, vecs, w0, lora)


RW_CHUNK = 64
RW_CHUNK_SHIFT = 6


def _mm(a, b):
    return jnp.dot(a.astype(BF16), b.astype(BF16), preferred_element_type=F32)


def _rwkv_kernel(*refs, rev, tb, nt):
    for _ in _rwkv_steps(*refs, rev=rev, tb=tb, nt=nt):
        pass


def _rwkv_steps(lw_ref, a_ref, b_ref, k_ref, r_ref, v_ref, s0_ref, y_ref, fin_ref, s_ref, g_ref,
                *, rev, tb, nt):
    jb = pl.program_id(1)
    ch = RW_CHUNK
    pair = 2 * HEAD_D

    @pl.when(jb == 0)
    def _():
        s_ref[...] = s0_ref[...]

    lw = lw_ref[...]
    rr = lax.broadcasted_iota(jnp.int32, (tb, tb), 0)
    cc = lax.broadcasted_iota(jnp.int32, (tb, tb), 1)
    same = (rr >> RW_CHUNK_SHIFT) == (cc >> RW_CHUNK_SHIFT)
    tri = jnp.where(same & ((cc >= rr) if rev else (cc <= rr)), 1.0, 0.0).astype(BF16)
    p1 = lw.astype(BF16)
    r1 = lw - p1.astype(F32)
    p2 = r1.astype(BF16)
    p3 = (r1 - p2.astype(F32)).astype(BF16)
    g_ref[...] = _dot(tri, p1) + _dot(tri, p2) + _dot(tri, p3)

    si = lax.broadcasted_iota(jnp.int32, (ch, ch), 0)
    ri = lax.broadcasted_iota(jnp.int32, (ch, ch), 1)
    if rev:
        si, ri = ch - 1 - si, ch - 1 - ri
    strict = ri < si
    incl = ri <= si
    eye = jnp.where(ri == si, 1.0, 0.0).astype(F32)
    levels = [((si >> (lv + 1)) == (ri >> (lv + 1))) & (((si >> lv) & 1) == 1) & (((ri >> lv) & 1) == 0)
              for lv in range(RW_CHUNK_SHIFT)]
    lo = lax.broadcasted_iota(jnp.int32, (ch, pair), 1) < HEAD_D
    same_head = ((lax.broadcasted_iota(jnp.int32, (pair, pair), 0) >> HEAD_SHIFT)
                 == (lax.broadcasted_iota(jnp.int32, (pair, pair), 1) >> HEAD_SHIFT))

    nch = tb // ch
    g = g_ref[...]
    e_g = jnp.exp(g)
    e_ng = jnp.exp(-g)
    at_f = a_ref[...] * jnp.exp(g - lw)
    rt_f = r_ref[...] * e_g
    at = at_f.astype(BF16)
    rt = rt_f.astype(BF16)
    first = (lax.broadcasted_iota(jnp.int32, (tb, MIX_W), 1) & HEAD_D) == 0
    at_h = [jnp.where(first, at_f, 0.0).astype(BF16), jnp.where(first, 0.0, at_f).astype(BF16)]
    rt_h = [jnp.where(first, rt_f, 0.0).astype(BF16), jnp.where(first, 0.0, rt_f).astype(BF16)]
    bt = (b_ref[...] * e_ng).astype(BF16)
    kt = (k_ref[...] * e_ng).astype(BF16)
    vb = v_ref[...].astype(BF16)

    items = [(c, hp, hh) for c in range(nch) for hp in range(2) for hh in range(2)]
    rows = lambda c: slice(c * ch, (c + 1) * ch)
    lanes = lambda hp: slice(hp * pair, (hp + 1) * pair)
    n_m, p_m, m_m, q_m = {}, {}, {}, {}
    for it in items:
        c, hp, hh = it
        ar = jnp.concatenate([at_h[hh][rows(c), lanes(hp)], rt_h[hh][rows(c), lanes(hp)]], axis=0)
        np_ = _dot_nt(ar, bt[rows(c), lanes(hp)])
        mq = _dot_nt(ar, kt[rows(c), lanes(hp)])
        n_m[it] = jnp.where(strict, np_[:ch], 0.0).astype(BF16)
        p_m[it] = jnp.where(incl, np_[ch:], 0.0).astype(BF16)
        m_m[it] = jnp.where(strict, mq[:ch], 0.0).astype(BF16)
        q_m[it] = jnp.where(incl, mq[ch:], 0.0).astype(BF16)
        if hp == 1 and hh == 1:
            yield
    t_m = {it: eye + jnp.where(levels[0], n_m[it].astype(F32), 0.0) for it in items}
    for lv in range(1, RW_CHUNK_SHIFT):
        tn = {it: _mm(t_m[it], n_m[it]) for it in items}
        yield
        t_m = {it: t_m[it] + jnp.where(levels[lv], _mm(tn[it], t_m[it]), 0.0) for it in items}
        yield
    mv = {it: _mm(m_m[it], vb[rows(it[0]), lanes(it[1])]) for it in items}
    qv = {it: _mm(q_m[it], vb[rows(it[0]), lanes(it[1])]) for it in items}
    yield
    y2 = {it: _mm(t_m[it], jnp.concatenate([at[rows(it[0]), lanes(it[1])], mv[it].astype(BF16)], axis=1))
          for it in items}
    yield

    last = 0 if rev else ch - 1
    for ci in range(nch):
        c = nch - 1 - ci if rev else ci
        g_c = g[c * ch + last:c * ch + last + 1, :]
        e_gc = jnp.exp(g_c - g[rows(c), :])
        bh = (b_ref[rows(c), :] * e_gc).astype(BF16)
        kh = (k_ref[rows(c), :] * e_gc).astype(BF16)
        dec_c = jnp.exp(g_c)
        xs, sas, ss = [], [], []
        for hp in range(2):
            s = s_ref[hp]
            w2 = jnp.where(lo, y2[(c, hp, 0)][:, :pair], y2[(c, hp, 1)][:, :pair]).astype(BF16)
            xs.append(_dot_nt(jnp.concatenate([w2, rt[rows(c), lanes(hp)]], axis=0), s.astype(BF16)))
            ss.append(s)
        for hp in range(2):
            w1 = jnp.where(lo, y2[(c, hp, 0)][:, pair:], y2[(c, hp, 1)][:, pair:])
            sas.append(w1 + xs[hp][:ch])
        for hp in range(2):
            sa = sas[hp]
            y_ref[rows(c), lanes(hp)] = xs[hp][ch:] + jnp.where(
                lo, _mm(p_m[(c, hp, 0)], sa) + qv[(c, hp, 0)], _mm(p_m[(c, hp, 1)], sa) + qv[(c, hp, 1)])
            upd = _dot_tn(jnp.concatenate([sa.astype(BF16), vb[rows(c), lanes(hp)]], axis=0),
                          jnp.concatenate([bh[:, lanes(hp)], kh[:, lanes(hp)]], axis=0))
            s_ref[hp] = ss[hp] * dec_c[:, lanes(hp)] + jnp.where(same_head, upd, 0.0)
        yield

    @pl.when(jb == nt - 1)
    def _():
        fin_ref[...] = s_ref[...]


def _hg_rw_kernel(*refs, rev, tb, nt):
    gla_in, rw_in = refs[0:5], refs[5:12]
    gla_out, rw_out = refs[12:14], refs[14:16]
    gla_scr, rw_scr = refs[16:20], refs[20:22]
    gla = _gla_steps(*gla_in, *gla_out, *gla_scr, rev=rev, tb=tb, nt=nt, unrolled=True)
    rwkv = _rwkv_steps(*rw_in, *rw_out, *rw_scr, rev=rev, tb=tb, nt=nt)
    live = [gla, rwkv]
    while live:
        for gen in list(live):
            if next(gen, StopIteration) is StopIteration:
                live.remove(gen)


def _head_pair_blockdiag(s):
    B = s.shape[0]
    s = s.reshape(B, 2, 2, HEAD_D, HEAD_D)
    zero = jnp.zeros_like(s[:, :, 0])
    return jnp.concatenate([jnp.concatenate([s[:, :, 0], zero], axis=-1),
                            jnp.concatenate([zero, s[:, :, 1]], axis=-1)], axis=-2)


def _head_pair_blocks(t):
    B = t.shape[0]
    t = jnp.stack([t[:, :, :HEAD_D, :HEAD_D], t[:, :, HEAD_D:, HEAD_D:]], axis=2)
    return t.reshape(B, N_HEADS, HEAD_D, HEAD_D)


def _rwkv(lw, a, b, k, r, v, s0, rev):
    B, T, _ = v.shape
    pair = 2 * HEAD_D
    tb = min(T, 256)
    nt = T // tb
    di = 1 if rev else 0
    tmap = (lambda j: nt - 1 - j) if rev else (lambda j: j)
    spec = pl.BlockSpec((None, tb, MIX_W), lambda bi, j: (bi, tmap(j), 0))
    sspec = pl.BlockSpec((None, 2, pair, pair), lambda bi, j: (bi, 0, 0, 0))
    y, fin = pl.pallas_call(
        functools.partial(_rwkv_kernel, rev=rev, tb=tb, nt=nt),
        grid=(B, nt),
        in_specs=[pl.BlockSpec((None, None, tb, MIX_W), lambda bi, j: (di, bi, tmap(j), 0)),
                  spec, spec, spec, spec, spec, sspec],
        out_specs=[spec, sspec],
        out_shape=[jax.ShapeDtypeStruct((B, T, MIX_W), F32),
                   jax.ShapeDtypeStruct((B, 2, pair, pair), F32)],
        scratch_shapes=[pltpu.VMEM((2, pair, pair), F32), pltpu.VMEM((tb, MIX_W), F32)],
        compiler_params=_cparams("parallel", "arbitrary"),
        name="rwkv_chunk_bwd" if rev else "rwkv_chunk_fwd",
    )(lw, a, b, k, r, v, _head_pair_blockdiag(s0))
    return y, _head_pair_blocks(fin)


def _hg_rw(z, lb, s0_hg, lw, a, b, k, r, v, s0_rw, rev):
    B, T, _ = v.shape
    pair = 2 * HEAD_D
    tb = min(T, 256)
    nt = T // tb
    di = 1 if rev else 0
    tmap = (lambda j: nt - 1 - j) if rev else (lambda j: j)
    zspec = lambda blk: pl.BlockSpec((None, tb, MIX_W), lambda bi, j: (bi, tmap(j), blk))
    spec = zspec(0)
    sspec = pl.BlockSpec((None, 2, pair, pair), lambda bi, j: (bi, 0, 0, 0))
    state = jax.ShapeDtypeStruct((B, 2, pair, pair), F32)
    seq = jax.ShapeDtypeStruct((B, T, MIX_W), F32)
    blk = pltpu.VMEM((tb, MIX_W), F32)
    st = pltpu.VMEM((2, pair, pair), F32)
    o, fin_hg, y, fin_rw = pl.pallas_call(
        functools.partial(_hg_rw_kernel, rev=rev, tb=tb, nt=nt),
        grid=(B, nt),
        in_specs=[zspec(ZB_HG), zspec(ZB_HG + 1 + di), zspec(ZB_HG + 3),
                  pl.BlockSpec((1, MIX_W), lambda bi, j: (0, 0)), sspec,
                  pl.BlockSpec((None, None, tb, MIX_W), lambda bi, j: (di, bi, tmap(j), 0)),
                  spec, spec, spec, spec, spec, sspec],
        out_specs=[spec, sspec, spec, sspec],
        out_shape=[seq, state, seq, state],
        scratch_shapes=[st, blk, blk, blk, st, blk],
        compiler_params=_cparams("parallel", "arbitrary"),
        name="hgrn_rwkv_bwd" if rev else "hgrn_rwkv_fwd",
    )(z, z, z, lb[di:di + 1], _head_pair_blockdiag(s0_hg),
      lw, a, b, k, r, v, _head_pair_blockdiag(s0_rw))
    return o, _head_pair_blocks(fin_hg), y, _head_pair_blocks(fin_rw)


def _mix_kernel(x_ref, sc_ref, sh_ref, g1_ref, nw0_ref, nw_ref, wg_ref,
                ret_ref, retg_ref, s5_ref, s5u_ref, hgf_ref, hgb_ref, hgg_ref, rwf_ref, rwr_ref,
                rwb_ref, rwg_ref,
                vec_ref, glu_w_ref, wbr_ref, wout_ref, o_ref):
    ones_bd = _head_ones()
    h = (_rms(x_ref[...], nw0_ref[...]) * (1.0 + sc_ref[...]) + sh_ref[...]).astype(BF16)
    gn_w, s5_d, glu_b, hg_w, ln_w, ln_b = (vec_ref[j:j + 1, :] for j in range(6))

    def group_norm(o, eps):
        mu = _head_sum(o, ones_bd) * (1.0 / HEAD_D)
        oc = o - mu
        var = _head_sum(oc * oc, ones_bd) * (1.0 / HEAD_D)
        return oc * lax.rsqrt(var + eps)

    g = retg_ref[...]
    y_ret = group_norm(ret_ref[0] + ret_ref[1], EPS) * gn_w * (g * jax.nn.sigmoid(g))

    y = s5_d * s5u_ref[...] + s5_ref[0] + s5_ref[1]
    yg = jax.nn.gelu(y)
    y_s5 = yg * jax.nn.sigmoid(_dot(yg.astype(BF16), glu_w_ref[...]) + glu_b)

    o = hgf_ref[...] + hgb_ref[...]
    g = hgg_ref[...]
    ms = _head_sum(o * o, ones_bd) * (1.0 / HEAD_D)
    y_hg = o * lax.rsqrt(ms + EPS) * hg_w * (g * jax.nn.sigmoid(g))

    y = group_norm(rwf_ref[...] + rwr_ref[...], RW_LN_EPS) * ln_w + ln_b
    y_rw = (y + rwb_ref[...]) * rwg_ref[...]

    mixed = None
    for m, ym in enumerate((y_ret, y_s5, y_hg, y_rw)):
        br = _dot(ym.astype(BF16), wbr_ref[m])
        term = jax.nn.sigmoid(_dot(h, wg_ref[:, m * D_MODEL:(m + 1) * D_MODEL])) * br
        mixed = term if mixed is None else mixed + term
    mixed = _dot(mixed.astype(BF16), wout_ref[...])
    o_ref[...] = x_ref[...] + g1_ref[...] * _rms(mixed, nw_ref[...])


def _mix(x, sc, sh, g1, nw0, nw, w_gate, z, ret_o, s5_y, hg_o, rw_y, rw_bonus, rw_g, vecs, glu_w,
         w_branch, w_out):
    B, T, _ = x.shape
    tm = min(T, 256)
    xspec = pl.BlockSpec((None, tm, D_MODEL), lambda b, i: (b, i, 0))
    zspec = lambda blk: pl.BlockSpec((None, tm, MIX_W), lambda b, i: (b, i, blk))
    dspec = pl.BlockSpec((2, None, tm, MIX_W), lambda b, i: (0, b, i, 0))
    const = lambda shape: pl.BlockSpec(shape, lambda b, i: (0,) * len(shape))
    mspec = lambda m: pl.BlockSpec((None, 1, D_MODEL), _mod_map(m.shape[0]))
    return pl.pallas_call(
        _mix_kernel,
        grid=(B, T // tm),
        in_specs=[
            xspec, mspec(sc), mspec(sh), mspec(g1),
            const((1, D_MODEL)), const((1, D_MODEL)),
            const((D_MODEL, 4 * D_MODEL)),
            dspec, zspec(ZB_RET + 3),
            dspec, zspec(ZB_S5),
            zspec(0), zspec(0), zspec(ZB_HG + 4),
            zspec(0), zspec(0), zspec(0), zspec(0),
            const((6, MIX_W)), const((MIX_W, MIX_W)),
            const((4, MIX_W, D_MODEL)), const((D_MODEL, D_MODEL)),
        ],
        out_specs=xspec,
        out_shape=jax.ShapeDtypeStruct((B, T, D_MODEL), F32),
        compiler_params=_cparams("parallel", "parallel"),
        name="mix_out",
    )(x, sc, sh, g1, nw0, nw, w_gate, ret_o, z, s5_y, z, hg_o[0], hg_o[1], z, rw_y[0], rw_y[1],
      rw_bonus, rw_g, vecs, glu_w, w_branch, w_out)


FFN_CHUNK = 1024


def _ffn_kernel(x_ref, sc_ref, sh_ref, g2_ref, nw2_ref, nw3_ref, w1_ref, w2_ref, o_ref):
    x = x_ref[...]
    h = (_rms(x, nw2_ref[...]) * (1.0 + sc_ref[...]) + sh_ref[...]).astype(BF16)
    acc = None
    for j in range(D_FF // FFN_CHUNK):
        cols = slice(j * FFN_CHUNK, (j + 1) * FFN_CHUNK)
        a = jnp.maximum(_dot(h, w1_ref[:, cols]), 0.0)
        part = _dot((a * a).astype(BF16), w2_ref[cols, :])
        acc = part if acc is None else acc + part
    o_ref[...] = x + g2_ref[...] * _rms(acc, nw3_ref[...])


def _ffn(x, sc, sh, g2, nw2, nw3, w1, w2):
    B, T, _ = x.shape
    tm = min(T, 512)
    xspec = pl.BlockSpec((None, tm, D_MODEL), lambda b, i: (b, i, 0))
    mspec = lambda m: pl.BlockSpec((None, 1, D_MODEL), _mod_map(m.shape[0]))
    const = pl.BlockSpec((1, D_MODEL), lambda b, i: (0, 0))
    return pl.pallas_call(
        _ffn_kernel,
        grid=(B, T // tm),
        in_specs=[xspec, mspec(sc), mspec(sh), mspec(g2), const, const,
                  pl.BlockSpec((D_MODEL, D_FF), lambda b, i: (0, 0)),
                  pl.BlockSpec((D_FF, D_MODEL), lambda b, i: (0, 0))],
        out_specs=xspec,
        out_shape=jax.ShapeDtypeStruct((B, T, D_MODEL), F32),
        compiler_params=_cparams("parallel", "parallel"),
        name="ffn",
    )(x, sc, sh, g2, nw2, nw3, w1, w2)


def _layer_params(l, p):
    w = p['w_in'][l]
    w_p = jnp.concatenate(
        [w[:, 2560:3488], jnp.zeros((D_MODEL, 4 * MIX_W - RW_COLS), F32),
         w[:, 0:1024], w[:, 1024:1280], w[:, 1280:2560]], axis=1).astype(BF16)
    w_gate = w[:, 3488:7584].astype(BF16)

    a_re, a_im, bb_re, bb_im = _s5_zoh(p['s5_lam_re'][l], p['s5_lam_im'][l], p['s5_log_dt'][l],
                                       p['s5_b_re'][l], p['s5_b_im'][l])
    eye = jnp.eye(S5_GROUPS, dtype=F32)
    bblk = lambda bb: jnp.einsum('dgph,gk->dghkp', bb, eye).reshape(2, MIX_W, S5_STATE).astype(BF16)
    cblk = lambda c: jnp.einsum('ghp,gk->gpkh', c, eye).reshape(S5_STATE, MIX_W).astype(BF16)

    def lora_pad(m, row0):
        return jnp.zeros((MIX_W, MIX_W), F32).at[row0:row0 + m.shape[0]].set(m)

    lora = jnp.stack([lora_pad(p['rw_w2'][l, 0], 0), lora_pad(p['rw_w2'][l, 1], 32),
                      lora_pad(p['rw_a2'][l], 64), lora_pad(p['rw_g2'][l], 96)]).astype(BF16)
    mu_p = jnp.concatenate([p['rw_mu'][l], jnp.zeros((4 * MIX_W - RW_COLS,), F32)]).reshape(1, 4 * MIX_W)
    return dict(
        w_p=w_p, w_gate=w_gate, nw=p['norm_w'][l],
        s5=(a_re, a_im, bblk(bb_re), bblk(bb_im), cblk(p['s5_c_re'][l]), cblk(p['s5_c_im'][l])),
        rw_mu=mu_p, rw_lora=lora, rw_w0=p['rw_w0'][l],
        rw_vecs=jnp.stack([p['rw_a0'][l], p['rw_k_k'][l], p['rw_k_a'][l], p['rw_r_k'][l]]),
        mix_vecs=jnp.stack([p['ret_gn_w'][l], p['s5_d'][l], p['s5_glu_b'][l], p['hg_norm_w'][l],
                            p['rw_ln_w'][l], p['rw_ln_b'][l]]),
        glu_w=p['s5_glu_w'][l].astype(BF16),
        w_branch=p['w_branch'][l].astype(BF16), w_out=p['w_out'][l].astype(BF16),
        ff_w1=p['ff_w1'][l].astype(BF16), ff_w2=p['ff_w2'][l].astype(BF16),
    )


def _trunk_layer(x, mod, init, grid_shift, lp, hg_lb):
    B, T, _ = x.shape
    sh1, sc1, g1, sh2, sc2, g2 = (m[:, None, :] for m in jnp.split(mod, 6, axis=-1))
    nw = lp['nw']
    if mod.shape[0] == 1:
        tok = lambda t: t.reshape(t.shape[:-3] + (1, B * T, t.shape[-1]))
    else:
        tok = lambda t: t
    z = _proj_in(tok(x), sc1, sh1, nw[0:1], lp['w_p']).reshape(B, T, Z_COLS)

    s_ret, s_s5r, s_s5i, s_hg, s_rw = init
    ret_o, f_ret = _retention(z, s_ret)

    s5_y, f_s5r, f_s5i = _s5(z, *lp['s5'], s_s5r.reshape(B, 2, 1, S5_STATE), s_s5i.reshape(B, 2, 1, S5_STATE))
    f_s5r = f_s5r.reshape(B, 2, S5_GROUPS, S5_P)
    f_s5i = f_s5i.reshape(B, 2, S5_GROUPS, S5_P)

    r, k2, v, w, nkk, kka, rw_g, bonus = _rw_prep(z, lp['rw_mu'], lp['rw_vecs'], lp['rw_w0'],
                                                  lp['rw_lora'], grid_shift)
    s_hg_t = jnp.swapaxes(s_hg, -1, -2)
    hg_f, f_hg_f, rw_f, f_rw_f = _hg_rw(z, hg_lb, s_hg_t[:, 0], w, nkk, kka, k2, r, v, s_rw[:, 0], False)
    hg_b, f_hg_b, rw_b, f_rw_b = _hg_rw(z, hg_lb, s_hg_t[:, 1], w, nkk, kka, k2, r, v, s_rw[:, 1], True)
    f_hg = jnp.swapaxes(jnp.stack([f_hg_f, f_hg_b], axis=1), -1, -2)
    f_rw = jnp.stack([f_rw_f, f_rw_b], axis=1)

    x = _mix(tok(x), sc1, sh1, g1, nw[0:1], nw[1:2], lp['w_gate'], tok(z), tok(ret_o), tok(s5_y),
             (tok(hg_f), tok(hg_b)), (tok(rw_f), tok(rw_b)), tok(bonus), tok(rw_g),
             lp['mix_vecs'], lp['glu_w'], lp['w_branch'], lp['w_out'])
    x = _ffn(x, sc2, sh2, g2, nw[2:3], nw[3:4], lp['ff_w1'], lp['ff_w2'])
    return x.reshape(B, T, D_MODEL), (f_ret, f_s5r, f_s5i, f_hg, f_rw)


def kernel(x_prompt, x_sample, state_ret, state_s5_re, state_s5_im, state_hgrn, state_rwkv, c, c_ctx, ada_w, ada_b, norm_w, w_in, ret_gn_w, s5_lam_re, s5_lam_im, s5_log_dt, s5_b_re, s5_b_im, s5_c_re, s5_c_im, s5_d, s5_glu_w, s5_glu_b, hg_lb, hg_norm_w, rw_mu, rw_w0, rw_w2, rw_a0, rw_a2, rw_g2, rw_k_k, rw_k_a, rw_r_k, rw_ln_w, rw_ln_b, w_branch, w_out, ff_w1, ff_w2):
    p = dict(norm_w=norm_w, w_in=w_in, ret_gn_w=ret_gn_w, s5_lam_re=s5_lam_re, s5_lam_im=s5_lam_im,
             s5_log_dt=s5_log_dt, s5_b_re=s5_b_re, s5_b_im=s5_b_im, s5_c_re=s5_c_re, s5_c_im=s5_c_im,
             s5_d=s5_d, s5_glu_w=s5_glu_w, s5_glu_b=s5_glu_b, hg_norm_w=hg_norm_w, rw_mu=rw_mu,
             rw_w0=rw_w0, rw_w2=rw_w2, rw_a0=rw_a0, rw_a2=rw_a2, rw_g2=rw_g2, rw_k_k=rw_k_k,
             rw_k_a=rw_k_a, rw_r_k=rw_r_k, rw_ln_w=rw_ln_w, rw_ln_b=rw_ln_b, w_branch=w_branch,
             w_out=w_out, ff_w1=ff_w1, ff_w2=ff_w2)
    depth = w_in.shape[0]
    n_ctx = x_prompt.shape[0]
    n_lat = x_sample.shape[0]
    assert 1 + n_lat <= 8

    lb_cum = jnp.cumsum(jax.nn.softmax(hg_lb.astype(F32), axis=0), axis=0)
    hg_lower = lb_cum - lb_cum[0]

    cond = jnp.concatenate([c_ctx[None, :], c, jnp.zeros((7 - n_lat, D_MODEL), F32)], axis=0)
    mod = _ada_mod(cond, ada_w, ada_b)

    zero_state = (jnp.zeros((n_ctx, 2, N_HEADS, HEAD_D, HEAD_D), F32),
                  jnp.zeros((n_ctx, 2, S5_GROUPS, S5_P), F32),
                  jnp.zeros((n_ctx, 2, S5_GROUPS, S5_P), F32),
                  jnp.zeros((n_ctx, 2, N_HEADS, HEAD_D, HEAD_D), F32),
                  jnp.zeros((n_ctx, 2, N_HEADS, HEAD_D, HEAD_D), F32))
    xp, xs = x_prompt, x_sample
    finals = []
    for l in range(depth):
        lp = _layer_params(l, p)
        xp, fin = _trunk_layer(xp, mod[l, 0:1], zero_state, False, lp, hg_lower[l])
        finals.append(fin)
        lat_init = (state_ret[:, l], state_s5_re[:, l], state_s5_im[:, l], state_hgrn[:, l], state_rwkv[:, l])
        xs, _ = _trunk_layer(xs, mod[l, 1:1 + n_lat], lat_init, True, lp, hg_lower[l])
    new_states = tuple(jnp.stack([f[i] for f in finals], axis=1) for i in range(5))
    return (xp, xs) + new_states
```

```python
import functools
import math

import jax
import jax.numpy as jnp
from jax import lax
from jax.experimental import pallas as pl
from jax.experimental.pallas import tpu as pltpu

F32 = jnp.float32
BF16 = jnp.bfloat16

D_MODEL = 1024
GRID_W = 64
MIX_W = 256
N_HEADS = 4
HEAD_D = 64
HEAD_SHIFT = 6
LANES = 128
S5_GROUPS = 16
S5_CH = 16
S5_P = 64
S5_STATE = S5_GROUPS * S5_P
D_FF = 4096
EPS = 1e-6
RW_LN_EPS = 64e-5
RW_COLS = 928

Z_COLS = 3584
ZB_RW = 0
ZB_RET = 4
ZB_S5 = 8
ZB_HG = 9

VMEM_LIMIT = 56 * 1024 * 1024

LOG_GAMMA = tuple(
    tuple(math.log1p(-2.0 ** (-(5.0 + 0.5 * di) - h)) for h in range(N_HEADS)) for di in range(2))


def _cparams(*sem):
    return pltpu.CompilerParams(dimension_semantics=sem, vmem_limit_bytes=VMEM_LIMIT)


def _dot(a, b):
    return jnp.dot(a, b, preferred_element_type=F32)


def _dot_nt(a, b):
    return lax.dot_general(a, b, (((1,), (1,)), ((), ())), preferred_element_type=F32)


def _dot_tn(a, b):
    return lax.dot_general(a, b, (((0,), (0,)), ((), ())), preferred_element_type=F32)


def _head_ones():
    r = lax.broadcasted_iota(jnp.int32, (MIX_W, MIX_W), 0) >> HEAD_SHIFT
    c = lax.broadcasted_iota(jnp.int32, (MIX_W, MIX_W), 1) >> HEAD_SHIFT
    return jnp.where(r == c, 1.0, 0.0).astype(F32)


def _head_sum(x, ones_bd):
    ones = ones_bd.astype(BF16)
    p1 = x.astype(BF16)
    r1 = x - p1.astype(F32)
    p2 = r1.astype(BF16)
    p3 = (r1 - p2.astype(F32)).astype(BF16)
    return _dot(p1, ones) + _dot(p2, ones) + _dot(p3, ones)


def _rms(x, w):
    return x * lax.rsqrt(jnp.mean(x * x, axis=-1, keepdims=True) + EPS) * w


def _ada_kernel(c_ref, w_ref, b_ref, o_ref):
    c = c_ref[...]
    s = c * jax.nn.sigmoid(c)
    o_ref[...] = _dot(s.astype(BF16), w_ref[...].astype(BF16)) + b_ref[...]


def _ada_mod(cond, ada_w, ada_b):
    L = ada_w.shape[0]
    n = ada_w.shape[2]
    tn = 1536
    return pl.pallas_call(
        _ada_kernel,
        grid=(L, n // tn),
        in_specs=[
            pl.BlockSpec((8, D_MODEL), lambda l, j: (0, 0)),
            pl.BlockSpec((None, D_MODEL, tn), lambda l, j: (l, 0, j)),
            pl.BlockSpec((None, 1, tn), lambda l, j: (l, 0, j)),
        ],
        out_specs=pl.BlockSpec((None, 8, tn), lambda l, j: (l, 0, j)),
        out_shape=jax.ShapeDtypeStruct((L, 8, n), F32),
        compiler_params=_cparams("parallel", "parallel"),
        name="ada_mod",
    )(cond, ada_w, ada_b.reshape(L, 1, n))


def _proj_in_kernel(x_ref, sc_ref, sh_ref, nw_ref, w_ref, z_ref):
    h = _rms(x_ref[...], nw_ref[...]) * (1.0 + sc_ref[...]) + sh_ref[...]
    z_ref[...] = _dot(h.astype(BF16), w_ref[...])


def _mod_map(bm):
    if bm == 1:
        return lambda b, *_: (0, 0, 0)
    return lambda b, *_: (b, 0, 0)


def _proj_in(x, sc, sh, nw, w_p):
    B, T, _ = x.shape
    tm = min(T, 512)
    return pl.pallas_call(
        _proj_in_kernel,
        grid=(B, T // tm),
        in_specs=[
            pl.BlockSpec((None, tm, D_MODEL), lambda b, i: (b, i, 0)),
            pl.BlockSpec((None, 1, D_MODEL), _mod_map(sc.shape[0])),
            pl.BlockSpec((None, 1, D_MODEL), _mod_map(sh.shape[0])),
            pl.BlockSpec((1, D_MODEL), lambda b, i: (0, 0)),
            pl.BlockSpec((D_MODEL, Z_COLS), lambda b, i: (0, 0)),
        ],
        out_specs=pl.BlockSpec((None, tm, Z_COLS), lambda b, i: (b, i, 0)),
        out_shape=jax.ShapeDtypeStruct((B, T, Z_COLS), F32),
        compiler_params=_cparams("parallel", "parallel"),
        name="proj_in",
    )(x, sc, sh, nw, w_p)


def _ret_kernel(q_ref, k_ref, v_ref, s0_ref, o_ref, fin_ref, s_ref, *, lc, nc):
    d = pl.program_id(0)
    c = pl.program_id(2)

    @pl.when(c == 0)
    def _():
        s_ref[...] = s0_ref[...]

    rows = lax.broadcasted_iota(jnp.int32, (lc, lc), 0)
    cols = lax.broadcasted_iota(jnp.int32, (lc, lc), 1)
    rel = jnp.where(d == 0, rows - cols, cols - rows).astype(F32)
    idx = lax.broadcasted_iota(jnp.int32, (lc, 1), 0)
    pos = jnp.where(d == 0, idx, lc - 1 - idx).astype(F32)
    for h in range(N_HEADS):
        lg = jnp.where(d == 0, LOG_GAMMA[0][h], LOG_GAMMA[1][h]).astype(F32)
        sl = slice(h * HEAD_D, (h + 1) * HEAD_D)
        q = q_ref[:, sl]
        k = k_ref[:, sl] * (HEAD_D ** -0.5)
        v = v_ref[:, sl].astype(BF16)
        att = _dot_nt(q.astype(BF16), k.astype(BF16))
        att = att * jnp.where(rel >= 0.0, jnp.exp(jnp.maximum(rel, 0.0) * lg), 0.0)
        qd = q * jnp.exp((pos + 1.0) * lg)
        kd = k * jnp.exp((lc - 1.0 - pos) * lg)
        s = s_ref[h]
        o_ref[:, sl] = _dot(att.astype(BF16), v) + _dot(qd.astype(BF16), s.astype(BF16))
        s_ref[h] = s * jnp.exp(lc * lg) + _dot_tn(kd.astype(BF16), v)

    @pl.when(c == nc - 1)
    def _():
        fin_ref[...] = s_ref[...]


def _retention(z, s0):
    B, T, _ = z.shape
    lc = min(T, 256)
    nc = T // lc

    def tmap(d, c):
        return jnp.where(d == 0, c, nc - 1 - c)

    def zspec(blk):
        return pl.BlockSpec((None, lc, MIX_W), lambda d, b, c: (b, tmap(d, c), blk))

    return pl.pallas_call(
        functools.partial(_ret_kernel, lc=lc, nc=nc),
        grid=(2, B, nc),
        in_specs=[
            zspec(ZB_RET), zspec(ZB_RET + 1), zspec(ZB_RET + 2),
            pl.BlockSpec((None, None, N_HEADS, HEAD_D, HEAD_D), lambda d, b, c: (b, d, 0, 0, 0)),
        ],
        out_specs=[
            pl.BlockSpec((None, None, lc, MIX_W), lambda d, b, c: (d, b, tmap(d, c), 0)),
            pl.BlockSpec((None, None, N_HEADS, HEAD_D, HEAD_D), lambda d, b, c: (b, d, 0, 0, 0)),
        ],
        out_shape=[
            jax.ShapeDtypeStruct((2, B, T, MIX_W), F32),
            jax.ShapeDtypeStruct((B, 2, N_HEADS, HEAD_D, HEAD_D), F32),
        ],
        scratch_shapes=[pltpu.VMEM((N_HEADS, HEAD_D, HEAD_D), F32)],
        compiler_params=_cparams("arbitrary", "arbitrary", "arbitrary"),
        name="retention",
    )(z, z, z, s0)


def _s5_zoh_kernel(lre_ref, lim_ref, ldt_ref, bre_ref, bim_ref, are_ref, aim_ref, bbre_ref, bbim_ref):
    lam_re = jnp.minimum(lre_ref[...], -1e-4)
    lam_im = lim_ref[...]
    dt = jnp.exp(ldt_ref[...])
    mag = jnp.exp(dt * lam_re)
    ang = dt * lam_im
    a_re = mag * jnp.cos(ang)
    a_im = mag * jnp.sin(ang)
    den = lam_re * lam_re + lam_im * lam_im
    f_re = ((a_re - 1.0) * lam_re + a_im * lam_im) / den
    f_im = (a_im * lam_re - (a_re - 1.0) * lam_im) / den
    b_re = bre_ref[...]
    b_im = bim_ref[...]
    are_ref[...] = a_re
    aim_ref[...] = a_im
    bbre_ref[...] = f_re * b_re - f_im * b_im
    bbim_ref[...] = f_re * b_im + f_im * b_re


def _s5_zoh(lam_re, lam_im, log_dt, b_re, b_im):
    n = 2 * S5_STATE
    col = lambda t: t.reshape(n, 1)
    ldt = jnp.broadcast_to(log_dt[:, :, None], (2, S5_GROUPS, S5_P))
    outs = pl.pallas_call(
        _s5_zoh_kernel,
        out_shape=[jax.ShapeDtypeStruct((n, 1), F32), jax.ShapeDtypeStruct((n, 1), F32),
                   jax.ShapeDtypeStruct((n, S5_CH), F32), jax.ShapeDtypeStruct((n, S5_CH), F32)],
        name="s5_zoh",
    )(col(lam_re), col(lam_im), col(ldt), b_re.reshape(n, S5_CH), b_im.reshape(n, S5_CH))
    a_re, a_im, bb_re, bb_im = outs
    shp = (2, S5_GROUPS, S5_P, S5_CH)
    return a_re.reshape(2, 1, S5_STATE), a_im.reshape(2, 1, S5_STATE), bb_re.reshape(shp), bb_im.reshape(shp)


S5_RADIX = 16


def _s5_kernel(uf_ref, ub_ref, are_ref, aim_ref, bre_ref, bim_ref, cre_ref, cim_ref, x0r_ref, x0i_ref,
               yf_ref, yb_ref, fr_ref, fi_ref, car_re, car_im, *bufs, L, nc):
    c = pl.program_id(1)
    R = S5_RADIX
    G = L // R

    @pl.when(c == 0)
    def _():
        car_re[...] = x0r_ref[...]
        car_im[...] = x0i_ref[...]

    rr = lax.broadcasted_iota(jnp.int32, (L, L), 0)
    cc = lax.broadcasted_iota(jnp.int32, (L, L), 1)
    lg, lr = G.bit_length() - 1, R.bit_length() - 1
    perm = jnp.where(cc == ((rr & (G - 1)) << lr) + (rr >> lg), 1.0, 0.0).astype(BF16)
    unperm = jnp.where(cc == ((rr & (R - 1)) << lg) + (rr >> lr), 1.0, 0.0).astype(BF16)

    def cmul(pr, pi, qr, qi):
        return pr * qr - pi * qi, pr * qi + pi * qr

    def steps(di, u_ref, y_ref, x_re, x_im, o_re, o_im):
        rev = di == 1
        u = _dot(perm, u_ref[...].astype(BF16)).astype(BF16)
        x_re[...] = _dot(u, bre_ref[di])
        x_im[...] = _dot(u, bim_ref[di])
        yield
        ar = are_ref[di]
        ai = aim_ref[di]
        order = list(range(R - 1, -1, -1)) if rev else list(range(R))
        slab = lambda j: slice(j * G, (j + 1) * G)

        er = x_re[slab(order[0]), :]
        ei = x_im[slab(order[0]), :]
        for n, j in enumerate(order[1:]):
            tr, ti = cmul(ar, ai, er, ei)
            er = tr + x_re[slab(j), :]
            ei = ti + x_im[slab(j), :]
            if n % 4 == 3:
                yield

        a_r, a_i = ar, ai
        for _ in range(R.bit_length() - 1):
            a_r, a_i = cmul(a_r, a_i, a_r, a_i)
        zr = car_re[di]
        zi = car_im[di]
        cin_r = [None] * G
        cin_i = [None] * G
        for k in (range(G - 1, -1, -1) if rev else range(G)):
            cin_r[k] = zr
            cin_i[k] = zi
            tr, ti = cmul(a_r, a_i, zr, zi)
            zr = tr + er[k:k + 1, :]
            zi = ti + ei[k:k + 1, :]
        car_re[di] = zr
        car_im[di] = zi
        yield

        xr = jnp.concatenate(cin_r, axis=0)
        xi = jnp.concatenate(cin_i, axis=0)
        for n, j in enumerate(order):
            tr, ti = cmul(ar, ai, xr, xi)
            xr = tr + x_re[slab(j), :]
            xi = ti + x_im[slab(j), :]
            o_re[slab(j), :] = xr
            o_im[slab(j), :] = xi
            if n % 4 == 3:
                yield
        xr_all = _dot(unperm, o_re[...].astype(BF16)).astype(BF16)
        xi_all = _dot(unperm, o_im[...].astype(BF16)).astype(BF16)
        y_ref[...] = _dot(xr_all, cre_ref[...]) - _dot(xi_all, cim_ref[...])

    live = [steps(0, uf_ref, yf_ref, *bufs[0:4]), steps(1, ub_ref, yb_ref, *bufs[4:8])]
    while live:
        for gen in list(live):
            if next(gen, StopIteration) is StopIteration:
                live.remove(gen)

    @pl.when(c == nc - 1)
    def _():
        fr_ref[...] = car_re[...]
        fi_ref[...] = car_im[...]


def _s5(z, a_re, a_im, bblk_re, bblk_im, cblk_re, cblk_im, x0_re, x0_im):
    B, T, _ = z.shape
    L = min(T, 128)
    nc = T // L
    assert L % S5_RADIX == 0 and (L // S5_RADIX) & (L // S5_RADIX - 1) == 0
    full = lambda shape: pl.BlockSpec(shape, lambda b, c: (0,) * len(shape))
    sspec = pl.BlockSpec((None, 2, 1, S5_STATE), lambda b, c: (b, 0, 0, 0))
    uspec = lambda cmap, blk: pl.BlockSpec((None, L, MIX_W), lambda b, c: (b, cmap(c), blk))
    fwd = lambda c: c
    bwd = lambda c: nc - 1 - c
    buf = pltpu.VMEM((L, S5_STATE), F32)
    car = pltpu.VMEM((2, 1, S5_STATE), F32)
    seq = jax.ShapeDtypeStruct((B, T, MIX_W), F32)
    state = jax.ShapeDtypeStruct((B, 2, 1, S5_STATE), F32)
    return pl.pallas_call(
        functools.partial(_s5_kernel, L=L, nc=nc),
        grid=(B, nc),
        in_specs=[
            uspec(fwd, ZB_S5), uspec(bwd, ZB_S5),
            full((2, 1, S5_STATE)), full((2, 1, S5_STATE)),
            full((2, MIX_W, S5_STATE)), full((2, MIX_W, S5_STATE)),
            full((S5_STATE, MIX_W)), full((S5_STATE, MIX_W)),
            sspec, sspec,
        ],
        out_specs=[uspec(fwd, 0), uspec(bwd, 0), sspec, sspec],
        out_shape=[seq, seq, state, state],
        scratch_shapes=[car, car] + [buf] * 8,
        compiler_params=_cparams("parallel", "arbitrary"),
        name="s5_scan",
    )(z, z, a_re, a_im, bblk_re, bblk_im, cblk_re, cblk_im, x0_re, x0_im)


GLA_CHUNK = 16
GLA_CHUNK_SHIFT = 4
SUB = 8


def _gla_steps(q_ref, zf_ref, v_ref, lb_ref, s0_ref, o_ref, fin_ref, s_ref, a_ref, qs_ref, key_ref,
               *, rev, tb, nt, unrolled):
    jb = pl.program_id(1)
    ch = GLA_CHUNK
    half = 2 * HEAD_D

    @pl.when(jb == 0)
    def _():
        s_ref[...] = s0_ref[...]

    lb = lb_ref[...]
    zf = zf_ref[...]
    q = q_ref[...]
    qs_ref[...] = q * jax.nn.sigmoid(q)
    key_ref[...] = (1.0 - lb) * jax.nn.sigmoid(-zf)
    l1 = jnp.log(lb)
    l2 = jnp.log1p(-lb) + jnp.minimum(zf, 0.0) - jnp.log1p(jnp.exp(-jnp.abs(zf)))
    lf = jnp.maximum(l1, l2) + jnp.log1p(jnp.exp(-jnp.abs(l1 - l2)))
    r = lax.broadcasted_iota(jnp.int32, (tb, tb), 0)
    c = lax.broadcasted_iota(jnp.int32, (tb, tb), 1)
    same = (r >> GLA_CHUNK_SHIFT) == (c >> GLA_CHUNK_SHIFT)
    tri = jnp.where(same & ((c >= r) if rev else (c <= r)), 1.0, 0.0).astype(BF16)
    p1 = lf.astype(BF16)
    r1 = lf - p1.astype(F32)
    p2 = r1.astype(BF16)
    p3 = (r1 - p2.astype(F32)).astype(BF16)
    a_ref[...] = _dot(tri, p1) + _dot(tri, p2) + _dot(tri, p3)

    rows = lax.broadcasted_iota(jnp.int32, (SUB, half), 0)
    lo = lax.broadcasted_iota(jnp.int32, (SUB, half), 1) < HEAD_D
    same_head = ((lax.broadcasted_iota(jnp.int32, (half, half), 0) >> HEAD_SHIFT)
                 == (lax.broadcasted_iota(jnp.int32, (half, half), 1) >> HEAD_SHIFT))

    def pair_sums(p):
        s0 = jnp.sum(jnp.where(lo, p, 0.0), axis=1, keepdims=True)
        s1 = jnp.sum(jnp.where(lo, 0.0, p), axis=1, keepdims=True)
        return jnp.where(lo, s0, s1)

    def chunk(ci, carry):
        t0 = (tb // ch - 1 - ci if rev else ci) * ch
        win = pl.ds(t0 if isinstance(ci, int) else pl.multiple_of(t0, ch), ch)
        a = a_ref[win, :]
        qc = qs_ref[win, :]
        kc = key_ref[win, :]
        vc = v_ref[win, :]
        last = 0 if rev else ch - 1
        a_last = a[last:last + 1, :]
        qe = (qc * jnp.exp(a)).astype(BF16)
        ke = (kc * jnp.exp(a_last - a)).astype(BF16)
        ea = jnp.exp(a_last)
        vb = vc.astype(BF16)
        nslab = ch // SUB
        acc = [[jnp.zeros((SUB, half), F32) for _ in range(2)] for _ in range(nslab)]
        for jj in range(ch):
            for sb in range(nslab):
                r0 = sb * SUB
                if (r0 > jj) if rev else (r0 + SUB - 1 < jj):
                    continue
                whole = (r0 + SUB - 1 <= jj) if rev else (r0 >= jj)
                rs = slice(r0, r0 + SUB)
                valid = (rows + r0 <= jj) if rev else (rows + r0 >= jj)
                for hp in range(2):
                    sl = slice(hp * half, (hp + 1) * half)
                    dec = jnp.exp(jnp.minimum(a[rs, sl] - a[jj:jj + 1, sl], 0.0))
                    p = qc[rs, sl] * dec * kc[jj:jj + 1, sl]
                    if not whole:
                        p = jnp.where(valid, p, 0.0)
                    acc[sb][hp] = acc[sb][hp] + pair_sums(p) * vc[jj:jj + 1, sl]
        for hp in range(2):
            sl = slice(hp * half, (hp + 1) * half)
            s = s_ref[hp]
            inter = _dot_nt(qe[:, sl], s.astype(BF16))
            intra = jnp.concatenate([acc[sb][hp] for sb in range(nslab)], axis=0)
            o_ref[win, sl] = inter + intra
            s_ref[hp] = s * ea[:, sl] + jnp.where(same_head, _dot_tn(vb[:, sl], ke[:, sl]), 0.0)
        return carry

    yield
    if unrolled:
        for ci in range(tb // ch):
            chunk(ci, 0)
            yield
    else:
        lax.fori_loop(0, tb // ch, chunk, 0, unroll=2)

    @pl.when(jb == nt - 1)
    def _():
        fin_ref[...] = s_ref[...]


def _gla_kernel(*refs, rev, tb, nt):
    for _ in _gla_steps(*refs, rev=rev, tb=tb, nt=nt, unrolled=False):
        pass


def _gla(z, lb, s0, rev):
    B, T, _ = z.shape
    pair = 2 * HEAD_D
    s0 = s0.reshape(B, 2, 2, HEAD_D, HEAD_D)
    zero = jnp.zeros_like(s0[:, :, 0])
    s0 = jnp.concatenate([jnp.concatenate([s0[:, :, 0], zero], axis=-1),
                          jnp.concatenate([zero, s0[:, :, 1]], axis=-1)], axis=-2)
    tb = min(T, 256)
    nt = T // tb
    di = 1 if rev else 0
    tmap = (lambda j: nt - 1 - j) if rev else (lambda j: j)
    zspec = lambda blk: pl.BlockSpec((None, tb, MIX_W), lambda b, j: (b, tmap(j), blk))
    sspec = pl.BlockSpec((None, 2, pair, pair), lambda b, j: (b, 0, 0, 0))
    blk = pltpu.VMEM((tb, MIX_W), F32)
    o, fin = pl.pallas_call(
        functools.partial(_gla_kernel, rev=rev, tb=tb, nt=nt),
        grid=(B, nt),
        in_specs=[zspec(ZB_HG), zspec(ZB_HG + 1 + di), zspec(ZB_HG + 3),
                  pl.BlockSpec((1, MIX_W), lambda b, j: (0, 0)), sspec],
        out_specs=[pl.BlockSpec((None, tb, MIX_W), lambda b, j: (b, tmap(j), 0)), sspec],
        out_shape=[jax.ShapeDtypeStruct((B, T, MIX_W), F32),
                   jax.ShapeDtypeStruct((B, 2, pair, pair), F32)],
        scratch_shapes=[pltpu.VMEM((2, pair, pair), F32), blk, blk, blk],
        compiler_params=_cparams("parallel", "arbitrary"),
        name="hgrn_gla_bwd" if rev else "hgrn_gla_fwd",
    )(z, z, z, lb[di:di + 1], s0)
    fin = jnp.stack([fin[:, :, :HEAD_D, :HEAD_D], fin[:, :, HEAD_D:, HEAD_D:]], axis=2)
    return o, fin.reshape(B, N_HEADS, HEAD_D, HEAD_D)


def _rw_prep_kernel(*refs, grid_shift, tm):
    if grid_shift:
        (zc_ref, zu_ref, zd_ref, mu_ref, vec_ref, w0_ref, lora_ref,
         r_ref, k2_ref, v_ref, w_ref, nkk_ref, kka_ref, g_ref, bonus_ref, buf_ref) = refs
    else:
        (zc_ref, mu_ref, vec_ref, w0_ref, lora_ref,
         r_ref, k2_ref, v_ref, w_ref, nkk_ref, kka_ref, g_ref, bonus_ref, buf_ref) = refs
    i = pl.program_id(1)
    nt = pl.num_programs(1)
    halo = GRID_W
    width = 4 * MIX_W
    z = zc_ref[...]
    buf_ref[halo:halo + tm, :] = z
    lane = lax.broadcasted_iota(jnp.int32, (tm, width), 1)
    row = lax.broadcasted_iota(jnp.int32, (tm, width), 0)
    if grid_shift:
        buf_ref[0:halo, :] = jnp.where(i > 0, zu_ref[...], 0.0)
        buf_ref[halo + tm:2 * halo + tm, :] = jnp.where(i < nt - 1, zd_ref[...], 0.0)
        col = row & (GRID_W - 1)
        left = jnp.where(col > 0, buf_ref[halo - 1:halo - 1 + tm, :], 0.0)
        right = jnp.where(col < GRID_W - 1, buf_ref[halo + 1:halo + 1 + tm, :], 0.0)
        up = buf_ref[0:tm, :]
        down = buf_ref[2 * halo:2 * halo + tm, :]
        sel = lane & 3
        shifted = jnp.where(sel == 0, left, jnp.where(sel == 1, right, jnp.where(sel == 2, up, down)))
    else:
        zrow = jnp.zeros((1, width), F32)
        buf_ref[halo - 1:halo, :] = zrow
        buf_ref[halo + tm:halo + tm + 1, :] = zrow
        prev = buf_ref[halo - 1:halo - 1 + tm, :]
        nxt = buf_ref[halo + 1:halo + 1 + tm, :]
        shifted = jnp.where((lane & 1) == 0, prev, nxt)
    zs = z + mu_ref[...] * (shifted - z)
    r = zs[:, 0:MIX_W]
    k = zs[:, MIX_W:2 * MIX_W]
    v = zs[:, 2 * MIX_W:3 * MIX_W]
    sm = zs[:, 3 * MIX_W:4 * MIX_W]
    ones_bd = _head_ones()
    a0, k_k, k_a, r_k = (vec_ref[j:j + 1, :] for j in range(4))
    a = jax.nn.sigmoid(a0 + _dot(sm.astype(BF16), lora_ref[2]))
    g_ref[...] = _dot(jax.nn.sigmoid(sm).astype(BF16), lora_ref[3])
    kk = k * k_k
    kk = kk * lax.rsqrt(_head_sum(kk * kk, ones_bd) + 1e-12)
    k2 = k * (1.0 + (a - 1.0) * k_a)
    th = jnp.tanh(sm).astype(BF16)
    for di in range(2):
        w_ref[di] = -math.exp(-0.5) * jax.nn.sigmoid(w0_ref[di:di + 1, :] + _dot(th, lora_ref[di]))
    r_ref[...] = r
    k2_ref[...] = k2
    v_ref[...] = v
    nkk_ref[...] = -kk
    kka_ref[...] = kk * a
    bonus_ref[...] = _head_sum(r * k2 * r_k, ones_bd) * v


def _rw_prep(z, mu_p, vecs, w0, lora, grid_shift):
    B, T, _ = z.shape
    width = 4 * MIX_W
    wblk = ZB_RW // 4
    if grid_shift:
        tm = min(T, 512)
        hb = tm // GRID_W
        nh = T // GRID_W
        z_specs = [
            pl.BlockSpec((None, tm, width), lambda b, i: (b, i, wblk)),
            pl.BlockSpec((None, GRID_W, width), lambda b, i: (b, jnp.maximum(i * hb - 1, 0), wblk)),
            pl.BlockSpec((None, GRID_W, width), lambda b, i: (b, jnp.minimum((i + 1) * hb, nh - 1), wblk)),
        ]
        z_args = (z, z, z)
    else:
        tm = T
        z_specs = [pl.BlockSpec((None, tm, width), lambda b, i: (b, i, wblk))]
        z_args = (z,)
    const = lambda shape: pl.BlockSpec(shape, lambda b, i: (0,) * len(shape))
    ospec = pl.BlockSpec((None, tm, MIX_W), lambda b, i: (b, i, 0))
    oshape = jax.ShapeDtypeStruct((B, T, MIX_W), F32)
    dspec = pl.BlockSpec((2, None, tm, MIX_W), lambda b, i: (0, b, i, 0))
    dshape = jax.ShapeDtypeStruct((2, B, T, MIX_W), F32)
    return pl.pallas_call(
        functools.partial(_rw_prep_kernel, grid_shift=grid_shift, tm=tm),
        grid=(B, T // tm),
        in_specs=z_specs + [const((1, width)), const((4, MIX_W)), const((2, MIX_W)),
                            const((4, MIX_W, MIX_W))],
        out_specs=[ospec, ospec, ospec, dspec, ospec, ospec, ospec, ospec],
        out_shape=[oshape, oshape, oshape, dshape, oshape, oshape, oshape, oshape],
        scratch_shapes=[pltpu.VMEM((tm + 2 * GRID_W, width), F32)],
        compiler_params=_cparams("parallel", "parallel"),
        name="rwkv_prep",
    )(*z_args, mu_p, vecs, w0, lora)


RW_CHUNK = 64
RW_CHUNK_SHIFT = 6


def _mm(a, b):
    return jnp.dot(a.astype(BF16), b.astype(BF16), preferred_element_type=F32)


def _rwkv_kernel(*refs, rev, tb, nt):
    for _ in _rwkv_steps(*refs, rev=rev, tb=tb, nt=nt):
        pass


def _rwkv_steps(lw_ref, a_ref, b_ref, k_ref, r_ref, v_ref, s0_ref, y_ref, fin_ref, s_ref, g_ref,
                *, rev, tb, nt):
    jb = pl.program_id(1)
    ch = RW_CHUNK
    pair = 2 * HEAD_D

    @pl.when(jb == 0)
    def _():
        s_ref[...] = s0_ref[...]

    lw = lw_ref[...]
    rr = lax.broadcasted_iota(jnp.int32, (tb, tb), 0)
    cc = lax.broadcasted_iota(jnp.int32, (tb, tb), 1)
    same = (rr >> RW_CHUNK_SHIFT) == (cc >> RW_CHUNK_SHIFT)
    tri = jnp.where(same & ((cc >= rr) if rev else (cc <= rr)), 1.0, 0.0).astype(BF16)
    p1 = lw.astype(BF16)
    r1 = lw - p1.astype(F32)
    p2 = r1.astype(BF16)
    p3 = (r1 - p2.astype(F32)).astype(BF16)
    g_ref[...] = _dot(tri, p1) + _dot(tri, p2) + _dot(tri, p3)

    si = lax.broadcasted_iota(jnp.int32, (ch, ch), 0)
    ri = lax.broadcasted_iota(jnp.int32, (ch, ch), 1)
    if rev:
        si, ri = ch - 1 - si, ch - 1 - ri
    strict = ri < si
    incl = ri <= si
    eye = jnp.where(ri == si, 1.0, 0.0).astype(F32)
    levels = [((si >> (lv + 1)) == (ri >> (lv + 1))) & (((si >> lv) & 1) == 1) & (((ri >> lv) & 1) == 0)
              for lv in range(RW_CHUNK_SHIFT)]
    lo = lax.broadcasted_iota(jnp.int32, (ch, pair), 1) < HEAD_D
    same_head = ((lax.broadcasted_iota(jnp.int32, (pair, pair), 0) >> HEAD_SHIFT)
                 == (lax.broadcasted_iota(jnp.int32, (pair, pair), 1) >> HEAD_SHIFT))

    nch = tb // ch
    g = g_ref[...]
    e_g = jnp.exp(g)
    e_ng = jnp.exp(-g)
    at_f = a_ref[...] * jnp.exp(g - lw)
    rt_f = r_ref[...] * e_g
    at = at_f.astype(BF16)
    rt = rt_f.astype(BF16)
    first = (lax.broadcasted_iota(jnp.int32, (tb, MIX_W), 1) & HEAD_D) == 0
    at_h = [jnp.where(first, at_f, 0.0).astype(BF16), jnp.where(first, 0.0, at_f).astype(BF16)]
    rt_h = [jnp.where(first, rt_f, 0.0).astype(BF16), jnp.where(first, 0.0, rt_f).astype(BF16)]
    bt = (b_ref[...] * e_ng).astype(BF16)
    kt = (k_ref[...] * e_ng).astype(BF16)
    vb = v_ref[...].astype(BF16)

    items = [(c, hp, hh) for c in range(nch) for hp in range(2) for hh in range(2)]
    rows = lambda c: slice(c * ch, (c + 1) * ch)
    lanes = lambda hp: slice(hp * pair, (hp + 1) * pair)
    n_m, p_m, m_m, q_m = {}, {}, {}, {}
    for it in items:
        c, hp, hh = it
        ar = jnp.concatenate([at_h[hh][rows(c), lanes(hp)], rt_h[hh][rows(c), lanes(hp)]], axis=0)
        np_ = _dot_nt(ar, bt[rows(c), lanes(hp)])
        mq = _dot_nt(ar, kt[rows(c), lanes(hp)])
        n_m[it] = jnp.where(strict, np_[:ch], 0.0).astype(BF16)
        p_m[it] = jnp.where(incl, np_[ch:], 0.0).astype(BF16)
        m_m[it] = jnp.where(strict, mq[:ch], 0.0).astype(BF16)
        q_m[it] = jnp.where(incl, mq[ch:], 0.0).astype(BF16)
        if hp == 1 and hh == 1:
            yield
    t_m = {it: eye + jnp.where(levels[0], n_m[it].astype(F32), 0.0) for it in items}
    for lv in range(1, RW_CHUNK_SHIFT):
        tn = {it: _mm(t_m[it], n_m[it]) for it in items}
        yield
        t_m = {it: t_m[it] + jnp.where(levels[lv], _mm(tn[it], t_m[it]), 0.0) for it in items}
        yield
    mv = {it: _mm(m_m[it], vb[rows(it[0]), lanes(it[1])]) for it in items}
    qv = {it: _mm(q_m[it], vb[rows(it[0]), lanes(it[1])]) for it in items}
    yield
    y2 = {it: _mm(t_m[it], jnp.concatenate([at[rows(it[0]), lanes(it[1])], mv[it].astype(BF16)], axis=1))
          for it in items}
    yield

    last = 0 if rev else ch - 1
    for ci in range(nch):
        c = nch - 1 - ci if rev else ci
        g_c = g[c * ch + last:c * ch + last + 1, :]
        e_gc = jnp.exp(g_c - g[rows(c), :])
        bh = (b_ref[rows(c), :] * e_gc).astype(BF16)
        kh = (k_ref[rows(c), :] * e_gc).astype(BF16)
        dec_c = jnp.exp(g_c)
        xs, sas, ss = [], [], []
        for hp in range(2):
            s = s_ref[hp]
            w2 = jnp.where(lo, y2[(c, hp, 0)][:, :pair], y2[(c, hp, 1)][:, :pair]).astype(BF16)
            xs.append(_dot_nt(jnp.concatenate([w2, rt[rows(c), lanes(hp)]], axis=0), s.astype(BF16)))
            ss.append(s)
        for hp in range(2):
            w1 = jnp.where(lo, y2[(c, hp, 0)][:, pair:], y2[(c, hp, 1)][:, pair:])
            sas.append(w1 + xs[hp][:ch])
        for hp in range(2):
            sa = sas[hp]
            y_ref[rows(c), lanes(hp)] = xs[hp][ch:] + jnp.where(
                lo, _mm(p_m[(c, hp, 0)], sa) + qv[(c, hp, 0)], _mm(p_m[(c, hp, 1)], sa) + qv[(c, hp, 1)])
            upd = _dot_tn(jnp.concatenate([sa.astype(BF16), vb[rows(c), lanes(hp)]], axis=0),
                          jnp.concatenate([bh[:, lanes(hp)], kh[:, lanes(hp)]], axis=0))
            s_ref[hp] = ss[hp] * dec_c[:, lanes(hp)] + jnp.where(same_head, upd, 0.0)
        yield

    @pl.when(jb == nt - 1)
    def _():
        fin_ref[...] = s_ref[...]


def _hg_rw_kernel(*refs, rev, tb, nt):
    gla_in, rw_in = refs[0:5], refs[5:12]
    gla_out, rw_out = refs[12:14], refs[14:16]
    gla_scr, rw_scr = refs[16:20], refs[20:22]
    gla = _gla_steps(*gla_in, *gla_out, *gla_scr, rev=rev, tb=tb, nt=nt, unrolled=True)
    rwkv = _rwkv_steps(*rw_in, *rw_out, *rw_scr, rev=rev, tb=tb, nt=nt)
    live = [gla, rwkv]
    while live:
        for gen in list(live):
            if next(gen, StopIteration) is StopIteration:
                live.remove(gen)


def _head_pair_blockdiag(s):
    B = s.shape[0]
    s = s.reshape(B, 2, 2, HEAD_D, HEAD_D)
    zero = jnp.zeros_like(s[:, :, 0])
    return jnp.concatenate([jnp.concatenate([s[:, :, 0], zero], axis=-1),
                            jnp.concatenate([zero, s[:, :, 1]], axis=-1)], axis=-2)


def _head_pair_blocks(t):
    B = t.shape[0]
    t = jnp.stack([t[:, :, :HEAD_D, :HEAD_D], t[:, :, HEAD_D:, HEAD_D:]], axis=2)
    return t.reshape(B, N_HEADS, HEAD_D, HEAD_D)


def _rwkv(lw, a, b, k, r, v, s0, rev):
    B, T, _ = v.shape
    pair = 2 * HEAD_D
    tb = min(T, 256)
    nt = T // tb
    di = 1 if rev else 0
    tmap = (lambda j: nt - 1 - j) if rev else (lambda j: j)
    spec = pl.BlockSpec((None, tb, MIX_W), lambda bi, j: (bi, tmap(j), 0))
    sspec = pl.BlockSpec((None, 2, pair, pair), lambda bi, j: (bi, 0, 0, 0))
    y, fin = pl.pallas_call(
        functools.partial(_rwkv_kernel, rev=rev, tb=tb, nt=nt),
        grid=(B, nt),
        in_specs=[pl.BlockSpec((None, None, tb, MIX_W), lambda bi, j: (di, bi, tmap(j), 0)),
                  spec, spec, spec, spec, spec, sspec],
        out_specs=[spec, sspec],
        out_shape=[jax.ShapeDtypeStruct((B, T, MIX_W), F32),
                   jax.ShapeDtypeStruct((B, 2, pair, pair), F32)],
        scratch_shapes=[pltpu.VMEM((2, pair, pair), F32), pltpu.VMEM((tb, MIX_W), F32)],
        compiler_params=_cparams("parallel", "arbitrary"),
        name="rwkv_chunk_bwd" if rev else "rwkv_chunk_fwd",
    )(lw, a, b, k, r, v, _head_pair_blockdiag(s0))
    return y, _head_pair_blocks(fin)


def _hg_rw(z, lb, s0_hg, lw, a, b, k, r, v, s0_rw, rev):
    B, T, _ = v.shape
    pair = 2 * HEAD_D
    tb = min(T, 256)
    nt = T // tb
    di = 1 if rev else 0
    tmap = (lambda j: nt - 1 - j) if rev else (lambda j: j)
    zspec = lambda blk: pl.BlockSpec((None, tb, MIX_W), lambda bi, j: (bi, tmap(j), blk))
    spec = zspec(0)
    sspec = pl.BlockSpec((None, 2, pair, pair), lambda bi, j: (bi, 0, 0, 0))
    state = jax.ShapeDtypeStruct((B, 2, pair, pair), F32)
    seq = jax.ShapeDtypeStruct((B, T, MIX_W), F32)
    blk = pltpu.VMEM((tb, MIX_W), F32)
    st = pltpu.VMEM((2, pair, pair), F32)
    o, fin_hg, y, fin_rw = pl.pallas_call(
        functools.partial(_hg_rw_kernel, rev=rev, tb=tb, nt=nt),
        grid=(B, nt),
        in_specs=[zspec(ZB_HG), zspec(ZB_HG + 1 + di), zspec(ZB_HG + 3),
                  pl.BlockSpec((1, MIX_W), lambda bi, j: (0, 0)), sspec,
                  pl.BlockSpec((None, None, tb, MIX_W), lambda bi, j: (di, bi, tmap(j), 0)),
                  spec, spec, spec, spec, spec, sspec],
        out_specs=[spec, sspec, spec, sspec],
        out_shape=[seq, state, seq, state],
        scratch_shapes=[st, blk, blk, blk, st, blk],
        compiler_params=_cparams("parallel", "arbitrary"),
        name="hgrn_rwkv_bwd" if rev else "hgrn_rwkv_fwd",
    )(z, z, z, lb[di:di + 1], _head_pair_blockdiag(s0_hg),
      lw, a, b, k, r, v, _head_pair_blockdiag(s0_rw))
    return o, _head_pair_blocks(fin_hg), y, _head_pair_blocks(fin_rw)


def _mix_kernel(x_ref, sc_ref, sh_ref, g1_ref, nw0_ref, nw_ref, wg_ref,
                ret_ref, retg_ref, s5f_ref, s5b_ref, s5u_ref, hgf_ref, hgb_ref, hgg_ref, rwf_ref, rwr_ref,
                rwb_ref, rwg_ref,
                vec_ref, glu_w_ref, wbr_ref, wout_ref, o_ref):
    ones_bd = _head_ones()
    h = (_rms(x_ref[...], nw0_ref[...]) * (1.0 + sc_ref[...]) + sh_ref[...]).astype(BF16)
    gn_w, s5_d, glu_b, hg_w, ln_w, ln_b = (vec_ref[j:j + 1, :] for j in range(6))

    def group_norm(o, eps):
        mu = _head_sum(o, ones_bd) * (1.0 / HEAD_D)
        oc = o - mu
        var = _head_sum(oc * oc, ones_bd) * (1.0 / HEAD_D)
        return oc * lax.rsqrt(var + eps)

    g = retg_ref[...]
    y_ret = group_norm(ret_ref[0] + ret_ref[1], EPS) * gn_w * (g * jax.nn.sigmoid(g))

    y = s5_d * s5u_ref[...] + s5f_ref[...] + s5b_ref[...]
    yg = jax.nn.gelu(y)
    y_s5 = yg * jax.nn.sigmoid(_dot(yg.astype(BF16), glu_w_ref[...]) + glu_b)

    o = hgf_ref[...] + hgb_ref[...]
    g = hgg_ref[...]
    ms = _head_sum(o * o, ones_bd) * (1.0 / HEAD_D)
    y_hg = o * lax.rsqrt(ms + EPS) * hg_w * (g * jax.nn.sigmoid(g))

    y = group_norm(rwf_ref[...] + rwr_ref[...], RW_LN_EPS) * ln_w + ln_b
    y_rw = (y + rwb_ref[...]) * rwg_ref[...]

    mixed = None
    for m, ym in enumerate((y_ret, y_s5, y_hg, y_rw)):
        br = _dot(ym.astype(BF16), wbr_ref[m])
        term = jax.nn.sigmoid(_dot(h, wg_ref[:, m * D_MODEL:(m + 1) * D_MODEL])) * br
        mixed = term if mixed is None else mixed + term
    mixed = _dot(mixed.astype(BF16), wout_ref[...])
    o_ref[...] = x_ref[...] + g1_ref[...] * _rms(mixed, nw_ref[...])


def _mix(x, sc, sh, g1, nw0, nw, w_gate, z, ret_o, s5_y, hg_o, rw_y, rw_bonus, rw_g, vecs, glu_w,
         w_branch, w_out):
    B, T, _ = x.shape
    tm = min(T, 256)
    xspec = pl.BlockSpec((None, tm, D_MODEL), lambda b, i: (b, i, 0))
    zspec = lambda blk: pl.BlockSpec((None, tm, MIX_W), lambda b, i: (b, i, blk))
    dspec = pl.BlockSpec((2, None, tm, MIX_W), lambda b, i: (0, b, i, 0))
    const = lambda shape: pl.BlockSpec(shape, lambda b, i: (0,) * len(shape))
    mspec = lambda m: pl.BlockSpec((None, 1, D_MODEL), _mod_map(m.shape[0]))
    return pl.pallas_call(
        _mix_kernel,
        grid=(B, T // tm),
        in_specs=[
            xspec, mspec(sc), mspec(sh), mspec(g1),
            const((1, D_MODEL)), const((1, D_MODEL)),
            const((D_MODEL, 4 * D_MODEL)),
            dspec, zspec(ZB_RET + 3),
            zspec(0), zspec(0), zspec(ZB_S5),
            zspec(0), zspec(0), zspec(ZB_HG + 4),
            zspec(0), zspec(0), zspec(0), zspec(0),
            const((6, MIX_W)), const((MIX_W, MIX_W)),
            const((4, MIX_W, D_MODEL)), const((D_MODEL, D_MODEL)),
        ],
        out_specs=xspec,
        out_shape=jax.ShapeDtypeStruct((B, T, D_MODEL), F32),
        compiler_params=_cparams("parallel", "parallel"),
        name="mix_out",
    )(x, sc, sh, g1, nw0, nw, w_gate, ret_o, z, s5_y[0], s5_y[1], z, hg_o[0], hg_o[1], z, rw_y[0], rw_y[1],
      rw_bonus, rw_g, vecs, glu_w, w_branch, w_out)


FFN_CHUNK = 1024


def _ffn_kernel(x_ref, sc_ref, sh_ref, g2_ref, nw2_ref, nw3_ref, w1_ref, w2_ref, o_ref):
    x = x_ref[...]
    h = (_rms(x, nw2_ref[...]) * (1.0 + sc_ref[...]) + sh_ref[...]).astype(BF16)
    acc = None
    for j in range(D_FF // FFN_CHUNK):
        cols = slice(j * FFN_CHUNK, (j + 1) * FFN_CHUNK)
        a = jnp.maximum(_dot(h, w1_ref[:, cols]), 0.0)
        part = _dot((a * a).astype(BF16), w2_ref[cols, :])
        acc = part if acc is None else acc + part
    o_ref[...] = x + g2_ref[...] * _rms(acc, nw3_ref[...])


def _ffn(x, sc, sh, g2, nw2, nw3, w1, w2):
    B, T, _ = x.shape
    tm = min(T, 512)
    xspec = pl.BlockSpec((None, tm, D_MODEL), lambda b, i: (b, i, 0))
    mspec = lambda m: pl.BlockSpec((None, 1, D_MODEL), _mod_map(m.shape[0]))
    const = pl.BlockSpec((1, D_MODEL), lambda b, i: (0, 0))
    return pl.pallas_call(
        _ffn_kernel,
        grid=(B, T // tm),
        in_specs=[xspec, mspec(sc), mspec(sh), mspec(g2), const, const,
                  pl.BlockSpec((D_MODEL, D_FF), lambda b, i: (0, 0)),
                  pl.BlockSpec((D_FF, D_MODEL), lambda b, i: (0, 0))],
        out_specs=xspec,
        out_shape=jax.ShapeDtypeStruct((B, T, D_MODEL), F32),
        compiler_params=_cparams("parallel", "parallel"),
        name="ffn",
    )(x, sc, sh, g2, nw2, nw3, w1, w2)


def _layer_params(l, p):
    w = p['w_in'][l]
    w_p = jnp.concatenate(
        [w[:, 2560:3488], jnp.zeros((D_MODEL, 4 * MIX_W - RW_COLS), F32),
         w[:, 0:1024], w[:, 1024:1280], w[:, 1280:2560]], axis=1).astype(BF16)
    w_gate = w[:, 3488:7584].astype(BF16)

    a_re, a_im, bb_re, bb_im = _s5_zoh(p['s5_lam_re'][l], p['s5_lam_im'][l], p['s5_log_dt'][l],
                                       p['s5_b_re'][l], p['s5_b_im'][l])
    eye = jnp.eye(S5_GROUPS, dtype=F32)
    bblk = lambda bb: jnp.einsum('dgph,gk->dghkp', bb, eye).reshape(2, MIX_W, S5_STATE).astype(BF16)
    cblk = lambda c: jnp.einsum('ghp,gk->gpkh', c, eye).reshape(S5_STATE, MIX_W).astype(BF16)

    def lora_pad(m, row0):
        return jnp.zeros((MIX_W, MIX_W), F32).at[row0:row0 + m.shape[0]].set(m)

    lora = jnp.stack([lora_pad(p['rw_w2'][l, 0], 0), lora_pad(p['rw_w2'][l, 1], 32),
                      lora_pad(p['rw_a2'][l], 64), lora_pad(p['rw_g2'][l], 96)]).astype(BF16)
    mu_p = jnp.concatenate([p['rw_mu'][l], jnp.zeros((4 * MIX_W - RW_COLS,), F32)]).reshape(1, 4 * MIX_W)
    return dict(
        w_p=w_p, w_gate=w_gate, nw=p['norm_w'][l],
        s5=(a_re, a_im, bblk(bb_re), bblk(bb_im), cblk(p['s5_c_re'][l]), cblk(p['s5_c_im'][l])),
        rw_mu=mu_p, rw_lora=lora, rw_w0=p['rw_w0'][l],
        rw_vecs=jnp.stack([p['rw_a0'][l], p['rw_k_k'][l], p['rw_k_a'][l], p['rw_r_k'][l]]),
        mix_vecs=jnp.stack([p['ret_gn_w'][l], p['s5_d'][l], p['s5_glu_b'][l], p['hg_norm_w'][l],
                            p['rw_ln_w'][l], p['rw_ln_b'][l]]),
        glu_w=p['s5_glu_w'][l].astype(BF16),
        w_branch=p['w_branch'][l].astype(BF16), w_out=p['w_out'][l].astype(BF16),
        ff_w1=p['ff_w1'][l].astype(BF16), ff_w2=p['ff_w2'][l].astype(BF16),
    )


def _trunk_layer(x, mod, init, grid_shift, lp, hg_lb):
    B, T, _ = x.shape
    sh1, sc1, g1, sh2, sc2, g2 = (m[:, None, :] for m in jnp.split(mod, 6, axis=-1))
    nw = lp['nw']
    if mod.shape[0] == 1:
        tok = lambda t: t.reshape(t.shape[:-3] + (1, B * T, t.shape[-1]))
    else:
        tok = lambda t: t
    z = _proj_in(tok(x), sc1, sh1, nw[0:1], lp['w_p']).reshape(B, T, Z_COLS)

    s_ret, s_s5r, s_s5i, s_hg, s_rw = init
    ret_o, f_ret = _retention(z, s_ret)

    s5_f, s5_b, f_s5r, f_s5i = _s5(z, *lp['s5'], s_s5r.reshape(B, 2, 1, S5_STATE), s_s5i.reshape(B, 2, 1, S5_STATE))
    f_s5r = f_s5r.reshape(B, 2, S5_GROUPS, S5_P)
    f_s5i = f_s5i.reshape(B, 2, S5_GROUPS, S5_P)

    r, k2, v, w, nkk, kka, rw_g, bonus = _rw_prep(z, lp['rw_mu'], lp['rw_vecs'], lp['rw_w0'],
                                                  lp['rw_lora'], grid_shift)
    s_hg_t = jnp.swapaxes(s_hg, -1, -2)
    hg_f, f_hg_f, rw_f, f_rw_f = _hg_rw(z, hg_lb, s_hg_t[:, 0], w, nkk, kka, k2, r, v, s_rw[:, 0], False)
    hg_b, f_hg_b, rw_b, f_rw_b = _hg_rw(z, hg_lb, s_hg_t[:, 1], w, nkk, kka, k2, r, v, s_rw[:, 1], True)
    f_hg = jnp.swapaxes(jnp.stack([f_hg_f, f_hg_b], axis=1), -1, -2)
    f_rw = jnp.stack([f_rw_f, f_rw_b], axis=1)

    x = _mix(tok(x), sc1, sh1, g1, nw[0:1], nw[1:2], lp['w_gate'], tok(z), tok(ret_o), (tok(s5_f), tok(s5_b)),
             (tok(hg_f), tok(hg_b)), (tok(rw_f), tok(rw_b)), tok(bonus), tok(rw_g),
             lp['mix_vecs'], lp['glu_w'], lp['w_branch'], lp['w_out'])
    x = _ffn(x, sc2, sh2, g2, nw[2:3], nw[3:4], lp['ff_w1'], lp['ff_w2'])
    return x.reshape(B, T, D_MODEL), (f_ret, f_s5r, f_s5i, f_hg, f_rw)


def kernel(x_prompt, x_sample, state_ret, state_s5_re, state_s5_im, state_hgrn, state_rwkv, c, c_ctx, ada_w, ada_b, norm_w, w_in, ret_gn_w, s5_lam_re, s5_lam_im, s5_log_dt, s5_b_re, s5_b_im, s5_c_re, s5_c_im, s5_d, s5_glu_w, s5_glu_b, hg_lb, hg_norm_w, rw_mu, rw_w0, rw_w2, rw_a0, rw_a2, rw_g2, rw_k_k, rw_k_a, rw_r_k, rw_ln_w, rw_ln_b, w_branch, w_out, ff_w1, ff_w2):
    p = dict(norm_w=norm_w, w_in=w_in, ret_gn_w=ret_gn_w, s5_lam_re=s5_lam_re, s5_lam_im=s5_lam_im,
             s5_log_dt=s5_log_dt, s5_b_re=s5_b_re, s5_b_im=s5_b_im, s5_c_re=s5_c_re, s5_c_im=s5_c_im,
             s5_d=s5_d, s5_glu_w=s5_glu_w, s5_glu_b=s5_glu_b, hg_norm_w=hg_norm_w, rw_mu=rw_mu,
             rw_w0=rw_w0, rw_w2=rw_w2, rw_a0=rw_a0, rw_a2=rw_a2, rw_g2=rw_g2, rw_k_k=rw_k_k,
             rw_k_a=rw_k_a, rw_r_k=rw_r_k, rw_ln_w=rw_ln_w, rw_ln_b=rw_ln_b, w_branch=w_branch,
             w_out=w_out, ff_w1=ff_w1, ff_w2=ff_w2)
    depth = w_in.shape[0]
    n_ctx = x_prompt.shape[0]
    n_lat = x_sample.shape[0]
    assert 1 + n_lat <= 8

    lb_cum = jnp.cumsum(jax.nn.softmax(hg_lb.astype(F32), axis=0), axis=0)
    hg_lower = lb_cum - lb_cum[0]

    cond = jnp.concatenate([c_ctx[None, :], c, jnp.zeros((7 - n_lat, D_MODEL), F32)], axis=0)
    mod = _ada_mod(cond, ada_w, ada_b)

    zero_state = (jnp.zeros((n_ctx, 2, N_HEADS, HEAD_D, HEAD_D), F32),
                  jnp.zeros((n_ctx, 2, S5_GROUPS, S5_P), F32),
                  jnp.zeros((n_ctx, 2, S5_GROUPS, S5_P), F32),
                  jnp.zeros((n_ctx, 2, N_HEADS, HEAD_D, HEAD_D), F32),
                  jnp.zeros((n_ctx, 2, N_HEADS, HEAD_D, HEAD_D), F32))
    xp, xs = x_prompt, x_sample
    finals = []
    for l in range(depth):
        lp = _layer_params(l, p)
        xp, fin = _trunk_layer(xp, mod[l, 0:1], zero_state, False, lp, hg_lower[l])
        finals.append(fin)
        lat_init = (state_ret[:, l], state_s5_re[:, l], state_s5_im[:, l], state_hgrn[:, l], state_rwkv[:, l])
        xs, _ = _trunk_layer(xs, mod[l, 1:1 + n_lat], lat_init, True, lp, hg_lower[l])
    new_states = tuple(jnp.stack([f[i] for f in finals], axis=1) for i in range(5))
    return (xp, xs) + new_states
```

```python
import functools
import math

import jax
import jax.numpy as jnp
from jax import lax
from jax.experimental import pallas as pl
from jax.experimental.pallas import tpu as pltpu

F32 = jnp.float32
BF16 = jnp.bfloat16

D_MODEL = 1024
GRID_W = 64
MIX_W = 256
N_HEADS = 4
HEAD_D = 64
HEAD_SHIFT = 6
LANES = 128
S5_GROUPS = 16
S5_CH = 16
S5_P = 64
S5_STATE = S5_GROUPS * S5_P
D_FF = 4096
EPS = 1e-6
RW_LN_EPS = 64e-5
RW_COLS = 928

Z_COLS = 3584
ZB_RW = 0
ZB_RET = 4
ZB_S5 = 8
ZB_HG = 9

VMEM_LIMIT = 56 * 1024 * 1024

LOG_GAMMA = tuple(
    tuple(math.log1p(-2.0 ** (-(5.0 + 0.5 * di) - h)) for h in range(N_HEADS)) for di in range(2))


def _cparams(*sem):
    return pltpu.CompilerParams(dimension_semantics=sem, vmem_limit_bytes=VMEM_LIMIT)


def _dot(a, b):
    return jnp.dot(a, b, preferred_element_type=F32)


def _dot_nt(a, b):
    return lax.dot_general(a, b, (((1,), (1,)), ((), ())), preferred_element_type=F32)


def _dot_tn(a, b):
    return lax.dot_general(a, b, (((0,), (0,)), ((), ())), preferred_element_type=F32)


def _head_ones():
    r = lax.broadcasted_iota(jnp.int32, (MIX_W, MIX_W), 0) >> HEAD_SHIFT
    c = lax.broadcasted_iota(jnp.int32, (MIX_W, MIX_W), 1) >> HEAD_SHIFT
    return jnp.where(r == c, 1.0, 0.0).astype(F32)


def _head_sum(x, ones_bd):
    ones = ones_bd.astype(BF16)
    p1 = x.astype(BF16)
    r1 = x - p1.astype(F32)
    p2 = r1.astype(BF16)
    p3 = (r1 - p2.astype(F32)).astype(BF16)
    return _dot(p1, ones) + _dot(p2, ones) + _dot(p3, ones)


def _rms(x, w):
    return x * lax.rsqrt(jnp.mean(x * x, axis=-1, keepdims=True) + EPS) * w


def _ada_kernel(c_ref, w_ref, b_ref, o_ref):
    c = c_ref[...]
    s = c * jax.nn.sigmoid(c)
    o_ref[...] = _dot(s.astype(BF16), w_ref[...].astype(BF16)) + b_ref[...]


def _ada_mod(cond, ada_w, ada_b):
    L = ada_w.shape[0]
    n = ada_w.shape[2]
    tn = 1536
    return pl.pallas_call(
        _ada_kernel,
        grid=(L, n // tn),
        in_specs=[
            pl.BlockSpec((8, D_MODEL), lambda l, j: (0, 0)),
            pl.BlockSpec((None, D_MODEL, tn), lambda l, j: (l, 0, j)),
            pl.BlockSpec((None, 1, tn), lambda l, j: (l, 0, j)),
        ],
        out_specs=pl.BlockSpec((None, 8, tn), lambda l, j: (l, 0, j)),
        out_shape=jax.ShapeDtypeStruct((L, 8, n), F32),
        compiler_params=_cparams("parallel", "parallel"),
        name="ada_mod",
    )(cond, ada_w, ada_b.reshape(L, 1, n))


def _proj_in_kernel(x_ref, sc_ref, sh_ref, nw_ref, w_ref, z_ref):
    h = _rms(x_ref[...], nw_ref[...]) * (1.0 + sc_ref[...]) + sh_ref[...]
    z_ref[...] = _dot(h.astype(BF16), w_ref[...])


def _mod_map(bm):
    if bm == 1:
        return lambda b, *_: (0, 0, 0)
    return lambda b, *_: (b, 0, 0)


def _proj_in(x, sc, sh, nw, w_p):
    B, T, _ = x.shape
    tm = min(T, 512)
    return pl.pallas_call(
        _proj_in_kernel,
        grid=(B, T // tm),
        in_specs=[
            pl.BlockSpec((None, tm, D_MODEL), lambda b, i: (b, i, 0)),
            pl.BlockSpec((None, 1, D_MODEL), _mod_map(sc.shape[0])),
            pl.BlockSpec((None, 1, D_MODEL), _mod_map(sh.shape[0])),
            pl.BlockSpec((1, D_MODEL), lambda b, i: (0, 0)),
            pl.BlockSpec((D_MODEL, Z_COLS), lambda b, i: (0, 0)),
        ],
        out_specs=pl.BlockSpec((None, tm, Z_COLS), lambda b, i: (b, i, 0)),
        out_shape=jax.ShapeDtypeStruct((B, T, Z_COLS), F32),
        compiler_params=_cparams("parallel", "parallel"),
        name="proj_in",
    )(x, sc, sh, nw, w_p)


def _ret_steps(di, q_ref, k_ref, v_ref, o_ref, s_ref, lc):
    rev = di == 1
    rows = lax.broadcasted_iota(jnp.int32, (lc, lc), 0)
    cols = lax.broadcasted_iota(jnp.int32, (lc, lc), 1)
    rel = ((cols - rows) if rev else (rows - cols)).astype(F32)
    idx = lax.broadcasted_iota(jnp.int32, (lc, 1), 0)
    pos = ((lc - 1 - idx) if rev else idx).astype(F32)
    for h in range(N_HEADS):
        lg = LOG_GAMMA[di][h]
        sl = slice(h * HEAD_D, (h + 1) * HEAD_D)
        q = q_ref[:, sl]
        k = k_ref[:, sl] * (HEAD_D ** -0.5)
        v = v_ref[:, sl].astype(BF16)
        att = _dot_nt(q.astype(BF16), k.astype(BF16))
        att = att * jnp.where(rel >= 0.0, jnp.exp(jnp.maximum(rel, 0.0) * lg), 0.0)
        qd = q * jnp.exp((pos + 1.0) * lg)
        kd = k * jnp.exp((lc - 1.0 - pos) * lg)
        s = s_ref[di, h]
        o_ref[:, sl] = _dot(att.astype(BF16), v) + _dot(qd.astype(BF16), s.astype(BF16))
        s_ref[di, h] = s * math.exp(lc * lg) + _dot_tn(kd.astype(BF16), v)
        yield


def _s5_zoh_kernel(lre_ref, lim_ref, ldt_ref, bre_ref, bim_ref, are_ref, aim_ref, bbre_ref, bbim_ref):
    lam_re = jnp.minimum(lre_ref[...], -1e-4)
    lam_im = lim_ref[...]
    dt = jnp.exp(ldt_ref[...])
    mag = jnp.exp(dt * lam_re)
    ang = dt * lam_im
    a_re = mag * jnp.cos(ang)
    a_im = mag * jnp.sin(ang)
    den = lam_re * lam_re + lam_im * lam_im
    f_re = ((a_re - 1.0) * lam_re + a_im * lam_im) / den
    f_im = (a_im * lam_re - (a_re - 1.0) * lam_im) / den
    b_re = bre_ref[...]
    b_im = bim_ref[...]
    are_ref[...] = a_re
    aim_ref[...] = a_im
    bbre_ref[...] = f_re * b_re - f_im * b_im
    bbim_ref[...] = f_re * b_im + f_im * b_re


def _s5_zoh(lam_re, lam_im, log_dt, b_re, b_im):
    n = 2 * S5_STATE
    col = lambda t: t.reshape(n, 1)
    ldt = jnp.broadcast_to(log_dt[:, :, None], (2, S5_GROUPS, S5_P))
    outs = pl.pallas_call(
        _s5_zoh_kernel,
        out_shape=[jax.ShapeDtypeStruct((n, 1), F32), jax.ShapeDtypeStruct((n, 1), F32),
                   jax.ShapeDtypeStruct((n, S5_CH), F32), jax.ShapeDtypeStruct((n, S5_CH), F32)],
        name="s5_zoh",
    )(col(lam_re), col(lam_im), col(ldt), b_re.reshape(n, S5_CH), b_im.reshape(n, S5_CH))
    a_re, a_im, bb_re, bb_im = outs
    shp = (2, S5_GROUPS, S5_P, S5_CH)
    return a_re.reshape(2, 1, S5_STATE), a_im.reshape(2, 1, S5_STATE), bb_re.reshape(shp), bb_im.reshape(shp)


S5_RADIX = 16


def _ret_s5_kernel(*refs, L, nc):
    (uf_ref, ub_ref, are_ref, aim_ref, bre_ref, bim_ref, cre_ref, cim_ref, x0r_ref, x0i_ref,
     qf_ref, kf_ref, vf_ref, qb_ref, kb_ref, vb_ref, rs0_ref,
     yf_ref, yb_ref, fr_ref, fi_ref, rof_ref, rob_ref, rfin_ref,
     car_re, car_im) = refs[:26]
    bufs, rs_ref = refs[26:34], refs[34]
    c = pl.program_id(1)
    R = S5_RADIX
    G = L // R

    @pl.when(c == 0)
    def _():
        car_re[...] = x0r_ref[...]
        car_im[...] = x0i_ref[...]
        rs_ref[...] = rs0_ref[...]

    rr = lax.broadcasted_iota(jnp.int32, (L, L), 0)
    cc = lax.broadcasted_iota(jnp.int32, (L, L), 1)
    lg, lr = G.bit_length() - 1, R.bit_length() - 1
    perm = jnp.where(cc == ((rr & (G - 1)) << lr) + (rr >> lg), 1.0, 0.0).astype(BF16)
    unperm = jnp.where(cc == ((rr & (R - 1)) << lg) + (rr >> lr), 1.0, 0.0).astype(BF16)

    def cmul(pr, pi, qr, qi):
        return pr * qr - pi * qi, pr * qi + pi * qr

    def steps(di, u_ref, y_ref, x_re, x_im, o_re, o_im):
        rev = di == 1
        u = _dot(perm, u_ref[...].astype(BF16)).astype(BF16)
        x_re[...] = _dot(u, bre_ref[di])
        x_im[...] = _dot(u, bim_ref[di])
        yield
        ar = are_ref[di]
        ai = aim_ref[di]
        order = list(range(R - 1, -1, -1)) if rev else list(range(R))
        slab = lambda j: slice(j * G, (j + 1) * G)

        er = x_re[slab(order[0]), :]
        ei = x_im[slab(order[0]), :]
        for n, j in enumerate(order[1:]):
            tr, ti = cmul(ar, ai, er, ei)
            er = tr + x_re[slab(j), :]
            ei = ti + x_im[slab(j), :]
            if n % 4 == 3:
                yield

        a_r, a_i = ar, ai
        for _ in range(R.bit_length() - 1):
            a_r, a_i = cmul(a_r, a_i, a_r, a_i)
        zr = car_re[di]
        zi = car_im[di]
        cin_r = [None] * G
        cin_i = [None] * G
        for k in (range(G - 1, -1, -1) if rev else range(G)):
            cin_r[k] = zr
            cin_i[k] = zi
            tr, ti = cmul(a_r, a_i, zr, zi)
            zr = tr + er[k:k + 1, :]
            zi = ti + ei[k:k + 1, :]
        car_re[di] = zr
        car_im[di] = zi
        yield

        xr = jnp.concatenate(cin_r, axis=0)
        xi = jnp.concatenate(cin_i, axis=0)
        for n, j in enumerate(order):
            tr, ti = cmul(ar, ai, xr, xi)
            xr = tr + x_re[slab(j), :]
            xi = ti + x_im[slab(j), :]
            o_re[slab(j), :] = xr
            o_im[slab(j), :] = xi
            if n % 4 == 3:
                yield
        xr_all = _dot(unperm, o_re[...].astype(BF16)).astype(BF16)
        xi_all = _dot(unperm, o_im[...].astype(BF16)).astype(BF16)
        y_ref[...] = _dot(xr_all, cre_ref[...]) - _dot(xi_all, cim_ref[...])

    live = [steps(0, uf_ref, yf_ref, *bufs[0:4]), steps(1, ub_ref, yb_ref, *bufs[4:8]),
            _ret_steps(0, qf_ref, kf_ref, vf_ref, rof_ref, rs_ref, L),
            _ret_steps(1, qb_ref, kb_ref, vb_ref, rob_ref, rs_ref, L)]
    while live:
        for gen in list(live):
            if next(gen, StopIteration) is StopIteration:
                live.remove(gen)

    @pl.when(c == nc - 1)
    def _():
        fr_ref[...] = car_re[...]
        fi_ref[...] = car_im[...]
        rfin_ref[...] = rs_ref[...]


def _ret_s5(z, a_re, a_im, bblk_re, bblk_im, cblk_re, cblk_im, x0_re, x0_im, s0_ret):
    B, T, _ = z.shape
    L = min(T, 128)
    nc = T // L
    assert L % S5_RADIX == 0 and (L // S5_RADIX) & (L // S5_RADIX - 1) == 0
    full = lambda shape: pl.BlockSpec(shape, lambda b, c: (0,) * len(shape))
    sspec = pl.BlockSpec((None, 2, 1, S5_STATE), lambda b, c: (b, 0, 0, 0))
    uspec = lambda cmap, blk: pl.BlockSpec((None, L, MIX_W), lambda b, c: (b, cmap(c), blk))
    fwd = lambda c: c
    bwd = lambda c: nc - 1 - c
    buf = pltpu.VMEM((L, S5_STATE), F32)
    car = pltpu.VMEM((2, 1, S5_STATE), F32)
    seq = jax.ShapeDtypeStruct((B, T, MIX_W), F32)
    state = jax.ShapeDtypeStruct((B, 2, 1, S5_STATE), F32)
    rshape = (2, N_HEADS, HEAD_D, HEAD_D)
    rspec = pl.BlockSpec((None,) + rshape, lambda b, c: (b, 0, 0, 0, 0))
    return pl.pallas_call(
        functools.partial(_ret_s5_kernel, L=L, nc=nc),
        grid=(B, nc),
        in_specs=[
            uspec(fwd, ZB_S5), uspec(bwd, ZB_S5),
            full((2, 1, S5_STATE)), full((2, 1, S5_STATE)),
            full((2, MIX_W, S5_STATE)), full((2, MIX_W, S5_STATE)),
            full((S5_STATE, MIX_W)), full((S5_STATE, MIX_W)),
            sspec, sspec,
            uspec(fwd, ZB_RET), uspec(fwd, ZB_RET + 1), uspec(fwd, ZB_RET + 2),
            uspec(bwd, ZB_RET), uspec(bwd, ZB_RET + 1), uspec(bwd, ZB_RET + 2),
            rspec,
        ],
        out_specs=[uspec(fwd, 0), uspec(bwd, 0), sspec, sspec, uspec(fwd, 0), uspec(bwd, 0), rspec],
        out_shape=[seq, seq, state, state, seq, seq, jax.ShapeDtypeStruct((B,) + rshape, F32)],
        scratch_shapes=[car, car] + [buf] * 8 + [pltpu.VMEM(rshape, F32)],
        compiler_params=_cparams("parallel", "arbitrary"),
        name="ret_s5",
    )(z, z, a_re, a_im, bblk_re, bblk_im, cblk_re, cblk_im, x0_re, x0_im, z, z, z, z, z, z, s0_ret)


GLA_CHUNK = 16
GLA_CHUNK_SHIFT = 4
SUB = 8


def _gla_steps(q_ref, zf_ref, v_ref, lb_ref, s0_ref, o_ref, fin_ref, s_ref, a_ref, qs_ref, key_ref,
               *, rev, tb, nt, unrolled):
    jb = pl.program_id(1)
    ch = GLA_CHUNK
    half = 2 * HEAD_D

    @pl.when(jb == 0)
    def _():
        s_ref[...] = s0_ref[...]

    lb = lb_ref[...]
    zf = zf_ref[...]
    q = q_ref[...]
    qs_ref[...] = q * jax.nn.sigmoid(q)
    key_ref[...] = (1.0 - lb) * jax.nn.sigmoid(-zf)
    l1 = jnp.log(lb)
    l2 = jnp.log1p(-lb) + jnp.minimum(zf, 0.0) - jnp.log1p(jnp.exp(-jnp.abs(zf)))
    lf = jnp.maximum(l1, l2) + jnp.log1p(jnp.exp(-jnp.abs(l1 - l2)))
    r = lax.broadcasted_iota(jnp.int32, (tb, tb), 0)
    c = lax.broadcasted_iota(jnp.int32, (tb, tb), 1)
    same = (r >> GLA_CHUNK_SHIFT) == (c >> GLA_CHUNK_SHIFT)
    tri = jnp.where(same & ((c >= r) if rev else (c <= r)), 1.0, 0.0).astype(BF16)
    p1 = lf.astype(BF16)
    r1 = lf - p1.astype(F32)
    p2 = r1.astype(BF16)
    p3 = (r1 - p2.astype(F32)).astype(BF16)
    a_ref[...] = _dot(tri, p1) + _dot(tri, p2) + _dot(tri, p3)

    rows = lax.broadcasted_iota(jnp.int32, (SUB, half), 0)
    lo = lax.broadcasted_iota(jnp.int32, (SUB, half), 1) < HEAD_D
    same_head = ((lax.broadcasted_iota(jnp.int32, (half, half), 0) >> HEAD_SHIFT)
                 == (lax.broadcasted_iota(jnp.int32, (half, half), 1) >> HEAD_SHIFT))

    def pair_sums(p):
        s0 = jnp.sum(jnp.where(lo, p, 0.0), axis=1, keepdims=True)
        s1 = jnp.sum(jnp.where(lo, 0.0, p), axis=1, keepdims=True)
        return jnp.where(lo, s0, s1)

    def chunk(ci, carry):
        t0 = (tb // ch - 1 - ci if rev else ci) * ch
        win = pl.ds(t0 if isinstance(ci, int) else pl.multiple_of(t0, ch), ch)
        a = a_ref[win, :]
        qc = qs_ref[win, :]
        kc = key_ref[win, :]
        vc = v_ref[win, :]
        last = 0 if rev else ch - 1
        a_last = a[last:last + 1, :]
        qe = (qc * jnp.exp(a)).astype(BF16)
        ke = (kc * jnp.exp(a_last - a)).astype(BF16)
        ea = jnp.exp(a_last)
        vb = vc.astype(BF16)
        nslab = ch // SUB
        acc = [[jnp.zeros((SUB, half), F32) for _ in range(2)] for _ in range(nslab)]
        for jj in range(ch):
            for sb in range(nslab):
                r0 = sb * SUB
                if (r0 > jj) if rev else (r0 + SUB - 1 < jj):
                    continue
                whole = (r0 + SUB - 1 <= jj) if rev else (r0 >= jj)
                rs = slice(r0, r0 + SUB)
                valid = (rows + r0 <= jj) if rev else (rows + r0 >= jj)
                for hp in range(2):
                    sl = slice(hp * half, (hp + 1) * half)
                    dec = jnp.exp(jnp.minimum(a[rs, sl] - a[jj:jj + 1, sl], 0.0))
                    p = qc[rs, sl] * dec * kc[jj:jj + 1, sl]
                    if not whole:
                        p = jnp.where(valid, p, 0.0)
                    acc[sb][hp] = acc[sb][hp] + pair_sums(p) * vc[jj:jj + 1, sl]
        for hp in range(2):
            sl = slice(hp * half, (hp + 1) * half)
            s = s_ref[hp]
            inter = _dot_nt(qe[:, sl], s.astype(BF16))
            intra = jnp.concatenate([acc[sb][hp] for sb in range(nslab)], axis=0)
            o_ref[win, sl] = inter + intra
            s_ref[hp] = s * ea[:, sl] + jnp.where(same_head, _dot_tn(vb[:, sl], ke[:, sl]), 0.0)
        return carry

    yield
    if unrolled:
        for ci in range(tb // ch):
            chunk(ci, 0)
            yield
    else:
        lax.fori_loop(0, tb // ch, chunk, 0, unroll=2)

    @pl.when(jb == nt - 1)
    def _():
        fin_ref[...] = s_ref[...]


def _gla_kernel(*refs, rev, tb, nt):
    for _ in _gla_steps(*refs, rev=rev, tb=tb, nt=nt, unrolled=False):
        pass


def _gla(z, lb, s0, rev):
    B, T, _ = z.shape
    pair = 2 * HEAD_D
    s0 = s0.reshape(B, 2, 2, HEAD_D, HEAD_D)
    zero = jnp.zeros_like(s0[:, :, 0])
    s0 = jnp.concatenate([jnp.concatenate([s0[:, :, 0], zero], axis=-1),
                          jnp.concatenate([zero, s0[:, :, 1]], axis=-1)], axis=-2)
    tb = min(T, 256)
    nt = T // tb
    di = 1 if rev else 0
    tmap = (lambda j: nt - 1 - j) if rev else (lambda j: j)
    zspec = lambda blk: pl.BlockSpec((None, tb, MIX_W), lambda b, j: (b, tmap(j), blk))
    sspec = pl.BlockSpec((None, 2, pair, pair), lambda b, j: (b, 0, 0, 0))
    blk = pltpu.VMEM((tb, MIX_W), F32)
    o, fin = pl.pallas_call(
        functools.partial(_gla_kernel, rev=rev, tb=tb, nt=nt),
        grid=(B, nt),
        in_specs=[zspec(ZB_HG), zspec(ZB_HG + 1 + di), zspec(ZB_HG + 3),
                  pl.BlockSpec((1, MIX_W), lambda b, j: (0, 0)), sspec],
        out_specs=[pl.BlockSpec((None, tb, MIX_W), lambda b, j: (b, tmap(j), 0)), sspec],
        out_shape=[jax.ShapeDtypeStruct((B, T, MIX_W), F32),
                   jax.ShapeDtypeStruct((B, 2, pair, pair), F32)],
        scratch_shapes=[pltpu.VMEM((2, pair, pair), F32), blk, blk, blk],
        compiler_params=_cparams("parallel", "arbitrary"),
        name="hgrn_gla_bwd" if rev else "hgrn_gla_fwd",
    )(z, z, z, lb[di:di + 1], s0)
    fin = jnp.stack([fin[:, :, :HEAD_D, :HEAD_D], fin[:, :, HEAD_D:, HEAD_D:]], axis=2)
    return o, fin.reshape(B, N_HEADS, HEAD_D, HEAD_D)


def _rw_prep_kernel(*refs, grid_shift, tm):
    if grid_shift:
        (zc_ref, zu_ref, zd_ref, mu_ref, vec_ref, w0_ref, lora_ref,
         r_ref, k2_ref, v_ref, w_ref, nkk_ref, kka_ref, g_ref, bonus_ref, buf_ref) = refs
    else:
        (zc_ref, mu_ref, vec_ref, w0_ref, lora_ref,
         r_ref, k2_ref, v_ref, w_ref, nkk_ref, kka_ref, g_ref, bonus_ref, buf_ref) = refs
    i = pl.program_id(1)
    nt = pl.num_programs(1)
    halo = GRID_W
    width = 4 * MIX_W
    z = zc_ref[...]
    buf_ref[halo:halo + tm, :] = z
    lane = lax.broadcasted_iota(jnp.int32, (tm, width), 1)
    row = lax.broadcasted_iota(jnp.int32, (tm, width), 0)
    if grid_shift:
        buf_ref[0:halo, :] = jnp.where(i > 0, zu_ref[...], 0.0)
        buf_ref[halo + tm:2 * halo + tm, :] = jnp.where(i < nt - 1, zd_ref[...], 0.0)
        col = row & (GRID_W - 1)
        left = jnp.where(col > 0, buf_ref[halo - 1:halo - 1 + tm, :], 0.0)
        right = jnp.where(col < GRID_W - 1, buf_ref[halo + 1:halo + 1 + tm, :], 0.0)
        up = buf_ref[0:tm, :]
        down = buf_ref[2 * halo:2 * halo + tm, :]
        sel = lane & 3
        shifted = jnp.where(sel == 0, left, jnp.where(sel == 1, right, jnp.where(sel == 2, up, down)))
    else:
        zrow = jnp.zeros((1, width), F32)
        buf_ref[halo - 1:halo, :] = zrow
        buf_ref[halo + tm:halo + tm + 1, :] = zrow
        prev = buf_ref[halo - 1:halo - 1 + tm, :]
        nxt = buf_ref[halo + 1:halo + 1 + tm, :]
        shifted = jnp.where((lane & 1) == 0, prev, nxt)
    zs = z + mu_ref[...] * (shifted - z)
    r = zs[:, 0:MIX_W]
    k = zs[:, MIX_W:2 * MIX_W]
    v = zs[:, 2 * MIX_W:3 * MIX_W]
    sm = zs[:, 3 * MIX_W:4 * MIX_W]
    ones_bd = _head_ones()
    a0, k_k, k_a, r_k = (vec_ref[j:j + 1, :] for j in range(4))
    a = jax.nn.sigmoid(a0 + _dot(sm.astype(BF16), lora_ref[2]))
    g_ref[...] = _dot(jax.nn.sigmoid(sm).astype(BF16), lora_ref[3])
    kk = k * k_k
    kk = kk * lax.rsqrt(_head_sum(kk * kk, ones_bd) + 1e-12)
    k2 = k * (1.0 + (a - 1.0) * k_a)
    th = jnp.tanh(sm).astype(BF16)
    for di in range(2):
        w_ref[di] = -math.exp(-0.5) * jax.nn.sigmoid(w0_ref[di:di + 1, :] + _dot(th, lora_ref[di]))
    r_ref[...] = r
    k2_ref[...] = k2
    v_ref[...] = v
    nkk_ref[...] = -kk
    kka_ref[...] = kk * a
    bonus_ref[...] = _head_sum(r * k2 * r_k, ones_bd) * v


def _rw_prep(z, mu_p, vecs, w0, lora, grid_shift):
    B, T, _ = z.shape
    width = 4 * MIX_W
    wblk = ZB_RW // 4
    if grid_shift:
        tm = min(T, 512)
        hb = tm // GRID_W
        nh = T // GRID_W
        z_specs = [
            pl.BlockSpec((None, tm, width), lambda b, i: (b, i, wblk)),
            pl.BlockSpec((None, GRID_W, width), lambda b, i: (b, jnp.maximum(i * hb - 1, 0), wblk)),
            pl.BlockSpec((None, GRID_W, width), lambda b, i: (b, jnp.minimum((i + 1) * hb, nh - 1), wblk)),
        ]
        z_args = (z, z, z)
    else:
        tm = T
        z_specs = [pl.BlockSpec((None, tm, width), lambda b, i: (b, i, wblk))]
        z_args = (z,)
    const = lambda shape: pl.BlockSpec(shape, lambda b, i: (0,) * len(shape))
    ospec = pl.BlockSpec((None, tm, MIX_W), lambda b, i: (b, i, 0))
    oshape = jax.ShapeDtypeStruct((B, T, MIX_W), F32)
    dspec = pl.BlockSpec((2, None, tm, MIX_W), lambda b, i: (0, b, i, 0))
    dshape = jax.ShapeDtypeStruct((2, B, T, MIX_W), F32)
    return pl.pallas_call(
        functools.partial(_rw_prep_kernel, grid_shift=grid_shift, tm=tm),
        grid=(B, T // tm),
        in_specs=z_specs + [const((1, width)), const((4, MIX_W)), const((2, MIX_W)),
                            const((4, MIX_W, MIX_W))],
        out_specs=[ospec, ospec, ospec, dspec, ospec, ospec, ospec, ospec],
        out_shape=[oshape, oshape, oshape, dshape, oshape, oshape, oshape, oshape],
        scratch_shapes=[pltpu.VMEM((tm + 2 * GRID_W, width), F32)],
        compiler_params=_cparams("parallel", "parallel"),
        name="rwkv_prep",
    )(*z_args, mu_p, vecs, w0, lora)


RW_CHUNK = 64
RW_CHUNK_SHIFT = 6


def _mm(a, b):
    return jnp.dot(a.astype(BF16), b.astype(BF16), preferred_element_type=F32)


def _rwkv_kernel(*refs, rev, tb, nt):
    for _ in _rwkv_steps(*refs, rev=rev, tb=tb, nt=nt):
        pass


def _rwkv_steps(lw_ref, a_ref, b_ref, k_ref, r_ref, v_ref, s0_ref, y_ref, fin_ref, s_ref, g_ref,
                *, rev, tb, nt):
    jb = pl.program_id(1)
    ch = RW_CHUNK
    pair = 2 * HEAD_D

    @pl.when(jb == 0)
    def _():
        s_ref[...] = s0_ref[...]

    lw = lw_ref[...]
    rr = lax.broadcasted_iota(jnp.int32, (tb, tb), 0)
    cc = lax.broadcasted_iota(jnp.int32, (tb, tb), 1)
    same = (rr >> RW_CHUNK_SHIFT) == (cc >> RW_CHUNK_SHIFT)
    tri = jnp.where(same & ((cc >= rr) if rev else (cc <= rr)), 1.0, 0.0).astype(BF16)
    p1 = lw.astype(BF16)
    r1 = lw - p1.astype(F32)
    p2 = r1.astype(BF16)
    p3 = (r1 - p2.astype(F32)).astype(BF16)
    g_ref[...] = _dot(tri, p1) + _dot(tri, p2) + _dot(tri, p3)

    si = lax.broadcasted_iota(jnp.int32, (ch, ch), 0)
    ri = lax.broadcasted_iota(jnp.int32, (ch, ch), 1)
    if rev:
        si, ri = ch - 1 - si, ch - 1 - ri
    strict = ri < si
    incl = ri <= si
    eye = jnp.where(ri == si, 1.0, 0.0).astype(F32)
    levels = [((si >> (lv + 1)) == (ri >> (lv + 1))) & (((si >> lv) & 1) == 1) & (((ri >> lv) & 1) == 0)
              for lv in range(RW_CHUNK_SHIFT)]
    lo = lax.broadcasted_iota(jnp.int32, (ch, pair), 1) < HEAD_D
    same_head = ((lax.broadcasted_iota(jnp.int32, (pair, pair), 0) >> HEAD_SHIFT)
                 == (lax.broadcasted_iota(jnp.int32, (pair, pair), 1) >> HEAD_SHIFT))

    nch = tb // ch
    g = g_ref[...]
    e_g = jnp.exp(g)
    e_ng = jnp.exp(-g)
    at_f = a_ref[...] * jnp.exp(g - lw)
    rt_f = r_ref[...] * e_g
    at = at_f.astype(BF16)
    rt = rt_f.astype(BF16)
    first = (lax.broadcasted_iota(jnp.int32, (tb, MIX_W), 1) & HEAD_D) == 0
    at_h = [jnp.where(first, at_f, 0.0).astype(BF16), jnp.where(first, 0.0, at_f).astype(BF16)]
    rt_h = [jnp.where(first, rt_f, 0.0).astype(BF16), jnp.where(first, 0.0, rt_f).astype(BF16)]
    bt = (b_ref[...] * e_ng).astype(BF16)
    kt = (k_ref[...] * e_ng).astype(BF16)
    vb = v_ref[...].astype(BF16)

    items = [(c, hp, hh) for c in range(nch) for hp in range(2) for hh in range(2)]
    rows = lambda c: slice(c * ch, (c + 1) * ch)
    lanes = lambda hp: slice(hp * pair, (hp + 1) * pair)
    n_m, p_m, m_m, q_m = {}, {}, {}, {}
    for it in items:
        c, hp, hh = it
        ar = jnp.concatenate([at_h[hh][rows(c), lanes(hp)], rt_h[hh][rows(c), lanes(hp)]], axis=0)
        np_ = _dot_nt(ar, bt[rows(c), lanes(hp)])
        mq = _dot_nt(ar, kt[rows(c), lanes(hp)])
        n_m[it] = jnp.where(strict, np_[:ch], 0.0).astype(BF16)
        p_m[it] = jnp.where(incl, np_[ch:], 0.0).astype(BF16)
        m_m[it] = jnp.where(strict, mq[:ch], 0.0).astype(BF16)
        q_m[it] = jnp.where(incl, mq[ch:], 0.0).astype(BF16)
        if hp == 1 and hh == 1:
            yield
    t_m = {it: eye + jnp.where(levels[0], n_m[it].astype(F32), 0.0) for it in items}
    for lv in range(1, RW_CHUNK_SHIFT):
        tn = {it: _mm(t_m[it], n_m[it]) for it in items}
        yield
        t_m = {it: t_m[it] + jnp.where(levels[lv], _mm(tn[it], t_m[it]), 0.0) for it in items}
        yield
    mv = {it: _mm(m_m[it], vb[rows(it[0]), lanes(it[1])]) for it in items}
    qv = {it: _mm(q_m[it], vb[rows(it[0]), lanes(it[1])]) for it in items}
    yield
    y2 = {it: _mm(t_m[it], jnp.concatenate([at[rows(it[0]), lanes(it[1])], mv[it].astype(BF16)], axis=1))
          for it in items}
    yield

    last = 0 if rev else ch - 1
    for ci in range(nch):
        c = nch - 1 - ci if rev else ci
        g_c = g[c * ch + last:c * ch + last + 1, :]
        e_gc = jnp.exp(g_c - g[rows(c), :])
        bh = (b_ref[rows(c), :] * e_gc).astype(BF16)
        kh = (k_ref[rows(c), :] * e_gc).astype(BF16)
        dec_c = jnp.exp(g_c)
        xs, sas, ss = [], [], []
        for hp in range(2):
            s = s_ref[hp]
            w2 = jnp.where(lo, y2[(c, hp, 0)][:, :pair], y2[(c, hp, 1)][:, :pair]).astype(BF16)
            xs.append(_dot_nt(jnp.concatenate([w2, rt[rows(c), lanes(hp)]], axis=0), s.astype(BF16)))
            ss.append(s)
        for hp in range(2):
            w1 = jnp.where(lo, y2[(c, hp, 0)][:, pair:], y2[(c, hp, 1)][:, pair:])
            sas.append(w1 + xs[hp][:ch])
        for hp in range(2):
            sa = sas[hp]
            y_ref[rows(c), lanes(hp)] = xs[hp][ch:] + jnp.where(
                lo, _mm(p_m[(c, hp, 0)], sa) + qv[(c, hp, 0)], _mm(p_m[(c, hp, 1)], sa) + qv[(c, hp, 1)])
            upd = _dot_tn(jnp.concatenate([sa.astype(BF16), vb[rows(c), lanes(hp)]], axis=0),
                          jnp.concatenate([bh[:, lanes(hp)], kh[:, lanes(hp)]], axis=0))
            s_ref[hp] = ss[hp] * dec_c[:, lanes(hp)] + jnp.where(same_head, upd, 0.0)
        yield

    @pl.when(jb == nt - 1)
    def _():
        fin_ref[...] = s_ref[...]


def _hg_rw_kernel(*refs, rev, tb, nt):
    gla_in, rw_in = refs[0:5], refs[5:12]
    gla_out, rw_out = refs[12:14], refs[14:16]
    gla_scr, rw_scr = refs[16:20], refs[20:22]
    gla = _gla_steps(*gla_in, *gla_out, *gla_scr, rev=rev, tb=tb, nt=nt, unrolled=True)
    rwkv = _rwkv_steps(*rw_in, *rw_out, *rw_scr, rev=rev, tb=tb, nt=nt)
    live = [gla, rwkv]
    while live:
        for gen in list(live):
            if next(gen, StopIteration) is StopIteration:
                live.remove(gen)


def _head_pair_blockdiag(s):
    B = s.shape[0]
    s = s.reshape(B, 2, 2, HEAD_D, HEAD_D)
    zero = jnp.zeros_like(s[:, :, 0])
    return jnp.concatenate([jnp.concatenate([s[:, :, 0], zero], axis=-1),
                            jnp.concatenate([zero, s[:, :, 1]], axis=-1)], axis=-2)


def _head_pair_blocks(t):
    B = t.shape[0]
    t = jnp.stack([t[:, :, :HEAD_D, :HEAD_D], t[:, :, HEAD_D:, HEAD_D:]], axis=2)
    return t.reshape(B, N_HEADS, HEAD_D, HEAD_D)


def _rwkv(lw, a, b, k, r, v, s0, rev):
    B, T, _ = v.shape
    pair = 2 * HEAD_D
    tb = min(T, 256)
    nt = T // tb
    di = 1 if rev else 0
    tmap = (lambda j: nt - 1 - j) if rev else (lambda j: j)
    spec = pl.BlockSpec((None, tb, MIX_W), lambda bi, j: (bi, tmap(j), 0))
    sspec = pl.BlockSpec((None, 2, pair, pair), lambda bi, j: (bi, 0, 0, 0))
    y, fin = pl.pallas_call(
        functools.partial(_rwkv_kernel, rev=rev, tb=tb, nt=nt),
        grid=(B, nt),
        in_specs=[pl.BlockSpec((None, None, tb, MIX_W), lambda bi, j: (di, bi, tmap(j), 0)),
                  spec, spec, spec, spec, spec, sspec],
        out_specs=[spec, sspec],
        out_shape=[jax.ShapeDtypeStruct((B, T, MIX_W), F32),
                   jax.ShapeDtypeStruct((B, 2, pair, pair), F32)],
        scratch_shapes=[pltpu.VMEM((2, pair, pair), F32), pltpu.VMEM((tb, MIX_W), F32)],
        compiler_params=_cparams("parallel", "arbitrary"),
        name="rwkv_chunk_bwd" if rev else "rwkv_chunk_fwd",
    )(lw, a, b, k, r, v, _head_pair_blockdiag(s0))
    return y, _head_pair_blocks(fin)


def _hg_rw(z, lb, s0_hg, lw, a, b, k, r, v, s0_rw, rev):
    B, T, _ = v.shape
    pair = 2 * HEAD_D
    tb = min(T, 256)
    nt = T // tb
    di = 1 if rev else 0
    tmap = (lambda j: nt - 1 - j) if rev else (lambda j: j)
    zspec = lambda blk: pl.BlockSpec((None, tb, MIX_W), lambda bi, j: (bi, tmap(j), blk))
    spec = zspec(0)
    sspec = pl.BlockSpec((None, 2, pair, pair), lambda bi, j: (bi, 0, 0, 0))
    state = jax.ShapeDtypeStruct((B, 2, pair, pair), F32)
    seq = jax.ShapeDtypeStruct((B, T, MIX_W), F32)
    blk = pltpu.VMEM((tb, MIX_W), F32)
    st = pltpu.VMEM((2, pair, pair), F32)
    o, fin_hg, y, fin_rw = pl.pallas_call(
        functools.partial(_hg_rw_kernel, rev=rev, tb=tb, nt=nt),
        grid=(B, nt),
        in_specs=[zspec(ZB_HG), zspec(ZB_HG + 1 + di), zspec(ZB_HG + 3),
                  pl.BlockSpec((1, MIX_W), lambda bi, j: (0, 0)), sspec,
                  pl.BlockSpec((None, None, tb, MIX_W), lambda bi, j: (di, bi, tmap(j), 0)),
                  spec, spec, spec, spec, spec, sspec],
        out_specs=[spec, sspec, spec, sspec],
        out_shape=[seq, state, seq, state],
        scratch_shapes=[st, blk, blk, blk, st, blk],
        compiler_params=_cparams("parallel", "arbitrary"),
        name="hgrn_rwkv_bwd" if rev else "hgrn_rwkv_fwd",
    )(z, z, z, lb[di:di + 1], _head_pair_blockdiag(s0_hg),
      lw, a, b, k, r, v, _head_pair_blockdiag(s0_rw))
    return o, _head_pair_blocks(fin_hg), y, _head_pair_blocks(fin_rw)


def _mix_kernel(x_ref, sc_ref, sh_ref, g1_ref, nw0_ref, nw_ref, wg_ref,
                retf_ref, retb_ref, retg_ref, s5f_ref, s5b_ref, s5u_ref, hgf_ref, hgb_ref, hgg_ref, rwf_ref, rwr_ref,
                rwb_ref, rwg_ref,
                vec_ref, glu_w_ref, wbr_ref, wout_ref, o_ref):
    ones_bd = _head_ones()
    h = (_rms(x_ref[...], nw0_ref[...]) * (1.0 + sc_ref[...]) + sh_ref[...]).astype(BF16)
    gn_w, s5_d, glu_b, hg_w, ln_w, ln_b = (vec_ref[j:j + 1, :] for j in range(6))

    def group_norm(o, eps):
        mu = _head_sum(o, ones_bd) * (1.0 / HEAD_D)
        oc = o - mu
        var = _head_sum(oc * oc, ones_bd) * (1.0 / HEAD_D)
        return oc * lax.rsqrt(var + eps)

    g = retg_ref[...]
    y_ret = group_norm(retf_ref[...] + retb_ref[...], EPS) * gn_w * (g * jax.nn.sigmoid(g))

    y = s5_d * s5u_ref[...] + s5f_ref[...] + s5b_ref[...]
    yg = jax.nn.gelu(y)
    y_s5 = yg * jax.nn.sigmoid(_dot(yg.astype(BF16), glu_w_ref[...]) + glu_b)

    o = hgf_ref[...] + hgb_ref[...]
    g = hgg_ref[...]
    ms = _head_sum(o * o, ones_bd) * (1.0 / HEAD_D)
    y_hg = o * lax.rsqrt(ms + EPS) * hg_w * (g * jax.nn.sigmoid(g))

    y = group_norm(rwf_ref[...] + rwr_ref[...], RW_LN_EPS) * ln_w + ln_b
    y_rw = (y + rwb_ref[...]) * rwg_ref[...]

    mixed = None
    for m, ym in enumerate((y_ret, y_s5, y_hg, y_rw)):
        br = _dot(ym.astype(BF16), wbr_ref[m])
        term = jax.nn.sigmoid(_dot(h, wg_ref[:, m * D_MODEL:(m + 1) * D_MODEL])) * br
        mixed = term if mixed is None else mixed + term
    mixed = _dot(mixed.astype(BF16), wout_ref[...])
    o_ref[...] = x_ref[...] + g1_ref[...] * _rms(mixed, nw_ref[...])


def _mix(x, sc, sh, g1, nw0, nw, w_gate, z, ret_o, s5_y, hg_o, rw_y, rw_bonus, rw_g, vecs, glu_w,
         w_branch, w_out):
    B, T, _ = x.shape
    tm = min(T, 256)
    xspec = pl.BlockSpec((None, tm, D_MODEL), lambda b, i: (b, i, 0))
    zspec = lambda blk: pl.BlockSpec((None, tm, MIX_W), lambda b, i: (b, i, blk))
    const = lambda shape: pl.BlockSpec(shape, lambda b, i: (0,) * len(shape))
    mspec = lambda m: pl.BlockSpec((None, 1, D_MODEL), _mod_map(m.shape[0]))
    return pl.pallas_call(
        _mix_kernel,
        grid=(B, T // tm),
        in_specs=[
            xspec, mspec(sc), mspec(sh), mspec(g1),
            const((1, D_MODEL)), const((1, D_MODEL)),
            const((D_MODEL, 4 * D_MODEL)),
            zspec(0), zspec(0), zspec(ZB_RET + 3),
            zspec(0), zspec(0), zspec(ZB_S5),
            zspec(0), zspec(0), zspec(ZB_HG + 4),
            zspec(0), zspec(0), zspec(0), zspec(0),
            const((6, MIX_W)), const((MIX_W, MIX_W)),
            const((4, MIX_W, D_MODEL)), const((D_MODEL, D_MODEL)),
        ],
        out_specs=xspec,
        out_shape=jax.ShapeDtypeStruct((B, T, D_MODEL), F32),
        compiler_params=_cparams("parallel", "parallel"),
        name="mix_out",
    )(x, sc, sh, g1, nw0, nw, w_gate, ret_o[0], ret_o[1], z, s5_y[0], s5_y[1], z, hg_o[0], hg_o[1], z, rw_y[0], rw_y[1],
      rw_bonus, rw_g, vecs, glu_w, w_branch, w_out)


FFN_CHUNK = 1024


def _ffn_kernel(x_ref, sc_ref, sh_ref, g2_ref, nw2_ref, nw3_ref, w1_ref, w2_ref, o_ref):
    x = x_ref[...]
    h = (_rms(x, nw2_ref[...]) * (1.0 + sc_ref[...]) + sh_ref[...]).astype(BF16)
    acc = None
    for j in range(D_FF // FFN_CHUNK):
        cols = slice(j * FFN_CHUNK, (j + 1) * FFN_CHUNK)
        a = jnp.maximum(_dot(h, w1_ref[:, cols]), 0.0)
        part = _dot((a * a).astype(BF16), w2_ref[cols, :])
        acc = part if acc is None else acc + part
    o_ref[...] = x + g2_ref[...] * _rms(acc, nw3_ref[...])


def _ffn(x, sc, sh, g2, nw2, nw3, w1, w2):
    B, T, _ = x.shape
    tm = min(T, 512)
    xspec = pl.BlockSpec((None, tm, D_MODEL), lambda b, i: (b, i, 0))
    mspec = lambda m: pl.BlockSpec((None, 1, D_MODEL), _mod_map(m.shape[0]))
    const = pl.BlockSpec((1, D_MODEL), lambda b, i: (0, 0))
    return pl.pallas_call(
        _ffn_kernel,
        grid=(B, T // tm),
        in_specs=[xspec, mspec(sc), mspec(sh), mspec(g2), const, const,
                  pl.BlockSpec((D_MODEL, D_FF), lambda b, i: (0, 0)),
                  pl.BlockSpec((D_FF, D_MODEL), lambda b, i: (0, 0))],
        out_specs=xspec,
        out_shape=jax.ShapeDtypeStruct((B, T, D_MODEL), F32),
        compiler_params=_cparams("parallel", "parallel"),
        name="ffn",
    )(x, sc, sh, g2, nw2, nw3, w1, w2)


def _layer_params(l, p):
    w = p['w_in'][l]
    w_p = jnp.concatenate(
        [w[:, 2560:3488], jnp.zeros((D_MODEL, 4 * MIX_W - RW_COLS), F32),
         w[:, 0:1024], w[:, 1024:1280], w[:, 1280:2560]], axis=1).astype(BF16)
    w_gate = w[:, 3488:7584].astype(BF16)

    a_re, a_im, bb_re, bb_im = _s5_zoh(p['s5_lam_re'][l], p['s5_lam_im'][l], p['s5_log_dt'][l],
                                       p['s5_b_re'][l], p['s5_b_im'][l])
    eye = jnp.eye(S5_GROUPS, dtype=F32)
    bblk = lambda bb: jnp.einsum('dgph,gk->dghkp', bb, eye).reshape(2, MIX_W, S5_STATE).astype(BF16)
    cblk = lambda c: jnp.einsum('ghp,gk->gpkh', c, eye).reshape(S5_STATE, MIX_W).astype(BF16)

    def lora_pad(m, row0):
        return jnp.zeros((MIX_W, MIX_W), F32).at[row0:row0 + m.shape[0]].set(m)

    lora = jnp.stack([lora_pad(p['rw_w2'][l, 0], 0), lora_pad(p['rw_w2'][l, 1], 32),
                      lora_pad(p['rw_a2'][l], 64), lora_pad(p['rw_g2'][l], 96)]).astype(BF16)
    mu_p = jnp.concatenate([p['rw_mu'][l], jnp.zeros((4 * MIX_W - RW_COLS,), F32)]).reshape(1, 4 * MIX_W)
    return dict(
        w_p=w_p, w_gate=w_gate, nw=p['norm_w'][l],
        s5=(a_re, a_im, bblk(bb_re), bblk(bb_im), cblk(p['s5_c_re'][l]), cblk(p['s5_c_im'][l])),
        rw_mu=mu_p, rw_lora=lora, rw_w0=p['rw_w0'][l],
        rw_vecs=jnp.stack([p['rw_a0'][l], p['rw_k_k'][l], p['rw_k_a'][l], p['rw_r_k'][l]]),
        mix_vecs=jnp.stack([p['ret_gn_w'][l], p['s5_d'][l], p['s5_glu_b'][l], p['hg_norm_w'][l],
                            p['rw_ln_w'][l], p['rw_ln_b'][l]]),
        glu_w=p['s5_glu_w'][l].astype(BF16),
        w_branch=p['w_branch'][l].astype(BF16), w_out=p['w_out'][l].astype(BF16),
        ff_w1=p['ff_w1'][l].astype(BF16), ff_w2=p['ff_w2'][l].astype(BF16),
    )


def _trunk_layer(x, mod, init, grid_shift, lp, hg_lb):
    B, T, _ = x.shape
    sh1, sc1, g1, sh2, sc2, g2 = (m[:, None, :] for m in jnp.split(mod, 6, axis=-1))
    nw = lp['nw']
    if mod.shape[0] == 1:
        tok = lambda t: t.reshape(t.shape[:-3] + (1, B * T, t.shape[-1]))
    else:
        tok = lambda t: t
    z = _proj_in(tok(x), sc1, sh1, nw[0:1], lp['w_p']).reshape(B, T, Z_COLS)

    s_ret, s_s5r, s_s5i, s_hg, s_rw = init
    s5_f, s5_b, f_s5r, f_s5i, ret_f, ret_b, f_ret = _ret_s5(
        z, *lp['s5'], s_s5r.reshape(B, 2, 1, S5_STATE), s_s5i.reshape(B, 2, 1, S5_STATE), s_ret)
    f_s5r = f_s5r.reshape(B, 2, S5_GROUPS, S5_P)
    f_s5i = f_s5i.reshape(B, 2, S5_GROUPS, S5_P)

    r, k2, v, w, nkk, kka, rw_g, bonus = _rw_prep(z, lp['rw_mu'], lp['rw_vecs'], lp['rw_w0'],
                                                  lp['rw_lora'], grid_shift)
    s_hg_t = jnp.swapaxes(s_hg, -1, -2)
    hg_f, f_hg_f, rw_f, f_rw_f = _hg_rw(z, hg_lb, s_hg_t[:, 0], w, nkk, kka, k2, r, v, s_rw[:, 0], False)
    hg_b, f_hg_b, rw_b, f_rw_b = _hg_rw(z, hg_lb, s_hg_t[:, 1], w, nkk, kka, k2, r, v, s_rw[:, 1], True)
    f_hg = jnp.swapaxes(jnp.stack([f_hg_f, f_hg_b], axis=1), -1, -2)
    f_rw = jnp.stack([f_rw_f, f_rw_b], axis=1)

    x = _mix(tok(x), sc1, sh1, g1, nw[0:1], nw[1:2], lp['w_gate'], tok(z), (tok(ret_f), tok(ret_b)), (tok(s5_f), tok(s5_b)),
             (tok(hg_f), tok(hg_b)), (tok(rw_f), tok(rw_b)), tok(bonus), tok(rw_g),
             lp['mix_vecs'], lp['glu_w'], lp['w_branch'], lp['w_out'])
    x = _ffn(x, sc2, sh2, g2, nw[2:3], nw[3:4], lp['ff_w1'], lp['ff_w2'])
    return x.reshape(B, T, D_MODEL), (f_ret, f_s5r, f_s5i, f_hg, f_rw)


def kernel(x_prompt, x_sample, state_ret, state_s5_re, state_s5_im, state_hgrn, state_rwkv, c, c_ctx, ada_w, ada_b, norm_w, w_in, ret_gn_w, s5_lam_re, s5_lam_im, s5_log_dt, s5_b_re, s5_b_im, s5_c_re, s5_c_im, s5_d, s5_glu_w, s5_glu_b, hg_lb, hg_norm_w, rw_mu, rw_w0, rw_w2, rw_a0, rw_a2, rw_g2, rw_k_k, rw_k_a, rw_r_k, rw_ln_w, rw_ln_b, w_branch, w_out, ff_w1, ff_w2):
    p = dict(norm_w=norm_w, w_in=w_in, ret_gn_w=ret_gn_w, s5_lam_re=s5_lam_re, s5_lam_im=s5_lam_im,
             s5_log_dt=s5_log_dt, s5_b_re=s5_b_re, s5_b_im=s5_b_im, s5_c_re=s5_c_re, s5_c_im=s5_c_im,
             s5_d=s5_d, s5_glu_w=s5_glu_w, s5_glu_b=s5_glu_b, hg_norm_w=hg_norm_w, rw_mu=rw_mu,
             rw_w0=rw_w0, rw_w2=rw_w2, rw_a0=rw_a0, rw_a2=rw_a2, rw_g2=rw_g2, rw_k_k=rw_k_k,
             rw_k_a=rw_k_a, rw_r_k=rw_r_k, rw_ln_w=rw_ln_w, rw_ln_b=rw_ln_b, w_branch=w_branch,
             w_out=w_out, ff_w1=ff_w1, ff_w2=ff_w2)
    depth = w_in.shape[0]
    n_ctx = x_prompt.shape[0]
    n_lat = x_sample.shape[0]
    assert 1 + n_lat <= 8

    lb_cum = jnp.cumsum(jax.nn.softmax(hg_lb.astype(F32), axis=0), axis=0)
    hg_lower = lb_cum - lb_cum[0]

    cond = jnp.concatenate([c_ctx[None, :], c, jnp.zeros((7 - n_lat, D_MODEL), F32)], axis=0)
    mod = _ada_mod(cond, ada_w, ada_b)

    zero_state = (jnp.zeros((n_ctx, 2, N_HEADS, HEAD_D, HEAD_D), F32),
                  jnp.zeros((n_ctx, 2, S5_GROUPS, S5_P), F32),
                  jnp.zeros((n_ctx, 2, S5_GROUPS, S5_P), F32),
                  jnp.zeros((n_ctx, 2, N_HEADS, HEAD_D, HEAD_D), F32),
                  jnp.zeros((n_ctx, 2, N_HEADS, HEAD_D, HEAD_D), F32))
    xp, xs = x_prompt, x_sample
    finals = []
    for l in range(depth):
        lp = _layer_params(l, p)
        xp, fin = _trunk_layer(xp, mod[l, 0:1], zero_state, False, lp, hg_lower[l])
        finals.append(fin)
        lat_init = (state_ret[:, l], state_s5_re[:, l], state_s5_im[:, l], state_hgrn[:, l], state_rwkv[:, l])
        xs, _ = _trunk_layer(xs, mod[l, 1:1 + n_lat], lat_init, True, lp, hg_lower[l])
    new_states = tuple(jnp.stack([f[i] for f in finals], axis=1) for i in range(5))
    return (xp, xs) + new_states
```

```python
import functools
import math

import jax
import jax.numpy as jnp
from jax import lax
from jax.experimental import pallas as pl
from jax.experimental.pallas import tpu as pltpu

F32 = jnp.float32
BF16 = jnp.bfloat16

D_MODEL = 1024
GRID_W = 64
MIX_W = 256
N_HEADS = 4
HEAD_D = 64
HEAD_SHIFT = 6
LANES = 128
S5_GROUPS = 16
S5_CH = 16
S5_P = 64
S5_STATE = S5_GROUPS * S5_P
D_FF = 4096
EPS = 1e-6
RW_LN_EPS = 64e-5
RW_COLS = 928

Z_COLS = 3584
ZB_RW = 0
ZB_RET = 4
ZB_S5 = 8
ZB_HG = 9

VMEM_LIMIT = 56 * 1024 * 1024

LOG_GAMMA = tuple(
    tuple(math.log1p(-2.0 ** (-(5.0 + 0.5 * di) - h)) for h in range(N_HEADS)) for di in range(2))


def _cparams(*sem):
    return pltpu.CompilerParams(dimension_semantics=sem, vmem_limit_bytes=VMEM_LIMIT)


def _dot(a, b):
    return jnp.dot(a, b, preferred_element_type=F32)


def _dot_nt(a, b):
    return lax.dot_general(a, b, (((1,), (1,)), ((), ())), preferred_element_type=F32)


def _dot_tn(a, b):
    return lax.dot_general(a, b, (((0,), (0,)), ((), ())), preferred_element_type=F32)


def _head_ones():
    r = lax.broadcasted_iota(jnp.int32, (MIX_W, MIX_W), 0) >> HEAD_SHIFT
    c = lax.broadcasted_iota(jnp.int32, (MIX_W, MIX_W), 1) >> HEAD_SHIFT
    return jnp.where(r == c, 1.0, 0.0).astype(F32)


def _head_sum(x, ones_bd):
    ones = ones_bd.astype(BF16)
    p1 = x.astype(BF16)
    r1 = x - p1.astype(F32)
    p2 = r1.astype(BF16)
    p3 = (r1 - p2.astype(F32)).astype(BF16)
    return _dot(p1, ones) + _dot(p2, ones) + _dot(p3, ones)


def _rms(x, w):
    return x * lax.rsqrt(jnp.mean(x * x, axis=-1, keepdims=True) + EPS) * w


def _ada_kernel(c_ref, w_ref, b_ref, o_ref):
    c = c_ref[...]
    s = c * jax.nn.sigmoid(c)
    o_ref[...] = _dot(s.astype(BF16), w_ref[...].astype(BF16)) + b_ref[...]


def _ada_mod(cond, ada_w, ada_b):
    L = ada_w.shape[0]
    n = ada_w.shape[2]
    tn = 1536
    return pl.pallas_call(
        _ada_kernel,
        grid=(L, n // tn),
        in_specs=[
            pl.BlockSpec((8, D_MODEL), lambda l, j: (0, 0)),
            pl.BlockSpec((None, D_MODEL, tn), lambda l, j: (l, 0, j)),
            pl.BlockSpec((None, 1, tn), lambda l, j: (l, 0, j)),
        ],
        out_specs=pl.BlockSpec((None, 8, tn), lambda l, j: (l, 0, j)),
        out_shape=jax.ShapeDtypeStruct((L, 8, n), F32),
        compiler_params=_cparams("parallel", "parallel"),
        name="ada_mod",
    )(cond, ada_w, ada_b.reshape(L, 1, n))


def _proj_in_kernel(x_ref, sc_ref, sh_ref, nw_ref, w_ref, z_ref):
    h = _rms(x_ref[...], nw_ref[...]) * (1.0 + sc_ref[...]) + sh_ref[...]
    z_ref[...] = _dot(h.astype(BF16), w_ref[...])


def _mod_map(bm):
    if bm == 1:
        return lambda b, *_: (0, 0, 0)
    return lambda b, *_: (b, 0, 0)


def _proj_in(x, sc, sh, nw, w_p):
    B, T, _ = x.shape
    tm = min(T, 512)
    return pl.pallas_call(
        _proj_in_kernel,
        grid=(B, T // tm),
        in_specs=[
            pl.BlockSpec((None, tm, D_MODEL), lambda b, i: (b, i, 0)),
            pl.BlockSpec((None, 1, D_MODEL), _mod_map(sc.shape[0])),
            pl.BlockSpec((None, 1, D_MODEL), _mod_map(sh.shape[0])),
            pl.BlockSpec((1, D_MODEL), lambda b, i: (0, 0)),
            pl.BlockSpec((D_MODEL, Z_COLS), lambda b, i: (0, 0)),
        ],
        out_specs=pl.BlockSpec((None, tm, Z_COLS), lambda b, i: (b, i, 0)),
        out_shape=jax.ShapeDtypeStruct((B, T, Z_COLS), F32),
        compiler_params=_cparams("parallel", "parallel"),
        name="proj_in",
    )(x, sc, sh, nw, w_p)


def _ret_steps(di, q_ref, k_ref, v_ref, o_ref, s_ref, lc):
    rev = di == 1
    rows = lax.broadcasted_iota(jnp.int32, (lc, lc), 0)
    cols = lax.broadcasted_iota(jnp.int32, (lc, lc), 1)
    rel = ((cols - rows) if rev else (rows - cols)).astype(F32)
    idx = lax.broadcasted_iota(jnp.int32, (lc, 1), 0)
    pos = ((lc - 1 - idx) if rev else idx).astype(F32)
    for h in range(N_HEADS):
        lg = LOG_GAMMA[di][h]
        sl = slice(h * HEAD_D, (h + 1) * HEAD_D)
        q = q_ref[:, sl]
        k = k_ref[:, sl] * (HEAD_D ** -0.5)
        v = v_ref[:, sl].astype(BF16)
        att = _dot_nt(q.astype(BF16), k.astype(BF16))
        att = att * jnp.where(rel >= 0.0, jnp.exp(jnp.maximum(rel, 0.0) * lg), 0.0)
        qd = q * jnp.exp((pos + 1.0) * lg)
        kd = k * jnp.exp((lc - 1.0 - pos) * lg)
        s = s_ref[di, h]
        o_ref[:, sl] = _dot(att.astype(BF16), v) + _dot(qd.astype(BF16), s.astype(BF16))
        s_ref[di, h] = s * math.exp(lc * lg) + _dot_tn(kd.astype(BF16), v)
        yield


def _s5_zoh_kernel(lre_ref, lim_ref, ldt_ref, bre_ref, bim_ref, are_ref, aim_ref, bbre_ref, bbim_ref):
    lam_re = jnp.minimum(lre_ref[...], -1e-4)
    lam_im = lim_ref[...]
    dt = jnp.exp(ldt_ref[...])
    mag = jnp.exp(dt * lam_re)
    ang = dt * lam_im
    a_re = mag * jnp.cos(ang)
    a_im = mag * jnp.sin(ang)
    den = lam_re * lam_re + lam_im * lam_im
    f_re = ((a_re - 1.0) * lam_re + a_im * lam_im) / den
    f_im = (a_im * lam_re - (a_re - 1.0) * lam_im) / den
    b_re = bre_ref[...]
    b_im = bim_ref[...]
    are_ref[...] = a_re
    aim_ref[...] = a_im
    bbre_ref[...] = f_re * b_re - f_im * b_im
    bbim_ref[...] = f_re * b_im + f_im * b_re


def _s5_zoh(lam_re, lam_im, log_dt, b_re, b_im):
    n = 2 * S5_STATE
    col = lambda t: t.reshape(n, 1)
    ldt = jnp.broadcast_to(log_dt[:, :, None], (2, S5_GROUPS, S5_P))
    outs = pl.pallas_call(
        _s5_zoh_kernel,
        out_shape=[jax.ShapeDtypeStruct((n, 1), F32), jax.ShapeDtypeStruct((n, 1), F32),
                   jax.ShapeDtypeStruct((n, S5_CH), F32), jax.ShapeDtypeStruct((n, S5_CH), F32)],
        name="s5_zoh",
    )(col(lam_re), col(lam_im), col(ldt), b_re.reshape(n, S5_CH), b_im.reshape(n, S5_CH))
    a_re, a_im, bb_re, bb_im = outs
    shp = (2, S5_GROUPS, S5_P, S5_CH)
    return a_re.reshape(2, 1, S5_STATE), a_im.reshape(2, 1, S5_STATE), bb_re.reshape(shp), bb_im.reshape(shp)


S5_RADIX = 16


def _ret_s5_kernel(*refs, L, nc):
    (uf_ref, ub_ref, are_ref, aim_ref, bre_ref, bim_ref, cre_ref, cim_ref, x0r_ref, x0i_ref,
     qf_ref, kf_ref, vf_ref, qb_ref, kb_ref, vb_ref, rs0_ref,
     yf_ref, yb_ref, fr_ref, fi_ref, rof_ref, rob_ref, rfin_ref,
     car_re, car_im) = refs[:26]
    bufs, rs_ref = refs[26:34], refs[34]
    c = pl.program_id(1)
    R = S5_RADIX
    G = L // R

    @pl.when(c == 0)
    def _():
        car_re[...] = x0r_ref[...]
        car_im[...] = x0i_ref[...]
        rs_ref[...] = rs0_ref[...]

    rr = lax.broadcasted_iota(jnp.int32, (L, L), 0)
    cc = lax.broadcasted_iota(jnp.int32, (L, L), 1)
    lg, lr = G.bit_length() - 1, R.bit_length() - 1
    perm = jnp.where(cc == ((rr & (G - 1)) << lr) + (rr >> lg), 1.0, 0.0).astype(BF16)
    unperm = jnp.where(cc == ((rr & (R - 1)) << lg) + (rr >> lr), 1.0, 0.0).astype(BF16)

    def cmul(pr, pi, qr, qi):
        return pr * qr - pi * qi, pr * qi + pi * qr

    def steps(di, u_ref, y_ref, x_re, x_im, o_re, o_im):
        rev = di == 1
        u = _dot(perm, u_ref[...].astype(BF16)).astype(BF16)
        x_re[...] = _dot(u, bre_ref[di])
        x_im[...] = _dot(u, bim_ref[di])
        yield
        ar = are_ref[di]
        ai = aim_ref[di]
        order = list(range(R - 1, -1, -1)) if rev else list(range(R))
        slab = lambda j: slice(j * G, (j + 1) * G)

        er = x_re[slab(order[0]), :]
        ei = x_im[slab(order[0]), :]
        for n, j in enumerate(order[1:]):
            tr, ti = cmul(ar, ai, er, ei)
            er = tr + x_re[slab(j), :]
            ei = ti + x_im[slab(j), :]
            if n % 4 == 3:
                yield

        a_r, a_i = ar, ai
        for _ in range(R.bit_length() - 1):
            a_r, a_i = cmul(a_r, a_i, a_r, a_i)
        zr = car_re[di]
        zi = car_im[di]
        cin_r = [None] * G
        cin_i = [None] * G
        for k in (range(G - 1, -1, -1) if rev else range(G)):
            cin_r[k] = zr
            cin_i[k] = zi
            tr, ti = cmul(a_r, a_i, zr, zi)
            zr = tr + er[k:k + 1, :]
            zi = ti + ei[k:k + 1, :]
        car_re[di] = zr
        car_im[di] = zi
        yield

        xr = jnp.concatenate(cin_r, axis=0)
        xi = jnp.concatenate(cin_i, axis=0)
        for n, j in enumerate(order):
            tr, ti = cmul(ar, ai, xr, xi)
            xr = tr + x_re[slab(j), :]
            xi = ti + x_im[slab(j), :]
            o_re[slab(j), :] = xr
            o_im[slab(j), :] = xi
            if n % 4 == 3:
                yield
        xr_all = _dot(unperm, o_re[...].astype(BF16)).astype(BF16)
        xi_all = _dot(unperm, o_im[...].astype(BF16)).astype(BF16)
        y_ref[...] = _dot(xr_all, cre_ref[...]) - _dot(xi_all, cim_ref[...])

    live = [steps(0, uf_ref, yf_ref, *bufs[0:4]), steps(1, ub_ref, yb_ref, *bufs[4:8]),
            _ret_steps(0, qf_ref, kf_ref, vf_ref, rof_ref, rs_ref, L),
            _ret_steps(1, qb_ref, kb_ref, vb_ref, rob_ref, rs_ref, L)]
    while live:
        for gen in list(live):
            if next(gen, StopIteration) is StopIteration:
                live.remove(gen)

    @pl.when(c == nc - 1)
    def _():
        fr_ref[...] = car_re[...]
        fi_ref[...] = car_im[...]
        rfin_ref[...] = rs_ref[...]


def _ret_s5(z, a_re, a_im, bblk_re, bblk_im, cblk_re, cblk_im, x0_re, x0_im, s0_ret):
    B, T, _ = z.shape
    L = min(T, 128)
    nc = T // L
    assert L % S5_RADIX == 0 and (L // S5_RADIX) & (L // S5_RADIX - 1) == 0
    full = lambda shape: pl.BlockSpec(shape, lambda b, c: (0,) * len(shape))
    sspec = pl.BlockSpec((None, 2, 1, S5_STATE), lambda b, c: (b, 0, 0, 0))
    uspec = lambda cmap, blk: pl.BlockSpec((None, L, MIX_W), lambda b, c: (b, cmap(c), blk))
    fwd = lambda c: c
    bwd = lambda c: nc - 1 - c
    buf = pltpu.VMEM((L, S5_STATE), F32)
    car = pltpu.VMEM((2, 1, S5_STATE), F32)
    seq = jax.ShapeDtypeStruct((B, T, MIX_W), F32)
    state = jax.ShapeDtypeStruct((B, 2, 1, S5_STATE), F32)
    rshape = (2, N_HEADS, HEAD_D, HEAD_D)
    rspec = pl.BlockSpec((None,) + rshape, lambda b, c: (b, 0, 0, 0, 0))
    return pl.pallas_call(
        functools.partial(_ret_s5_kernel, L=L, nc=nc),
        grid=(B, nc),
        in_specs=[
            uspec(fwd, ZB_S5), uspec(bwd, ZB_S5),
            full((2, 1, S5_STATE)), full((2, 1, S5_STATE)),
            full((2, MIX_W, S5_STATE)), full((2, MIX_W, S5_STATE)),
            full((S5_STATE, MIX_W)), full((S5_STATE, MIX_W)),
            sspec, sspec,
            uspec(fwd, ZB_RET), uspec(fwd, ZB_RET + 1), uspec(fwd, ZB_RET + 2),
            uspec(bwd, ZB_RET), uspec(bwd, ZB_RET + 1), uspec(bwd, ZB_RET + 2),
            rspec,
        ],
        out_specs=[uspec(fwd, 0), uspec(bwd, 0), sspec, sspec, uspec(fwd, 0), uspec(bwd, 0), rspec],
        out_shape=[seq, seq, state, state, seq, seq, jax.ShapeDtypeStruct((B,) + rshape, F32)],
        scratch_shapes=[car, car] + [buf] * 8 + [pltpu.VMEM(rshape, F32)],
        compiler_params=_cparams("parallel", "arbitrary"),
        name="ret_s5",
    )(z, z, a_re, a_im, bblk_re, bblk_im, cblk_re, cblk_im, x0_re, x0_im, z, z, z, z, z, z, s0_ret)


GLA_CHUNK = 16
GLA_CHUNK_SHIFT = 4
SUB = 8


def _gla_steps(q_ref, zf_ref, v_ref, lb_ref, s0_ref, o_ref, fin_ref, s_ref, a_ref, qs_ref, key_ref,
               *, rev, tb, nt):
    jb = pl.program_id(1)
    ch = GLA_CHUNK
    half = 2 * HEAD_D

    @pl.when(jb == 0)
    def _():
        s_ref[...] = s0_ref[...]

    lb = lb_ref[...]
    zf = zf_ref[...]
    q = q_ref[...]
    qs_ref[...] = q * jax.nn.sigmoid(q)
    key_ref[...] = (1.0 - lb) * jax.nn.sigmoid(-zf)
    l1 = jnp.log(lb)
    l2 = jnp.log1p(-lb) + jnp.minimum(zf, 0.0) - jnp.log1p(jnp.exp(-jnp.abs(zf)))
    lf = jnp.maximum(l1, l2) + jnp.log1p(jnp.exp(-jnp.abs(l1 - l2)))
    r = lax.broadcasted_iota(jnp.int32, (tb, tb), 0)
    c = lax.broadcasted_iota(jnp.int32, (tb, tb), 1)
    same = (r >> GLA_CHUNK_SHIFT) == (c >> GLA_CHUNK_SHIFT)
    tri = jnp.where(same & ((c >= r) if rev else (c <= r)), 1.0, 0.0).astype(BF16)
    p1 = lf.astype(BF16)
    r1 = lf - p1.astype(F32)
    p2 = r1.astype(BF16)
    p3 = (r1 - p2.astype(F32)).astype(BF16)
    a_ref[...] = _dot(tri, p1) + _dot(tri, p2) + _dot(tri, p3)

    rows = lax.broadcasted_iota(jnp.int32, (SUB, half), 0)
    lo = lax.broadcasted_iota(jnp.int32, (SUB, half), 1) < HEAD_D
    same_head = ((lax.broadcasted_iota(jnp.int32, (half, half), 0) >> HEAD_SHIFT)
                 == (lax.broadcasted_iota(jnp.int32, (half, half), 1) >> HEAD_SHIFT))

    def pair_sums(p):
        s0 = jnp.sum(jnp.where(lo, p, 0.0), axis=1, keepdims=True)
        s1 = jnp.sum(jnp.where(lo, 0.0, p), axis=1, keepdims=True)
        return jnp.where(lo, s0, s1)

    def chunk(ci):
        t0 = (tb // ch - 1 - ci if rev else ci) * ch
        win = slice(t0, t0 + ch)
        a = a_ref[win, :]
        qc = qs_ref[win, :]
        kc = key_ref[win, :]
        vc = v_ref[win, :]
        last = 0 if rev else ch - 1
        a_last = a[last:last + 1, :]
        qe = (qc * jnp.exp(a)).astype(BF16)
        ke = (kc * jnp.exp(a_last - a)).astype(BF16)
        ea = jnp.exp(a_last)
        vb = vc.astype(BF16)
        nslab = ch // SUB
        acc = [[jnp.zeros((SUB, half), F32) for _ in range(2)] for _ in range(nslab)]
        for jj in range(ch):
            for sb in range(nslab):
                r0 = sb * SUB
                if (r0 > jj) if rev else (r0 + SUB - 1 < jj):
                    continue
                whole = (r0 + SUB - 1 <= jj) if rev else (r0 >= jj)
                rs = slice(r0, r0 + SUB)
                valid = (rows + r0 <= jj) if rev else (rows + r0 >= jj)
                for hp in range(2):
                    sl = slice(hp * half, (hp + 1) * half)
                    dec = jnp.exp(jnp.minimum(a[rs, sl] - a[jj:jj + 1, sl], 0.0))
                    p = qc[rs, sl] * dec * kc[jj:jj + 1, sl]
                    if not whole:
                        p = jnp.where(valid, p, 0.0)
                    acc[sb][hp] = acc[sb][hp] + pair_sums(p) * vc[jj:jj + 1, sl]
        for hp in range(2):
            sl = slice(hp * half, (hp + 1) * half)
            s = s_ref[hp]
            inter = _dot_nt(qe[:, sl], s.astype(BF16))
            intra = jnp.concatenate([acc[sb][hp] for sb in range(nslab)], axis=0)
            o_ref[win, sl] = inter + intra
            s_ref[hp] = s * ea[:, sl] + jnp.where(same_head, _dot_tn(vb[:, sl], ke[:, sl]), 0.0)

    yield
    for ci in range(tb // ch):
        chunk(ci)
        yield

    @pl.when(jb == nt - 1)
    def _():
        fin_ref[...] = s_ref[...]


def _rw_prep_kernel(*refs, grid_shift, tm):
    if grid_shift:
        (zc_ref, zu_ref, zd_ref, mu_ref, vec_ref, w0_ref, lora_ref,
         r_ref, k2_ref, v_ref, w_ref, nkk_ref, kka_ref, g_ref, bonus_ref, buf_ref) = refs
    else:
        (zc_ref, mu_ref, vec_ref, w0_ref, lora_ref,
         r_ref, k2_ref, v_ref, w_ref, nkk_ref, kka_ref, g_ref, bonus_ref, buf_ref) = refs
    i = pl.program_id(1)
    nt = pl.num_programs(1)
    halo = GRID_W
    width = 4 * MIX_W
    z = zc_ref[...]
    buf_ref[halo:halo + tm, :] = z
    lane = lax.broadcasted_iota(jnp.int32, (tm, width), 1)
    row = lax.broadcasted_iota(jnp.int32, (tm, width), 0)
    if grid_shift:
        buf_ref[0:halo, :] = jnp.where(i > 0, zu_ref[...], 0.0)
        buf_ref[halo + tm:2 * halo + tm, :] = jnp.where(i < nt - 1, zd_ref[...], 0.0)
        col = row & (GRID_W - 1)
        left = jnp.where(col > 0, buf_ref[halo - 1:halo - 1 + tm, :], 0.0)
        right = jnp.where(col < GRID_W - 1, buf_ref[halo + 1:halo + 1 + tm, :], 0.0)
        up = buf_ref[0:tm, :]
        down = buf_ref[2 * halo:2 * halo + tm, :]
        sel = lane & 3
        shifted = jnp.where(sel == 0, left, jnp.where(sel == 1, right, jnp.where(sel == 2, up, down)))
    else:
        zrow = jnp.zeros((1, width), F32)
        buf_ref[halo - 1:halo, :] = zrow
        buf_ref[halo + tm:halo + tm + 1, :] = zrow
        prev = buf_ref[halo - 1:halo - 1 + tm, :]
        nxt = buf_ref[halo + 1:halo + 1 + tm, :]
        shifted = jnp.where((lane & 1) == 0, prev, nxt)
    zs = z + mu_ref[...] * (shifted - z)
    r = zs[:, 0:MIX_W]
    k = zs[:, MIX_W:2 * MIX_W]
    v = zs[:, 2 * MIX_W:3 * MIX_W]
    sm = zs[:, 3 * MIX_W:4 * MIX_W]
    ones_bd = _head_ones()
    a0, k_k, k_a, r_k = (vec_ref[j:j + 1, :] for j in range(4))
    a = jax.nn.sigmoid(a0 + _dot(sm.astype(BF16), lora_ref[2]))
    g_ref[...] = _dot(jax.nn.sigmoid(sm).astype(BF16), lora_ref[3])
    kk = k * k_k
    kk = kk * lax.rsqrt(_head_sum(kk * kk, ones_bd) + 1e-12)
    k2 = k * (1.0 + (a - 1.0) * k_a)
    th = jnp.tanh(sm).astype(BF16)
    for di in range(2):
        w_ref[di] = -math.exp(-0.5) * jax.nn.sigmoid(w0_ref[di:di + 1, :] + _dot(th, lora_ref[di]))
    r_ref[...] = r
    k2_ref[...] = k2
    v_ref[...] = v
    nkk_ref[...] = -kk
    kka_ref[...] = kk * a
    bonus_ref[...] = _head_sum(r * k2 * r_k, ones_bd) * v


def _rw_prep(z, mu_p, vecs, w0, lora, grid_shift):
    B, T, _ = z.shape
    width = 4 * MIX_W
    wblk = ZB_RW // 4
    if grid_shift:
        tm = min(T, 512)
        hb = tm // GRID_W
        nh = T // GRID_W
        z_specs = [
            pl.BlockSpec((None, tm, width), lambda b, i: (b, i, wblk)),
            pl.BlockSpec((None, GRID_W, width), lambda b, i: (b, jnp.maximum(i * hb - 1, 0), wblk)),
            pl.BlockSpec((None, GRID_W, width), lambda b, i: (b, jnp.minimum((i + 1) * hb, nh - 1), wblk)),
        ]
        z_args = (z, z, z)
    else:
        tm = T
        z_specs = [pl.BlockSpec((None, tm, width), lambda b, i: (b, i, wblk))]
        z_args = (z,)
    const = lambda shape: pl.BlockSpec(shape, lambda b, i: (0,) * len(shape))
    ospec = pl.BlockSpec((None, tm, MIX_W), lambda b, i: (b, i, 0))
    oshape = jax.ShapeDtypeStruct((B, T, MIX_W), F32)
    dspec = pl.BlockSpec((2, None, tm, MIX_W), lambda b, i: (0, b, i, 0))
    dshape = jax.ShapeDtypeStruct((2, B, T, MIX_W), F32)
    return pl.pallas_call(
        functools.partial(_rw_prep_kernel, grid_shift=grid_shift, tm=tm),
        grid=(B, T // tm),
        in_specs=z_specs + [const((1, width)), const((4, MIX_W)), const((2, MIX_W)),
                            const((4, MIX_W, MIX_W))],
        out_specs=[ospec, ospec, ospec, dspec, ospec, ospec, ospec, ospec],
        out_shape=[oshape, oshape, oshape, dshape, oshape, oshape, oshape, oshape],
        scratch_shapes=[pltpu.VMEM((tm + 2 * GRID_W, width), F32)],
        compiler_params=_cparams("parallel", "parallel"),
        name="rwkv_prep",
    )(*z_args, mu_p, vecs, w0, lora)


RW_CHUNK = 64
RW_CHUNK_SHIFT = 6


def _mm(a, b):
    return jnp.dot(a.astype(BF16), b.astype(BF16), preferred_element_type=F32)


def _rwkv_steps(lw_ref, a_ref, b_ref, k_ref, r_ref, v_ref, s0_ref, y_ref, fin_ref, s_ref, g_ref,
                *, rev, tb, nt):
    jb = pl.program_id(1)
    ch = RW_CHUNK
    pair = 2 * HEAD_D

    @pl.when(jb == 0)
    def _():
        s_ref[...] = s0_ref[...]

    lw = lw_ref[...]
    rr = lax.broadcasted_iota(jnp.int32, (tb, tb), 0)
    cc = lax.broadcasted_iota(jnp.int32, (tb, tb), 1)
    same = (rr >> RW_CHUNK_SHIFT) == (cc >> RW_CHUNK_SHIFT)
    tri = jnp.where(same & ((cc >= rr) if rev else (cc <= rr)), 1.0, 0.0).astype(BF16)
    p1 = lw.astype(BF16)
    r1 = lw - p1.astype(F32)
    p2 = r1.astype(BF16)
    p3 = (r1 - p2.astype(F32)).astype(BF16)
    g_ref[...] = _dot(tri, p1) + _dot(tri, p2) + _dot(tri, p3)

    si = lax.broadcasted_iota(jnp.int32, (ch, ch), 0)
    ri = lax.broadcasted_iota(jnp.int32, (ch, ch), 1)
    if rev:
        si, ri = ch - 1 - si, ch - 1 - ri
    strict = ri < si
    incl = ri <= si
    eye = jnp.where(ri == si, 1.0, 0.0).astype(F32)
    levels = [((si >> (lv + 1)) == (ri >> (lv + 1))) & (((si >> lv) & 1) == 1) & (((ri >> lv) & 1) == 0)
              for lv in range(RW_CHUNK_SHIFT)]
    lo = lax.broadcasted_iota(jnp.int32, (ch, pair), 1) < HEAD_D
    same_head = ((lax.broadcasted_iota(jnp.int32, (pair, pair), 0) >> HEAD_SHIFT)
                 == (lax.broadcasted_iota(jnp.int32, (pair, pair), 1) >> HEAD_SHIFT))

    nch = tb // ch
    g = g_ref[...]
    e_g = jnp.exp(g)
    e_ng = jnp.exp(-g)
    at_f = a_ref[...] * jnp.exp(g - lw)
    rt_f = r_ref[...] * e_g
    at = at_f.astype(BF16)
    rt = rt_f.astype(BF16)
    first = (lax.broadcasted_iota(jnp.int32, (tb, MIX_W), 1) & HEAD_D) == 0
    at_h = [jnp.where(first, at_f, 0.0).astype(BF16), jnp.where(first, 0.0, at_f).astype(BF16)]
    rt_h = [jnp.where(first, rt_f, 0.0).astype(BF16), jnp.where(first, 0.0, rt_f).astype(BF16)]
    bt = (b_ref[...] * e_ng).astype(BF16)
    kt = (k_ref[...] * e_ng).astype(BF16)
    vb = v_ref[...].astype(BF16)

    items = [(c, hp, hh) for c in range(nch) for hp in range(2) for hh in range(2)]
    rows = lambda c: slice(c * ch, (c + 1) * ch)
    lanes = lambda hp: slice(hp * pair, (hp + 1) * pair)
    n_m, p_m, m_m, q_m = {}, {}, {}, {}
    for it in items:
        c, hp, hh = it
        ar = jnp.concatenate([at_h[hh][rows(c), lanes(hp)], rt_h[hh][rows(c), lanes(hp)]], axis=0)
        np_ = _dot_nt(ar, bt[rows(c), lanes(hp)])
        mq = _dot_nt(ar, kt[rows(c), lanes(hp)])
        n_m[it] = jnp.where(strict, np_[:ch], 0.0).astype(BF16)
        p_m[it] = jnp.where(incl, np_[ch:], 0.0).astype(BF16)
        m_m[it] = jnp.where(strict, mq[:ch], 0.0).astype(BF16)
        q_m[it] = jnp.where(incl, mq[ch:], 0.0).astype(BF16)
        if hp == 1 and hh == 1:
            yield
    t_m = {it: eye + jnp.where(levels[0], n_m[it].astype(F32), 0.0) for it in items}
    for lv in range(1, RW_CHUNK_SHIFT):
        tn = {it: _mm(t_m[it], n_m[it]) for it in items}
        yield
        t_m = {it: t_m[it] + jnp.where(levels[lv], _mm(tn[it], t_m[it]), 0.0) for it in items}
        yield
    mv = {it: _mm(m_m[it], vb[rows(it[0]), lanes(it[1])]) for it in items}
    qv = {it: _mm(q_m[it], vb[rows(it[0]), lanes(it[1])]) for it in items}
    yield
    y2 = {it: _mm(t_m[it], jnp.concatenate([at[rows(it[0]), lanes(it[1])], mv[it].astype(BF16)], axis=1))
          for it in items}
    yield

    last = 0 if rev else ch - 1
    for ci in range(nch):
        c = nch - 1 - ci if rev else ci
        g_c = g[c * ch + last:c * ch + last + 1, :]
        e_gc = jnp.exp(g_c - g[rows(c), :])
        bh = (b_ref[rows(c), :] * e_gc).astype(BF16)
        kh = (k_ref[rows(c), :] * e_gc).astype(BF16)
        dec_c = jnp.exp(g_c)
        xs, sas, ss = [], [], []
        for hp in range(2):
            s = s_ref[hp]
            w2 = jnp.where(lo, y2[(c, hp, 0)][:, :pair], y2[(c, hp, 1)][:, :pair]).astype(BF16)
            xs.append(_dot_nt(jnp.concatenate([w2, rt[rows(c), lanes(hp)]], axis=0), s.astype(BF16)))
            ss.append(s)
        for hp in range(2):
            w1 = jnp.where(lo, y2[(c, hp, 0)][:, pair:], y2[(c, hp, 1)][:, pair:])
            sas.append(w1 + xs[hp][:ch])
        for hp in range(2):
            sa = sas[hp]
            y_ref[rows(c), lanes(hp)] = xs[hp][ch:] + jnp.where(
                lo, _mm(p_m[(c, hp, 0)], sa) + qv[(c, hp, 0)], _mm(p_m[(c, hp, 1)], sa) + qv[(c, hp, 1)])
            upd = _dot_tn(jnp.concatenate([sa.astype(BF16), vb[rows(c), lanes(hp)]], axis=0),
                          jnp.concatenate([bh[:, lanes(hp)], kh[:, lanes(hp)]], axis=0))
            s_ref[hp] = ss[hp] * dec_c[:, lanes(hp)] + jnp.where(same_head, upd, 0.0)
        yield

    @pl.when(jb == nt - 1)
    def _():
        fin_ref[...] = s_ref[...]


def _hg_rw_kernel(*refs, rev, tb, nt):
    gla_in, rw_in = refs[0:5], refs[5:12]
    gla_out, rw_out = refs[12:14], refs[14:16]
    gla_scr, rw_scr = refs[16:20], refs[20:22]
    gla = _gla_steps(*gla_in, *gla_out, *gla_scr, rev=rev, tb=tb, nt=nt)
    rwkv = _rwkv_steps(*rw_in, *rw_out, *rw_scr, rev=rev, tb=tb, nt=nt)
    live = [gla, rwkv]
    while live:
        for gen in list(live):
            if next(gen, StopIteration) is StopIteration:
                live.remove(gen)


def _head_pair_blockdiag(s):
    B = s.shape[0]
    s = s.reshape(B, 2, 2, HEAD_D, HEAD_D)
    zero = jnp.zeros_like(s[:, :, 0])
    return jnp.concatenate([jnp.concatenate([s[:, :, 0], zero], axis=-1),
                            jnp.concatenate([zero, s[:, :, 1]], axis=-1)], axis=-2)


def _head_pair_blocks(t):
    B = t.shape[0]
    t = jnp.stack([t[:, :, :HEAD_D, :HEAD_D], t[:, :, HEAD_D:, HEAD_D:]], axis=2)
    return t.reshape(B, N_HEADS, HEAD_D, HEAD_D)


def _hg_rw(z, lb, s0_hg, lw, a, b, k, r, v, s0_rw, rev):
    B, T, _ = v.shape
    pair = 2 * HEAD_D
    tb = min(T, 256)
    nt = T // tb
    di = 1 if rev else 0
    tmap = (lambda j: nt - 1 - j) if rev else (lambda j: j)
    zspec = lambda blk: pl.BlockSpec((None, tb, MIX_W), lambda bi, j: (bi, tmap(j), blk))
    spec = zspec(0)
    sspec = pl.BlockSpec((None, 2, pair, pair), lambda bi, j: (bi, 0, 0, 0))
    state = jax.ShapeDtypeStruct((B, 2, pair, pair), F32)
    seq = jax.ShapeDtypeStruct((B, T, MIX_W), F32)
    blk = pltpu.VMEM((tb, MIX_W), F32)
    st = pltpu.VMEM((2, pair, pair), F32)
    o, fin_hg, y, fin_rw = pl.pallas_call(
        functools.partial(_hg_rw_kernel, rev=rev, tb=tb, nt=nt),
        grid=(B, nt),
        in_specs=[zspec(ZB_HG), zspec(ZB_HG + 1 + di), zspec(ZB_HG + 3),
                  pl.BlockSpec((1, MIX_W), lambda bi, j: (0, 0)), sspec,
                  pl.BlockSpec((None, None, tb, MIX_W), lambda bi, j: (di, bi, tmap(j), 0)),
                  spec, spec, spec, spec, spec, sspec],
        out_specs=[spec, sspec, spec, sspec],
        out_shape=[seq, state, seq, state],
        scratch_shapes=[st, blk, blk, blk, st, blk],
        compiler_params=_cparams("parallel", "arbitrary"),
        name="hgrn_rwkv_bwd" if rev else "hgrn_rwkv_fwd",
    )(z, z, z, lb[di:di + 1], _head_pair_blockdiag(s0_hg),
      lw, a, b, k, r, v, _head_pair_blockdiag(s0_rw))
    return o, _head_pair_blocks(fin_hg), y, _head_pair_blocks(fin_rw)


def _mix_kernel(x_ref, sc_ref, sh_ref, g1_ref, nw0_ref, nw_ref, wg_ref,
                retf_ref, retb_ref, retg_ref, s5f_ref, s5b_ref, s5u_ref, hgf_ref, hgb_ref, hgg_ref, rwf_ref, rwr_ref,
                rwb_ref, rwg_ref,
                vec_ref, glu_w_ref, wbr_ref, wout_ref, o_ref):
    ones_bd = _head_ones()
    h = (_rms(x_ref[...], nw0_ref[...]) * (1.0 + sc_ref[...]) + sh_ref[...]).astype(BF16)
    gn_w, s5_d, glu_b, hg_w, ln_w, ln_b = (vec_ref[j:j + 1, :] for j in range(6))

    def group_norm(o, eps):
        mu = _head_sum(o, ones_bd) * (1.0 / HEAD_D)
        oc = o - mu
        var = _head_sum(oc * oc, ones_bd) * (1.0 / HEAD_D)
        return oc * lax.rsqrt(var + eps)

    g = retg_ref[...]
    y_ret = group_norm(retf_ref[...] + retb_ref[...], EPS) * gn_w * (g * jax.nn.sigmoid(g))

    y = s5_d * s5u_ref[...] + s5f_ref[...] + s5b_ref[...]
    yg = jax.nn.gelu(y)
    y_s5 = yg * jax.nn.sigmoid(_dot(yg.astype(BF16), glu_w_ref[...]) + glu_b)

    o = hgf_ref[...] + hgb_ref[...]
    g = hgg_ref[...]
    ms = _head_sum(o * o, ones_bd) * (1.0 / HEAD_D)
    y_hg = o * lax.rsqrt(ms + EPS) * hg_w * (g * jax.nn.sigmoid(g))

    y = group_norm(rwf_ref[...] + rwr_ref[...], RW_LN_EPS) * ln_w + ln_b
    y_rw = (y + rwb_ref[...]) * rwg_ref[...]

    mixed = None
    for m, ym in enumerate((y_ret, y_s5, y_hg, y_rw)):
        br = _dot(ym.astype(BF16), wbr_ref[m])
        term = jax.nn.sigmoid(_dot(h, wg_ref[:, m * D_MODEL:(m + 1) * D_MODEL])) * br
        mixed = term if mixed is None else mixed + term
    mixed = _dot(mixed.astype(BF16), wout_ref[...])
    o_ref[...] = x_ref[...] + g1_ref[...] * _rms(mixed, nw_ref[...])


def _mix(x, sc, sh, g1, nw0, nw, w_gate, z, ret_o, s5_y, hg_o, rw_y, rw_bonus, rw_g, vecs, glu_w,
         w_branch, w_out):
    B, T, _ = x.shape
    tm = min(T, 256)
    xspec = pl.BlockSpec((None, tm, D_MODEL), lambda b, i: (b, i, 0))
    zspec = lambda blk: pl.BlockSpec((None, tm, MIX_W), lambda b, i: (b, i, blk))
    const = lambda shape: pl.BlockSpec(shape, lambda b, i: (0,) * len(shape))
    mspec = lambda m: pl.BlockSpec((None, 1, D_MODEL), _mod_map(m.shape[0]))
    return pl.pallas_call(
        _mix_kernel,
        grid=(B, T // tm),
        in_specs=[
            xspec, mspec(sc), mspec(sh), mspec(g1),
            const((1, D_MODEL)), const((1, D_MODEL)),
            const((D_MODEL, 4 * D_MODEL)),
            zspec(0), zspec(0), zspec(ZB_RET + 3),
            zspec(0), zspec(0), zspec(ZB_S5),
            zspec(0), zspec(0), zspec(ZB_HG + 4),
            zspec(0), zspec(0), zspec(0), zspec(0),
            const((6, MIX_W)), const((MIX_W, MIX_W)),
            const((4, MIX_W, D_MODEL)), const((D_MODEL, D_MODEL)),
        ],
        out_specs=xspec,
        out_shape=jax.ShapeDtypeStruct((B, T, D_MODEL), F32),
        compiler_params=_cparams("parallel", "parallel"),
        name="mix_out",
    )(x, sc, sh, g1, nw0, nw, w_gate, ret_o[0], ret_o[1], z, s5_y[0], s5_y[1], z, hg_o[0], hg_o[1], z, rw_y[0], rw_y[1],
      rw_bonus, rw_g, vecs, glu_w, w_branch, w_out)


FFN_CHUNK = 1024


def _ffn_kernel(x_ref, sc_ref, sh_ref, g2_ref, nw2_ref, nw3_ref, w1_ref, w2_ref, o_ref):
    x = x_ref[...]
    h = (_rms(x, nw2_ref[...]) * (1.0 + sc_ref[...]) + sh_ref[...]).astype(BF16)
    acc = None
    for j in range(D_FF // FFN_CHUNK):
        cols = slice(j * FFN_CHUNK, (j + 1) * FFN_CHUNK)
        a = jnp.maximum(_dot(h, w1_ref[:, cols]), 0.0)
        part = _dot((a * a).astype(BF16), w2_ref[cols, :])
        acc = part if acc is None else acc + part
    o_ref[...] = x + g2_ref[...] * _rms(acc, nw3_ref[...])


def _ffn(x, sc, sh, g2, nw2, nw3, w1, w2):
    B, T, _ = x.shape
    tm = min(T, 512)
    xspec = pl.BlockSpec((None, tm, D_MODEL), lambda b, i: (b, i, 0))
    mspec = lambda m: pl.BlockSpec((None, 1, D_MODEL), _mod_map(m.shape[0]))
    const = pl.BlockSpec((1, D_MODEL), lambda b, i: (0, 0))
    return pl.pallas_call(
        _ffn_kernel,
        grid=(B, T // tm),
        in_specs=[xspec, mspec(sc), mspec(sh), mspec(g2), const, const,
                  pl.BlockSpec((D_MODEL, D_FF), lambda b, i: (0, 0)),
                  pl.BlockSpec((D_FF, D_MODEL), lambda b, i: (0, 0))],
        out_specs=xspec,
        out_shape=jax.ShapeDtypeStruct((B, T, D_MODEL), F32),
        compiler_params=_cparams("parallel", "parallel"),
        name="ffn",
    )(x, sc, sh, g2, nw2, nw3, w1, w2)


def _layer_params(l, p):
    w = p['w_in'][l]
    w_p = jnp.concatenate(
        [w[:, 2560:3488], jnp.zeros((D_MODEL, 4 * MIX_W - RW_COLS), F32),
         w[:, 0:1024], w[:, 1024:1280], w[:, 1280:2560]], axis=1).astype(BF16)
    w_gate = w[:, 3488:7584].astype(BF16)

    a_re, a_im, bb_re, bb_im = _s5_zoh(p['s5_lam_re'][l], p['s5_lam_im'][l], p['s5_log_dt'][l],
                                       p['s5_b_re'][l], p['s5_b_im'][l])
    eye = jnp.eye(S5_GROUPS, dtype=F32)
    bblk = lambda bb: jnp.einsum('dgph,gk->dghkp', bb, eye).reshape(2, MIX_W, S5_STATE).astype(BF16)
    cblk = lambda c: jnp.einsum('ghp,gk->gpkh', c, eye).reshape(S5_STATE, MIX_W).astype(BF16)

    def lora_pad(m, row0):
        return jnp.zeros((MIX_W, MIX_W), F32).at[row0:row0 + m.shape[0]].set(m)

    lora = jnp.stack([lora_pad(p['rw_w2'][l, 0], 0), lora_pad(p['rw_w2'][l, 1], 32),
                      lora_pad(p['rw_a2'][l], 64), lora_pad(p['rw_g2'][l], 96)]).astype(BF16)
    mu_p = jnp.concatenate([p['rw_mu'][l], jnp.zeros((4 * MIX_W - RW_COLS,), F32)]).reshape(1, 4 * MIX_W)
    return dict(
        w_p=w_p, w_gate=w_gate, nw=p['norm_w'][l],
        s5=(a_re, a_im, bblk(bb_re), bblk(bb_im), cblk(p['s5_c_re'][l]), cblk(p['s5_c_im'][l])),
        rw_mu=mu_p, rw_lora=lora, rw_w0=p['rw_w0'][l],
        rw_vecs=jnp.stack([p['rw_a0'][l], p['rw_k_k'][l], p['rw_k_a'][l], p['rw_r_k'][l]]),
        mix_vecs=jnp.stack([p['ret_gn_w'][l], p['s5_d'][l], p['s5_glu_b'][l], p['hg_norm_w'][l],
                            p['rw_ln_w'][l], p['rw_ln_b'][l]]),
        glu_w=p['s5_glu_w'][l].astype(BF16),
        w_branch=p['w_branch'][l].astype(BF16), w_out=p['w_out'][l].astype(BF16),
        ff_w1=p['ff_w1'][l].astype(BF16), ff_w2=p['ff_w2'][l].astype(BF16),
    )


def _trunk_layer(x, mod, init, grid_shift, lp, hg_lb):
    B, T, _ = x.shape
    sh1, sc1, g1, sh2, sc2, g2 = (m[:, None, :] for m in jnp.split(mod, 6, axis=-1))
    nw = lp['nw']
    if mod.shape[0] == 1:
        tok = lambda t: t.reshape(t.shape[:-3] + (1, B * T, t.shape[-1]))
    else:
        tok = lambda t: t
    z = _proj_in(tok(x), sc1, sh1, nw[0:1], lp['w_p']).reshape(B, T, Z_COLS)

    s_ret, s_s5r, s_s5i, s_hg, s_rw = init
    s5_f, s5_b, f_s5r, f_s5i, ret_f, ret_b, f_ret = _ret_s5(
        z, *lp['s5'], s_s5r.reshape(B, 2, 1, S5_STATE), s_s5i.reshape(B, 2, 1, S5_STATE), s_ret)
    f_s5r = f_s5r.reshape(B, 2, S5_GROUPS, S5_P)
    f_s5i = f_s5i.reshape(B, 2, S5_GROUPS, S5_P)

    r, k2, v, w, nkk, kka, rw_g, bonus = _rw_prep(z, lp['rw_mu'], lp['rw_vecs'], lp['rw_w0'],
                                                  lp['rw_lora'], grid_shift)
    s_hg_t = jnp.swapaxes(s_hg, -1, -2)
    hg_f, f_hg_f, rw_f, f_rw_f = _hg_rw(z, hg_lb, s_hg_t[:, 0], w, nkk, kka, k2, r, v, s_rw[:, 0], False)
    hg_b, f_hg_b, rw_b, f_rw_b = _hg_rw(z, hg_lb, s_hg_t[:, 1], w, nkk, kka, k2, r, v, s_rw[:, 1], True)
    f_hg = jnp.swapaxes(jnp.stack([f_hg_f, f_hg_b], axis=1), -1, -2)
    f_rw = jnp.stack([f_rw_f, f_rw_b], axis=1)

    x = _mix(tok(x), sc1, sh1, g1, nw[0:1], nw[1:2], lp['w_gate'], tok(z), (tok(ret_f), tok(ret_b)), (tok(s5_f), tok(s5_b)),
             (tok(hg_f), tok(hg_b)), (tok(rw_f), tok(rw_b)), tok(bonus), tok(rw_g),
             lp['mix_vecs'], lp['glu_w'], lp['w_branch'], lp['w_out'])
    x = _ffn(x, sc2, sh2, g2, nw[2:3], nw[3:4], lp['ff_w1'], lp['ff_w2'])
    return x.reshape(B, T, D_MODEL), (f_ret, f_s5r, f_s5i, f_hg, f_rw)


def kernel(x_prompt, x_sample, state_ret, state_s5_re, state_s5_im, state_hgrn, state_rwkv, c, c_ctx, ada_w, ada_b, norm_w, w_in, ret_gn_w, s5_lam_re, s5_lam_im, s5_log_dt, s5_b_re, s5_b_im, s5_c_re, s5_c_im, s5_d, s5_glu_w, s5_glu_b, hg_lb, hg_norm_w, rw_mu, rw_w0, rw_w2, rw_a0, rw_a2, rw_g2, rw_k_k, rw_k_a, rw_r_k, rw_ln_w, rw_ln_b, w_branch, w_out, ff_w1, ff_w2):
    p = dict(norm_w=norm_w, w_in=w_in, ret_gn_w=ret_gn_w, s5_lam_re=s5_lam_re, s5_lam_im=s5_lam_im,
             s5_log_dt=s5_log_dt, s5_b_re=s5_b_re, s5_b_im=s5_b_im, s5_c_re=s5_c_re, s5_c_im=s5_c_im,
             s5_d=s5_d, s5_glu_w=s5_glu_w, s5_glu_b=s5_glu_b, hg_norm_w=hg_norm_w, rw_mu=rw_mu,
             rw_w0=rw_w0, rw_w2=rw_w2, rw_a0=rw_a0, rw_a2=rw_a2, rw_g2=rw_g2, rw_k_k=rw_k_k,
             rw_k_a=rw_k_a, rw_r_k=rw_r_k, rw_ln_w=rw_ln_w, rw_ln_b=rw_ln_b, w_branch=w_branch,
             w_out=w_out, ff_w1=ff_w1, ff_w2=ff_w2)
    depth = w_in.shape[0]
    n_ctx = x_prompt.shape[0]
    n_lat = x_sample.shape[0]
    assert 1 + n_lat <= 8

    lb_cum = jnp.cumsum(jax.nn.softmax(hg_lb.astype(F32), axis=0), axis=0)
    hg_lower = lb_cum - lb_cum[0]

    cond = jnp.concatenate([c_ctx[None, :], c, jnp.zeros((7 - n_lat, D_MODEL), F32)], axis=0)
    mod = _ada_mod(cond, ada_w, ada_b)

    zero_state = (jnp.zeros((n_ctx, 2, N_HEADS, HEAD_D, HEAD_D), F32),
                  jnp.zeros((n_ctx, 2, S5_GROUPS, S5_P), F32),
                  jnp.zeros((n_ctx, 2, S5_GROUPS, S5_P), F32),
                  jnp.zeros((n_ctx, 2, N_HEADS, HEAD_D, HEAD_D), F32),
                  jnp.zeros((n_ctx, 2, N_HEADS, HEAD_D, HEAD_D), F32))
    xp, xs = x_prompt, x_sample
    finals = []
    for l in range(depth):
        lp = _layer_params(l, p)
        xp, fin = _trunk_layer(xp, mod[l, 0:1], zero_state, False, lp, hg_lower[l])
        finals.append(fin)
        lat_init = (state_ret[:, l], state_s5_re[:, l], state_s5_im[:, l], state_hgrn[:, l], state_rwkv[:, l])
        xs, _ = _trunk_layer(xs, mod[l, 1:1 + n_lat], lat_init, True, lp, hg_lower[l])
    new_states = tuple(jnp.stack([f[i] for f in finals], axis=1) for i in range(5))
    return (xp, xs) + new_states
```

```python
import functools
import math

import jax
import jax.numpy as jnp
from jax import lax
from jax.experimental import pallas as pl
from jax.experimental.pallas import tpu as pltpu

F32 = jnp.float32
BF16 = jnp.bfloat16

D_MODEL = 1024
GRID_W = 64
MIX_W = 256
N_HEADS = 4
HEAD_D = 64
HEAD_SHIFT = 6
LANES = 128
S5_GROUPS = 16
S5_CH = 16
S5_P = 64
S5_STATE = S5_GROUPS * S5_P
D_FF = 4096
EPS = 1e-6
RW_LN_EPS = 64e-5
RW_COLS = 928

Z_COLS = 3584
ZB_RW = 0
ZB_RET = 4
ZB_S5 = 8
ZB_HG = 9

VMEM_LIMIT = 56 * 1024 * 1024

LOG_GAMMA = tuple(
    tuple(math.log1p(-2.0 ** (-(5.0 + 0.5 * di) - h)) for h in range(N_HEADS)) for di in range(2))


def _cparams(*sem):
    return pltpu.CompilerParams(dimension_semantics=sem, vmem_limit_bytes=VMEM_LIMIT)


def _dot(a, b):
    return jnp.dot(a, b, preferred_element_type=F32)


def _dot_nt(a, b):
    return lax.dot_general(a, b, (((1,), (1,)), ((), ())), preferred_element_type=F32)


def _dot_tn(a, b):
    return lax.dot_general(a, b, (((0,), (0,)), ((), ())), preferred_element_type=F32)


def _head_ones():
    r = lax.broadcasted_iota(jnp.int32, (MIX_W, MIX_W), 0) >> HEAD_SHIFT
    c = lax.broadcasted_iota(jnp.int32, (MIX_W, MIX_W), 1) >> HEAD_SHIFT
    return jnp.where(r == c, 1.0, 0.0).astype(F32)


def _head_sum(x, ones_bd):
    ones = ones_bd.astype(BF16)
    p1 = x.astype(BF16)
    r1 = x - p1.astype(F32)
    p2 = r1.astype(BF16)
    p3 = (r1 - p2.astype(F32)).astype(BF16)
    return _dot(p1, ones) + _dot(p2, ones) + _dot(p3, ones)


def _head_sum_lanes(x):
    lo = lax.broadcasted_iota(jnp.int32, (x.shape[0], LANES), 1) < HEAD_D
    parts = []
    for q in range(x.shape[1] // LANES):
        p = x[:, q * LANES:(q + 1) * LANES]
        s0 = jnp.sum(jnp.where(lo, p, 0.0), axis=1, keepdims=True)
        s1 = jnp.sum(jnp.where(lo, 0.0, p), axis=1, keepdims=True)
        parts.append(jnp.where(lo, s0, s1))
    return jnp.concatenate(parts, axis=1)


def _rms(x, w):
    return x * lax.rsqrt(jnp.mean(x * x, axis=-1, keepdims=True) + EPS) * w


def _ada_kernel(c_ref, w_ref, b_ref, o_ref):
    c = c_ref[...]
    s = c * jax.nn.sigmoid(c)
    o_ref[...] = _dot(s.astype(BF16), w_ref[...].astype(BF16)) + b_ref[...]


def _ada_mod(cond, ada_w, ada_b):
    L = ada_w.shape[0]
    n = ada_w.shape[2]
    tn = 1536
    return pl.pallas_call(
        _ada_kernel,
        grid=(L, n // tn),
        in_specs=[
            pl.BlockSpec((8, D_MODEL), lambda l, j: (0, 0)),
            pl.BlockSpec((None, D_MODEL, tn), lambda l, j: (l, 0, j)),
            pl.BlockSpec((None, 1, tn), lambda l, j: (l, 0, j)),
        ],
        out_specs=pl.BlockSpec((None, 8, tn), lambda l, j: (l, 0, j)),
        out_shape=jax.ShapeDtypeStruct((L, 8, n), F32),
        compiler_params=_cparams("parallel", "parallel"),
        name="ada_mod",
    )(cond, ada_w, ada_b.reshape(L, 1, n))


def _proj_in_kernel(x_ref, sc_ref, sh_ref, nw_ref, w_ref, z_ref):
    h = _rms(x_ref[...], nw_ref[...]) * (1.0 + sc_ref[...]) + sh_ref[...]
    z_ref[...] = _dot(h.astype(BF16), w_ref[...])


def _mod_map(bm):
    if bm == 1:
        return lambda b, *_: (0, 0, 0)
    return lambda b, *_: (b, 0, 0)


def _proj_in(x, sc, sh, nw, w_p):
    B, T, _ = x.shape
    tm = min(T, 512)
    return pl.pallas_call(
        _proj_in_kernel,
        grid=(B, T // tm),
        in_specs=[
            pl.BlockSpec((None, tm, D_MODEL), lambda b, i: (b, i, 0)),
            pl.BlockSpec((None, 1, D_MODEL), _mod_map(sc.shape[0])),
            pl.BlockSpec((None, 1, D_MODEL), _mod_map(sh.shape[0])),
            pl.BlockSpec((1, D_MODEL), lambda b, i: (0, 0)),
            pl.BlockSpec((D_MODEL, Z_COLS), lambda b, i: (0, 0)),
        ],
        out_specs=pl.BlockSpec((None, tm, Z_COLS), lambda b, i: (b, i, 0)),
        out_shape=jax.ShapeDtypeStruct((B, T, Z_COLS), F32),
        compiler_params=_cparams("parallel", "parallel"),
        name="proj_in",
    )(x, sc, sh, nw, w_p)


def _ret_steps(di, q_ref, k_ref, v_ref, o_ref, s_ref, lc):
    rev = di == 1
    rows = lax.broadcasted_iota(jnp.int32, (lc, lc), 0)
    cols = lax.broadcasted_iota(jnp.int32, (lc, lc), 1)
    rel = ((cols - rows) if rev else (rows - cols)).astype(F32)
    idx = lax.broadcasted_iota(jnp.int32, (lc, 1), 0)
    pos = ((lc - 1 - idx) if rev else idx).astype(F32)
    for h in range(N_HEADS):
        lg = LOG_GAMMA[di][h]
        sl = slice(h * HEAD_D, (h + 1) * HEAD_D)
        q = q_ref[:, sl]
        k = k_ref[:, sl] * (HEAD_D ** -0.5)
        v = v_ref[:, sl].astype(BF16)
        att = _dot_nt(q.astype(BF16), k.astype(BF16))
        att = att * jnp.where(rel >= 0.0, jnp.exp(jnp.maximum(rel, 0.0) * lg), 0.0)
        qd = q * jnp.exp((pos + 1.0) * lg)
        kd = k * jnp.exp((lc - 1.0 - pos) * lg)
        s = s_ref[di, h]
        o_ref[:, sl] = _dot(att.astype(BF16), v) + _dot(qd.astype(BF16), s.astype(BF16))
        s_ref[di, h] = s * math.exp(lc * lg) + _dot_tn(kd.astype(BF16), v)
        yield


def _s5_zoh_kernel(lre_ref, lim_ref, ldt_ref, bre_ref, bim_ref, are_ref, aim_ref, bbre_ref, bbim_ref):
    lam_re = jnp.minimum(lre_ref[...], -1e-4)
    lam_im = lim_ref[...]
    dt = jnp.exp(ldt_ref[...])
    mag = jnp.exp(dt * lam_re)
    ang = dt * lam_im
    a_re = mag * jnp.cos(ang)
    a_im = mag * jnp.sin(ang)
    den = lam_re * lam_re + lam_im * lam_im
    f_re = ((a_re - 1.0) * lam_re + a_im * lam_im) / den
    f_im = (a_im * lam_re - (a_re - 1.0) * lam_im) / den
    b_re = bre_ref[...]
    b_im = bim_ref[...]
    are_ref[...] = a_re
    aim_ref[...] = a_im
    bbre_ref[...] = f_re * b_re - f_im * b_im
    bbim_ref[...] = f_re * b_im + f_im * b_re


def _s5_zoh(lam_re, lam_im, log_dt, b_re, b_im):
    n = 2 * S5_STATE
    col = lambda t: t.reshape(n, 1)
    ldt = jnp.broadcast_to(log_dt[:, :, None], (2, S5_GROUPS, S5_P))
    outs = pl.pallas_call(
        _s5_zoh_kernel,
        out_shape=[jax.ShapeDtypeStruct((n, 1), F32), jax.ShapeDtypeStruct((n, 1), F32),
                   jax.ShapeDtypeStruct((n, S5_CH), F32), jax.ShapeDtypeStruct((n, S5_CH), F32)],
        name="s5_zoh",
    )(col(lam_re), col(lam_im), col(ldt), b_re.reshape(n, S5_CH), b_im.reshape(n, S5_CH))
    a_re, a_im, bb_re, bb_im = outs
    shp = (2, S5_GROUPS, S5_P, S5_CH)
    return a_re.reshape(2, 1, S5_STATE), a_im.reshape(2, 1, S5_STATE), bb_re.reshape(shp), bb_im.reshape(shp)


S5_RADIX = 16


def _ret_s5_kernel(*refs, L, nc):
    (uf_ref, ub_ref, are_ref, aim_ref, bre_ref, bim_ref, cre_ref, cim_ref, x0r_ref, x0i_ref,
     qf_ref, kf_ref, vf_ref, qb_ref, kb_ref, vb_ref, rs0_ref,
     yf_ref, yb_ref, fr_ref, fi_ref, rof_ref, rob_ref, rfin_ref,
     car_re, car_im) = refs[:26]
    bufs, rs_ref = refs[26:34], refs[34]
    c = pl.program_id(1)
    R = S5_RADIX
    G = L // R

    @pl.when(c == 0)
    def _():
        car_re[...] = x0r_ref[...]
        car_im[...] = x0i_ref[...]
        rs_ref[...] = rs0_ref[...]

    rr = lax.broadcasted_iota(jnp.int32, (L, L), 0)
    cc = lax.broadcasted_iota(jnp.int32, (L, L), 1)
    lg, lr = G.bit_length() - 1, R.bit_length() - 1
    perm = jnp.where(cc == ((rr & (G - 1)) << lr) + (rr >> lg), 1.0, 0.0).astype(BF16)
    unperm = jnp.where(cc == ((rr & (R - 1)) << lg) + (rr >> lr), 1.0, 0.0).astype(BF16)

    def cmul(pr, pi, qr, qi):
        return pr * qr - pi * qi, pr * qi + pi * qr

    def steps(di, u_ref, y_ref, x_re, x_im, o_re, o_im):
        rev = di == 1
        u = _dot(perm, u_ref[...].astype(BF16)).astype(BF16)
        x_re[...] = _dot(u, bre_ref[di])
        x_im[...] = _dot(u, bim_ref[di])
        yield
        ar = are_ref[di]
        ai = aim_ref[di]
        order = list(range(R - 1, -1, -1)) if rev else list(range(R))
        slab = lambda j: slice(j * G, (j + 1) * G)

        er = x_re[slab(order[0]), :]
        ei = x_im[slab(order[0]), :]
        for n, j in enumerate(order[1:]):
            tr, ti = cmul(ar, ai, er, ei)
            er = tr + x_re[slab(j), :]
            ei = ti + x_im[slab(j), :]
            if n % 4 == 3:
                yield

        a_r, a_i = ar, ai
        for _ in range(R.bit_length() - 1):
            a_r, a_i = cmul(a_r, a_i, a_r, a_i)
        zr = car_re[di]
        zi = car_im[di]
        cin_r = [None] * G
        cin_i = [None] * G
        for k in (range(G - 1, -1, -1) if rev else range(G)):
            cin_r[k] = zr
            cin_i[k] = zi
            tr, ti = cmul(a_r, a_i, zr, zi)
            zr = tr + er[k:k + 1, :]
            zi = ti + ei[k:k + 1, :]
        car_re[di] = zr
        car_im[di] = zi
        yield

        xr = jnp.concatenate(cin_r, axis=0)
        xi = jnp.concatenate(cin_i, axis=0)
        for n, j in enumerate(order):
            tr, ti = cmul(ar, ai, xr, xi)
            xr = tr + x_re[slab(j), :]
            xi = ti + x_im[slab(j), :]
            o_re[slab(j), :] = xr
            o_im[slab(j), :] = xi
            if n % 4 == 3:
                yield
        xr_all = _dot(unperm, o_re[...].astype(BF16)).astype(BF16)
        xi_all = _dot(unperm, o_im[...].astype(BF16)).astype(BF16)
        y_ref[...] = _dot(xr_all, cre_ref[...]) - _dot(xi_all, cim_ref[...])

    live = [steps(0, uf_ref, yf_ref, *bufs[0:4]), steps(1, ub_ref, yb_ref, *bufs[4:8]),
            _ret_steps(0, qf_ref, kf_ref, vf_ref, rof_ref, rs_ref, L),
            _ret_steps(1, qb_ref, kb_ref, vb_ref, rob_ref, rs_ref, L)]
    while live:
        for gen in list(live):
            if next(gen, StopIteration) is StopIteration:
                live.remove(gen)

    @pl.when(c == nc - 1)
    def _():
        fr_ref[...] = car_re[...]
        fi_ref[...] = car_im[...]
        rfin_ref[...] = rs_ref[...]


def _ret_s5(z, a_re, a_im, bblk_re, bblk_im, cblk_re, cblk_im, x0_re, x0_im, s0_ret):
    B, T, _ = z.shape
    L = min(T, 128)
    nc = T // L
    assert L % S5_RADIX == 0 and (L // S5_RADIX) & (L // S5_RADIX - 1) == 0
    full = lambda shape: pl.BlockSpec(shape, lambda b, c: (0,) * len(shape))
    sspec = pl.BlockSpec((None, 2, 1, S5_STATE), lambda b, c: (b, 0, 0, 0))
    uspec = lambda cmap, blk: pl.BlockSpec((None, L, MIX_W), lambda b, c: (b, cmap(c), blk))
    fwd = lambda c: c
    bwd = lambda c: nc - 1 - c
    buf = pltpu.VMEM((L, S5_STATE), F32)
    car = pltpu.VMEM((2, 1, S5_STATE), F32)
    seq = jax.ShapeDtypeStruct((B, T, MIX_W), F32)
    state = jax.ShapeDtypeStruct((B, 2, 1, S5_STATE), F32)
    rshape = (2, N_HEADS, HEAD_D, HEAD_D)
    rspec = pl.BlockSpec((None,) + rshape, lambda b, c: (b, 0, 0, 0, 0))
    return pl.pallas_call(
        functools.partial(_ret_s5_kernel, L=L, nc=nc),
        grid=(B, nc),
        in_specs=[
            uspec(fwd, ZB_S5), uspec(bwd, ZB_S5),
            full((2, 1, S5_STATE)), full((2, 1, S5_STATE)),
            full((2, MIX_W, S5_STATE)), full((2, MIX_W, S5_STATE)),
            full((S5_STATE, MIX_W)), full((S5_STATE, MIX_W)),
            sspec, sspec,
            uspec(fwd, ZB_RET), uspec(fwd, ZB_RET + 1), uspec(fwd, ZB_RET + 2),
            uspec(bwd, ZB_RET), uspec(bwd, ZB_RET + 1), uspec(bwd, ZB_RET + 2),
            rspec,
        ],
        out_specs=[uspec(fwd, 0), uspec(bwd, 0), sspec, sspec, uspec(fwd, 0), uspec(bwd, 0), rspec],
        out_shape=[seq, seq, state, state, seq, seq, jax.ShapeDtypeStruct((B,) + rshape, F32)],
        scratch_shapes=[car, car] + [buf] * 8 + [pltpu.VMEM(rshape, F32)],
        compiler_params=_cparams("parallel", "arbitrary"),
        name="ret_s5",
    )(z, z, a_re, a_im, bblk_re, bblk_im, cblk_re, cblk_im, x0_re, x0_im, z, z, z, z, z, z, s0_ret)


GLA_CHUNK = 16
GLA_CHUNK_SHIFT = 4
SUB = 8


def _gla_steps(q_ref, zf_ref, v_ref, lb_ref, s0_ref, o_ref, fin_ref, s_ref, a_ref, qs_ref, key_ref,
               *, rev, tb, nt):
    jb = pl.program_id(1)
    ch = GLA_CHUNK
    half = 2 * HEAD_D

    @pl.when(jb == 0)
    def _():
        s_ref[...] = s0_ref[...]

    lb = lb_ref[...]
    zf = zf_ref[...]
    q = q_ref[...]
    qs_ref[...] = q * jax.nn.sigmoid(q)
    key_ref[...] = (1.0 - lb) * jax.nn.sigmoid(-zf)
    l1 = jnp.log(lb)
    l2 = jnp.log1p(-lb) + jnp.minimum(zf, 0.0) - jnp.log1p(jnp.exp(-jnp.abs(zf)))
    lf = jnp.maximum(l1, l2) + jnp.log1p(jnp.exp(-jnp.abs(l1 - l2)))
    r = lax.broadcasted_iota(jnp.int32, (tb, tb), 0)
    c = lax.broadcasted_iota(jnp.int32, (tb, tb), 1)
    same = (r >> GLA_CHUNK_SHIFT) == (c >> GLA_CHUNK_SHIFT)
    tri = jnp.where(same & ((c >= r) if rev else (c <= r)), 1.0, 0.0).astype(BF16)
    p1 = lf.astype(BF16)
    r1 = lf - p1.astype(F32)
    p2 = r1.astype(BF16)
    p3 = (r1 - p2.astype(F32)).astype(BF16)
    a_ref[...] = _dot(tri, p1) + _dot(tri, p2) + _dot(tri, p3)

    rows = lax.broadcasted_iota(jnp.int32, (SUB, half), 0)
    lo = lax.broadcasted_iota(jnp.int32, (SUB, half), 1) < HEAD_D
    same_head = ((lax.broadcasted_iota(jnp.int32, (half, half), 0) >> HEAD_SHIFT)
                 == (lax.broadcasted_iota(jnp.int32, (half, half), 1) >> HEAD_SHIFT))

    def pair_sums(p):
        s0 = jnp.sum(jnp.where(lo, p, 0.0), axis=1, keepdims=True)
        s1 = jnp.sum(jnp.where(lo, 0.0, p), axis=1, keepdims=True)
        return jnp.where(lo, s0, s1)

    def chunk(ci):
        t0 = (tb // ch - 1 - ci if rev else ci) * ch
        win = slice(t0, t0 + ch)
        a = a_ref[win, :]
        qc = qs_ref[win, :]
        kc = key_ref[win, :]
        vc = v_ref[win, :]
        last = 0 if rev else ch - 1
        a_last = a[last:last + 1, :]
        qe = (qc * jnp.exp(a)).astype(BF16)
        ke = (kc * jnp.exp(a_last - a)).astype(BF16)
        ea = jnp.exp(a_last)
        vb = vc.astype(BF16)
        nslab = ch // SUB
        acc = [[jnp.zeros((SUB, half), F32) for _ in range(2)] for _ in range(nslab)]
        for jj in range(ch):
            for sb in range(nslab):
                r0 = sb * SUB
                if (r0 > jj) if rev else (r0 + SUB - 1 < jj):
                    continue
                whole = (r0 + SUB - 1 <= jj) if rev else (r0 >= jj)
                rs = slice(r0, r0 + SUB)
                valid = (rows + r0 <= jj) if rev else (rows + r0 >= jj)
                for hp in range(2):
                    sl = slice(hp * half, (hp + 1) * half)
                    dec = jnp.exp(jnp.minimum(a[rs, sl] - a[jj:jj + 1, sl], 0.0))
                    p = qc[rs, sl] * dec * kc[jj:jj + 1, sl]
                    if not whole:
                        p = jnp.where(valid, p, 0.0)
                    acc[sb][hp] = acc[sb][hp] + pair_sums(p) * vc[jj:jj + 1, sl]
        for hp in range(2):
            sl = slice(hp * half, (hp + 1) * half)
            s = s_ref[hp]
            inter = _dot_nt(qe[:, sl], s.astype(BF16))
            intra = jnp.concatenate([acc[sb][hp] for sb in range(nslab)], axis=0)
            o_ref[win, sl] = inter + intra
            s_ref[hp] = s * ea[:, sl] + jnp.where(same_head, _dot_tn(vb[:, sl], ke[:, sl]), 0.0)

    yield
    for ci in range(tb // ch):
        chunk(ci)
        yield

    @pl.when(jb == nt - 1)
    def _():
        fin_ref[...] = s_ref[...]


def _rw_prep_kernel(*refs, grid_shift, tm):
    if grid_shift:
        (zc_ref, zu_ref, zd_ref, mu_ref, vec_ref, w0_ref, lora_ref,
         r_ref, k2_ref, v_ref, w_ref, nkk_ref, kka_ref, g_ref, bonus_ref, buf_ref) = refs
    else:
        (zc_ref, mu_ref, vec_ref, w0_ref, lora_ref,
         r_ref, k2_ref, v_ref, w_ref, nkk_ref, kka_ref, g_ref, bonus_ref, buf_ref) = refs
    i = pl.program_id(1)
    nt = pl.num_programs(1)
    halo = GRID_W
    width = 4 * MIX_W
    z = zc_ref[...]
    buf_ref[halo:halo + tm, :] = z
    lane = lax.broadcasted_iota(jnp.int32, (tm, width), 1)
    row = lax.broadcasted_iota(jnp.int32, (tm, width), 0)
    if grid_shift:
        buf_ref[0:halo, :] = jnp.where(i > 0, zu_ref[...], 0.0)
        buf_ref[halo + tm:2 * halo + tm, :] = jnp.where(i < nt - 1, zd_ref[...], 0.0)
        col = row & (GRID_W - 1)
        left = jnp.where(col > 0, buf_ref[halo - 1:halo - 1 + tm, :], 0.0)
        right = jnp.where(col < GRID_W - 1, buf_ref[halo + 1:halo + 1 + tm, :], 0.0)
        up = buf_ref[0:tm, :]
        down = buf_ref[2 * halo:2 * halo + tm, :]
        sel = lane & 3
        shifted = jnp.where(sel == 0, left, jnp.where(sel == 1, right, jnp.where(sel == 2, up, down)))
    else:
        zrow = jnp.zeros((1, width), F32)
        buf_ref[halo - 1:halo, :] = zrow
        buf_ref[halo + tm:halo + tm + 1, :] = zrow
        prev = buf_ref[halo - 1:halo - 1 + tm, :]
        nxt = buf_ref[halo + 1:halo + 1 + tm, :]
        shifted = jnp.where((lane & 1) == 0, prev, nxt)
    zs = z + mu_ref[...] * (shifted - z)
    r = zs[:, 0:MIX_W]
    k = zs[:, MIX_W:2 * MIX_W]
    v = zs[:, 2 * MIX_W:3 * MIX_W]
    sm = zs[:, 3 * MIX_W:4 * MIX_W]
    ones_bd = _head_ones()
    a0, k_k, k_a, r_k = (vec_ref[j:j + 1, :] for j in range(4))
    a = jax.nn.sigmoid(a0 + _dot(sm.astype(BF16), lora_ref[2]))
    g_ref[...] = _dot(jax.nn.sigmoid(sm).astype(BF16), lora_ref[3])
    kk = k * k_k
    kk = kk * lax.rsqrt(_head_sum(kk * kk, ones_bd) + 1e-12)
    k2 = k * (1.0 + (a - 1.0) * k_a)
    th = jnp.tanh(sm).astype(BF16)
    for di in range(2):
        w_ref[di] = -math.exp(-0.5) * jax.nn.sigmoid(w0_ref[di:di + 1, :] + _dot(th, lora_ref[di]))
    r_ref[...] = r
    k2_ref[...] = k2
    v_ref[...] = v
    nkk_ref[...] = -kk
    kka_ref[...] = kk * a
    bonus_ref[...] = _head_sum(r * k2 * r_k, ones_bd) * v


def _rw_prep(z, mu_p, vecs, w0, lora, grid_shift):
    B, T, _ = z.shape
    width = 4 * MIX_W
    wblk = ZB_RW // 4
    if grid_shift:
        tm = min(T, 512)
        hb = tm // GRID_W
        nh = T // GRID_W
        z_specs = [
            pl.BlockSpec((None, tm, width), lambda b, i: (b, i, wblk)),
            pl.BlockSpec((None, GRID_W, width), lambda b, i: (b, jnp.maximum(i * hb - 1, 0), wblk)),
            pl.BlockSpec((None, GRID_W, width), lambda b, i: (b, jnp.minimum((i + 1) * hb, nh - 1), wblk)),
        ]
        z_args = (z, z, z)
    else:
        tm = T
        z_specs = [pl.BlockSpec((None, tm, width), lambda b, i: (b, i, wblk))]
        z_args = (z,)
    const = lambda shape: pl.BlockSpec(shape, lambda b, i: (0,) * len(shape))
    ospec = pl.BlockSpec((None, tm, MIX_W), lambda b, i: (b, i, 0))
    oshape = jax.ShapeDtypeStruct((B, T, MIX_W), F32)
    dspec = pl.BlockSpec((2, None, tm, MIX_W), lambda b, i: (0, b, i, 0))
    dshape = jax.ShapeDtypeStruct((2, B, T, MIX_W), F32)
    return pl.pallas_call(
        functools.partial(_rw_prep_kernel, grid_shift=grid_shift, tm=tm),
        grid=(B, T // tm),
        in_specs=z_specs + [const((1, width)), const((4, MIX_W)), const((2, MIX_W)),
                            const((4, MIX_W, MIX_W))],
        out_specs=[ospec, ospec, ospec, dspec, ospec, ospec, ospec, ospec],
        out_shape=[oshape, oshape, oshape, dshape, oshape, oshape, oshape, oshape],
        scratch_shapes=[pltpu.VMEM((tm + 2 * GRID_W, width), F32)],
        compiler_params=_cparams("parallel", "parallel"),
        name="rwkv_prep",
    )(*z_args, mu_p, vecs, w0, lora)


RW_CHUNK = 64
RW_CHUNK_SHIFT = 6


def _mm(a, b):
    return jnp.dot(a.astype(BF16), b.astype(BF16), preferred_element_type=F32)


def _rwkv_steps(lw_ref, a_ref, b_ref, k_ref, r_ref, v_ref, s0_ref, y_ref, fin_ref, s_ref, g_ref,
                *, rev, tb, nt):
    jb = pl.program_id(1)
    ch = RW_CHUNK
    pair = 2 * HEAD_D

    @pl.when(jb == 0)
    def _():
        s_ref[...] = s0_ref[...]

    lw = lw_ref[...]
    rr = lax.broadcasted_iota(jnp.int32, (tb, tb), 0)
    cc = lax.broadcasted_iota(jnp.int32, (tb, tb), 1)
    same = (rr >> RW_CHUNK_SHIFT) == (cc >> RW_CHUNK_SHIFT)
    tri = jnp.where(same & ((cc >= rr) if rev else (cc <= rr)), 1.0, 0.0).astype(BF16)
    p1 = lw.astype(BF16)
    r1 = lw - p1.astype(F32)
    p2 = r1.astype(BF16)
    p3 = (r1 - p2.astype(F32)).astype(BF16)
    g_ref[...] = _dot(tri, p1) + _dot(tri, p2) + _dot(tri, p3)

    si = lax.broadcasted_iota(jnp.int32, (ch, ch), 0)
    ri = lax.broadcasted_iota(jnp.int32, (ch, ch), 1)
    if rev:
        si, ri = ch - 1 - si, ch - 1 - ri
    strict = ri < si
    incl = ri <= si
    eye = jnp.where(ri == si, 1.0, 0.0).astype(F32)
    levels = [((si >> (lv + 1)) == (ri >> (lv + 1))) & (((si >> lv) & 1) == 1) & (((ri >> lv) & 1) == 0)
              for lv in range(RW_CHUNK_SHIFT)]
    lo = lax.broadcasted_iota(jnp.int32, (ch, pair), 1) < HEAD_D
    same_head = ((lax.broadcasted_iota(jnp.int32, (pair, pair), 0) >> HEAD_SHIFT)
                 == (lax.broadcasted_iota(jnp.int32, (pair, pair), 1) >> HEAD_SHIFT))

    nch = tb // ch
    g = g_ref[...]
    e_g = jnp.exp(g)
    e_ng = jnp.exp(-g)
    at_f = a_ref[...] * jnp.exp(g - lw)
    rt_f = r_ref[...] * e_g
    at = at_f.astype(BF16)
    rt = rt_f.astype(BF16)
    first = (lax.broadcasted_iota(jnp.int32, (tb, MIX_W), 1) & HEAD_D) == 0
    at_h = [jnp.where(first, at_f, 0.0).astype(BF16), jnp.where(first, 0.0, at_f).astype(BF16)]
    rt_h = [jnp.where(first, rt_f, 0.0).astype(BF16), jnp.where(first, 0.0, rt_f).astype(BF16)]
    bt = (b_ref[...] * e_ng).astype(BF16)
    kt = (k_ref[...] * e_ng).astype(BF16)
    vb = v_ref[...].astype(BF16)

    items = [(c, hp, hh) for c in range(nch) for hp in range(2) for hh in range(2)]
    rows = lambda c: slice(c * ch, (c + 1) * ch)
    lanes = lambda hp: slice(hp * pair, (hp + 1) * pair)
    n_m, p_m, m_m, q_m = {}, {}, {}, {}
    for it in items:
        c, hp, hh = it
        ar = jnp.concatenate([at_h[hh][rows(c), lanes(hp)], rt_h[hh][rows(c), lanes(hp)]], axis=0)
        np_ = _dot_nt(ar, bt[rows(c), lanes(hp)])
        mq = _dot_nt(ar, kt[rows(c), lanes(hp)])
        n_m[it] = jnp.where(strict, np_[:ch], 0.0).astype(BF16)
        p_m[it] = jnp.where(incl, np_[ch:], 0.0).astype(BF16)
        m_m[it] = jnp.where(strict, mq[:ch], 0.0).astype(BF16)
        q_m[it] = jnp.where(incl, mq[ch:], 0.0).astype(BF16)
        if hp == 1 and hh == 1:
            yield
    t_m = {it: eye + jnp.where(levels[0], n_m[it].astype(F32), 0.0) for it in items}
    for lv in range(1, RW_CHUNK_SHIFT):
        tn = {it: _mm(t_m[it], n_m[it]) for it in items}
        yield
        t_m = {it: t_m[it] + jnp.where(levels[lv], _mm(tn[it], t_m[it]), 0.0) for it in items}
        yield
    mv = {it: _mm(m_m[it], vb[rows(it[0]), lanes(it[1])]) for it in items}
    qv = {it: _mm(q_m[it], vb[rows(it[0]), lanes(it[1])]) for it in items}
    yield
    y2 = {it: _mm(t_m[it], jnp.concatenate([at[rows(it[0]), lanes(it[1])], mv[it].astype(BF16)], axis=1))
          for it in items}
    yield

    last = 0 if rev else ch - 1
    for ci in range(nch):
        c = nch - 1 - ci if rev else ci
        g_c = g[c * ch + last:c * ch + last + 1, :]
        e_gc = jnp.exp(g_c - g[rows(c), :])
        bh = (b_ref[rows(c), :] * e_gc).astype(BF16)
        kh = (k_ref[rows(c), :] * e_gc).astype(BF16)
        dec_c = jnp.exp(g_c)
        xs, sas, ss = [], [], []
        for hp in range(2):
            s = s_ref[hp]
            w2 = jnp.where(lo, y2[(c, hp, 0)][:, :pair], y2[(c, hp, 1)][:, :pair]).astype(BF16)
            xs.append(_dot_nt(jnp.concatenate([w2, rt[rows(c), lanes(hp)]], axis=0), s.astype(BF16)))
            ss.append(s)
        for hp in range(2):
            w1 = jnp.where(lo, y2[(c, hp, 0)][:, pair:], y2[(c, hp, 1)][:, pair:])
            sas.append(w1 + xs[hp][:ch])
        for hp in range(2):
            sa = sas[hp]
            y_ref[rows(c), lanes(hp)] = xs[hp][ch:] + jnp.where(
                lo, _mm(p_m[(c, hp, 0)], sa) + qv[(c, hp, 0)], _mm(p_m[(c, hp, 1)], sa) + qv[(c, hp, 1)])
            upd = _dot_tn(jnp.concatenate([sa.astype(BF16), vb[rows(c), lanes(hp)]], axis=0),
                          jnp.concatenate([bh[:, lanes(hp)], kh[:, lanes(hp)]], axis=0))
            s_ref[hp] = ss[hp] * dec_c[:, lanes(hp)] + jnp.where(same_head, upd, 0.0)
        yield

    @pl.when(jb == nt - 1)
    def _():
        fin_ref[...] = s_ref[...]


def _hg_rw_kernel(*refs, rev, tb, nt):
    gla_in, rw_in = refs[0:5], refs[5:12]
    gla_out, rw_out = refs[12:14], refs[14:16]
    gla_scr, rw_scr = refs[16:20], refs[20:22]
    gla = _gla_steps(*gla_in, *gla_out, *gla_scr, rev=rev, tb=tb, nt=nt)
    rwkv = _rwkv_steps(*rw_in, *rw_out, *rw_scr, rev=rev, tb=tb, nt=nt)
    live = [gla, rwkv]
    while live:
        for gen in list(live):
            if next(gen, StopIteration) is StopIteration:
                live.remove(gen)


def _head_pair_blockdiag(s):
    B = s.shape[0]
    s = s.reshape(B, 2, 2, HEAD_D, HEAD_D)
    zero = jnp.zeros_like(s[:, :, 0])
    return jnp.concatenate([jnp.concatenate([s[:, :, 0], zero], axis=-1),
                            jnp.concatenate([zero, s[:, :, 1]], axis=-1)], axis=-2)


def _head_pair_blocks(t):
    B = t.shape[0]
    t = jnp.stack([t[:, :, :HEAD_D, :HEAD_D], t[:, :, HEAD_D:, HEAD_D:]], axis=2)
    return t.reshape(B, N_HEADS, HEAD_D, HEAD_D)


def _hg_rw(z, lb, s0_hg, lw, a, b, k, r, v, s0_rw, rev):
    B, T, _ = v.shape
    pair = 2 * HEAD_D
    tb = min(T, 256)
    nt = T // tb
    di = 1 if rev else 0
    tmap = (lambda j: nt - 1 - j) if rev else (lambda j: j)
    zspec = lambda blk: pl.BlockSpec((None, tb, MIX_W), lambda bi, j: (bi, tmap(j), blk))
    spec = zspec(0)
    sspec = pl.BlockSpec((None, 2, pair, pair), lambda bi, j: (bi, 0, 0, 0))
    state = jax.ShapeDtypeStruct((B, 2, pair, pair), F32)
    seq = jax.ShapeDtypeStruct((B, T, MIX_W), F32)
    blk = pltpu.VMEM((tb, MIX_W), F32)
    st = pltpu.VMEM((2, pair, pair), F32)
    o, fin_hg, y, fin_rw = pl.pallas_call(
        functools.partial(_hg_rw_kernel, rev=rev, tb=tb, nt=nt),
        grid=(B, nt),
        in_specs=[zspec(ZB_HG), zspec(ZB_HG + 1 + di), zspec(ZB_HG + 3),
                  pl.BlockSpec((1, MIX_W), lambda bi, j: (0, 0)), sspec,
                  pl.BlockSpec((None, None, tb, MIX_W), lambda bi, j: (di, bi, tmap(j), 0)),
                  spec, spec, spec, spec, spec, sspec],
        out_specs=[spec, sspec, spec, sspec],
        out_shape=[seq, state, seq, state],
        scratch_shapes=[st, blk, blk, blk, st, blk],
        compiler_params=_cparams("parallel", "arbitrary"),
        name="hgrn_rwkv_bwd" if rev else "hgrn_rwkv_fwd",
    )(z, z, z, lb[di:di + 1], _head_pair_blockdiag(s0_hg),
      lw, a, b, k, r, v, _head_pair_blockdiag(s0_rw))
    return o, _head_pair_blocks(fin_hg), y, _head_pair_blocks(fin_rw)


def _mix_kernel(x_ref, sc_ref, sh_ref, g1_ref, nw0_ref, nw_ref, wg_ref,
                retf_ref, retb_ref, retg_ref, s5f_ref, s5b_ref, s5u_ref, hgf_ref, hgb_ref, hgg_ref, rwf_ref, rwr_ref,
                rwb_ref, rwg_ref,
                vec_ref, glu_w_ref, wbr_ref, wout_ref, o_ref):
    h = (_rms(x_ref[...], nw0_ref[...]) * (1.0 + sc_ref[...]) + sh_ref[...]).astype(BF16)
    gn_w, s5_d, glu_b, hg_w, ln_w, ln_b = (vec_ref[j:j + 1, :] for j in range(6))

    def group_norm(o, eps):
        mu = _head_sum_lanes(o) * (1.0 / HEAD_D)
        oc = o - mu
        var = _head_sum_lanes(oc * oc) * (1.0 / HEAD_D)
        return oc * lax.rsqrt(var + eps)

    g = retg_ref[...]
    y_ret = group_norm(retf_ref[...] + retb_ref[...], EPS) * gn_w * (g * jax.nn.sigmoid(g))

    y = s5_d * s5u_ref[...] + s5f_ref[...] + s5b_ref[...]
    yg = jax.nn.gelu(y)
    y_s5 = yg * jax.nn.sigmoid(_dot(yg.astype(BF16), glu_w_ref[...]) + glu_b)

    o = hgf_ref[...] + hgb_ref[...]
    g = hgg_ref[...]
    ms = _head_sum_lanes(o * o) * (1.0 / HEAD_D)
    y_hg = o * lax.rsqrt(ms + EPS) * hg_w * (g * jax.nn.sigmoid(g))

    y = group_norm(rwf_ref[...] + rwr_ref[...], RW_LN_EPS) * ln_w + ln_b
    y_rw = (y + rwb_ref[...]) * rwg_ref[...]

    mixed = None
    for m, ym in enumerate((y_ret, y_s5, y_hg, y_rw)):
        br = _dot(ym.astype(BF16), wbr_ref[m])
        term = jax.nn.sigmoid(_dot(h, wg_ref[:, m * D_MODEL:(m + 1) * D_MODEL])) * br
        mixed = term if mixed is None else mixed + term
    mixed = _dot(mixed.astype(BF16), wout_ref[...])
    o_ref[...] = x_ref[...] + g1_ref[...] * _rms(mixed, nw_ref[...])


def _mix(x, sc, sh, g1, nw0, nw, w_gate, z, ret_o, s5_y, hg_o, rw_y, rw_bonus, rw_g, vecs, glu_w,
         w_branch, w_out):
    B, T, _ = x.shape
    tm = min(T, 256)
    xspec = pl.BlockSpec((None, tm, D_MODEL), lambda b, i: (b, i, 0))
    zspec = lambda blk: pl.BlockSpec((None, tm, MIX_W), lambda b, i: (b, i, blk))
    const = lambda shape: pl.BlockSpec(shape, lambda b, i: (0,) * len(shape))
    mspec = lambda m: pl.BlockSpec((None, 1, D_MODEL), _mod_map(m.shape[0]))
    return pl.pallas_call(
        _mix_kernel,
        grid=(B, T // tm),
        in_specs=[
            xspec, mspec(sc), mspec(sh), mspec(g1),
            const((1, D_MODEL)), const((1, D_MODEL)),
            const((D_MODEL, 4 * D_MODEL)),
            zspec(0), zspec(0), zspec(ZB_RET + 3),
            zspec(0), zspec(0), zspec(ZB_S5),
            zspec(0), zspec(0), zspec(ZB_HG + 4),
            zspec(0), zspec(0), zspec(0), zspec(0),
            const((6, MIX_W)), const((MIX_W, MIX_W)),
            const((4, MIX_W, D_MODEL)), const((D_MODEL, D_MODEL)),
        ],
        out_specs=xspec,
        out_shape=jax.ShapeDtypeStruct((B, T, D_MODEL), F32),
        compiler_params=_cparams("parallel", "parallel"),
        name="mix_out",
    )(x, sc, sh, g1, nw0, nw, w_gate, ret_o[0], ret_o[1], z, s5_y[0], s5_y[1], z, hg_o[0], hg_o[1], z, rw_y[0], rw_y[1],
      rw_bonus, rw_g, vecs, glu_w, w_branch, w_out)


FFN_CHUNK = 1024


def _ffn_kernel(x_ref, sc_ref, sh_ref, g2_ref, nw2_ref, nw3_ref, w1_ref, w2_ref, o_ref):
    x = x_ref[...]
    h = (_rms(x, nw2_ref[...]) * (1.0 + sc_ref[...]) + sh_ref[...]).astype(BF16)
    acc = None
    for j in range(D_FF // FFN_CHUNK):
        cols = slice(j * FFN_CHUNK, (j + 1) * FFN_CHUNK)
        a = jnp.maximum(_dot(h, w1_ref[:, cols]), 0.0)
        part = _dot((a * a).astype(BF16), w2_ref[cols, :])
        acc = part if acc is None else acc + part
    o_ref[...] = x + g2_ref[...] * _rms(acc, nw3_ref[...])


def _ffn(x, sc, sh, g2, nw2, nw3, w1, w2):
    B, T, _ = x.shape
    tm = min(T, 512)
    xspec = pl.BlockSpec((None, tm, D_MODEL), lambda b, i: (b, i, 0))
    mspec = lambda m: pl.BlockSpec((None, 1, D_MODEL), _mod_map(m.shape[0]))
    const = pl.BlockSpec((1, D_MODEL), lambda b, i: (0, 0))
    return pl.pallas_call(
        _ffn_kernel,
        grid=(B, T // tm),
        in_specs=[xspec, mspec(sc), mspec(sh), mspec(g2), const, const,
                  pl.BlockSpec((D_MODEL, D_FF), lambda b, i: (0, 0)),
                  pl.BlockSpec((D_FF, D_MODEL), lambda b, i: (0, 0))],
        out_specs=xspec,
        out_shape=jax.ShapeDtypeStruct((B, T, D_MODEL), F32),
        compiler_params=_cparams("parallel", "parallel"),
        name="ffn",
    )(x, sc, sh, g2, nw2, nw3, w1, w2)


def _layer_params(l, p):
    w = p['w_in'][l]
    w_p = jnp.concatenate(
        [w[:, 2560:3488], jnp.zeros((D_MODEL, 4 * MIX_W - RW_COLS), F32),
         w[:, 0:1024], w[:, 1024:1280], w[:, 1280:2560]], axis=1).astype(BF16)
    w_gate = w[:, 3488:7584].astype(BF16)

    a_re, a_im, bb_re, bb_im = _s5_zoh(p['s5_lam_re'][l], p['s5_lam_im'][l], p['s5_log_dt'][l],
                                       p['s5_b_re'][l], p['s5_b_im'][l])
    eye = jnp.eye(S5_GROUPS, dtype=F32)
    bblk = lambda bb: jnp.einsum('dgph,gk->dghkp', bb, eye).reshape(2, MIX_W, S5_STATE).astype(BF16)
    cblk = lambda c: jnp.einsum('ghp,gk->gpkh', c, eye).reshape(S5_STATE, MIX_W).astype(BF16)

    def lora_pad(m, row0):
        return jnp.zeros((MIX_W, MIX_W), F32).at[row0:row0 + m.shape[0]].set(m)

    lora = jnp.stack([lora_pad(p['rw_w2'][l, 0], 0), lora_pad(p['rw_w2'][l, 1], 32),
                      lora_pad(p['rw_a2'][l], 64), lora_pad(p['rw_g2'][l], 96)]).astype(BF16)
    mu_p = jnp.concatenate([p['rw_mu'][l], jnp.zeros((4 * MIX_W - RW_COLS,), F32)]).reshape(1, 4 * MIX_W)
    return dict(
        w_p=w_p, w_gate=w_gate, nw=p['norm_w'][l],
        s5=(a_re, a_im, bblk(bb_re), bblk(bb_im), cblk(p['s5_c_re'][l]), cblk(p['s5_c_im'][l])),
        rw_mu=mu_p, rw_lora=lora, rw_w0=p['rw_w0'][l],
        rw_vecs=jnp.stack([p['rw_a0'][l], p['rw_k_k'][l], p['rw_k_a'][l], p['rw_r_k'][l]]),
        mix_vecs=jnp.stack([p['ret_gn_w'][l], p['s5_d'][l], p['s5_glu_b'][l], p['hg_norm_w'][l],
                            p['rw_ln_w'][l], p['rw_ln_b'][l]]),
        glu_w=p['s5_glu_w'][l].astype(BF16),
        w_branch=p['w_branch'][l].astype(BF16), w_out=p['w_out'][l].astype(BF16),
        ff_w1=p['ff_w1'][l].astype(BF16), ff_w2=p['ff_w2'][l].astype(BF16),
    )


def _trunk_layer(x, mod, init, grid_shift, lp, hg_lb):
    B, T, _ = x.shape
    sh1, sc1, g1, sh2, sc2, g2 = (m[:, None, :] for m in jnp.split(mod, 6, axis=-1))
    nw = lp['nw']
    if mod.shape[0] == 1:
        tok = lambda t: t.reshape(t.shape[:-3] + (1, B * T, t.shape[-1]))
    else:
        tok = lambda t: t
    z = _proj_in(tok(x), sc1, sh1, nw[0:1], lp['w_p']).reshape(B, T, Z_COLS)

    s_ret, s_s5r, s_s5i, s_hg, s_rw = init
    s5_f, s5_b, f_s5r, f_s5i, ret_f, ret_b, f_ret = _ret_s5(
        z, *lp['s5'], s_s5r.reshape(B, 2, 1, S5_STATE), s_s5i.reshape(B, 2, 1, S5_STATE), s_ret)
    f_s5r = f_s5r.reshape(B, 2, S5_GROUPS, S5_P)
    f_s5i = f_s5i.reshape(B, 2, S5_GROUPS, S5_P)

    r, k2, v, w, nkk, kka, rw_g, bonus = _rw_prep(z, lp['rw_mu'], lp['rw_vecs'], lp['rw_w0'],
                                                  lp['rw_lora'], grid_shift)
    s_hg_t = jnp.swapaxes(s_hg, -1, -2)
    hg_f, f_hg_f, rw_f, f_rw_f = _hg_rw(z, hg_lb, s_hg_t[:, 0], w, nkk, kka, k2, r, v, s_rw[:, 0], False)
    hg_b, f_hg_b, rw_b, f_rw_b = _hg_rw(z, hg_lb, s_hg_t[:, 1], w, nkk, kka, k2, r, v, s_rw[:, 1], True)
    f_hg = jnp.swapaxes(jnp.stack([f_hg_f, f_hg_b], axis=1), -1, -2)
    f_rw = jnp.stack([f_rw_f, f_rw_b], axis=1)

    x = _mix(tok(x), sc1, sh1, g1, nw[0:1], nw[1:2], lp['w_gate'], tok(z), (tok(ret_f), tok(ret_b)), (tok(s5_f), tok(s5_b)),
             (tok(hg_f), tok(hg_b)), (tok(rw_f), tok(rw_b)), tok(bonus), tok(rw_g),
             lp['mix_vecs'], lp['glu_w'], lp['w_branch'], lp['w_out'])
    x = _ffn(x, sc2, sh2, g2, nw[2:3], nw[3:4], lp['ff_w1'], lp['ff_w2'])
    return x.reshape(B, T, D_MODEL), (f_ret, f_s5r, f_s5i, f_hg, f_rw)


def kernel(x_prompt, x_sample, state_ret, state_s5_re, state_s5_im, state_hgrn, state_rwkv, c, c_ctx, ada_w, ada_b, norm_w, w_in, ret_gn_w, s5_lam_re, s5_lam_im, s5_log_dt, s5_b_re, s5_b_im, s5_c_re, s5_c_im, s5_d, s5_glu_w, s5_glu_b, hg_lb, hg_norm_w, rw_mu, rw_w0, rw_w2, rw_a0, rw_a2, rw_g2, rw_k_k, rw_k_a, rw_r_k, rw_ln_w, rw_ln_b, w_branch, w_out, ff_w1, ff_w2):
    p = dict(norm_w=norm_w, w_in=w_in, ret_gn_w=ret_gn_w, s5_lam_re=s5_lam_re, s5_lam_im=s5_lam_im,
             s5_log_dt=s5_log_dt, s5_b_re=s5_b_re, s5_b_im=s5_b_im, s5_c_re=s5_c_re, s5_c_im=s5_c_im,
             s5_d=s5_d, s5_glu_w=s5_glu_w, s5_glu_b=s5_glu_b, hg_norm_w=hg_norm_w, rw_mu=rw_mu,
             rw_w0=rw_w0, rw_w2=rw_w2, rw_a0=rw_a0, rw_a2=rw_a2, rw_g2=rw_g2, rw_k_k=rw_k_k,
             rw_k_a=rw_k_a, rw_r_k=rw_r_k, rw_ln_w=rw_ln_w, rw_ln_b=rw_ln_b, w_branch=w_branch,
             w_out=w_out, ff_w1=ff_w1, ff_w2=ff_w2)
    depth = w_in.shape[0]
    n_ctx = x_prompt.shape[0]
    n_lat = x_sample.shape[0]
    assert 1 + n_lat <= 8

    lb_cum = jnp.cumsum(jax.nn.softmax(hg_lb.astype(F32), axis=0), axis=0)
    hg_lower = lb_cum - lb_cum[0]

    cond = jnp.concatenate([c_ctx[None, :], c, jnp.zeros((7 - n_lat, D_MODEL), F32)], axis=0)
    mod = _ada_mod(cond, ada_w, ada_b)

    zero_state = (jnp.zeros((n_ctx, 2, N_HEADS, HEAD_D, HEAD_D), F32),
                  jnp.zeros((n_ctx, 2, S5_GROUPS, S5_P), F32),
                  jnp.zeros((n_ctx, 2, S5_GROUPS, S5_P), F32),
                  jnp.zeros((n_ctx, 2, N_HEADS, HEAD_D, HEAD_D), F32),
                  jnp.zeros((n_ctx, 2, N_HEADS, HEAD_D, HEAD_D), F32))
    xp, xs = x_prompt, x_sample
    finals = []
    for l in range(depth):
        lp = _layer_params(l, p)
        xp, fin = _trunk_layer(xp, mod[l, 0:1], zero_state, False, lp, hg_lower[l])
        finals.append(fin)
        lat_init = (state_ret[:, l], state_s5_re[:, l], state_s5_im[:, l], state_hgrn[:, l], state_rwkv[:, l])
        xs, _ = _trunk_layer(xs, mod[l, 1:1 + n_lat], lat_init, True, lp, hg_lower[l])
    new_states = tuple(jnp.stack([f[i] for f in finals], axis=1) for i in range(5))
    return (xp, xs) + new_states
```

```python
import functools
import math

import jax
import jax.numpy as jnp
from jax import lax
from jax.experimental import pallas as pl
from jax.experimental.pallas import tpu as pltpu

F32 = jnp.float32
BF16 = jnp.bfloat16

D_MODEL = 1024
GRID_W = 64
MIX_W = 256
N_HEADS = 4
HEAD_D = 64
HEAD_SHIFT = 6
LANES = 128
S5_GROUPS = 16
S5_CH = 16
S5_P = 64
S5_STATE = S5_GROUPS * S5_P
D_FF = 4096
EPS = 1e-6
RW_LN_EPS = 64e-5
RW_COLS = 928

Z_COLS = 3584
ZB_RW = 0
ZB_RET = 4
ZB_S5 = 8
ZB_HG = 9

VMEM_LIMIT = 56 * 1024 * 1024

LOG_GAMMA = tuple(
    tuple(math.log1p(-2.0 ** (-(5.0 + 0.5 * di) - h)) for h in range(N_HEADS)) for di in range(2))


def _cparams(*sem):
    return pltpu.CompilerParams(dimension_semantics=sem, vmem_limit_bytes=VMEM_LIMIT)


def _dot(a, b):
    return jnp.dot(a, b, preferred_element_type=F32)


def _dot_nt(a, b):
    return lax.dot_general(a, b, (((1,), (1,)), ((), ())), preferred_element_type=F32)


def _dot_tn(a, b):
    return lax.dot_general(a, b, (((0,), (0,)), ((), ())), preferred_element_type=F32)


def _head_ones():
    r = lax.broadcasted_iota(jnp.int32, (MIX_W, MIX_W), 0) >> HEAD_SHIFT
    c = lax.broadcasted_iota(jnp.int32, (MIX_W, MIX_W), 1) >> HEAD_SHIFT
    return jnp.where(r == c, 1.0, 0.0).astype(F32)


def _head_sum(x, ones_bd):
    ones = ones_bd.astype(BF16)
    p1 = x.astype(BF16)
    r1 = x - p1.astype(F32)
    p2 = r1.astype(BF16)
    p3 = (r1 - p2.astype(F32)).astype(BF16)
    return _dot(p1, ones) + _dot(p2, ones) + _dot(p3, ones)


def _head_sum_lanes(x):
    lo = lax.broadcasted_iota(jnp.int32, (x.shape[0], LANES), 1) < HEAD_D
    parts = []
    for q in range(x.shape[1] // LANES):
        p = x[:, q * LANES:(q + 1) * LANES]
        s0 = jnp.sum(jnp.where(lo, p, 0.0), axis=1, keepdims=True)
        s1 = jnp.sum(jnp.where(lo, 0.0, p), axis=1, keepdims=True)
        parts.append(jnp.where(lo, s0, s1))
    return jnp.concatenate(parts, axis=1)


def _rms(x, w):
    return x * lax.rsqrt(jnp.mean(x * x, axis=-1, keepdims=True) + EPS) * w


def _ada_kernel(c_ref, w_ref, b_ref, o_ref):
    c = c_ref[...]
    s = c * jax.nn.sigmoid(c)
    o_ref[...] = _dot(s.astype(BF16), w_ref[...].astype(BF16)) + b_ref[...]


def _ada_mod(cond, ada_w, ada_b):
    L = ada_w.shape[0]
    n = ada_w.shape[2]
    tn = 1536
    return pl.pallas_call(
        _ada_kernel,
        grid=(L, n // tn),
        in_specs=[
            pl.BlockSpec((8, D_MODEL), lambda l, j: (0, 0)),
            pl.BlockSpec((None, D_MODEL, tn), lambda l, j: (l, 0, j)),
            pl.BlockSpec((None, 1, tn), lambda l, j: (l, 0, j)),
        ],
        out_specs=pl.BlockSpec((None, 8, tn), lambda l, j: (l, 0, j)),
        out_shape=jax.ShapeDtypeStruct((L, 8, n), F32),
        compiler_params=_cparams("parallel", "parallel"),
        name="ada_mod",
    )(cond, ada_w, ada_b.reshape(L, 1, n))


def _proj_in_kernel(x_ref, sc_ref, sh_ref, nw_ref, w_ref, z_ref):
    h = _rms(x_ref[...], nw_ref[...]) * (1.0 + sc_ref[...]) + sh_ref[...]
    z_ref[...] = _dot(h.astype(BF16), w_ref[...])


def _mod_map(bm):
    if bm == 1:
        return lambda b, *_: (0, 0, 0)
    return lambda b, *_: (b, 0, 0)


def _proj_in(x, sc, sh, nw, w_p):
    B, T, _ = x.shape
    tm = min(T, 512)
    return pl.pallas_call(
        _proj_in_kernel,
        grid=(B, T // tm),
        in_specs=[
            pl.BlockSpec((None, tm, D_MODEL), lambda b, i: (b, i, 0)),
            pl.BlockSpec((None, 1, D_MODEL), _mod_map(sc.shape[0])),
            pl.BlockSpec((None, 1, D_MODEL), _mod_map(sh.shape[0])),
            pl.BlockSpec((1, D_MODEL), lambda b, i: (0, 0)),
            pl.BlockSpec((D_MODEL, Z_COLS), lambda b, i: (0, 0)),
        ],
        out_specs=pl.BlockSpec((None, tm, Z_COLS), lambda b, i: (b, i, 0)),
        out_shape=jax.ShapeDtypeStruct((B, T, Z_COLS), F32),
        compiler_params=_cparams("parallel", "parallel"),
        name="proj_in",
    )(x, sc, sh, nw, w_p)


def _ret_steps(di, q_ref, k_ref, v_ref, o_ref, s_ref, lc):
    rev = di == 1
    rows = lax.broadcasted_iota(jnp.int32, (lc, lc), 0)
    cols = lax.broadcasted_iota(jnp.int32, (lc, lc), 1)
    rel = ((cols - rows) if rev else (rows - cols)).astype(F32)
    idx = lax.broadcasted_iota(jnp.int32, (lc, 1), 0)
    pos = ((lc - 1 - idx) if rev else idx).astype(F32)
    for h in range(N_HEADS):
        lg = LOG_GAMMA[di][h]
        sl = slice(h * HEAD_D, (h + 1) * HEAD_D)
        q = q_ref[:, sl]
        k = k_ref[:, sl] * (HEAD_D ** -0.5)
        v = v_ref[:, sl].astype(BF16)
        att = _dot_nt(q.astype(BF16), k.astype(BF16))
        att = att * jnp.where(rel >= 0.0, jnp.exp(jnp.maximum(rel, 0.0) * lg), 0.0)
        qd = q * jnp.exp((pos + 1.0) * lg)
        kd = k * jnp.exp((lc - 1.0 - pos) * lg)
        s = s_ref[di, h]
        o_ref[:, sl] = _dot(att.astype(BF16), v) + _dot(qd.astype(BF16), s.astype(BF16))
        s_ref[di, h] = s * math.exp(lc * lg) + _dot_tn(kd.astype(BF16), v)
        yield


def _s5_zoh_kernel(lre_ref, lim_ref, ldt_ref, bre_ref, bim_ref, are_ref, aim_ref, bbre_ref, bbim_ref):
    lam_re = jnp.minimum(lre_ref[...], -1e-4)
    lam_im = lim_ref[...]
    dt = jnp.exp(ldt_ref[...])
    mag = jnp.exp(dt * lam_re)
    ang = dt * lam_im
    a_re = mag * jnp.cos(ang)
    a_im = mag * jnp.sin(ang)
    den = lam_re * lam_re + lam_im * lam_im
    f_re = ((a_re - 1.0) * lam_re + a_im * lam_im) / den
    f_im = (a_im * lam_re - (a_re - 1.0) * lam_im) / den
    b_re = bre_ref[...]
    b_im = bim_ref[...]
    are_ref[...] = a_re
    aim_ref[...] = a_im
    bbre_ref[...] = f_re * b_re - f_im * b_im
    bbim_ref[...] = f_re * b_im + f_im * b_re


def _s5_zoh(lam_re, lam_im, log_dt, b_re, b_im):
    n = 2 * S5_STATE
    col = lambda t: t.reshape(n, 1)
    ldt = jnp.broadcast_to(log_dt[:, :, None], (2, S5_GROUPS, S5_P))
    outs = pl.pallas_call(
        _s5_zoh_kernel,
        out_shape=[jax.ShapeDtypeStruct((n, 1), F32), jax.ShapeDtypeStruct((n, 1), F32),
                   jax.ShapeDtypeStruct((n, S5_CH), F32), jax.ShapeDtypeStruct((n, S5_CH), F32)],
        name="s5_zoh",
    )(col(lam_re), col(lam_im), col(ldt), b_re.reshape(n, S5_CH), b_im.reshape(n, S5_CH))
    a_re, a_im, bb_re, bb_im = outs
    shp = (2, S5_GROUPS, S5_P, S5_CH)
    return a_re.reshape(2, 1, S5_STATE), a_im.reshape(2, 1, S5_STATE), bb_re.reshape(shp), bb_im.reshape(shp)


S5_RADIX = 16


def _ret_s5_kernel(*refs, L, nc):
    (uf_ref, ub_ref, are_ref, aim_ref, bre_ref, bim_ref, cre_ref, cim_ref, x0r_ref, x0i_ref,
     qf_ref, kf_ref, vf_ref, qb_ref, kb_ref, vb_ref, rs0_ref,
     yf_ref, yb_ref, fr_ref, fi_ref, rof_ref, rob_ref, rfin_ref,
     car_re, car_im) = refs[:26]
    bufs, rs_ref = refs[26:34], refs[34]
    c = pl.program_id(1)
    R = S5_RADIX
    G = L // R

    @pl.when(c == 0)
    def _():
        car_re[...] = x0r_ref[...]
        car_im[...] = x0i_ref[...]
        rs_ref[...] = rs0_ref[...]

    rr = lax.broadcasted_iota(jnp.int32, (L, L), 0)
    cc = lax.broadcasted_iota(jnp.int32, (L, L), 1)
    lg, lr = G.bit_length() - 1, R.bit_length() - 1
    perm = jnp.where(cc == ((rr & (G - 1)) << lr) + (rr >> lg), 1.0, 0.0).astype(BF16)
    unperm = jnp.where(cc == ((rr & (R - 1)) << lg) + (rr >> lr), 1.0, 0.0).astype(BF16)

    def cmul(pr, pi, qr, qi):
        return pr * qr - pi * qi, pr * qi + pi * qr

    def steps(di, u_ref, y_ref, x_re, x_im, o_re, o_im):
        rev = di == 1
        u = _dot(perm, u_ref[...].astype(BF16)).astype(BF16)
        x_re[...] = _dot(u, bre_ref[di])
        x_im[...] = _dot(u, bim_ref[di])
        yield
        ar = are_ref[di]
        ai = aim_ref[di]
        order = list(range(R - 1, -1, -1)) if rev else list(range(R))
        slab = lambda j: slice(j * G, (j + 1) * G)

        er = x_re[slab(order[0]), :]
        ei = x_im[slab(order[0]), :]
        for n, j in enumerate(order[1:]):
            tr, ti = cmul(ar, ai, er, ei)
            er = tr + x_re[slab(j), :]
            ei = ti + x_im[slab(j), :]
            if n % 4 == 3:
                yield

        a_r, a_i = ar, ai
        for _ in range(R.bit_length() - 1):
            a_r, a_i = cmul(a_r, a_i, a_r, a_i)
        zr = car_re[di]
        zi = car_im[di]
        cin_r = [None] * G
        cin_i = [None] * G
        for k in (range(G - 1, -1, -1) if rev else range(G)):
            cin_r[k] = zr
            cin_i[k] = zi
            tr, ti = cmul(a_r, a_i, zr, zi)
            zr = tr + er[k:k + 1, :]
            zi = ti + ei[k:k + 1, :]
        car_re[di] = zr
        car_im[di] = zi
        yield

        xr = jnp.concatenate(cin_r, axis=0)
        xi = jnp.concatenate(cin_i, axis=0)
        for n, j in enumerate(order):
            tr, ti = cmul(ar, ai, xr, xi)
            xr = tr + x_re[slab(j), :]
            xi = ti + x_im[slab(j), :]
            o_re[slab(j), :] = xr
            o_im[slab(j), :] = xi
            if n % 4 == 3:
                yield
        xr_all = _dot(unperm, o_re[...].astype(BF16)).astype(BF16)
        xi_all = _dot(unperm, o_im[...].astype(BF16)).astype(BF16)
        y_ref[...] = _dot(xr_all, cre_ref[...]) - _dot(xi_all, cim_ref[...])

    live = [steps(0, uf_ref, yf_ref, *bufs[0:4]), steps(1, ub_ref, yb_ref, *bufs[4:8]),
            _ret_steps(0, qf_ref, kf_ref, vf_ref, rof_ref, rs_ref, L),
            _ret_steps(1, qb_ref, kb_ref, vb_ref, rob_ref, rs_ref, L)]
    while live:
        for gen in list(live):
            if next(gen, StopIteration) is StopIteration:
                live.remove(gen)

    @pl.when(c == nc - 1)
    def _():
        fr_ref[...] = car_re[...]
        fi_ref[...] = car_im[...]
        rfin_ref[...] = rs_ref[...]


def _ret_s5(z, a_re, a_im, bblk_re, bblk_im, cblk_re, cblk_im, x0_re, x0_im, s0_ret):
    B, T, _ = z.shape
    L = min(T, 128)
    nc = T // L
    assert L % S5_RADIX == 0 and (L // S5_RADIX) & (L // S5_RADIX - 1) == 0
    full = lambda shape: pl.BlockSpec(shape, lambda b, c: (0,) * len(shape))
    sspec = pl.BlockSpec((None, 2, 1, S5_STATE), lambda b, c: (b, 0, 0, 0))
    uspec = lambda cmap, blk: pl.BlockSpec((None, L, MIX_W), lambda b, c: (b, cmap(c), blk))
    fwd = lambda c: c
    bwd = lambda c: nc - 1 - c
    buf = pltpu.VMEM((L, S5_STATE), F32)
    car = pltpu.VMEM((2, 1, S5_STATE), F32)
    seq = jax.ShapeDtypeStruct((B, T, MIX_W), F32)
    state = jax.ShapeDtypeStruct((B, 2, 1, S5_STATE), F32)
    rshape = (2, N_HEADS, HEAD_D, HEAD_D)
    rspec = pl.BlockSpec((None,) + rshape, lambda b, c: (b, 0, 0, 0, 0))
    return pl.pallas_call(
        functools.partial(_ret_s5_kernel, L=L, nc=nc),
        grid=(B, nc),
        in_specs=[
            uspec(fwd, ZB_S5), uspec(bwd, ZB_S5),
            full((2, 1, S5_STATE)), full((2, 1, S5_STATE)),
            full((2, MIX_W, S5_STATE)), full((2, MIX_W, S5_STATE)),
            full((S5_STATE, MIX_W)), full((S5_STATE, MIX_W)),
            sspec, sspec,
            uspec(fwd, ZB_RET), uspec(fwd, ZB_RET + 1), uspec(fwd, ZB_RET + 2),
            uspec(bwd, ZB_RET), uspec(bwd, ZB_RET + 1), uspec(bwd, ZB_RET + 2),
            rspec,
        ],
        out_specs=[uspec(fwd, 0), uspec(bwd, 0), sspec, sspec, uspec(fwd, 0), uspec(bwd, 0), rspec],
        out_shape=[seq, seq, state, state, seq, seq, jax.ShapeDtypeStruct((B,) + rshape, F32)],
        scratch_shapes=[car, car] + [buf] * 8 + [pltpu.VMEM(rshape, F32)],
        compiler_params=_cparams("parallel", "arbitrary"),
        name="ret_s5",
    )(z, z, a_re, a_im, bblk_re, bblk_im, cblk_re, cblk_im, x0_re, x0_im, z, z, z, z, z, z, s0_ret)


GLA_CHUNK = 16
GLA_CHUNK_SHIFT = 4
SUB = 8


def _gla_steps(q_ref, zf_ref, v_ref, lb_ref, s0_ref, o_ref, fin_ref, s_ref, a_ref, qs_ref, key_ref,
               *, rev, tb, nt):
    jb = pl.program_id(1)
    ch = GLA_CHUNK
    half = 2 * HEAD_D

    @pl.when(jb == 0)
    def _():
        _load_pair_state(s_ref, s0_ref, transpose=True)

    lb = lb_ref[...]
    zf = zf_ref[...]
    q = q_ref[...]
    qs_ref[...] = q * jax.nn.sigmoid(q)
    key_ref[...] = (1.0 - lb) * jax.nn.sigmoid(-zf)
    l1 = jnp.log(lb)
    l2 = jnp.log1p(-lb) + jnp.minimum(zf, 0.0) - jnp.log1p(jnp.exp(-jnp.abs(zf)))
    lf = jnp.maximum(l1, l2) + jnp.log1p(jnp.exp(-jnp.abs(l1 - l2)))
    r = lax.broadcasted_iota(jnp.int32, (tb, tb), 0)
    c = lax.broadcasted_iota(jnp.int32, (tb, tb), 1)
    same = (r >> GLA_CHUNK_SHIFT) == (c >> GLA_CHUNK_SHIFT)
    tri = jnp.where(same & ((c >= r) if rev else (c <= r)), 1.0, 0.0).astype(BF16)
    p1 = lf.astype(BF16)
    r1 = lf - p1.astype(F32)
    p2 = r1.astype(BF16)
    p3 = (r1 - p2.astype(F32)).astype(BF16)
    a_ref[...] = _dot(tri, p1) + _dot(tri, p2) + _dot(tri, p3)

    rows = lax.broadcasted_iota(jnp.int32, (SUB, half), 0)
    lo = lax.broadcasted_iota(jnp.int32, (SUB, half), 1) < HEAD_D
    same_head = ((lax.broadcasted_iota(jnp.int32, (half, half), 0) >> HEAD_SHIFT)
                 == (lax.broadcasted_iota(jnp.int32, (half, half), 1) >> HEAD_SHIFT))

    def pair_sums(p):
        s0 = jnp.sum(jnp.where(lo, p, 0.0), axis=1, keepdims=True)
        s1 = jnp.sum(jnp.where(lo, 0.0, p), axis=1, keepdims=True)
        return jnp.where(lo, s0, s1)

    def chunk(ci):
        t0 = (tb // ch - 1 - ci if rev else ci) * ch
        win = slice(t0, t0 + ch)
        a = a_ref[win, :]
        qc = qs_ref[win, :]
        kc = key_ref[win, :]
        vc = v_ref[win, :]
        last = 0 if rev else ch - 1
        a_last = a[last:last + 1, :]
        qe = (qc * jnp.exp(a)).astype(BF16)
        ke = (kc * jnp.exp(a_last - a)).astype(BF16)
        ea = jnp.exp(a_last)
        vb = vc.astype(BF16)
        nslab = ch // SUB
        acc = [[jnp.zeros((SUB, half), F32) for _ in range(2)] for _ in range(nslab)]
        for jj in range(ch):
            for sb in range(nslab):
                r0 = sb * SUB
                if (r0 > jj) if rev else (r0 + SUB - 1 < jj):
                    continue
                whole = (r0 + SUB - 1 <= jj) if rev else (r0 >= jj)
                rs = slice(r0, r0 + SUB)
                valid = (rows + r0 <= jj) if rev else (rows + r0 >= jj)
                for hp in range(2):
                    sl = slice(hp * half, (hp + 1) * half)
                    dec = jnp.exp(jnp.minimum(a[rs, sl] - a[jj:jj + 1, sl], 0.0))
                    p = qc[rs, sl] * dec * kc[jj:jj + 1, sl]
                    if not whole:
                        p = jnp.where(valid, p, 0.0)
                    acc[sb][hp] = acc[sb][hp] + pair_sums(p) * vc[jj:jj + 1, sl]
        for hp in range(2):
            sl = slice(hp * half, (hp + 1) * half)
            s = s_ref[hp]
            inter = _dot_nt(qe[:, sl], s.astype(BF16))
            intra = jnp.concatenate([acc[sb][hp] for sb in range(nslab)], axis=0)
            o_ref[win, sl] = inter + intra
            s_ref[hp] = s * ea[:, sl] + jnp.where(same_head, _dot_tn(vb[:, sl], ke[:, sl]), 0.0)

    yield
    for ci in range(tb // ch):
        chunk(ci)
        yield

    @pl.when(jb == nt - 1)
    def _():
        _store_pair_state(fin_ref, s_ref, transpose=True)


def _rw_prep_kernel(*refs, grid_shift, tm):
    if grid_shift:
        (zc_ref, zu_ref, zd_ref, mu_ref, vec_ref, w0_ref, lora_ref,
         r_ref, k2_ref, v_ref, w_ref, nkk_ref, kka_ref, g_ref, bonus_ref, buf_ref) = refs
    else:
        (zc_ref, mu_ref, vec_ref, w0_ref, lora_ref,
         r_ref, k2_ref, v_ref, w_ref, nkk_ref, kka_ref, g_ref, bonus_ref, buf_ref) = refs
    i = pl.program_id(1)
    nt = pl.num_programs(1)
    halo = GRID_W
    width = 4 * MIX_W
    z = zc_ref[...]
    buf_ref[halo:halo + tm, :] = z
    lane = lax.broadcasted_iota(jnp.int32, (tm, width), 1)
    row = lax.broadcasted_iota(jnp.int32, (tm, width), 0)
    if grid_shift:
        buf_ref[0:halo, :] = jnp.where(i > 0, zu_ref[...], 0.0)
        buf_ref[halo + tm:2 * halo + tm, :] = jnp.where(i < nt - 1, zd_ref[...], 0.0)
        col = row & (GRID_W - 1)
        left = jnp.where(col > 0, buf_ref[halo - 1:halo - 1 + tm, :], 0.0)
        right = jnp.where(col < GRID_W - 1, buf_ref[halo + 1:halo + 1 + tm, :], 0.0)
        up = buf_ref[0:tm, :]
        down = buf_ref[2 * halo:2 * halo + tm, :]
        sel = lane & 3
        shifted = jnp.where(sel == 0, left, jnp.where(sel == 1, right, jnp.where(sel == 2, up, down)))
    else:
        zrow = jnp.zeros((1, width), F32)
        buf_ref[halo - 1:halo, :] = zrow
        buf_ref[halo + tm:halo + tm + 1, :] = zrow
        prev = buf_ref[halo - 1:halo - 1 + tm, :]
        nxt = buf_ref[halo + 1:halo + 1 + tm, :]
        shifted = jnp.where((lane & 1) == 0, prev, nxt)
    zs = z + mu_ref[...] * (shifted - z)
    r = zs[:, 0:MIX_W]
    k = zs[:, MIX_W:2 * MIX_W]
    v = zs[:, 2 * MIX_W:3 * MIX_W]
    sm = zs[:, 3 * MIX_W:4 * MIX_W]
    ones_bd = _head_ones()
    a0, k_k, k_a, r_k = (vec_ref[j:j + 1, :] for j in range(4))
    a = jax.nn.sigmoid(a0 + _dot(sm.astype(BF16), lora_ref[2]))
    g_ref[...] = _dot(jax.nn.sigmoid(sm).astype(BF16), lora_ref[3])
    kk = k * k_k
    kk = kk * lax.rsqrt(_head_sum(kk * kk, ones_bd) + 1e-12)
    k2 = k * (1.0 + (a - 1.0) * k_a)
    th = jnp.tanh(sm).astype(BF16)
    for di in range(2):
        w_ref[di] = -math.exp(-0.5) * jax.nn.sigmoid(w0_ref[di:di + 1, :] + _dot(th, lora_ref[di]))
    r_ref[...] = r
    k2_ref[...] = k2
    v_ref[...] = v
    nkk_ref[...] = -kk
    kka_ref[...] = kk * a
    bonus_ref[...] = _head_sum(r * k2 * r_k, ones_bd) * v


def _rw_prep(z, mu_p, vecs, w0, lora, grid_shift):
    B, T, _ = z.shape
    width = 4 * MIX_W
    wblk = ZB_RW // 4
    if grid_shift:
        tm = min(T, 512)
        hb = tm // GRID_W
        nh = T // GRID_W
        z_specs = [
            pl.BlockSpec((None, tm, width), lambda b, i: (b, i, wblk)),
            pl.BlockSpec((None, GRID_W, width), lambda b, i: (b, jnp.maximum(i * hb - 1, 0), wblk)),
            pl.BlockSpec((None, GRID_W, width), lambda b, i: (b, jnp.minimum((i + 1) * hb, nh - 1), wblk)),
        ]
        z_args = (z, z, z)
    else:
        tm = T
        z_specs = [pl.BlockSpec((None, tm, width), lambda b, i: (b, i, wblk))]
        z_args = (z,)
    const = lambda shape: pl.BlockSpec(shape, lambda b, i: (0,) * len(shape))
    ospec = pl.BlockSpec((None, tm, MIX_W), lambda b, i: (b, i, 0))
    oshape = jax.ShapeDtypeStruct((B, T, MIX_W), F32)
    dspec = pl.BlockSpec((2, None, tm, MIX_W), lambda b, i: (0, b, i, 0))
    dshape = jax.ShapeDtypeStruct((2, B, T, MIX_W), F32)
    return pl.pallas_call(
        functools.partial(_rw_prep_kernel, grid_shift=grid_shift, tm=tm),
        grid=(B, T // tm),
        in_specs=z_specs + [const((1, width)), const((4, MIX_W)), const((2, MIX_W)),
                            const((4, MIX_W, MIX_W))],
        out_specs=[ospec, ospec, ospec, dspec, ospec, ospec, ospec, ospec],
        out_shape=[oshape, oshape, oshape, dshape, oshape, oshape, oshape, oshape],
        scratch_shapes=[pltpu.VMEM((tm + 2 * GRID_W, width), F32)],
        compiler_params=_cparams("parallel", "parallel"),
        name="rwkv_prep",
    )(*z_args, mu_p, vecs, w0, lora)


RW_CHUNK = 64
RW_CHUNK_SHIFT = 6


def _mm(a, b):
    return jnp.dot(a.astype(BF16), b.astype(BF16), preferred_element_type=F32)


def _rwkv_steps(lw_ref, a_ref, b_ref, k_ref, r_ref, v_ref, s0_ref, y_ref, fin_ref, s_ref, g_ref,
                *, rev, tb, nt):
    jb = pl.program_id(1)
    ch = RW_CHUNK
    pair = 2 * HEAD_D

    @pl.when(jb == 0)
    def _():
        _load_pair_state(s_ref, s0_ref, transpose=False)

    lw = lw_ref[...]
    rr = lax.broadcasted_iota(jnp.int32, (tb, tb), 0)
    cc = lax.broadcasted_iota(jnp.int32, (tb, tb), 1)
    same = (rr >> RW_CHUNK_SHIFT) == (cc >> RW_CHUNK_SHIFT)
    tri = jnp.where(same & ((cc >= rr) if rev else (cc <= rr)), 1.0, 0.0).astype(BF16)
    p1 = lw.astype(BF16)
    r1 = lw - p1.astype(F32)
    p2 = r1.astype(BF16)
    p3 = (r1 - p2.astype(F32)).astype(BF16)
    g_ref[...] = _dot(tri, p1) + _dot(tri, p2) + _dot(tri, p3)

    si = lax.broadcasted_iota(jnp.int32, (ch, ch), 0)
    ri = lax.broadcasted_iota(jnp.int32, (ch, ch), 1)
    if rev:
        si, ri = ch - 1 - si, ch - 1 - ri
    strict = ri < si
    incl = ri <= si
    eye = jnp.where(ri == si, 1.0, 0.0).astype(F32)
    levels = [((si >> (lv + 1)) == (ri >> (lv + 1))) & (((si >> lv) & 1) == 1) & (((ri >> lv) & 1) == 0)
              for lv in range(RW_CHUNK_SHIFT)]
    lo = lax.broadcasted_iota(jnp.int32, (ch, pair), 1) < HEAD_D
    same_head = ((lax.broadcasted_iota(jnp.int32, (pair, pair), 0) >> HEAD_SHIFT)
                 == (lax.broadcasted_iota(jnp.int32, (pair, pair), 1) >> HEAD_SHIFT))

    nch = tb // ch
    g = g_ref[...]
    e_g = jnp.exp(g)
    e_ng = jnp.exp(-g)
    at_f = a_ref[...] * jnp.exp(g - lw)
    rt_f = r_ref[...] * e_g
    at = at_f.astype(BF16)
    rt = rt_f.astype(BF16)
    first = (lax.broadcasted_iota(jnp.int32, (tb, MIX_W), 1) & HEAD_D) == 0
    at_h = [jnp.where(first, at_f, 0.0).astype(BF16), jnp.where(first, 0.0, at_f).astype(BF16)]
    rt_h = [jnp.where(first, rt_f, 0.0).astype(BF16), jnp.where(first, 0.0, rt_f).astype(BF16)]
    bt = (b_ref[...] * e_ng).astype(BF16)
    kt = (k_ref[...] * e_ng).astype(BF16)
    vb = v_ref[...].astype(BF16)

    items = [(c, hp, hh) for c in range(nch) for hp in range(2) for hh in range(2)]
    rows = lambda c: slice(c * ch, (c + 1) * ch)
    lanes = lambda hp: slice(hp * pair, (hp + 1) * pair)
    n_m, p_m, m_m, q_m = {}, {}, {}, {}
    for it in items:
        c, hp, hh = it
        ar = jnp.concatenate([at_h[hh][rows(c), lanes(hp)], rt_h[hh][rows(c), lanes(hp)]], axis=0)
        np_ = _dot_nt(ar, bt[rows(c), lanes(hp)])
        mq = _dot_nt(ar, kt[rows(c), lanes(hp)])
        n_m[it] = jnp.where(strict, np_[:ch], 0.0).astype(BF16)
        p_m[it] = jnp.where(incl, np_[ch:], 0.0).astype(BF16)
        m_m[it] = jnp.where(strict, mq[:ch], 0.0).astype(BF16)
        q_m[it] = jnp.where(incl, mq[ch:], 0.0).astype(BF16)
        if hp == 1 and hh == 1:
            yield
    t_m = {it: eye + jnp.where(levels[0], n_m[it].astype(F32), 0.0) for it in items}
    for lv in range(1, RW_CHUNK_SHIFT):
        tn = {it: _mm(t_m[it], n_m[it]) for it in items}
        yield
        t_m = {it: t_m[it] + jnp.where(levels[lv], _mm(tn[it], t_m[it]), 0.0) for it in items}
        yield
    mv = {it: _mm(m_m[it], vb[rows(it[0]), lanes(it[1])]) for it in items}
    qv = {it: _mm(q_m[it], vb[rows(it[0]), lanes(it[1])]) for it in items}
    yield
    y2 = {it: _mm(t_m[it], jnp.concatenate([at[rows(it[0]), lanes(it[1])], mv[it].astype(BF16)], axis=1))
          for it in items}
    yield

    last = 0 if rev else ch - 1
    for ci in range(nch):
        c = nch - 1 - ci if rev else ci
        g_c = g[c * ch + last:c * ch + last + 1, :]
        e_gc = jnp.exp(g_c - g[rows(c), :])
        bh = (b_ref[rows(c), :] * e_gc).astype(BF16)
        kh = (k_ref[rows(c), :] * e_gc).astype(BF16)
        dec_c = jnp.exp(g_c)
        xs, sas, ss = [], [], []
        for hp in range(2):
            s = s_ref[hp]
            w2 = jnp.where(lo, y2[(c, hp, 0)][:, :pair], y2[(c, hp, 1)][:, :pair]).astype(BF16)
            xs.append(_dot_nt(jnp.concatenate([w2, rt[rows(c), lanes(hp)]], axis=0), s.astype(BF16)))
            ss.append(s)
        for hp in range(2):
            w1 = jnp.where(lo, y2[(c, hp, 0)][:, pair:], y2[(c, hp, 1)][:, pair:])
            sas.append(w1 + xs[hp][:ch])
        for hp in range(2):
            sa = sas[hp]
            y_ref[rows(c), lanes(hp)] = xs[hp][ch:] + jnp.where(
                lo, _mm(p_m[(c, hp, 0)], sa) + qv[(c, hp, 0)], _mm(p_m[(c, hp, 1)], sa) + qv[(c, hp, 1)])
            upd = _dot_tn(jnp.concatenate([sa.astype(BF16), vb[rows(c), lanes(hp)]], axis=0),
                          jnp.concatenate([bh[:, lanes(hp)], kh[:, lanes(hp)]], axis=0))
            s_ref[hp] = ss[hp] * dec_c[:, lanes(hp)] + jnp.where(same_head, upd, 0.0)
        yield

    @pl.when(jb == nt - 1)
    def _():
        _store_pair_state(fin_ref, s_ref, transpose=False)


def _hg_rw_kernel(*refs, rev, tb, nt):
    gla_in, rw_in = refs[0:5], refs[5:12]
    gla_out, rw_out = refs[12:14], refs[14:16]
    gla_scr, rw_scr = refs[16:20], refs[20:22]
    gla = _gla_steps(*gla_in, *gla_out, *gla_scr, rev=rev, tb=tb, nt=nt)
    rwkv = _rwkv_steps(*rw_in, *rw_out, *rw_scr, rev=rev, tb=tb, nt=nt)
    live = [gla, rwkv]
    while live:
        for gen in list(live):
            if next(gen, StopIteration) is StopIteration:
                live.remove(gen)


def _load_pair_state(s_ref, s0_ref, transpose):
    s_ref[...] = jnp.zeros_like(s_ref)
    for h in range(N_HEADS):
        blk = slice((h % 2) * HEAD_D, (h % 2 + 1) * HEAD_D)
        t = s0_ref[h]
        s_ref[h // 2, blk, blk] = t.T if transpose else t


def _store_pair_state(fin_ref, s_ref, transpose):
    for h in range(N_HEADS):
        blk = slice((h % 2) * HEAD_D, (h % 2 + 1) * HEAD_D)
        t = s_ref[h // 2, blk, blk]
        fin_ref[h] = t.T if transpose else t


def _hg_rw(z, lb, s0_hg, lw, a, b, k, r, v, s0_rw, rev):
    B, T, _ = v.shape
    pair = 2 * HEAD_D
    tb = min(T, 256)
    nt = T // tb
    di = 1 if rev else 0
    tmap = (lambda j: nt - 1 - j) if rev else (lambda j: j)
    zspec = lambda blk: pl.BlockSpec((None, tb, MIX_W), lambda bi, j: (bi, tmap(j), blk))
    spec = zspec(0)
    heads = (N_HEADS, HEAD_D, HEAD_D)
    s0spec = pl.BlockSpec((None, None) + heads, lambda bi, j: (bi, di, 0, 0, 0))
    sspec = pl.BlockSpec((None,) + heads, lambda bi, j: (bi, 0, 0, 0))
    state = jax.ShapeDtypeStruct((B,) + heads, F32)
    seq = jax.ShapeDtypeStruct((B, T, MIX_W), F32)
    blk = pltpu.VMEM((tb, MIX_W), F32)
    st = pltpu.VMEM((2, pair, pair), F32)
    return pl.pallas_call(
        functools.partial(_hg_rw_kernel, rev=rev, tb=tb, nt=nt),
        grid=(B, nt),
        in_specs=[zspec(ZB_HG), zspec(ZB_HG + 1 + di), zspec(ZB_HG + 3),
                  pl.BlockSpec((1, MIX_W), lambda bi, j: (0, 0)), s0spec,
                  pl.BlockSpec((None, None, tb, MIX_W), lambda bi, j: (di, bi, tmap(j), 0)),
                  spec, spec, spec, spec, spec, s0spec],
        out_specs=[spec, sspec, spec, sspec],
        out_shape=[seq, state, seq, state],
        scratch_shapes=[st, blk, blk, blk, st, blk],
        compiler_params=_cparams("parallel", "arbitrary"),
        name="hgrn_rwkv_bwd" if rev else "hgrn_rwkv_fwd",
    )(z, z, z, lb[di:di + 1], s0_hg, lw, a, b, k, r, v, s0_rw)


def _mix_kernel(x_ref, sc_ref, sh_ref, g1_ref, nw0_ref, nw_ref, wg_ref,
                retf_ref, retb_ref, retg_ref, s5f_ref, s5b_ref, s5u_ref, hgf_ref, hgb_ref, hgg_ref, rwf_ref, rwr_ref,
                rwb_ref, rwg_ref,
                vec_ref, glu_w_ref, wbr_ref, wout_ref, o_ref):
    h = (_rms(x_ref[...], nw0_ref[...]) * (1.0 + sc_ref[...]) + sh_ref[...]).astype(BF16)
    gn_w, s5_d, glu_b, hg_w, ln_w, ln_b = (vec_ref[j:j + 1, :] for j in range(6))

    def group_norm(o, eps):
        mu = _head_sum_lanes(o) * (1.0 / HEAD_D)
        oc = o - mu
        var = _head_sum_lanes(oc * oc) * (1.0 / HEAD_D)
        return oc * lax.rsqrt(var + eps)

    g = retg_ref[...]
    y_ret = group_norm(retf_ref[...] + retb_ref[...], EPS) * gn_w * (g * jax.nn.sigmoid(g))

    y = s5_d * s5u_ref[...] + s5f_ref[...] + s5b_ref[...]
    yg = jax.nn.gelu(y)
    y_s5 = yg * jax.nn.sigmoid(_dot(yg.astype(BF16), glu_w_ref[...]) + glu_b)

    o = hgf_ref[...] + hgb_ref[...]
    g = hgg_ref[...]
    ms = _head_sum_lanes(o * o) * (1.0 / HEAD_D)
    y_hg = o * lax.rsqrt(ms + EPS) * hg_w * (g * jax.nn.sigmoid(g))

    y = group_norm(rwf_ref[...] + rwr_ref[...], RW_LN_EPS) * ln_w + ln_b
    y_rw = (y + rwb_ref[...]) * rwg_ref[...]

    mixed = None
    for m, ym in enumerate((y_ret, y_s5, y_hg, y_rw)):
        br = _dot(ym.astype(BF16), wbr_ref[m])
        term = jax.nn.sigmoid(_dot(h, wg_ref[:, m * D_MODEL:(m + 1) * D_MODEL])) * br
        mixed = term if mixed is None else mixed + term
    mixed = _dot(mixed.astype(BF16), wout_ref[...])
    o_ref[...] = x_ref[...] + g1_ref[...] * _rms(mixed, nw_ref[...])


def _mix(x, sc, sh, g1, nw0, nw, w_gate, z, ret_o, s5_y, hg_o, rw_y, rw_bonus, rw_g, vecs, glu_w,
         w_branch, w_out):
    B, T, _ = x.shape
    tm = min(T, 256)
    xspec = pl.BlockSpec((None, tm, D_MODEL), lambda b, i: (b, i, 0))
    zspec = lambda blk: pl.BlockSpec((None, tm, MIX_W), lambda b, i: (b, i, blk))
    const = lambda shape: pl.BlockSpec(shape, lambda b, i: (0,) * len(shape))
    mspec = lambda m: pl.BlockSpec((None, 1, D_MODEL), _mod_map(m.shape[0]))
    return pl.pallas_call(
        _mix_kernel,
        grid=(B, T // tm),
        in_specs=[
            xspec, mspec(sc), mspec(sh), mspec(g1),
            const((1, D_MODEL)), const((1, D_MODEL)),
            const((D_MODEL, 4 * D_MODEL)),
            zspec(0), zspec(0), zspec(ZB_RET + 3),
            zspec(0), zspec(0), zspec(ZB_S5),
            zspec(0), zspec(0), zspec(ZB_HG + 4),
            zspec(0), zspec(0), zspec(0), zspec(0),
            const((6, MIX_W)), const((MIX_W, MIX_W)),
            const((4, MIX_W, D_MODEL)), const((D_MODEL, D_MODEL)),
        ],
        out_specs=xspec,
        out_shape=jax.ShapeDtypeStruct((B, T, D_MODEL), F32),
        compiler_params=_cparams("parallel", "parallel"),
        name="mix_out",
    )(x, sc, sh, g1, nw0, nw, w_gate, ret_o[0], ret_o[1], z, s5_y[0], s5_y[1], z, hg_o[0], hg_o[1], z, rw_y[0], rw_y[1],
      rw_bonus, rw_g, vecs, glu_w, w_branch, w_out)


FFN_CHUNK = 1024


def _ffn_kernel(x_ref, sc_ref, sh_ref, g2_ref, nw2_ref, nw3_ref, w1_ref, w2_ref, o_ref):
    x = x_ref[...]
    h = (_rms(x, nw2_ref[...]) * (1.0 + sc_ref[...]) + sh_ref[...]).astype(BF16)
    acc = None
    for j in range(D_FF // FFN_CHUNK):
        cols = slice(j * FFN_CHUNK, (j + 1) * FFN_CHUNK)
        a = jnp.maximum(_dot(h, w1_ref[:, cols]), 0.0)
        part = _dot((a * a).astype(BF16), w2_ref[cols, :])
        acc = part if acc is None else acc + part
    o_ref[...] = x + g2_ref[...] * _rms(acc, nw3_ref[...])


def _ffn(x, sc, sh, g2, nw2, nw3, w1, w2):
    B, T, _ = x.shape
    tm = min(T, 512)
    xspec = pl.BlockSpec((None, tm, D_MODEL), lambda b, i: (b, i, 0))
    mspec = lambda m: pl.BlockSpec((None, 1, D_MODEL), _mod_map(m.shape[0]))
    const = pl.BlockSpec((1, D_MODEL), lambda b, i: (0, 0))
    return pl.pallas_call(
        _ffn_kernel,
        grid=(B, T // tm),
        in_specs=[xspec, mspec(sc), mspec(sh), mspec(g2), const, const,
                  pl.BlockSpec((D_MODEL, D_FF), lambda b, i: (0, 0)),
                  pl.BlockSpec((D_FF, D_MODEL), lambda b, i: (0, 0))],
        out_specs=xspec,
        out_shape=jax.ShapeDtypeStruct((B, T, D_MODEL), F32),
        compiler_params=_cparams("parallel", "parallel"),
        name="ffn",
    )(x, sc, sh, g2, nw2, nw3, w1, w2)


def _layer_params(l, p):
    w = p['w_in'][l]
    w_p = jnp.concatenate(
        [w[:, 2560:3488], jnp.zeros((D_MODEL, 4 * MIX_W - RW_COLS), F32),
         w[:, 0:1024], w[:, 1024:1280], w[:, 1280:2560]], axis=1).astype(BF16)
    w_gate = w[:, 3488:7584].astype(BF16)

    a_re, a_im, bb_re, bb_im = _s5_zoh(p['s5_lam_re'][l], p['s5_lam_im'][l], p['s5_log_dt'][l],
                                       p['s5_b_re'][l], p['s5_b_im'][l])
    eye = jnp.eye(S5_GROUPS, dtype=F32)
    bblk = lambda bb: jnp.einsum('dgph,gk->dghkp', bb, eye).reshape(2, MIX_W, S5_STATE).astype(BF16)
    cblk = lambda c: jnp.einsum('ghp,gk->gpkh', c, eye).reshape(S5_STATE, MIX_W).astype(BF16)

    def lora_pad(m, row0):
        return jnp.zeros((MIX_W, MIX_W), F32).at[row0:row0 + m.shape[0]].set(m)

    lora = jnp.stack([lora_pad(p['rw_w2'][l, 0], 0), lora_pad(p['rw_w2'][l, 1], 32),
                      lora_pad(p['rw_a2'][l], 64), lora_pad(p['rw_g2'][l], 96)]).astype(BF16)
    mu_p = jnp.concatenate([p['rw_mu'][l], jnp.zeros((4 * MIX_W - RW_COLS,), F32)]).reshape(1, 4 * MIX_W)
    return dict(
        w_p=w_p, w_gate=w_gate, nw=p['norm_w'][l],
        s5=(a_re, a_im, bblk(bb_re), bblk(bb_im), cblk(p['s5_c_re'][l]), cblk(p['s5_c_im'][l])),
        rw_mu=mu_p, rw_lora=lora, rw_w0=p['rw_w0'][l],
        rw_vecs=jnp.stack([p['rw_a0'][l], p['rw_k_k'][l], p['rw_k_a'][l], p['rw_r_k'][l]]),
        mix_vecs=jnp.stack([p['ret_gn_w'][l], p['s5_d'][l], p['s5_glu_b'][l], p['hg_norm_w'][l],
                            p['rw_ln_w'][l], p['rw_ln_b'][l]]),
        glu_w=p['s5_glu_w'][l].astype(BF16),
        w_branch=p['w_branch'][l].astype(BF16), w_out=p['w_out'][l].astype(BF16),
        ff_w1=p['ff_w1'][l].astype(BF16), ff_w2=p['ff_w2'][l].astype(BF16),
    )


def _trunk_layer(x, mod, init, grid_shift, lp, hg_lb):
    B, T, _ = x.shape
    sh1, sc1, g1, sh2, sc2, g2 = (m[:, None, :] for m in jnp.split(mod, 6, axis=-1))
    nw = lp['nw']
    if mod.shape[0] == 1:
        tok = lambda t: t.reshape(t.shape[:-3] + (1, B * T, t.shape[-1]))
    else:
        tok = lambda t: t
    z = _proj_in(tok(x), sc1, sh1, nw[0:1], lp['w_p']).reshape(B, T, Z_COLS)

    s_ret, s_s5r, s_s5i, s_hg, s_rw = init
    s5_f, s5_b, f_s5r, f_s5i, ret_f, ret_b, f_ret = _ret_s5(
        z, *lp['s5'], s_s5r.reshape(B, 2, 1, S5_STATE), s_s5i.reshape(B, 2, 1, S5_STATE), s_ret)
    f_s5r = f_s5r.reshape(B, 2, S5_GROUPS, S5_P)
    f_s5i = f_s5i.reshape(B, 2, S5_GROUPS, S5_P)

    r, k2, v, w, nkk, kka, rw_g, bonus = _rw_prep(z, lp['rw_mu'], lp['rw_vecs'], lp['rw_w0'],
                                                  lp['rw_lora'], grid_shift)
    hg_f, f_hg_f, rw_f, f_rw_f = _hg_rw(z, hg_lb, s_hg, w, nkk, kka, k2, r, v, s_rw, False)
    hg_b, f_hg_b, rw_b, f_rw_b = _hg_rw(z, hg_lb, s_hg, w, nkk, kka, k2, r, v, s_rw, True)
    f_hg = jnp.stack([f_hg_f, f_hg_b], axis=1)
    f_rw = jnp.stack([f_rw_f, f_rw_b], axis=1)

    x = _mix(tok(x), sc1, sh1, g1, nw[0:1], nw[1:2], lp['w_gate'], tok(z), (tok(ret_f), tok(ret_b)), (tok(s5_f), tok(s5_b)),
             (tok(hg_f), tok(hg_b)), (tok(rw_f), tok(rw_b)), tok(bonus), tok(rw_g),
             lp['mix_vecs'], lp['glu_w'], lp['w_branch'], lp['w_out'])
    x = _ffn(x, sc2, sh2, g2, nw[2:3], nw[3:4], lp['ff_w1'], lp['ff_w2'])
    return x.reshape(B, T, D_MODEL), (f_ret, f_s5r, f_s5i, f_hg, f_rw)


def kernel(x_prompt, x_sample, state_ret, state_s5_re, state_s5_im, state_hgrn, state_rwkv, c, c_ctx, ada_w, ada_b, norm_w, w_in, ret_gn_w, s5_lam_re, s5_lam_im, s5_log_dt, s5_b_re, s5_b_im, s5_c_re, s5_c_im, s5_d, s5_glu_w, s5_glu_b, hg_lb, hg_norm_w, rw_mu, rw_w0, rw_w2, rw_a0, rw_a2, rw_g2, rw_k_k, rw_k_a, rw_r_k, rw_ln_w, rw_ln_b, w_branch, w_out, ff_w1, ff_w2):
    p = dict(norm_w=norm_w, w_in=w_in, ret_gn_w=ret_gn_w, s5_lam_re=s5_lam_re, s5_lam_im=s5_lam_im,
             s5_log_dt=s5_log_dt, s5_b_re=s5_b_re, s5_b_im=s5_b_im, s5_c_re=s5_c_re, s5_c_im=s5_c_im,
             s5_d=s5_d, s5_glu_w=s5_glu_w, s5_glu_b=s5_glu_b, hg_norm_w=hg_norm_w, rw_mu=rw_mu,
             rw_w0=rw_w0, rw_w2=rw_w2, rw_a0=rw_a0, rw_a2=rw_a2, rw_g2=rw_g2, rw_k_k=rw_k_k,
             rw_k_a=rw_k_a, rw_r_k=rw_r_k, rw_ln_w=rw_ln_w, rw_ln_b=rw_ln_b, w_branch=w_branch,
             w_out=w_out, ff_w1=ff_w1, ff_w2=ff_w2)
    depth = w_in.shape[0]
    n_ctx = x_prompt.shape[0]
    n_lat = x_sample.shape[0]
    assert 1 + n_lat <= 8

    lb_cum = jnp.cumsum(jax.nn.softmax(hg_lb.astype(F32), axis=0), axis=0)
    hg_lower = lb_cum - lb_cum[0]

    cond = jnp.concatenate([c_ctx[None, :], c, jnp.zeros((7 - n_lat, D_MODEL), F32)], axis=0)
    mod = _ada_mod(cond, ada_w, ada_b)

    zero_state = (jnp.zeros((n_ctx, 2, N_HEADS, HEAD_D, HEAD_D), F32),
                  jnp.zeros((n_ctx, 2, S5_GROUPS, S5_P), F32),
                  jnp.zeros((n_ctx, 2, S5_GROUPS, S5_P), F32),
                  jnp.zeros((n_ctx, 2, N_HEADS, HEAD_D, HEAD_D), F32),
                  jnp.zeros((n_ctx, 2, N_HEADS, HEAD_D, HEAD_D), F32))
    xp, xs = x_prompt, x_sample
    finals = []
    for l in range(depth):
        lp = _layer_params(l, p)
        xp, fin = _trunk_layer(xp, mod[l, 0:1], zero_state, False, lp, hg_lower[l])
        finals.append(fin)
        lat_init = (state_ret[:, l], state_s5_re[:, l], state_s5_im[:, l], state_hgrn[:, l], state_rwkv[:, l])
        xs, _ = _trunk_layer(xs, mod[l, 1:1 + n_lat], lat_init, True, lp, hg_lower[l])
    new_states = tuple(jnp.stack([f[i] for f in finals], axis=1) for i in range(5))
    return (xp, xs) + new_states
```

```python
import functools
import math

import jax
import jax.numpy as jnp
from jax import lax
from jax.experimental import pallas as pl
from jax.experimental.pallas import tpu as pltpu

F32 = jnp.float32
BF16 = jnp.bfloat16

D_MODEL = 1024
GRID_W = 64
MIX_W = 256
N_HEADS = 4
HEAD_D = 64
HEAD_SHIFT = 6
LANES = 128
S5_GROUPS = 16
S5_CH = 16
S5_P = 64
S5_STATE = S5_GROUPS * S5_P
D_FF = 4096
EPS = 1e-6
RW_LN_EPS = 64e-5
RW_COLS = 928

Z_COLS = 3584
ZB_RW = 0
ZB_RET = 4
ZB_S5 = 8
ZB_HG = 9

VMEM_LIMIT = 56 * 1024 * 1024

LOG_GAMMA = tuple(
    tuple(math.log1p(-2.0 ** (-(5.0 + 0.5 * di) - h)) for h in range(N_HEADS)) for di in range(2))


def _cparams(*sem):
    return pltpu.CompilerParams(dimension_semantics=sem, vmem_limit_bytes=VMEM_LIMIT)


def _dot(a, b):
    return jnp.dot(a, b, preferred_element_type=F32)


def _dot_nt(a, b):
    return lax.dot_general(a, b, (((1,), (1,)), ((), ())), preferred_element_type=F32)


def _dot_tn(a, b):
    return lax.dot_general(a, b, (((0,), (0,)), ((), ())), preferred_element_type=F32)


def _head_ones():
    r = lax.broadcasted_iota(jnp.int32, (MIX_W, MIX_W), 0) >> HEAD_SHIFT
    c = lax.broadcasted_iota(jnp.int32, (MIX_W, MIX_W), 1) >> HEAD_SHIFT
    return jnp.where(r == c, 1.0, 0.0).astype(F32)


def _head_sum(x, ones_bd):
    ones = ones_bd.astype(BF16)
    p1 = x.astype(BF16)
    r1 = x - p1.astype(F32)
    p2 = r1.astype(BF16)
    p3 = (r1 - p2.astype(F32)).astype(BF16)
    return _dot(p1, ones) + _dot(p2, ones) + _dot(p3, ones)


def _head_sum_lanes(x):
    lo = lax.broadcasted_iota(jnp.int32, (x.shape[0], LANES), 1) < HEAD_D
    parts = []
    for q in range(x.shape[1] // LANES):
        p = x[:, q * LANES:(q + 1) * LANES]
        s0 = jnp.sum(jnp.where(lo, p, 0.0), axis=1, keepdims=True)
        s1 = jnp.sum(jnp.where(lo, 0.0, p), axis=1, keepdims=True)
        parts.append(jnp.where(lo, s0, s1))
    return jnp.concatenate(parts, axis=1)


def _rms(x, w):
    return x * lax.rsqrt(jnp.mean(x * x, axis=-1, keepdims=True) + EPS) * w


def _ada_kernel(c_ref, w_ref, b_ref, o_ref):
    c = c_ref[...]
    s = c * jax.nn.sigmoid(c)
    o_ref[...] = _dot(s.astype(BF16), w_ref[...].astype(BF16)) + b_ref[...]


def _ada_mod(cond, ada_w, ada_b):
    L = ada_w.shape[0]
    n = ada_w.shape[2]
    tn = 1536
    return pl.pallas_call(
        _ada_kernel,
        grid=(L, n // tn),
        in_specs=[
            pl.BlockSpec((8, D_MODEL), lambda l, j: (0, 0)),
            pl.BlockSpec((None, D_MODEL, tn), lambda l, j: (l, 0, j)),
            pl.BlockSpec((None, 1, tn), lambda l, j: (l, 0, j)),
        ],
        out_specs=pl.BlockSpec((None, 8, tn), lambda l, j: (l, 0, j)),
        out_shape=jax.ShapeDtypeStruct((L, 8, n), F32),
        compiler_params=_cparams("parallel", "parallel"),
        name="ada_mod",
    )(cond, ada_w, ada_b.reshape(L, 1, n))


def _proj_in_kernel(x_ref, sc_ref, sh_ref, nw_ref, w_ref, z_ref):
    h = _rms(x_ref[...], nw_ref[...]) * (1.0 + sc_ref[...]) + sh_ref[...]
    z_ref[...] = _dot(h.astype(BF16), w_ref[...])


def _mod_map(bm):
    if bm == 1:
        return lambda b, *_: (0, 0, 0)
    return lambda b, *_: (b, 0, 0)


def _proj_in(x, sc, sh, nw, w_p):
    B, T, _ = x.shape
    tm = min(T, 512)
    return pl.pallas_call(
        _proj_in_kernel,
        grid=(B, T // tm),
        in_specs=[
            pl.BlockSpec((None, tm, D_MODEL), lambda b, i: (b, i, 0)),
            pl.BlockSpec((None, 1, D_MODEL), _mod_map(sc.shape[0])),
            pl.BlockSpec((None, 1, D_MODEL), _mod_map(sh.shape[0])),
            pl.BlockSpec((1, D_MODEL), lambda b, i: (0, 0)),
            pl.BlockSpec((D_MODEL, Z_COLS), lambda b, i: (0, 0)),
        ],
        out_specs=pl.BlockSpec((None, tm, Z_COLS), lambda b, i: (b, i, 0)),
        out_shape=jax.ShapeDtypeStruct((B, T, Z_COLS), F32),
        compiler_params=_cparams("parallel", "parallel"),
        name="proj_in",
    )(x, sc, sh, nw, w_p)


def _ret_steps(di, q_ref, k_ref, v_ref, o_ref, s_ref, lc):
    rev = di == 1
    rows = lax.broadcasted_iota(jnp.int32, (lc, lc), 0)
    cols = lax.broadcasted_iota(jnp.int32, (lc, lc), 1)
    rel = ((cols - rows) if rev else (rows - cols)).astype(F32)
    idx = lax.broadcasted_iota(jnp.int32, (lc, 1), 0)
    pos = ((lc - 1 - idx) if rev else idx).astype(F32)
    for h in range(N_HEADS):
        lg = LOG_GAMMA[di][h]
        sl = slice(h * HEAD_D, (h + 1) * HEAD_D)
        q = q_ref[:, sl]
        k = k_ref[:, sl] * (HEAD_D ** -0.5)
        v = v_ref[:, sl].astype(BF16)
        att = _dot_nt(q.astype(BF16), k.astype(BF16))
        att = att * jnp.where(rel >= 0.0, jnp.exp(jnp.maximum(rel, 0.0) * lg), 0.0)
        qd = q * jnp.exp((pos + 1.0) * lg)
        kd = k * jnp.exp((lc - 1.0 - pos) * lg)
        s = s_ref[di, h]
        o_ref[:, sl] = _dot(att.astype(BF16), v) + _dot(qd.astype(BF16), s.astype(BF16))
        s_ref[di, h] = s * math.exp(lc * lg) + _dot_tn(kd.astype(BF16), v)
        yield


def _s5_zoh_kernel(lre_ref, lim_ref, ldt_ref, bre_ref, bim_ref, are_ref, aim_ref, bbre_ref, bbim_ref):
    lam_re = jnp.minimum(lre_ref[...], -1e-4)
    lam_im = lim_ref[...]
    dt = jnp.exp(ldt_ref[...])
    mag = jnp.exp(dt * lam_re)
    ang = dt * lam_im
    a_re = mag * jnp.cos(ang)
    a_im = mag * jnp.sin(ang)
    den = lam_re * lam_re + lam_im * lam_im
    f_re = ((a_re - 1.0) * lam_re + a_im * lam_im) / den
    f_im = (a_im * lam_re - (a_re - 1.0) * lam_im) / den
    b_re = bre_ref[...]
    b_im = bim_ref[...]
    are_ref[...] = a_re
    aim_ref[...] = a_im
    bbre_ref[...] = f_re * b_re - f_im * b_im
    bbim_ref[...] = f_re * b_im + f_im * b_re


def _s5_zoh(lam_re, lam_im, log_dt, b_re, b_im):
    n = 2 * S5_STATE
    col = lambda t: t.reshape(n, 1)
    ldt = jnp.broadcast_to(log_dt[:, :, None], (2, S5_GROUPS, S5_P))
    outs = pl.pallas_call(
        _s5_zoh_kernel,
        out_shape=[jax.ShapeDtypeStruct((n, 1), F32), jax.ShapeDtypeStruct((n, 1), F32),
                   jax.ShapeDtypeStruct((n, S5_CH), F32), jax.ShapeDtypeStruct((n, S5_CH), F32)],
        name="s5_zoh",
    )(col(lam_re), col(lam_im), col(ldt), b_re.reshape(n, S5_CH), b_im.reshape(n, S5_CH))
    a_re, a_im, bb_re, bb_im = outs
    shp = (2, S5_GROUPS, S5_P, S5_CH)
    return a_re.reshape(2, 1, S5_STATE), a_im.reshape(2, 1, S5_STATE), bb_re.reshape(shp), bb_im.reshape(shp)


S5_RADIX = 16


def _ret_s5_kernel(*refs, L, nc):
    (uf_ref, ub_ref, are_ref, aim_ref, bre_ref, bim_ref, cre_ref, cim_ref, x0r_ref, x0i_ref,
     qf_ref, kf_ref, vf_ref, qb_ref, kb_ref, vb_ref, rs0_ref,
     yf_ref, yb_ref, fr_ref, fi_ref, rof_ref, rob_ref, rfin_ref,
     car_re, car_im) = refs[:26]
    bufs, rs_ref = refs[26:34], refs[34]
    c = pl.program_id(1)
    R = S5_RADIX
    G = L // R

    @pl.when(c == 0)
    def _():
        car_re[...] = x0r_ref[...]
        car_im[...] = x0i_ref[...]
        rs_ref[...] = rs0_ref[...]

    rr = lax.broadcasted_iota(jnp.int32, (L, L), 0)
    cc = lax.broadcasted_iota(jnp.int32, (L, L), 1)
    lg, lr = G.bit_length() - 1, R.bit_length() - 1
    perm = jnp.where(cc == ((rr & (G - 1)) << lr) + (rr >> lg), 1.0, 0.0).astype(BF16)
    unperm = jnp.where(cc == ((rr & (R - 1)) << lg) + (rr >> lr), 1.0, 0.0).astype(BF16)

    def cmul(pr, pi, qr, qi):
        return pr * qr - pi * qi, pr * qi + pi * qr

    def steps(di, u_ref, y_ref, x_re, x_im, o_re, o_im):
        rev = di == 1
        u = _dot(perm, u_ref[...].astype(BF16)).astype(BF16)
        x_re[...] = _dot(u, bre_ref[di])
        x_im[...] = _dot(u, bim_ref[di])
        yield
        ar = are_ref[di]
        ai = aim_ref[di]
        order = list(range(R - 1, -1, -1)) if rev else list(range(R))
        slab = lambda j: slice(j * G, (j + 1) * G)

        er = x_re[slab(order[0]), :]
        ei = x_im[slab(order[0]), :]
        for n, j in enumerate(order[1:]):
            tr, ti = cmul(ar, ai, er, ei)
            er = tr + x_re[slab(j), :]
            ei = ti + x_im[slab(j), :]
            if n % 4 == 3:
                yield

        a_r, a_i = ar, ai
        for _ in range(R.bit_length() - 1):
            a_r, a_i = cmul(a_r, a_i, a_r, a_i)
        zr = car_re[di]
        zi = car_im[di]
        cin_r = [None] * G
        cin_i = [None] * G
        for k in (range(G - 1, -1, -1) if rev else range(G)):
            cin_r[k] = zr
            cin_i[k] = zi
            tr, ti = cmul(a_r, a_i, zr, zi)
            zr = tr + er[k:k + 1, :]
            zi = ti + ei[k:k + 1, :]
        car_re[di] = zr
        car_im[di] = zi
        yield

        xr = jnp.concatenate(cin_r, axis=0)
        xi = jnp.concatenate(cin_i, axis=0)
        for n, j in enumerate(order):
            tr, ti = cmul(ar, ai, xr, xi)
            xr = tr + x_re[slab(j), :]
            xi = ti + x_im[slab(j), :]
            o_re[slab(j), :] = xr
            o_im[slab(j), :] = xi
            if n % 4 == 3:
                yield
        xr_all = _dot(unperm, o_re[...].astype(BF16)).astype(BF16)
        xi_all = _dot(unperm, o_im[...].astype(BF16)).astype(BF16)
        y_ref[...] = _dot(xr_all, cre_ref[...]) - _dot(xi_all, cim_ref[...])

    live = [_ret_steps(0, qf_ref, kf_ref, vf_ref, rof_ref, rs_ref, L),
            steps(0, uf_ref, yf_ref, *bufs[0:4]),
            _ret_steps(1, qb_ref, kb_ref, vb_ref, rob_ref, rs_ref, L),
            steps(1, ub_ref, yb_ref, *bufs[4:8])]
    while live:
        for gen in list(live):
            if next(gen, StopIteration) is StopIteration:
                live.remove(gen)

    @pl.when(c == nc - 1)
    def _():
        fr_ref[...] = car_re[...]
        fi_ref[...] = car_im[...]
        rfin_ref[...] = rs_ref[...]


def _ret_s5(z, a_re, a_im, bblk_re, bblk_im, cblk_re, cblk_im, x0_re, x0_im, s0_ret):
    B, T, _ = z.shape
    L = min(T, 128)
    nc = T // L
    assert L % S5_RADIX == 0 and (L // S5_RADIX) & (L // S5_RADIX - 1) == 0
    full = lambda shape: pl.BlockSpec(shape, lambda b, c: (0,) * len(shape))
    sspec = pl.BlockSpec((None, 2, 1, S5_STATE), lambda b, c: (b, 0, 0, 0))
    uspec = lambda cmap, blk: pl.BlockSpec((None, L, MIX_W), lambda b, c: (b, cmap(c), blk))
    fwd = lambda c: c
    bwd = lambda c: nc - 1 - c
    buf = pltpu.VMEM((L, S5_STATE), F32)
    car = pltpu.VMEM((2, 1, S5_STATE), F32)
    seq = jax.ShapeDtypeStruct((B, T, MIX_W), F32)
    state = jax.ShapeDtypeStruct((B, 2, 1, S5_STATE), F32)
    rshape = (2, N_HEADS, HEAD_D, HEAD_D)
    rspec = pl.BlockSpec((None,) + rshape, lambda b, c: (b, 0, 0, 0, 0))
    return pl.pallas_call(
        functools.partial(_ret_s5_kernel, L=L, nc=nc),
        grid=(B, nc),
        in_specs=[
            uspec(fwd, ZB_S5), uspec(bwd, ZB_S5),
            full((2, 1, S5_STATE)), full((2, 1, S5_STATE)),
            full((2, MIX_W, S5_STATE)), full((2, MIX_W, S5_STATE)),
            full((S5_STATE, MIX_W)), full((S5_STATE, MIX_W)),
            sspec, sspec,
            uspec(fwd, ZB_RET), uspec(fwd, ZB_RET + 1), uspec(fwd, ZB_RET + 2),
            uspec(bwd, ZB_RET), uspec(bwd, ZB_RET + 1), uspec(bwd, ZB_RET + 2),
            rspec,
        ],
        out_specs=[uspec(fwd, 0), uspec(bwd, 0), sspec, sspec, uspec(fwd, 0), uspec(bwd, 0), rspec],
        out_shape=[seq, seq, state, state, seq, seq, jax.ShapeDtypeStruct((B,) + rshape, F32)],
        scratch_shapes=[car, car] + [buf] * 8 + [pltpu.VMEM(rshape, F32)],
        compiler_params=_cparams("parallel", "arbitrary"),
        name="ret_s5",
    )(z, z, a_re, a_im, bblk_re, bblk_im, cblk_re, cblk_im, x0_re, x0_im, z, z, z, z, z, z, s0_ret)


GLA_CHUNK = 16
GLA_CHUNK_SHIFT = 4
SUB = 8


def _gla_steps(q_ref, zf_ref, v_ref, lb_ref, s0_ref, o_ref, fin_ref, s_ref, a_ref, qs_ref, key_ref,
               *, rev, tb, nt):
    jb = pl.program_id(1)
    ch = GLA_CHUNK
    half = 2 * HEAD_D

    @pl.when(jb == 0)
    def _():
        s_ref[...] = s0_ref[...]

    lb = lb_ref[...]
    zf = zf_ref[...]
    q = q_ref[...]
    qs_ref[...] = q * jax.nn.sigmoid(q)
    key_ref[...] = (1.0 - lb) * jax.nn.sigmoid(-zf)
    l1 = jnp.log(lb)
    l2 = jnp.log1p(-lb) + jnp.minimum(zf, 0.0) - jnp.log1p(jnp.exp(-jnp.abs(zf)))
    lf = jnp.maximum(l1, l2) + jnp.log1p(jnp.exp(-jnp.abs(l1 - l2)))
    r = lax.broadcasted_iota(jnp.int32, (tb, tb), 0)
    c = lax.broadcasted_iota(jnp.int32, (tb, tb), 1)
    same = (r >> GLA_CHUNK_SHIFT) == (c >> GLA_CHUNK_SHIFT)
    tri = jnp.where(same & ((c >= r) if rev else (c <= r)), 1.0, 0.0).astype(BF16)
    p1 = lf.astype(BF16)
    r1 = lf - p1.astype(F32)
    p2 = r1.astype(BF16)
    p3 = (r1 - p2.astype(F32)).astype(BF16)
    a_ref[...] = _dot(tri, p1) + _dot(tri, p2) + _dot(tri, p3)

    rows = lax.broadcasted_iota(jnp.int32, (SUB, half), 0)
    lo = lax.broadcasted_iota(jnp.int32, (SUB, half), 1) < HEAD_D
    same_head = ((lax.broadcasted_iota(jnp.int32, (half, half), 0) >> HEAD_SHIFT)
                 == (lax.broadcasted_iota(jnp.int32, (half, half), 1) >> HEAD_SHIFT))

    def pair_sums(p):
        s0 = jnp.sum(jnp.where(lo, p, 0.0), axis=1, keepdims=True)
        s1 = jnp.sum(jnp.where(lo, 0.0, p), axis=1, keepdims=True)
        return jnp.where(lo, s0, s1)

    def chunk(ci):
        t0 = (tb // ch - 1 - ci if rev else ci) * ch
        win = slice(t0, t0 + ch)
        a = a_ref[win, :]
        qc = qs_ref[win, :]
        kc = key_ref[win, :]
        vc = v_ref[win, :]
        last = 0 if rev else ch - 1
        a_last = a[last:last + 1, :]
        qe = (qc * jnp.exp(a)).astype(BF16)
        ke = (kc * jnp.exp(a_last - a)).astype(BF16)
        ea = jnp.exp(a_last)
        vb = vc.astype(BF16)
        nslab = ch // SUB
        acc = [[jnp.zeros((SUB, half), F32) for _ in range(2)] for _ in range(nslab)]
        for jj in range(ch):
            for sb in range(nslab):
                r0 = sb * SUB
                if (r0 > jj) if rev else (r0 + SUB - 1 < jj):
                    continue
                whole = (r0 + SUB - 1 <= jj) if rev else (r0 >= jj)
                rs = slice(r0, r0 + SUB)
                valid = (rows + r0 <= jj) if rev else (rows + r0 >= jj)
                for hp in range(2):
                    sl = slice(hp * half, (hp + 1) * half)
                    dec = jnp.exp(jnp.minimum(a[rs, sl] - a[jj:jj + 1, sl], 0.0))
                    p = qc[rs, sl] * dec * kc[jj:jj + 1, sl]
                    if not whole:
                        p = jnp.where(valid, p, 0.0)
                    acc[sb][hp] = acc[sb][hp] + pair_sums(p) * vc[jj:jj + 1, sl]
        for hp in range(2):
            sl = slice(hp * half, (hp + 1) * half)
            s = s_ref[hp]
            inter = _dot_nt(qe[:, sl], s.astype(BF16))
            intra = jnp.concatenate([acc[sb][hp] for sb in range(nslab)], axis=0)
            o_ref[win, sl] = inter + intra
            s_ref[hp] = s * ea[:, sl] + jnp.where(same_head, _dot_tn(vb[:, sl], ke[:, sl]), 0.0)

    yield
    for ci in range(tb // ch):
        chunk(ci)
        yield

    @pl.when(jb == nt - 1)
    def _():
        fin_ref[...] = s_ref[...]


def _rw_prep_kernel(*refs, grid_shift, tm):
    if grid_shift:
        (zc_ref, zu_ref, zd_ref, mu_ref, vec_ref, w0_ref, lora_ref,
         r_ref, k2_ref, v_ref, w_ref, nkk_ref, kka_ref, g_ref, bonus_ref, buf_ref) = refs
    else:
        (zc_ref, mu_ref, vec_ref, w0_ref, lora_ref,
         r_ref, k2_ref, v_ref, w_ref, nkk_ref, kka_ref, g_ref, bonus_ref, buf_ref) = refs
    i = pl.program_id(1)
    nt = pl.num_programs(1)
    halo = GRID_W
    width = 4 * MIX_W
    z = zc_ref[...]
    buf_ref[halo:halo + tm, :] = z
    lane = lax.broadcasted_iota(jnp.int32, (tm, width), 1)
    row = lax.broadcasted_iota(jnp.int32, (tm, width), 0)
    if grid_shift:
        buf_ref[0:halo, :] = jnp.where(i > 0, zu_ref[...], 0.0)
        buf_ref[halo + tm:2 * halo + tm, :] = jnp.where(i < nt - 1, zd_ref[...], 0.0)
        col = row & (GRID_W - 1)
        left = jnp.where(col > 0, buf_ref[halo - 1:halo - 1 + tm, :], 0.0)
        right = jnp.where(col < GRID_W - 1, buf_ref[halo + 1:halo + 1 + tm, :], 0.0)
        up = buf_ref[0:tm, :]
        down = buf_ref[2 * halo:2 * halo + tm, :]
        sel = lane & 3
        shifted = jnp.where(sel == 0, left, jnp.where(sel == 1, right, jnp.where(sel == 2, up, down)))
    else:
        zrow = jnp.zeros((1, width), F32)
        buf_ref[halo - 1:halo, :] = zrow
        buf_ref[halo + tm:halo + tm + 1, :] = zrow
        prev = buf_ref[halo - 1:halo - 1 + tm, :]
        nxt = buf_ref[halo + 1:halo + 1 + tm, :]
        shifted = jnp.where((lane & 1) == 0, prev, nxt)
    zs = z + mu_ref[...] * (shifted - z)
    r = zs[:, 0:MIX_W]
    k = zs[:, MIX_W:2 * MIX_W]
    v = zs[:, 2 * MIX_W:3 * MIX_W]
    sm = zs[:, 3 * MIX_W:4 * MIX_W]
    ones_bd = _head_ones()
    a0, k_k, k_a, r_k = (vec_ref[j:j + 1, :] for j in range(4))
    a = jax.nn.sigmoid(a0 + _dot(sm.astype(BF16), lora_ref[2]))
    g_ref[...] = _dot(jax.nn.sigmoid(sm).astype(BF16), lora_ref[3])
    kk = k * k_k
    kk = kk * lax.rsqrt(_head_sum(kk * kk, ones_bd) + 1e-12)
    k2 = k * (1.0 + (a - 1.0) * k_a)
    th = jnp.tanh(sm).astype(BF16)
    for di in range(2):
        w_ref[di] = -math.exp(-0.5) * jax.nn.sigmoid(w0_ref[di:di + 1, :] + _dot(th, lora_ref[di]))
    r_ref[...] = r
    k2_ref[...] = k2
    v_ref[...] = v
    nkk_ref[...] = -kk
    kka_ref[...] = kk * a
    bonus_ref[...] = _head_sum(r * k2 * r_k, ones_bd) * v


def _rw_prep(z, mu_p, vecs, w0, lora, grid_shift):
    B, T, _ = z.shape
    width = 4 * MIX_W
    wblk = ZB_RW // 4
    if grid_shift:
        tm = min(T, 512)
        hb = tm // GRID_W
        nh = T // GRID_W
        z_specs = [
            pl.BlockSpec((None, tm, width), lambda b, i: (b, i, wblk)),
            pl.BlockSpec((None, GRID_W, width), lambda b, i: (b, jnp.maximum(i * hb - 1, 0), wblk)),
            pl.BlockSpec((None, GRID_W, width), lambda b, i: (b, jnp.minimum((i + 1) * hb, nh - 1), wblk)),
        ]
        z_args = (z, z, z)
    else:
        tm = T
        z_specs = [pl.BlockSpec((None, tm, width), lambda b, i: (b, i, wblk))]
        z_args = (z,)
    const = lambda shape: pl.BlockSpec(shape, lambda b, i: (0,) * len(shape))
    ospec = pl.BlockSpec((None, tm, MIX_W), lambda b, i: (b, i, 0))
    oshape = jax.ShapeDtypeStruct((B, T, MIX_W), F32)
    dspec = pl.BlockSpec((2, None, tm, MIX_W), lambda b, i: (0, b, i, 0))
    dshape = jax.ShapeDtypeStruct((2, B, T, MIX_W), F32)
    return pl.pallas_call(
        functools.partial(_rw_prep_kernel, grid_shift=grid_shift, tm=tm),
        grid=(B, T // tm),
        in_specs=z_specs + [const((1, width)), const((4, MIX_W)), const((2, MIX_W)),
                            const((4, MIX_W, MIX_W))],
        out_specs=[ospec, ospec, ospec, dspec, ospec, ospec, ospec, ospec],
        out_shape=[oshape, oshape, oshape, dshape, oshape, oshape, oshape, oshape],
        scratch_shapes=[pltpu.VMEM((tm + 2 * GRID_W, width), F32)],
        compiler_params=_cparams("parallel", "parallel"),
        name="rwkv_prep",
    )(*z_args, mu_p, vecs, w0, lora)


RW_CHUNK = 64
RW_CHUNK_SHIFT = 6


def _mm(a, b):
    return jnp.dot(a.astype(BF16), b.astype(BF16), preferred_element_type=F32)


def _rwkv_steps(lw_ref, a_ref, b_ref, k_ref, r_ref, v_ref, s0_ref, y_ref, fin_ref, s_ref, g_ref,
                *, rev, tb, nt):
    jb = pl.program_id(1)
    ch = RW_CHUNK
    pair = 2 * HEAD_D

    @pl.when(jb == 0)
    def _():
        s_ref[...] = s0_ref[...]

    lw = lw_ref[...]
    rr = lax.broadcasted_iota(jnp.int32, (tb, tb), 0)
    cc = lax.broadcasted_iota(jnp.int32, (tb, tb), 1)
    same = (rr >> RW_CHUNK_SHIFT) == (cc >> RW_CHUNK_SHIFT)
    tri = jnp.where(same & ((cc >= rr) if rev else (cc <= rr)), 1.0, 0.0).astype(BF16)
    p1 = lw.astype(BF16)
    r1 = lw - p1.astype(F32)
    p2 = r1.astype(BF16)
    p3 = (r1 - p2.astype(F32)).astype(BF16)
    g_ref[...] = _dot(tri, p1) + _dot(tri, p2) + _dot(tri, p3)

    si = lax.broadcasted_iota(jnp.int32, (ch, ch), 0)
    ri = lax.broadcasted_iota(jnp.int32, (ch, ch), 1)
    if rev:
        si, ri = ch - 1 - si, ch - 1 - ri
    strict = ri < si
    incl = ri <= si
    eye = jnp.where(ri == si, 1.0, 0.0).astype(F32)
    levels = [((si >> (lv + 1)) == (ri >> (lv + 1))) & (((si >> lv) & 1) == 1) & (((ri >> lv) & 1) == 0)
              for lv in range(RW_CHUNK_SHIFT)]
    lo = lax.broadcasted_iota(jnp.int32, (ch, pair), 1) < HEAD_D
    same_head = ((lax.broadcasted_iota(jnp.int32, (pair, pair), 0) >> HEAD_SHIFT)
                 == (lax.broadcasted_iota(jnp.int32, (pair, pair), 1) >> HEAD_SHIFT))

    nch = tb // ch
    g = g_ref[...]
    e_g = jnp.exp(g)
    e_ng = jnp.exp(-g)
    at_f = a_ref[...] * jnp.exp(g - lw)
    rt_f = r_ref[...] * e_g
    at = at_f.astype(BF16)
    rt = rt_f.astype(BF16)
    first = (lax.broadcasted_iota(jnp.int32, (tb, MIX_W), 1) & HEAD_D) == 0
    at_h = [jnp.where(first, at_f, 0.0).astype(BF16), jnp.where(first, 0.0, at_f).astype(BF16)]
    rt_h = [jnp.where(first, rt_f, 0.0).astype(BF16), jnp.where(first, 0.0, rt_f).astype(BF16)]
    bt = (b_ref[...] * e_ng).astype(BF16)
    kt = (k_ref[...] * e_ng).astype(BF16)
    vb = v_ref[...].astype(BF16)

    items = [(c, hp, hh) for c in range(nch) for hp in range(2) for hh in range(2)]
    rows = lambda c: slice(c * ch, (c + 1) * ch)
    lanes = lambda hp: slice(hp * pair, (hp + 1) * pair)
    n_m, p_m, m_m, q_m = {}, {}, {}, {}
    for it in items:
        c, hp, hh = it
        ar = jnp.concatenate([at_h[hh][rows(c), lanes(hp)], rt_h[hh][rows(c), lanes(hp)]], axis=0)
        np_ = _dot_nt(ar, bt[rows(c), lanes(hp)])
        mq = _dot_nt(ar, kt[rows(c), lanes(hp)])
        n_m[it] = jnp.where(strict, np_[:ch], 0.0).astype(BF16)
        p_m[it] = jnp.where(incl, np_[ch:], 0.0).astype(BF16)
        m_m[it] = jnp.where(strict, mq[:ch], 0.0).astype(BF16)
        q_m[it] = jnp.where(incl, mq[ch:], 0.0).astype(BF16)
        if hp == 1 and hh == 1:
            yield
    t_m = {it: eye + jnp.where(levels[0], n_m[it].astype(F32), 0.0) for it in items}
    for lv in range(1, RW_CHUNK_SHIFT):
        tn = {it: _mm(t_m[it], n_m[it]) for it in items}
        yield
        t_m = {it: t_m[it] + jnp.where(levels[lv], _mm(tn[it], t_m[it]), 0.0) for it in items}
        yield
    mv = {it: _mm(m_m[it], vb[rows(it[0]), lanes(it[1])]) for it in items}
    qv = {it: _mm(q_m[it], vb[rows(it[0]), lanes(it[1])]) for it in items}
    yield
    y2 = {it: _mm(t_m[it], jnp.concatenate([at[rows(it[0]), lanes(it[1])], mv[it].astype(BF16)], axis=1))
          for it in items}
    yield

    last = 0 if rev else ch - 1
    for ci in range(nch):
        c = nch - 1 - ci if rev else ci
        g_c = g[c * ch + last:c * ch + last + 1, :]
        e_gc = jnp.exp(g_c - g[rows(c), :])
        bh = (b_ref[rows(c), :] * e_gc).astype(BF16)
        kh = (k_ref[rows(c), :] * e_gc).astype(BF16)
        dec_c = jnp.exp(g_c)
        xs, sas, ss = [], [], []
        for hp in range(2):
            s = s_ref[hp]
            w2 = jnp.where(lo, y2[(c, hp, 0)][:, :pair], y2[(c, hp, 1)][:, :pair]).astype(BF16)
            xs.append(_dot_nt(jnp.concatenate([w2, rt[rows(c), lanes(hp)]], axis=0), s.astype(BF16)))
            ss.append(s)
        for hp in range(2):
            w1 = jnp.where(lo, y2[(c, hp, 0)][:, pair:], y2[(c, hp, 1)][:, pair:])
            sas.append(w1 + xs[hp][:ch])
        for hp in range(2):
            sa = sas[hp]
            y_ref[rows(c), lanes(hp)] = xs[hp][ch:] + jnp.where(
                lo, _mm(p_m[(c, hp, 0)], sa) + qv[(c, hp, 0)], _mm(p_m[(c, hp, 1)], sa) + qv[(c, hp, 1)])
            upd = _dot_tn(jnp.concatenate([sa.astype(BF16), vb[rows(c), lanes(hp)]], axis=0),
                          jnp.concatenate([bh[:, lanes(hp)], kh[:, lanes(hp)]], axis=0))
            s_ref[hp] = ss[hp] * dec_c[:, lanes(hp)] + jnp.where(same_head, upd, 0.0)
        yield

    @pl.when(jb == nt - 1)
    def _():
        fin_ref[...] = s_ref[...]


def _hg_rw_kernel(*refs, rev, tb, nt):
    gla_in, rw_in = refs[0:5], refs[5:12]
    gla_out, rw_out = refs[12:14], refs[14:16]
    gla_scr, rw_scr = refs[16:20], refs[20:22]
    gla = _gla_steps(*gla_in, *gla_out, *gla_scr, rev=rev, tb=tb, nt=nt)
    rwkv = _rwkv_steps(*rw_in, *rw_out, *rw_scr, rev=rev, tb=tb, nt=nt)
    live = [gla, rwkv]
    while live:
        for gen in list(live):
            if next(gen, StopIteration) is StopIteration:
                live.remove(gen)


def _head_pair_blockdiag(s):
    B = s.shape[0]
    s = s.reshape(B, 2, 2, HEAD_D, HEAD_D)
    zero = jnp.zeros_like(s[:, :, 0])
    return jnp.concatenate([jnp.concatenate([s[:, :, 0], zero], axis=-1),
                            jnp.concatenate([zero, s[:, :, 1]], axis=-1)], axis=-2)


def _head_pair_blocks(t):
    B = t.shape[0]
    t = jnp.stack([t[:, :, :HEAD_D, :HEAD_D], t[:, :, HEAD_D:, HEAD_D:]], axis=2)
    return t.reshape(B, N_HEADS, HEAD_D, HEAD_D)


def _hg_rw(z, lb, s0_hg, lw, a, b, k, r, v, s0_rw, rev):
    B, T, _ = v.shape
    pair = 2 * HEAD_D
    tb = min(T, 256)
    nt = T // tb
    di = 1 if rev else 0
    tmap = (lambda j: nt - 1 - j) if rev else (lambda j: j)
    zspec = lambda blk: pl.BlockSpec((None, tb, MIX_W), lambda bi, j: (bi, tmap(j), blk))
    spec = zspec(0)
    sspec = pl.BlockSpec((None, 2, pair, pair), lambda bi, j: (bi, 0, 0, 0))
    state = jax.ShapeDtypeStruct((B, 2, pair, pair), F32)
    seq = jax.ShapeDtypeStruct((B, T, MIX_W), F32)
    blk = pltpu.VMEM((tb, MIX_W), F32)
    st = pltpu.VMEM((2, pair, pair), F32)
    o, fin_hg, y, fin_rw = pl.pallas_call(
        functools.partial(_hg_rw_kernel, rev=rev, tb=tb, nt=nt),
        grid=(B, nt),
        in_specs=[zspec(ZB_HG), zspec(ZB_HG + 1 + di), zspec(ZB_HG + 3),
                  pl.BlockSpec((1, MIX_W), lambda bi, j: (0, 0)), sspec,
                  pl.BlockSpec((None, None, tb, MIX_W), lambda bi, j: (di, bi, tmap(j), 0)),
                  spec, spec, spec, spec, spec, sspec],
        out_specs=[spec, sspec, spec, sspec],
        out_shape=[seq, state, seq, state],
        scratch_shapes=[st, blk, blk, blk, st, blk],
        compiler_params=_cparams("parallel", "arbitrary"),
        name="hgrn_rwkv_bwd" if rev else "hgrn_rwkv_fwd",
    )(z, z, z, lb[di:di + 1], _head_pair_blockdiag(s0_hg),
      lw, a, b, k, r, v, _head_pair_blockdiag(s0_rw))
    return o, _head_pair_blocks(fin_hg), y, _head_pair_blocks(fin_rw)


def _mix_kernel(x_ref, sc_ref, sh_ref, g1_ref, nw0_ref, nw_ref, wg_ref,
                retf_ref, retb_ref, retg_ref, s5f_ref, s5b_ref, s5u_ref, hgf_ref, hgb_ref, hgg_ref, rwf_ref, rwr_ref,
                rwb_ref, rwg_ref,
                vec_ref, glu_w_ref, wbr_ref, wout_ref, o_ref):
    h = (_rms(x_ref[...], nw0_ref[...]) * (1.0 + sc_ref[...]) + sh_ref[...]).astype(BF16)
    gn_w, s5_d, glu_b, hg_w, ln_w, ln_b = (vec_ref[j:j + 1, :] for j in range(6))

    def group_norm(o, eps):
        mu = _head_sum_lanes(o) * (1.0 / HEAD_D)
        oc = o - mu
        var = _head_sum_lanes(oc * oc) * (1.0 / HEAD_D)
        return oc * lax.rsqrt(var + eps)

    g = retg_ref[...]
    y_ret = group_norm(retf_ref[...] + retb_ref[...], EPS) * gn_w * (g * jax.nn.sigmoid(g))

    y = s5_d * s5u_ref[...] + s5f_ref[...] + s5b_ref[...]
    yg = jax.nn.gelu(y)
    y_s5 = yg * jax.nn.sigmoid(_dot(yg.astype(BF16), glu_w_ref[...]) + glu_b)

    o = hgf_ref[...] + hgb_ref[...]
    g = hgg_ref[...]
    ms = _head_sum_lanes(o * o) * (1.0 / HEAD_D)
    y_hg = o * lax.rsqrt(ms + EPS) * hg_w * (g * jax.nn.sigmoid(g))

    y = group_norm(rwf_ref[...] + rwr_ref[...], RW_LN_EPS) * ln_w + ln_b
    y_rw = (y + rwb_ref[...]) * rwg_ref[...]

    mixed = None
    for m, ym in enumerate((y_ret, y_s5, y_hg, y_rw)):
        br = _dot(ym.astype(BF16), wbr_ref[m])
        term = jax.nn.sigmoid(_dot(h, wg_ref[:, m * D_MODEL:(m + 1) * D_MODEL])) * br
        mixed = term if mixed is None else mixed + term
    mixed = _dot(mixed.astype(BF16), wout_ref[...])
    o_ref[...] = x_ref[...] + g1_ref[...] * _rms(mixed, nw_ref[...])


def _mix(x, sc, sh, g1, nw0, nw, w_gate, z, ret_o, s5_y, hg_o, rw_y, rw_bonus, rw_g, vecs, glu_w,
         w_branch, w_out):
    B, T, _ = x.shape
    tm = min(T, 256)
    xspec = pl.BlockSpec((None, tm, D_MODEL), lambda b, i: (b, i, 0))
    zspec = lambda blk: pl.BlockSpec((None, tm, MIX_W), lambda b, i: (b, i, blk))
    const = lambda shape: pl.BlockSpec(shape, lambda b, i: (0,) * len(shape))
    mspec = lambda m: pl.BlockSpec((None, 1, D_MODEL), _mod_map(m.shape[0]))
    return pl.pallas_call(
        _mix_kernel,
        grid=(B, T // tm),
        in_specs=[
            xspec, mspec(sc), mspec(sh), mspec(g1),
            const((1, D_MODEL)), const((1, D_MODEL)),
            const((D_MODEL, 4 * D_MODEL)),
            zspec(0), zspec(0), zspec(ZB_RET + 3),
            zspec(0), zspec(0), zspec(ZB_S5),
            zspec(0), zspec(0), zspec(ZB_HG + 4),
            zspec(0), zspec(0), zspec(0), zspec(0),
            const((6, MIX_W)), const((MIX_W, MIX_W)),
            const((4, MIX_W, D_MODEL)), const((D_MODEL, D_MODEL)),
        ],
        out_specs=xspec,
        out_shape=jax.ShapeDtypeStruct((B, T, D_MODEL), F32),
        compiler_params=_cparams("parallel", "parallel"),
        name="mix_out",
    )(x, sc, sh, g1, nw0, nw, w_gate, ret_o[0], ret_o[1], z, s5_y[0], s5_y[1], z, hg_o[0], hg_o[1], z, rw_y[0], rw_y[1],
      rw_bonus, rw_g, vecs, glu_w, w_branch, w_out)


FFN_CHUNK = 1024


def _ffn_kernel(x_ref, sc_ref, sh_ref, g2_ref, nw2_ref, nw3_ref, w1_ref, w2_ref, o_ref):
    x = x_ref[...]
    h = (_rms(x, nw2_ref[...]) * (1.0 + sc_ref[...]) + sh_ref[...]).astype(BF16)
    acc = None
    for j in range(D_FF // FFN_CHUNK):
        cols = slice(j * FFN_CHUNK, (j + 1) * FFN_CHUNK)
        a = jnp.maximum(_dot(h, w1_ref[:, cols]), 0.0)
        part = _dot((a * a).astype(BF16), w2_ref[cols, :])
        acc = part if acc is None else acc + part
    o_ref[...] = x + g2_ref[...] * _rms(acc, nw3_ref[...])


def _ffn(x, sc, sh, g2, nw2, nw3, w1, w2):
    B, T, _ = x.shape
    tm = min(T, 512)
    xspec = pl.BlockSpec((None, tm, D_MODEL), lambda b, i: (b, i, 0))
    mspec = lambda m: pl.BlockSpec((None, 1, D_MODEL), _mod_map(m.shape[0]))
    const = pl.BlockSpec((1, D_MODEL), lambda b, i: (0, 0))
    return pl.pallas_call(
        _ffn_kernel,
        grid=(B, T // tm),
        in_specs=[xspec, mspec(sc), mspec(sh), mspec(g2), const, const,
                  pl.BlockSpec((D_MODEL, D_FF), lambda b, i: (0, 0)),
                  pl.BlockSpec((D_FF, D_MODEL), lambda b, i: (0, 0))],
        out_specs=xspec,
        out_shape=jax.ShapeDtypeStruct((B, T, D_MODEL), F32),
        compiler_params=_cparams("parallel", "parallel"),
        name="ffn",
    )(x, sc, sh, g2, nw2, nw3, w1, w2)


def _layer_params(l, p):
    w = p['w_in'][l]
    w_p = jnp.concatenate(
        [w[:, 2560:3488], jnp.zeros((D_MODEL, 4 * MIX_W - RW_COLS), F32),
         w[:, 0:1024], w[:, 1024:1280], w[:, 1280:2560]], axis=1).astype(BF16)
    w_gate = w[:, 3488:7584].astype(BF16)

    a_re, a_im, bb_re, bb_im = _s5_zoh(p['s5_lam_re'][l], p['s5_lam_im'][l], p['s5_log_dt'][l],
                                       p['s5_b_re'][l], p['s5_b_im'][l])
    eye = jnp.eye(S5_GROUPS, dtype=F32)
    bblk = lambda bb: jnp.einsum('dgph,gk->dghkp', bb, eye).reshape(2, MIX_W, S5_STATE).astype(BF16)
    cblk = lambda c: jnp.einsum('ghp,gk->gpkh', c, eye).reshape(S5_STATE, MIX_W).astype(BF16)

    def lora_pad(m, row0):
        return jnp.zeros((MIX_W, MIX_W), F32).at[row0:row0 + m.shape[0]].set(m)

    lora = jnp.stack([lora_pad(p['rw_w2'][l, 0], 0), lora_pad(p['rw_w2'][l, 1], 32),
                      lora_pad(p['rw_a2'][l], 64), lora_pad(p['rw_g2'][l], 96)]).astype(BF16)
    mu_p = jnp.concatenate([p['rw_mu'][l], jnp.zeros((4 * MIX_W - RW_COLS,), F32)]).reshape(1, 4 * MIX_W)
    return dict(
        w_p=w_p, w_gate=w_gate, nw=p['norm_w'][l],
        s5=(a_re, a_im, bblk(bb_re), bblk(bb_im), cblk(p['s5_c_re'][l]), cblk(p['s5_c_im'][l])),
        rw_mu=mu_p, rw_lora=lora, rw_w0=p['rw_w0'][l],
        rw_vecs=jnp.stack([p['rw_a0'][l], p['rw_k_k'][l], p['rw_k_a'][l], p['rw_r_k'][l]]),
        mix_vecs=jnp.stack([p['ret_gn_w'][l], p['s5_d'][l], p['s5_glu_b'][l], p['hg_norm_w'][l],
                            p['rw_ln_w'][l], p['rw_ln_b'][l]]),
        glu_w=p['s5_glu_w'][l].astype(BF16),
        w_branch=p['w_branch'][l].astype(BF16), w_out=p['w_out'][l].astype(BF16),
        ff_w1=p['ff_w1'][l].astype(BF16), ff_w2=p['ff_w2'][l].astype(BF16),
    )


def _trunk_layer(x, mod, init, grid_shift, lp, hg_lb):
    B, T, _ = x.shape
    sh1, sc1, g1, sh2, sc2, g2 = (m[:, None, :] for m in jnp.split(mod, 6, axis=-1))
    nw = lp['nw']
    if mod.shape[0] == 1:
        tok = lambda t: t.reshape(t.shape[:-3] + (1, B * T, t.shape[-1]))
    else:
        tok = lambda t: t
    z = _proj_in(tok(x), sc1, sh1, nw[0:1], lp['w_p']).reshape(B, T, Z_COLS)

    s_ret, s_s5r, s_s5i, s_hg, s_rw = init
    s5_f, s5_b, f_s5r, f_s5i, ret_f, ret_b, f_ret = _ret_s5(
        z, *lp['s5'], s_s5r.reshape(B, 2, 1, S5_STATE), s_s5i.reshape(B, 2, 1, S5_STATE), s_ret)
    f_s5r = f_s5r.reshape(B, 2, S5_GROUPS, S5_P)
    f_s5i = f_s5i.reshape(B, 2, S5_GROUPS, S5_P)

    r, k2, v, w, nkk, kka, rw_g, bonus = _rw_prep(z, lp['rw_mu'], lp['rw_vecs'], lp['rw_w0'],
                                                  lp['rw_lora'], grid_shift)
    s_hg_t = jnp.swapaxes(s_hg, -1, -2)
    hg_f, f_hg_f, rw_f, f_rw_f = _hg_rw(z, hg_lb, s_hg_t[:, 0], w, nkk, kka, k2, r, v, s_rw[:, 0], False)
    hg_b, f_hg_b, rw_b, f_rw_b = _hg_rw(z, hg_lb, s_hg_t[:, 1], w, nkk, kka, k2, r, v, s_rw[:, 1], True)
    f_hg = jnp.swapaxes(jnp.stack([f_hg_f, f_hg_b], axis=1), -1, -2)
    f_rw = jnp.stack([f_rw_f, f_rw_b], axis=1)

    x = _mix(tok(x), sc1, sh1, g1, nw[0:1], nw[1:2], lp['w_gate'], tok(z), (tok(ret_f), tok(ret_b)), (tok(s5_f), tok(s5_b)),
             (tok(hg_f), tok(hg_b)), (tok(rw_f), tok(rw_b)), tok(bonus), tok(rw_g),
             lp['mix_vecs'], lp['glu_w'], lp['w_branch'], lp['w_out'])
    x = _ffn(x, sc2, sh2, g2, nw[2:3], nw[3:4], lp['ff_w1'], lp['ff_w2'])
    return x.reshape(B, T, D_MODEL), (f_ret, f_s5r, f_s5i, f_hg, f_rw)


def kernel(x_prompt, x_sample, state_ret, state_s5_re, state_s5_im, state_hgrn, state_rwkv, c, c_ctx, ada_w, ada_b, norm_w, w_in, ret_gn_w, s5_lam_re, s5_lam_im, s5_log_dt, s5_b_re, s5_b_im, s5_c_re, s5_c_im, s5_d, s5_glu_w, s5_glu_b, hg_lb, hg_norm_w, rw_mu, rw_w0, rw_w2, rw_a0, rw_a2, rw_g2, rw_k_k, rw_k_a, rw_r_k, rw_ln_w, rw_ln_b, w_branch, w_out, ff_w1, ff_w2):
    p = dict(norm_w=norm_w, w_in=w_in, ret_gn_w=ret_gn_w, s5_lam_re=s5_lam_re, s5_lam_im=s5_lam_im,
             s5_log_dt=s5_log_dt, s5_b_re=s5_b_re, s5_b_im=s5_b_im, s5_c_re=s5_c_re, s5_c_im=s5_c_im,
             s5_d=s5_d, s5_glu_w=s5_glu_w, s5_glu_b=s5_glu_b, hg_norm_w=hg_norm_w, rw_mu=rw_mu,
             rw_w0=rw_w0, rw_w2=rw_w2, rw_a0=rw_a0, rw_a2=rw_a2, rw_g2=rw_g2, rw_k_k=rw_k_k,
             rw_k_a=rw_k_a, rw_r_k=rw_r_k, rw_ln_w=rw_ln_w, rw_ln_b=rw_ln_b, w_branch=w_branch,
             w_out=w_out, ff_w1=ff_w1, ff_w2=ff_w2)
    depth = w_in.shape[0]
    n_ctx = x_prompt.shape[0]
    n_lat = x_sample.shape[0]
    assert 1 + n_lat <= 8

    lb_cum = jnp.cumsum(jax.nn.softmax(hg_lb.astype(F32), axis=0), axis=0)
    hg_lower = lb_cum - lb_cum[0]

    cond = jnp.concatenate([c_ctx[None, :], c, jnp.zeros((7 - n_lat, D_MODEL), F32)], axis=0)
    mod = _ada_mod(cond, ada_w, ada_b)

    zero_state = (jnp.zeros((n_ctx, 2, N_HEADS, HEAD_D, HEAD_D), F32),
                  jnp.zeros((n_ctx, 2, S5_GROUPS, S5_P), F32),
                  jnp.zeros((n_ctx, 2, S5_GROUPS, S5_P), F32),
                  jnp.zeros((n_ctx, 2, N_HEADS, HEAD_D, HEAD_D), F32),
                  jnp.zeros((n_ctx, 2, N_HEADS, HEAD_D, HEAD_D), F32))
    xp, xs = x_prompt, x_sample
    finals = []
    for l in range(depth):
        lp = _layer_params(l, p)
        xp, fin = _trunk_layer(xp, mod[l, 0:1], zero_state, False, lp, hg_lower[l])
        finals.append(fin)
        lat_init = (state_ret[:, l], state_s5_re[:, l], state_s5_im[:, l], state_hgrn[:, l], state_rwkv[:, l])
        xs, _ = _trunk_layer(xs, mod[l, 1:1 + n_lat], lat_init, True, lp, hg_lower[l])
    new_states = tuple(jnp.stack([f[i] for f in finals], axis=1) for i in range(5))
    return (xp, xs) + new_states
```

```python
import functools
import math

import jax
import jax.numpy as jnp
from jax import lax
from jax.experimental import pallas as pl
from jax.experimental.pallas import tpu as pltpu

F32 = jnp.float32
BF16 = jnp.bfloat16

D_MODEL = 1024
GRID_W = 64
MIX_W = 256
N_HEADS = 4
HEAD_D = 64
HEAD_SHIFT = 6
LANES = 128
S5_GROUPS = 16
S5_CH = 16
S5_P = 64
S5_STATE = S5_GROUPS * S5_P
D_FF = 4096
EPS = 1e-6
RW_LN_EPS = 64e-5
RW_COLS = 928

Z_COLS = 3584
ZB_RW = 0
ZB_RET = 4
ZB_S5 = 8
ZB_HG = 9

VMEM_LIMIT = 56 * 1024 * 1024

LOG_GAMMA = tuple(
    tuple(math.log1p(-2.0 ** (-(5.0 + 0.5 * di) - h)) for h in range(N_HEADS)) for di in range(2))


def _cparams(*sem):
    return pltpu.CompilerParams(dimension_semantics=sem, vmem_limit_bytes=VMEM_LIMIT)


def _dot(a, b):
    return jnp.dot(a, b, preferred_element_type=F32)


def _dot_nt(a, b):
    return lax.dot_general(a, b, (((1,), (1,)), ((), ())), preferred_element_type=F32)


def _dot_tn(a, b):
    return lax.dot_general(a, b, (((0,), (0,)), ((), ())), preferred_element_type=F32)


def _head_ones():
    r = lax.broadcasted_iota(jnp.int32, (MIX_W, MIX_W), 0) >> HEAD_SHIFT
    c = lax.broadcasted_iota(jnp.int32, (MIX_W, MIX_W), 1) >> HEAD_SHIFT
    return jnp.where(r == c, 1.0, 0.0).astype(F32)


def _head_sum(x, ones_bd):
    ones = ones_bd.astype(BF16)
    p1 = x.astype(BF16)
    r1 = x - p1.astype(F32)
    p2 = r1.astype(BF16)
    p3 = (r1 - p2.astype(F32)).astype(BF16)
    return _dot(p1, ones) + _dot(p2, ones) + _dot(p3, ones)


def _head_sum_lanes(x):
    lo = lax.broadcasted_iota(jnp.int32, (x.shape[0], LANES), 1) < HEAD_D
    parts = []
    for q in range(x.shape[1] // LANES):
        p = x[:, q * LANES:(q + 1) * LANES]
        s0 = jnp.sum(jnp.where(lo, p, 0.0), axis=1, keepdims=True)
        s1 = jnp.sum(jnp.where(lo, 0.0, p), axis=1, keepdims=True)
        parts.append(jnp.where(lo, s0, s1))
    return jnp.concatenate(parts, axis=1)


def _rms(x, w):
    return x * lax.rsqrt(jnp.mean(x * x, axis=-1, keepdims=True) + EPS) * w


def _ada_kernel(c_ref, w_ref, b_ref, o_ref):
    c = c_ref[...]
    s = c * jax.nn.sigmoid(c)
    o_ref[...] = _dot(s.astype(BF16), w_ref[...].astype(BF16)) + b_ref[...]


def _ada_mod(cond, ada_w, ada_b):
    L = ada_w.shape[0]
    n = ada_w.shape[2]
    tn = 1536
    return pl.pallas_call(
        _ada_kernel,
        grid=(L, n // tn),
        in_specs=[
            pl.BlockSpec((8, D_MODEL), lambda l, j: (0, 0)),
            pl.BlockSpec((None, D_MODEL, tn), lambda l, j: (l, 0, j)),
            pl.BlockSpec((None, 1, tn), lambda l, j: (l, 0, j)),
        ],
        out_specs=pl.BlockSpec((None, 8, tn), lambda l, j: (l, 0, j)),
        out_shape=jax.ShapeDtypeStruct((L, 8, n), F32),
        compiler_params=_cparams("parallel", "parallel"),
        name="ada_mod",
    )(cond, ada_w, ada_b.reshape(L, 1, n))


def _proj_in_kernel(x_ref, sc_ref, sh_ref, nw_ref, w_ref, z_ref):
    h = _rms(x_ref[...], nw_ref[...]) * (1.0 + sc_ref[...]) + sh_ref[...]
    z_ref[...] = _dot(h.astype(BF16), w_ref[...])


def _mod_map(bm):
    if bm == 1:
        return lambda b, *_: (0, 0, 0)
    return lambda b, *_: (b, 0, 0)


def _proj_in(x, sc, sh, nw, w_p):
    B, T, _ = x.shape
    tm = min(T, 512)
    return pl.pallas_call(
        _proj_in_kernel,
        grid=(B, T // tm),
        in_specs=[
            pl.BlockSpec((None, tm, D_MODEL), lambda b, i: (b, i, 0)),
            pl.BlockSpec((None, 1, D_MODEL), _mod_map(sc.shape[0])),
            pl.BlockSpec((None, 1, D_MODEL), _mod_map(sh.shape[0])),
            pl.BlockSpec((1, D_MODEL), lambda b, i: (0, 0)),
            pl.BlockSpec((D_MODEL, Z_COLS), lambda b, i: (0, 0)),
        ],
        out_specs=pl.BlockSpec((None, tm, Z_COLS), lambda b, i: (b, i, 0)),
        out_shape=jax.ShapeDtypeStruct((B, T, Z_COLS), F32),
        compiler_params=_cparams("parallel", "parallel"),
        name="proj_in",
    )(x, sc, sh, nw, w_p)


def _ret_steps(di, q_ref, k_ref, v_ref, o_ref, s_ref, lc):
    rev = di == 1
    rows = lax.broadcasted_iota(jnp.int32, (lc, lc), 0)
    cols = lax.broadcasted_iota(jnp.int32, (lc, lc), 1)
    rel = ((cols - rows) if rev else (rows - cols)).astype(F32)
    idx = lax.broadcasted_iota(jnp.int32, (lc, 1), 0)
    pos = ((lc - 1 - idx) if rev else idx).astype(F32)
    for h in range(N_HEADS):
        lg = LOG_GAMMA[di][h]
        sl = slice(h * HEAD_D, (h + 1) * HEAD_D)
        q = q_ref[:, sl]
        k = k_ref[:, sl] * (HEAD_D ** -0.5)
        v = v_ref[:, sl].astype(BF16)
        att = _dot_nt(q.astype(BF16), k.astype(BF16))
        att = att * jnp.where(rel >= 0.0, jnp.exp(jnp.maximum(rel, 0.0) * lg), 0.0)
        qd = q * jnp.exp((pos + 1.0) * lg)
        kd = k * jnp.exp((lc - 1.0 - pos) * lg)
        s = s_ref[di, h]
        o_ref[:, sl] = _dot(att.astype(BF16), v) + _dot(qd.astype(BF16), s.astype(BF16))
        s_ref[di, h] = s * math.exp(lc * lg) + _dot_tn(kd.astype(BF16), v)
        yield


def _s5_zoh_kernel(lre_ref, lim_ref, ldt_ref, bre_ref, bim_ref, are_ref, aim_ref, bbre_ref, bbim_ref):
    lam_re = jnp.minimum(lre_ref[...], -1e-4)
    lam_im = lim_ref[...]
    dt = jnp.exp(ldt_ref[...])
    mag = jnp.exp(dt * lam_re)
    ang = dt * lam_im
    a_re = mag * jnp.cos(ang)
    a_im = mag * jnp.sin(ang)
    den = lam_re * lam_re + lam_im * lam_im
    f_re = ((a_re - 1.0) * lam_re + a_im * lam_im) / den
    f_im = (a_im * lam_re - (a_re - 1.0) * lam_im) / den
    b_re = bre_ref[...]
    b_im = bim_ref[...]
    are_ref[...] = a_re
    aim_ref[...] = a_im
    bbre_ref[...] = f_re * b_re - f_im * b_im
    bbim_ref[...] = f_re * b_im + f_im * b_re


def _s5_zoh(lam_re, lam_im, log_dt, b_re, b_im):
    n = 2 * S5_STATE
    col = lambda t: t.reshape(n, 1)
    ldt = jnp.broadcast_to(log_dt[:, :, None], (2, S5_GROUPS, S5_P))
    outs = pl.pallas_call(
        _s5_zoh_kernel,
        out_shape=[jax.ShapeDtypeStruct((n, 1), F32), jax.ShapeDtypeStruct((n, 1), F32),
                   jax.ShapeDtypeStruct((n, S5_CH), F32), jax.ShapeDtypeStruct((n, S5_CH), F32)],
        name="s5_zoh",
    )(col(lam_re), col(lam_im), col(ldt), b_re.reshape(n, S5_CH), b_im.reshape(n, S5_CH))
    a_re, a_im, bb_re, bb_im = outs
    shp = (2, S5_GROUPS, S5_P, S5_CH)
    return a_re.reshape(2, 1, S5_STATE), a_im.reshape(2, 1, S5_STATE), bb_re.reshape(shp), bb_im.reshape(shp)


S5_RADIX = 16


def _ret_s5_kernel(*refs, L, nc):
    (uf_ref, ub_ref, are_ref, aim_ref, bre_ref, bim_ref, cre_ref, cim_ref, x0r_ref, x0i_ref,
     qf_ref, kf_ref, vf_ref, qb_ref, kb_ref, vb_ref, rs0_ref,
     yf_ref, yb_ref, fr_ref, fi_ref, rof_ref, rob_ref, rfin_ref,
     car_re, car_im) = refs[:26]
    bufs, rs_ref = refs[26:34], refs[34]
    c = pl.program_id(1)
    R = S5_RADIX
    G = L // R

    @pl.when(c == 0)
    def _():
        car_re[...] = x0r_ref[...]
        car_im[...] = x0i_ref[...]
        rs_ref[...] = rs0_ref[...]

    rr = lax.broadcasted_iota(jnp.int32, (L, L), 0)
    cc = lax.broadcasted_iota(jnp.int32, (L, L), 1)
    lg, lr = G.bit_length() - 1, R.bit_length() - 1
    perm = jnp.where(cc == ((rr & (G - 1)) << lr) + (rr >> lg), 1.0, 0.0).astype(BF16)
    unperm = jnp.where(cc == ((rr & (R - 1)) << lg) + (rr >> lr), 1.0, 0.0).astype(BF16)

    def cmul(pr, pi, qr, qi):
        return pr * qr - pi * qi, pr * qi + pi * qr

    def steps(di, u_ref, y_ref, x_re, x_im, o_re, o_im):
        rev = di == 1
        u = _dot(perm, u_ref[...].astype(BF16)).astype(BF16)
        x_re[...] = _dot(u, bre_ref[di])
        x_im[...] = _dot(u, bim_ref[di])
        yield
        ar = are_ref[di]
        ai = aim_ref[di]
        order = list(range(R - 1, -1, -1)) if rev else list(range(R))
        slab = lambda j: slice(j * G, (j + 1) * G)

        er = x_re[slab(order[0]), :]
        ei = x_im[slab(order[0]), :]
        for n, j in enumerate(order[1:]):
            tr, ti = cmul(ar, ai, er, ei)
            er = tr + x_re[slab(j), :]
            ei = ti + x_im[slab(j), :]
            if n % 4 == 3:
                yield

        a_r, a_i = ar, ai
        for _ in range(R.bit_length() - 1):
            a_r, a_i = cmul(a_r, a_i, a_r, a_i)
        zr = car_re[di]
        zi = car_im[di]
        cin_r = [None] * G
        cin_i = [None] * G
        for k in (range(G - 1, -1, -1) if rev else range(G)):
            cin_r[k] = zr
            cin_i[k] = zi
            tr, ti = cmul(a_r, a_i, zr, zi)
            zr = tr + er[k:k + 1, :]
            zi = ti + ei[k:k + 1, :]
        car_re[di] = zr
        car_im[di] = zi
        yield

        xr = jnp.concatenate(cin_r, axis=0)
        xi = jnp.concatenate(cin_i, axis=0)
        for n, j in enumerate(order):
            tr, ti = cmul(ar, ai, xr, xi)
            xr = tr + x_re[slab(j), :]
            xi = ti + x_im[slab(j), :]
            o_re[slab(j), :] = xr
            o_im[slab(j), :] = xi
            if n % 4 == 3:
                yield
        xr_all = _dot(unperm, o_re[...].astype(BF16)).astype(BF16)
        xi_all = _dot(unperm, o_im[...].astype(BF16)).astype(BF16)
        y_ref[...] = _dot(xr_all, cre_ref[...]) - _dot(xi_all, cim_ref[...])

    live = [_ret_steps(0, qf_ref, kf_ref, vf_ref, rof_ref, rs_ref, L),
            steps(0, uf_ref, yf_ref, *bufs[0:4]),
            _ret_steps(1, qb_ref, kb_ref, vb_ref, rob_ref, rs_ref, L),
            steps(1, ub_ref, yb_ref, *bufs[4:8])]
    while live:
        for gen in list(live):
            if next(gen, StopIteration) is StopIteration:
                live.remove(gen)

    @pl.when(c == nc - 1)
    def _():
        fr_ref[...] = car_re[...]
        fi_ref[...] = car_im[...]
        rfin_ref[...] = rs_ref[...]


def _ret_s5(z, a_re, a_im, bblk_re, bblk_im, cblk_re, cblk_im, x0_re, x0_im, s0_ret):
    B, T, _ = z.shape
    L = min(T, 128)
    nc = T // L
    assert L % S5_RADIX == 0 and (L // S5_RADIX) & (L // S5_RADIX - 1) == 0
    full = lambda shape: pl.BlockSpec(shape, lambda b, c: (0,) * len(shape))
    sspec = pl.BlockSpec((None, 2, 1, S5_STATE), lambda b, c: (b, 0, 0, 0))
    uspec = lambda cmap, blk: pl.BlockSpec((None, L, MIX_W), lambda b, c: (b, cmap(c), blk))
    fwd = lambda c: c
    bwd = lambda c: nc - 1 - c
    buf = pltpu.VMEM((L, S5_STATE), F32)
    car = pltpu.VMEM((2, 1, S5_STATE), F32)
    seq = jax.ShapeDtypeStruct((B, T, MIX_W), F32)
    state = jax.ShapeDtypeStruct((B, 2, 1, S5_STATE), F32)
    rshape = (2, N_HEADS, HEAD_D, HEAD_D)
    rspec = pl.BlockSpec((None,) + rshape, lambda b, c: (b, 0, 0, 0, 0))
    return pl.pallas_call(
        functools.partial(_ret_s5_kernel, L=L, nc=nc),
        grid=(B, nc),
        in_specs=[
            uspec(fwd, ZB_S5), uspec(bwd, ZB_S5),
            full((2, 1, S5_STATE)), full((2, 1, S5_STATE)),
            full((2, MIX_W, S5_STATE)), full((2, MIX_W, S5_STATE)),
            full((S5_STATE, MIX_W)), full((S5_STATE, MIX_W)),
            sspec, sspec,
            uspec(fwd, ZB_RET), uspec(fwd, ZB_RET + 1), uspec(fwd, ZB_RET + 2),
            uspec(bwd, ZB_RET), uspec(bwd, ZB_RET + 1), uspec(bwd, ZB_RET + 2),
            rspec,
        ],
        out_specs=[uspec(fwd, 0), uspec(bwd, 0), sspec, sspec, uspec(fwd, 0), uspec(bwd, 0), rspec],
        out_shape=[seq, seq, state, state, seq, seq, jax.ShapeDtypeStruct((B,) + rshape, F32)],
        scratch_shapes=[car, car] + [buf] * 8 + [pltpu.VMEM(rshape, F32)],
        compiler_params=_cparams("parallel", "arbitrary"),
        name="ret_s5",
    )(z, z, a_re, a_im, bblk_re, bblk_im, cblk_re, cblk_im, x0_re, x0_im, z, z, z, z, z, z, s0_ret)


GLA_CHUNK = 16
GLA_CHUNK_SHIFT = 4
SUB = 8


def _gla_steps(q_ref, zf_ref, v_ref, lb_ref, s0_ref, o_ref, fin_ref, s_ref, a_ref, qs_ref, key_ref,
               *, rev, tb, nt):
    jb = pl.program_id(1)
    ch = GLA_CHUNK
    half = 2 * HEAD_D

    @pl.when(jb == 0)
    def _():
        s_ref[...] = s0_ref[...]

    lb = lb_ref[...]
    zf = zf_ref[...]
    q = q_ref[...]
    qs_ref[...] = q * jax.nn.sigmoid(q)
    key_ref[...] = (1.0 - lb) * jax.nn.sigmoid(-zf)
    l1 = jnp.log(lb)
    l2 = jnp.log1p(-lb) + jnp.minimum(zf, 0.0) - jnp.log1p(jnp.exp(-jnp.abs(zf)))
    lf = jnp.maximum(l1, l2) + jnp.log1p(jnp.exp(-jnp.abs(l1 - l2)))
    r = lax.broadcasted_iota(jnp.int32, (tb, tb), 0)
    c = lax.broadcasted_iota(jnp.int32, (tb, tb), 1)
    same = (r >> GLA_CHUNK_SHIFT) == (c >> GLA_CHUNK_SHIFT)
    tri = jnp.where(same & ((c >= r) if rev else (c <= r)), 1.0, 0.0).astype(BF16)
    p1 = lf.astype(BF16)
    r1 = lf - p1.astype(F32)
    p2 = r1.astype(BF16)
    p3 = (r1 - p2.astype(F32)).astype(BF16)
    a_ref[...] = _dot(tri, p1) + _dot(tri, p2) + _dot(tri, p3)

    rows = lax.broadcasted_iota(jnp.int32, (SUB, half), 0)
    lo = lax.broadcasted_iota(jnp.int32, (SUB, half), 1) < HEAD_D
    same_head = ((lax.broadcasted_iota(jnp.int32, (half, half), 0) >> HEAD_SHIFT)
                 == (lax.broadcasted_iota(jnp.int32, (half, half), 1) >> HEAD_SHIFT))

    def pair_sums(p):
        s0 = jnp.sum(jnp.where(lo, p, 0.0), axis=1, keepdims=True)
        s1 = jnp.sum(jnp.where(lo, 0.0, p), axis=1, keepdims=True)
        return jnp.where(lo, s0, s1)

    def chunk(ci):
        t0 = (tb // ch - 1 - ci if rev else ci) * ch
        win = slice(t0, t0 + ch)
        a = a_ref[win, :]
        qc = qs_ref[win, :]
        kc = key_ref[win, :]
        vc = v_ref[win, :]
        last = 0 if rev else ch - 1
        a_last = a[last:last + 1, :]
        qe = (qc * jnp.exp(a)).astype(BF16)
        ke = (kc * jnp.exp(a_last - a)).astype(BF16)
        ea = jnp.exp(a_last)
        vb = vc.astype(BF16)
        nslab = ch // SUB
        acc = [[jnp.zeros((SUB, half), F32) for _ in range(2)] for _ in range(nslab)]
        for jj in range(ch):
            for sb in range(nslab):
                r0 = sb * SUB
                if (r0 > jj) if rev else (r0 + SUB - 1 < jj):
                    continue
                whole = (r0 + SUB - 1 <= jj) if rev else (r0 >= jj)
                rs = slice(r0, r0 + SUB)
                valid = (rows + r0 <= jj) if rev else (rows + r0 >= jj)
                for hp in range(2):
                    sl = slice(hp * half, (hp + 1) * half)
                    dec = jnp.exp(jnp.minimum(a[rs, sl] - a[jj:jj + 1, sl], 0.0))
                    p = qc[rs, sl] * dec * kc[jj:jj + 1, sl]
                    if not whole:
                        p = jnp.where(valid, p, 0.0)
                    acc[sb][hp] = acc[sb][hp] + pair_sums(p) * vc[jj:jj + 1, sl]
        for hp in range(2):
            sl = slice(hp * half, (hp + 1) * half)
            s = s_ref[hp]
            inter = _dot_nt(qe[:, sl], s.astype(BF16))
            intra = jnp.concatenate([acc[sb][hp] for sb in range(nslab)], axis=0)
            o_ref[win, sl] = inter + intra
            s_ref[hp] = s * ea[:, sl] + jnp.where(same_head, _dot_tn(vb[:, sl], ke[:, sl]), 0.0)

    yield
    for ci in range(tb // ch):
        chunk(ci)
        yield

    @pl.when(jb == nt - 1)
    def _():
        fin_ref[...] = s_ref[...]


def _rw_prep_kernel(*refs, grid_shift, tm):
    if grid_shift:
        (zc_ref, zu_ref, zd_ref, mu_ref, vec_ref, w0_ref, lora_ref,
         r_ref, k2_ref, v_ref, w_ref, nkk_ref, kka_ref, g_ref, bonus_ref, buf_ref) = refs
    else:
        (zc_ref, mu_ref, vec_ref, w0_ref, lora_ref,
         r_ref, k2_ref, v_ref, w_ref, nkk_ref, kka_ref, g_ref, bonus_ref, buf_ref) = refs
    i = pl.program_id(1)
    nt = pl.num_programs(1)
    halo = GRID_W
    width = 4 * MIX_W
    z = zc_ref[...]
    buf_ref[halo:halo + tm, :] = z
    lane = lax.broadcasted_iota(jnp.int32, (tm, width), 1)
    row = lax.broadcasted_iota(jnp.int32, (tm, width), 0)
    if grid_shift:
        buf_ref[0:halo, :] = jnp.where(i > 0, zu_ref[...], 0.0)
        buf_ref[halo + tm:2 * halo + tm, :] = jnp.where(i < nt - 1, zd_ref[...], 0.0)
        col = row & (GRID_W - 1)
        left = jnp.where(col > 0, buf_ref[halo - 1:halo - 1 + tm, :], 0.0)
        right = jnp.where(col < GRID_W - 1, buf_ref[halo + 1:halo + 1 + tm, :], 0.0)
        up = buf_ref[0:tm, :]
        down = buf_ref[2 * halo:2 * halo + tm, :]
        sel = lane & 3
        shifted = jnp.where(sel == 0, left, jnp.where(sel == 1, right, jnp.where(sel == 2, up, down)))
    else:
        zrow = jnp.zeros((1, width), F32)
        buf_ref[halo - 1:halo, :] = zrow
        buf_ref[halo + tm:halo + tm + 1, :] = zrow
        prev = buf_ref[halo - 1:halo - 1 + tm, :]
        nxt = buf_ref[halo + 1:halo + 1 + tm, :]
        shifted = jnp.where((lane & 1) == 0, prev, nxt)
    zs = z + mu_ref[...] * (shifted - z)
    r = zs[:, 0:MIX_W]
    k = zs[:, MIX_W:2 * MIX_W]
    v = zs[:, 2 * MIX_W:3 * MIX_W]
    sm = zs[:, 3 * MIX_W:4 * MIX_W]
    ones_bd = _head_ones()
    a0, k_k, k_a, r_k = (vec_ref[j:j + 1, :] for j in range(4))
    a = jax.nn.sigmoid(a0 + _dot(sm.astype(BF16), lora_ref[2]))
    g_ref[...] = _dot(jax.nn.sigmoid(sm).astype(BF16), lora_ref[3])
    kk = k * k_k
    kk = kk * lax.rsqrt(_head_sum(kk * kk, ones_bd) + 1e-12)
    k2 = k * (1.0 + (a - 1.0) * k_a)
    th = jnp.tanh(sm).astype(BF16)
    for di in range(2):
        w_ref[di] = -math.exp(-0.5) * jax.nn.sigmoid(w0_ref[di:di + 1, :] + _dot(th, lora_ref[di]))
    r_ref[...] = r
    k2_ref[...] = k2
    v_ref[...] = v
    nkk_ref[...] = -kk
    kka_ref[...] = kk * a
    bonus_ref[...] = _head_sum(r * k2 * r_k, ones_bd) * v


def _rw_prep(z, mu_p, vecs, w0, lora, grid_shift):
    B, T, _ = z.shape
    width = 4 * MIX_W
    wblk = ZB_RW // 4
    if grid_shift:
        tm = min(T, 512)
        hb = tm // GRID_W
        nh = T // GRID_W
        z_specs = [
            pl.BlockSpec((None, tm, width), lambda b, i: (b, i, wblk)),
            pl.BlockSpec((None, GRID_W, width), lambda b, i: (b, jnp.maximum(i * hb - 1, 0), wblk)),
            pl.BlockSpec((None, GRID_W, width), lambda b, i: (b, jnp.minimum((i + 1) * hb, nh - 1), wblk)),
        ]
        z_args = (z, z, z)
    else:
        tm = T
        z_specs = [pl.BlockSpec((None, tm, width), lambda b, i: (b, i, wblk))]
        z_args = (z,)
    const = lambda shape: pl.BlockSpec(shape, lambda b, i: (0,) * len(shape))
    ospec = pl.BlockSpec((None, tm, MIX_W), lambda b, i: (b, i, 0))
    oshape = jax.ShapeDtypeStruct((B, T, MIX_W), F32)
    dspec = pl.BlockSpec((2, None, tm, MIX_W), lambda b, i: (0, b, i, 0))
    dshape = jax.ShapeDtypeStruct((2, B, T, MIX_W), F32)
    return pl.pallas_call(
        functools.partial(_rw_prep_kernel, grid_shift=grid_shift, tm=tm),
        grid=(B, T // tm),
        in_specs=z_specs + [const((1, width)), const((4, MIX_W)), const((2, MIX_W)),
                            const((4, MIX_W, MIX_W))],
        out_specs=[ospec, ospec, ospec, dspec, ospec, ospec, ospec, ospec],
        out_shape=[oshape, oshape, oshape, dshape, oshape, oshape, oshape, oshape],
        scratch_shapes=[pltpu.VMEM((tm + 2 * GRID_W, width), F32)],
        compiler_params=_cparams("parallel", "parallel"),
        name="rwkv_prep",
    )(*z_args, mu_p, vecs, w0, lora)


RW_CHUNK = 64
RW_CHUNK_SHIFT = 6


def _mm(a, b):
    return jnp.dot(a.astype(BF16), b.astype(BF16), preferred_element_type=F32)


def _rwkv_steps(lw_ref, a_ref, b_ref, k_ref, r_ref, v_ref, s0_ref, y_ref, fin_ref, s_ref, g_ref,
                *, rev, tb, nt):
    jb = pl.program_id(1)
    ch = RW_CHUNK
    pair = 2 * HEAD_D

    @pl.when(jb == 0)
    def _():
        s_ref[...] = s0_ref[...]

    lw = lw_ref[...]
    rr = lax.broadcasted_iota(jnp.int32, (tb, tb), 0)
    cc = lax.broadcasted_iota(jnp.int32, (tb, tb), 1)
    same = (rr >> RW_CHUNK_SHIFT) == (cc >> RW_CHUNK_SHIFT)
    tri = jnp.where(same & ((cc >= rr) if rev else (cc <= rr)), 1.0, 0.0).astype(BF16)
    p1 = lw.astype(BF16)
    r1 = lw - p1.astype(F32)
    p2 = r1.astype(BF16)
    p3 = (r1 - p2.astype(F32)).astype(BF16)
    g_ref[...] = _dot(tri, p1) + _dot(tri, p2) + _dot(tri, p3)

    si = lax.broadcasted_iota(jnp.int32, (ch, ch), 0)
    ri = lax.broadcasted_iota(jnp.int32, (ch, ch), 1)
    if rev:
        si, ri = ch - 1 - si, ch - 1 - ri
    strict = ri < si
    incl = ri <= si
    eye = jnp.where(ri == si, 1.0, 0.0).astype(F32)
    levels = [((si >> (lv + 1)) == (ri >> (lv + 1))) & (((si >> lv) & 1) == 1) & (((ri >> lv) & 1) == 0)
              for lv in range(RW_CHUNK_SHIFT)]
    lo = lax.broadcasted_iota(jnp.int32, (ch, pair), 1) < HEAD_D
    same_head = ((lax.broadcasted_iota(jnp.int32, (pair, pair), 0) >> HEAD_SHIFT)
                 == (lax.broadcasted_iota(jnp.int32, (pair, pair), 1) >> HEAD_SHIFT))

    nch = tb // ch
    g = g_ref[...]
    e_g = jnp.exp(g)
    e_ng = jnp.exp(-g)
    at_f = a_ref[...] * jnp.exp(g - lw)
    rt_f = r_ref[...] * e_g
    at = at_f.astype(BF16)
    rt = rt_f.astype(BF16)
    first = (lax.broadcasted_iota(jnp.int32, (tb, MIX_W), 1) & HEAD_D) == 0
    at_h = [jnp.where(first, at_f, 0.0).astype(BF16), jnp.where(first, 0.0, at_f).astype(BF16)]
    rt_h = [jnp.where(first, rt_f, 0.0).astype(BF16), jnp.where(first, 0.0, rt_f).astype(BF16)]
    bt = (b_ref[...] * e_ng).astype(BF16)
    kt = (k_ref[...] * e_ng).astype(BF16)
    vb = v_ref[...].astype(BF16)

    items = [(c, hp, hh) for c in range(nch) for hp in range(2) for hh in range(2)]
    rows = lambda c: slice(c * ch, (c + 1) * ch)
    lanes = lambda hp: slice(hp * pair, (hp + 1) * pair)
    n_m, p_m, m_m, q_m = {}, {}, {}, {}
    for it in items:
        c, hp, hh = it
        ar = jnp.concatenate([at_h[hh][rows(c), lanes(hp)], rt_h[hh][rows(c), lanes(hp)]], axis=0)
        np_ = _dot_nt(ar, bt[rows(c), lanes(hp)])
        mq = _dot_nt(ar, kt[rows(c), lanes(hp)])
        n_m[it] = jnp.where(strict, np_[:ch], 0.0).astype(BF16)
        p_m[it] = jnp.where(incl, np_[ch:], 0.0).astype(BF16)
        m_m[it] = jnp.where(strict, mq[:ch], 0.0).astype(BF16)
        q_m[it] = jnp.where(incl, mq[ch:], 0.0).astype(BF16)
        if hp == 1 and hh == 1:
            yield
    t_m = {it: eye + jnp.where(levels[0], n_m[it].astype(F32), 0.0) for it in items}
    for lv in range(1, RW_CHUNK_SHIFT):
        tn = {it: _mm(t_m[it], n_m[it]) for it in items}
        yield
        t_m = {it: t_m[it] + jnp.where(levels[lv], _mm(tn[it], t_m[it]), 0.0) for it in items}
        yield
    mv = {it: _mm(m_m[it], vb[rows(it[0]), lanes(it[1])]) for it in items}
    qv = {it: _mm(q_m[it], vb[rows(it[0]), lanes(it[1])]) for it in items}
    yield
    y2 = {it: _mm(t_m[it], jnp.concatenate([at[rows(it[0]), lanes(it[1])], mv[it].astype(BF16)], axis=1))
          for it in items}
    yield

    last = 0 if rev else ch - 1
    for ci in range(nch):
        c = nch - 1 - ci if rev else ci
        g_c = g[c * ch + last:c * ch + last + 1, :]
        e_gc = jnp.exp(g_c - g[rows(c), :])
        bh = (b_ref[rows(c), :] * e_gc).astype(BF16)
        kh = (k_ref[rows(c), :] * e_gc).astype(BF16)
        dec_c = jnp.exp(g_c)
        xs, sas, ss = [], [], []
        for hp in range(2):
            s = s_ref[hp]
            w2 = jnp.where(lo, y2[(c, hp, 0)][:, :pair], y2[(c, hp, 1)][:, :pair]).astype(BF16)
            xs.append(_dot_nt(jnp.concatenate([w2, rt[rows(c), lanes(hp)]], axis=0), s.astype(BF16)))
            ss.append(s)
        for hp in range(2):
            w1 = jnp.where(lo, y2[(c, hp, 0)][:, pair:], y2[(c, hp, 1)][:, pair:])
            sas.append(w1 + xs[hp][:ch])
        for hp in range(2):
            sa = sas[hp]
            y_ref[rows(c), lanes(hp)] = xs[hp][ch:] + jnp.where(
                lo, _mm(p_m[(c, hp, 0)], sa) + qv[(c, hp, 0)], _mm(p_m[(c, hp, 1)], sa) + qv[(c, hp, 1)])
            upd = _dot_tn(jnp.concatenate([sa.astype(BF16), vb[rows(c), lanes(hp)]], axis=0),
                          jnp.concatenate([bh[:, lanes(hp)], kh[:, lanes(hp)]], axis=0))
            s_ref[hp] = ss[hp] * dec_c[:, lanes(hp)] + jnp.where(same_head, upd, 0.0)
        yield

    @pl.when(jb == nt - 1)
    def _():
        fin_ref[...] = s_ref[...]


def _hg_rw_kernel(*refs, rev, tb, nt):
    gla_in, rw_in = refs[0:5], refs[5:12]
    gla_out, rw_out = refs[12:14], refs[14:16]
    gla_scr, rw_scr = refs[16:20], refs[20:22]
    gla = _gla_steps(*gla_in, *gla_out, *gla_scr, rev=rev, tb=tb, nt=nt)
    rwkv = _rwkv_steps(*rw_in, *rw_out, *rw_scr, rev=rev, tb=tb, nt=nt)
    live = [gla, rwkv]
    while live:
        for gen in list(live):
            if next(gen, StopIteration) is StopIteration:
                live.remove(gen)


def _head_pair_blockdiag(s):
    B = s.shape[0]
    s = s.reshape(B, 2, 2, HEAD_D, HEAD_D)
    zero = jnp.zeros_like(s[:, :, 0])
    return jnp.concatenate([jnp.concatenate([s[:, :, 0], zero], axis=-1),
                            jnp.concatenate([zero, s[:, :, 1]], axis=-1)], axis=-2)


def _head_pair_blocks(t):
    B = t.shape[0]
    t = jnp.stack([t[:, :, :HEAD_D, :HEAD_D], t[:, :, HEAD_D:, HEAD_D:]], axis=2)
    return t.reshape(B, N_HEADS, HEAD_D, HEAD_D)


def _hg_rw(z, lb, s0_hg, lw, a, b, k, r, v, s0_rw, rev):
    B, T, _ = v.shape
    pair = 2 * HEAD_D
    tb = min(T, 256)
    nt = T // tb
    di = 1 if rev else 0
    tmap = (lambda j: nt - 1 - j) if rev else (lambda j: j)
    zspec = lambda blk: pl.BlockSpec((None, tb, MIX_W), lambda bi, j: (bi, tmap(j), blk))
    spec = zspec(0)
    sspec = pl.BlockSpec((None, 2, pair, pair), lambda bi, j: (bi, 0, 0, 0))
    state = jax.ShapeDtypeStruct((B, 2, pair, pair), F32)
    seq = jax.ShapeDtypeStruct((B, T, MIX_W), F32)
    blk = pltpu.VMEM((tb, MIX_W), F32)
    st = pltpu.VMEM((2, pair, pair), F32)
    o, fin_hg, y, fin_rw = pl.pallas_call(
        functools.partial(_hg_rw_kernel, rev=rev, tb=tb, nt=nt),
        grid=(B, nt),
        in_specs=[zspec(ZB_HG), zspec(ZB_HG + 1 + di), zspec(ZB_HG + 3),
                  pl.BlockSpec((1, MIX_W), lambda bi, j: (0, 0)), sspec,
                  pl.BlockSpec((None, None, tb, MIX_W), lambda bi, j: (di, bi, tmap(j), 0)),
                  spec, spec, spec, spec, spec, sspec],
        out_specs=[spec, sspec, spec, sspec],
        out_shape=[seq, state, seq, state],
        scratch_shapes=[st, blk, blk, blk, st, blk],
        compiler_params=_cparams("parallel", "arbitrary"),
        name="hgrn_rwkv_bwd" if rev else "hgrn_rwkv_fwd",
    )(z, z, z, lb[di:di + 1], _head_pair_blockdiag(s0_hg),
      lw, a, b, k, r, v, _head_pair_blockdiag(s0_rw))
    return o, _head_pair_blocks(fin_hg), y, _head_pair_blocks(fin_rw)


def _mix_kernel(x_ref, sc_ref, sh_ref, g1_ref, nw0_ref, nw_ref, wg_ref,
                retf_ref, retb_ref, retg_ref, s5f_ref, s5b_ref, s5u_ref, hgf_ref, hgb_ref, hgg_ref, rwf_ref, rwr_ref,
                rwb_ref, rwg_ref,
                vec_ref, glu_w_ref, wbr_ref, wout_ref, o_ref):
    h = (_rms(x_ref[...], nw0_ref[...]) * (1.0 + sc_ref[...]) + sh_ref[...]).astype(BF16)
    gn_w, s5_d, glu_b, hg_w, ln_w, ln_b = (vec_ref[j:j + 1, :] for j in range(6))

    def group_norm(o, eps):
        mu = _head_sum_lanes(o) * (1.0 / HEAD_D)
        oc = o - mu
        var = _head_sum_lanes(oc * oc) * (1.0 / HEAD_D)
        return oc * lax.rsqrt(var + eps)

    g = retg_ref[...]
    y_ret = group_norm(retf_ref[...] + retb_ref[...], EPS) * gn_w * (g * jax.nn.sigmoid(g))

    y = s5_d * s5u_ref[...] + s5f_ref[...] + s5b_ref[...]
    yg = jax.nn.gelu(y)
    y_s5 = yg * jax.nn.sigmoid(_dot(yg.astype(BF16), glu_w_ref[...]) + glu_b)

    o = hgf_ref[...] + hgb_ref[...]
    g = hgg_ref[...]
    ms = _head_sum_lanes(o * o) * (1.0 / HEAD_D)
    y_hg = o * lax.rsqrt(ms + EPS) * hg_w * (g * jax.nn.sigmoid(g))

    y = group_norm(rwf_ref[...] + rwr_ref[...], RW_LN_EPS) * ln_w + ln_b
    y_rw = (y + rwb_ref[...]) * rwg_ref[...]

    mixed = None
    for m, ym in enumerate((y_ret, y_s5, y_hg, y_rw)):
        br = _dot(ym.astype(BF16), wbr_ref[m])
        term = jax.nn.sigmoid(_dot(h, wg_ref[:, m * D_MODEL:(m + 1) * D_MODEL])) * br
        mixed = term if mixed is None else mixed + term
    mixed = _dot(mixed.astype(BF16), wout_ref[...])
    o_ref[...] = x_ref[...] + g1_ref[...] * _rms(mixed, nw_ref[...])


def _mix(x, sc, sh, g1, nw0, nw, w_gate, z, ret_o, s5_y, hg_o, rw_y, rw_bonus, rw_g, vecs, glu_w,
         w_branch, w_out):
    B, T, _ = x.shape
    tm = min(T, 256)
    xspec = pl.BlockSpec((None, tm, D_MODEL), lambda b, i: (b, i, 0))
    zspec = lambda blk: pl.BlockSpec((None, tm, MIX_W), lambda b, i: (b, i, blk))
    const = lambda shape: pl.BlockSpec(shape, lambda b, i: (0,) * len(shape))
    mspec = lambda m: pl.BlockSpec((None, 1, D_MODEL), _mod_map(m.shape[0]))
    return pl.pallas_call(
        _mix_kernel,
        grid=(B, T // tm),
        in_specs=[
            xspec, mspec(sc), mspec(sh), mspec(g1),
            const((1, D_MODEL)), const((1, D_MODEL)),
            const((D_MODEL, 4 * D_MODEL)),
            zspec(0), zspec(0), zspec(ZB_RET + 3),
            zspec(0), zspec(0), zspec(ZB_S5),
            zspec(0), zspec(0), zspec(ZB_HG + 4),
            zspec(0), zspec(0), zspec(0), zspec(0),
            const((6, MIX_W)), const((MIX_W, MIX_W)),
            const((4, MIX_W, D_MODEL)), const((D_MODEL, D_MODEL)),
        ],
        out_specs=xspec,
        out_shape=jax.ShapeDtypeStruct((B, T, D_MODEL), F32),
        compiler_params=_cparams("parallel", "parallel"),
        name="mix_out",
    )(x, sc, sh, g1, nw0, nw, w_gate, ret_o[0], ret_o[1], z, s5_y[0], s5_y[1], z, hg_o[0], hg_o[1], z, rw_y[0], rw_y[1],
      rw_bonus, rw_g, vecs, glu_w, w_branch, w_out)


FFN_CHUNK = 1024


def _ffn_kernel(x_ref, sc_ref, sh_ref, g2_ref, nw2_ref, nw3_ref, w1_ref, w2_ref, o_ref):
    x = x_ref[...]
    h = (_rms(x, nw2_ref[...]) * (1.0 + sc_ref[...]) + sh_ref[...]).astype(BF16)
    acc = None
    for j in range(D_FF // FFN_CHUNK):
        cols = slice(j * FFN_CHUNK, (j + 1) * FFN_CHUNK)
        a = jnp.maximum(_dot(h, w1_ref[:, cols]), 0.0)
        part = _dot((a * a).astype(BF16), w2_ref[cols, :])
        acc = part if acc is None else acc + part
    o_ref[...] = x + g2_ref[...] * _rms(acc, nw3_ref[...])


def _ffn(x, sc, sh, g2, nw2, nw3, w1, w2):
    B, T, _ = x.shape
    tm = min(T, 1024)
    xspec = pl.BlockSpec((None, tm, D_MODEL), lambda b, i: (b, i, 0))
    mspec = lambda m: pl.BlockSpec((None, 1, D_MODEL), _mod_map(m.shape[0]))
    const = pl.BlockSpec((1, D_MODEL), lambda b, i: (0, 0))
    once = pl.Buffered(1)
    return pl.pallas_call(
        _ffn_kernel,
        grid=(B, T // tm),
        in_specs=[xspec, mspec(sc), mspec(sh), mspec(g2), const, const,
                  pl.BlockSpec((D_MODEL, D_FF), lambda b, i: (0, 0), pipeline_mode=once),
                  pl.BlockSpec((D_FF, D_MODEL), lambda b, i: (0, 0), pipeline_mode=once)],
        out_specs=xspec,
        out_shape=jax.ShapeDtypeStruct((B, T, D_MODEL), F32),
        compiler_params=_cparams("parallel", "parallel"),
        name="ffn",
    )(x, sc, sh, g2, nw2, nw3, w1, w2)


def _layer_params(l, p):
    w = p['w_in'][l]
    w_p = jnp.concatenate(
        [w[:, 2560:3488], jnp.zeros((D_MODEL, 4 * MIX_W - RW_COLS), F32),
         w[:, 0:1024], w[:, 1024:1280], w[:, 1280:2560]], axis=1).astype(BF16)
    w_gate = w[:, 3488:7584].astype(BF16)

    a_re, a_im, bb_re, bb_im = _s5_zoh(p['s5_lam_re'][l], p['s5_lam_im'][l], p['s5_log_dt'][l],
                                       p['s5_b_re'][l], p['s5_b_im'][l])
    eye = jnp.eye(S5_GROUPS, dtype=F32)
    bblk = lambda bb: jnp.einsum('dgph,gk->dghkp', bb, eye).reshape(2, MIX_W, S5_STATE).astype(BF16)
    cblk = lambda c: jnp.einsum('ghp,gk->gpkh', c, eye).reshape(S5_STATE, MIX_W).astype(BF16)

    def lora_pad(m, row0):
        return jnp.zeros((MIX_W, MIX_W), F32).at[row0:row0 + m.shape[0]].set(m)

    lora = jnp.stack([lora_pad(p['rw_w2'][l, 0], 0), lora_pad(p['rw_w2'][l, 1], 32),
                      lora_pad(p['rw_a2'][l], 64), lora_pad(p['rw_g2'][l], 96)]).astype(BF16)
    mu_p = jnp.concatenate([p['rw_mu'][l], jnp.zeros((4 * MIX_W - RW_COLS,), F32)]).reshape(1, 4 * MIX_W)
    return dict(
        w_p=w_p, w_gate=w_gate, nw=p['norm_w'][l],
        s5=(a_re, a_im, bblk(bb_re), bblk(bb_im), cblk(p['s5_c_re'][l]), cblk(p['s5_c_im'][l])),
        rw_mu=mu_p, rw_lora=lora, rw_w0=p['rw_w0'][l],
        rw_vecs=jnp.stack([p['rw_a0'][l], p['rw_k_k'][l], p['rw_k_a'][l], p['rw_r_k'][l]]),
        mix_vecs=jnp.stack([p['ret_gn_w'][l], p['s5_d'][l], p['s5_glu_b'][l], p['hg_norm_w'][l],
                            p['rw_ln_w'][l], p['rw_ln_b'][l]]),
        glu_w=p['s5_glu_w'][l].astype(BF16),
        w_branch=p['w_branch'][l].astype(BF16), w_out=p['w_out'][l].astype(BF16),
        ff_w1=p['ff_w1'][l].astype(BF16), ff_w2=p['ff_w2'][l].astype(BF16),
    )


def _trunk_layer(x, mod, init, grid_shift, lp, hg_lb):
    B, T, _ = x.shape
    sh1, sc1, g1, sh2, sc2, g2 = (m[:, None, :] for m in jnp.split(mod, 6, axis=-1))
    nw = lp['nw']
    if mod.shape[0] == 1:
        tok = lambda t: t.reshape(t.shape[:-3] + (1, B * T, t.shape[-1]))
    else:
        tok = lambda t: t
    z = _proj_in(tok(x), sc1, sh1, nw[0:1], lp['w_p']).reshape(B, T, Z_COLS)

    s_ret, s_s5r, s_s5i, s_hg, s_rw = init
    s5_f, s5_b, f_s5r, f_s5i, ret_f, ret_b, f_ret = _ret_s5(
        z, *lp['s5'], s_s5r.reshape(B, 2, 1, S5_STATE), s_s5i.reshape(B, 2, 1, S5_STATE), s_ret)
    f_s5r = f_s5r.reshape(B, 2, S5_GROUPS, S5_P)
    f_s5i = f_s5i.reshape(B, 2, S5_GROUPS, S5_P)

    r, k2, v, w, nkk, kka, rw_g, bonus = _rw_prep(z, lp['rw_mu'], lp['rw_vecs'], lp['rw_w0'],
                                                  lp['rw_lora'], grid_shift)
    s_hg_t = jnp.swapaxes(s_hg, -1, -2)
    hg_f, f_hg_f, rw_f, f_rw_f = _hg_rw(z, hg_lb, s_hg_t[:, 0], w, nkk, kka, k2, r, v, s_rw[:, 0], False)
    hg_b, f_hg_b, rw_b, f_rw_b = _hg_rw(z, hg_lb, s_hg_t[:, 1], w, nkk, kka, k2, r, v, s_rw[:, 1], True)
    f_hg = jnp.swapaxes(jnp.stack([f_hg_f, f_hg_b], axis=1), -1, -2)
    f_rw = jnp.stack([f_rw_f, f_rw_b], axis=1)

    x = _mix(tok(x), sc1, sh1, g1, nw[0:1], nw[1:2], lp['w_gate'], tok(z), (tok(ret_f), tok(ret_b)), (tok(s5_f), tok(s5_b)),
             (tok(hg_f), tok(hg_b)), (tok(rw_f), tok(rw_b)), tok(bonus), tok(rw_g),
             lp['mix_vecs'], lp['glu_w'], lp['w_branch'], lp['w_out'])
    x = _ffn(x, sc2, sh2, g2, nw[2:3], nw[3:4], lp['ff_w1'], lp['ff_w2'])
    return x.reshape(B, T, D_MODEL), (f_ret, f_s5r, f_s5i, f_hg, f_rw)


def kernel(x_prompt, x_sample, state_ret, state_s5_re, state_s5_im, state_hgrn, state_rwkv, c, c_ctx, ada_w, ada_b, norm_w, w_in, ret_gn_w, s5_lam_re, s5_lam_im, s5_log_dt, s5_b_re, s5_b_im, s5_c_re, s5_c_im, s5_d, s5_glu_w, s5_glu_b, hg_lb, hg_norm_w, rw_mu, rw_w0, rw_w2, rw_a0, rw_a2, rw_g2, rw_k_k, rw_k_a, rw_r_k, rw_ln_w, rw_ln_b, w_branch, w_out, ff_w1, ff_w2):
    p = dict(norm_w=norm_w, w_in=w_in, ret_gn_w=ret_gn_w, s5_lam_re=s5_lam_re, s5_lam_im=s5_lam_im,
             s5_log_dt=s5_log_dt, s5_b_re=s5_b_re, s5_b_im=s5_b_im, s5_c_re=s5_c_re, s5_c_im=s5_c_im,
             s5_d=s5_d, s5_glu_w=s5_glu_w, s5_glu_b=s5_glu_b, hg_norm_w=hg_norm_w, rw_mu=rw_mu,
             rw_w0=rw_w0, rw_w2=rw_w2, rw_a0=rw_a0, rw_a2=rw_a2, rw_g2=rw_g2, rw_k_k=rw_k_k,
             rw_k_a=rw_k_a, rw_r_k=rw_r_k, rw_ln_w=rw_ln_w, rw_ln_b=rw_ln_b, w_branch=w_branch,
             w_out=w_out, ff_w1=ff_w1, ff_w2=ff_w2)
    depth = w_in.shape[0]
    n_ctx = x_prompt.shape[0]
    n_lat = x_sample.shape[0]
    assert 1 + n_lat <= 8

    lb_cum = jnp.cumsum(jax.nn.softmax(hg_lb.astype(F32), axis=0), axis=0)
    hg_lower = lb_cum - lb_cum[0]

    cond = jnp.concatenate([c_ctx[None, :], c, jnp.zeros((7 - n_lat, D_MODEL), F32)], axis=0)
    mod = _ada_mod(cond, ada_w, ada_b)

    zero_state = (jnp.zeros((n_ctx, 2, N_HEADS, HEAD_D, HEAD_D), F32),
                  jnp.zeros((n_ctx, 2, S5_GROUPS, S5_P), F32),
                  jnp.zeros((n_ctx, 2, S5_GROUPS, S5_P), F32),
                  jnp.zeros((n_ctx, 2, N_HEADS, HEAD_D, HEAD_D), F32),
                  jnp.zeros((n_ctx, 2, N_HEADS, HEAD_D, HEAD_D), F32))
    xp, xs = x_prompt, x_sample
    finals = []
    for l in range(depth):
        lp = _layer_params(l, p)
        xp, fin = _trunk_layer(xp, mod[l, 0:1], zero_state, False, lp, hg_lower[l])
        finals.append(fin)
        lat_init = (state_ret[:, l], state_s5_re[:, l], state_s5_im[:, l], state_hgrn[:, l], state_rwkv[:, l])
        xs, _ = _trunk_layer(xs, mod[l, 1:1 + n_lat], lat_init, True, lp, hg_lower[l])
    new_states = tuple(jnp.stack([f[i] for f in finals], axis=1) for i in range(5))
    return (xp, xs) + new_states
```
